```python
import jax, jax.numpy as jnp
from jax import lax
import numpy as np

D_MODEL = 1024
BATCH = 8
SEQ = 4096
DEPTH = 1

CTX_LEN = 256
GRID_W = 64
ROPE_THETA = 10000.0
NORM_EPS = 1e-6
N_MOD = 6
NEG_INF = -1e30

MLA_HEADS = 8
MLA_NOPE = 64
MLA_ROPE = 32
MLA_QK = MLA_NOPE + MLA_ROPE
MLA_V = 64
MLA_Q_RANK = 384
MLA_KV_RANK = 256
Q_BLOCK = 128

SWA_HEADS = 8
SWA_KV_HEADS = 2
SWA_GROUP = SWA_HEADS // SWA_KV_HEADS
SWA_HD = 64
WINDOW = 128
BAND_BLOCK = 128

N_EXPERTS = 32
TOP_K = 4
D_EXPERT = D_MODEL
SWIGLU_LIMIT = 7.0
SWIGLU_ALPHA = 1.702
MOE_BLOCK = 256

KV_COLS = MLA_KV_RANK + MLA_ROPE + 2 * SWA_KV_HEADS * SWA_HD
KV_OFFSETS = (MLA_KV_RANK, MLA_KV_RANK + MLA_ROPE, MLA_KV_RANK + MLA_ROPE + SWA_KV_HEADS * SWA_HD)
Q_OFFSETS = (MLA_Q_RANK, MLA_Q_RANK + SWA_HEADS * SWA_HD, MLA_Q_RANK + SWA_HEADS * SWA_HD + D_MODEL)
IN_COLS = KV_COLS + MLA_Q_RANK + SWA_HEADS * SWA_HD + 2 * D_MODEL

kernel_name = "hybrid_mla_swa_moe_dit_layer"


def rmsnorm(x, g):
    xf = x.astype(jnp.float32)
    y = xf * lax.rsqrt(jnp.mean(xf * xf, axis=-1, keepdims=True) + NORM_EPS)
    return y.astype(x.dtype) * g


def ada_params(cond, w_ada, b_ada):
    m = jax.nn.silu(cond) @ w_ada + b_ada
    return jnp.split(m, N_MOD, axis=-1)


def modulate(x, g, shift, scale):
    return rmsnorm(x, g) * (1 + scale[..., None, :]) + shift[..., None, :]


def axial_rope_tables(row, col, dim, dtype):
    q = dim // 4
    freqs = ROPE_THETA ** (-jnp.arange(q, dtype=jnp.float32) / q)
    ang_r = row.astype(jnp.float32)[:, None] * freqs
    ang_c = col.astype(jnp.float32)[:, None] * freqs
    ang = jnp.concatenate([ang_r, ang_r, ang_c, ang_c], axis=-1)
    return jnp.cos(ang).astype(dtype), jnp.sin(ang).astype(dtype)


def rotate_axial(x):
    x1, x2, x3, x4 = jnp.split(x, 4, axis=-1)
    return jnp.concatenate([-x2, x1, -x4, x3], axis=-1)


def apply_rope(x, cos, sin):
    return x * cos[:, None, :] + rotate_axial(x) * sin[:, None, :]


def kv_side(h, p, rope):
    b, n = h.shape[:2]
    kv_lat, k_rope, k_swa, v_swa = jnp.split(h @ p['w_in'][:, :KV_COLS], KV_OFFSETS, axis=-1)
    kv = (rmsnorm(kv_lat, p['mla_kv_a_g']) @ p['w_kv_up']).reshape(b, n, MLA_HEADS, MLA_NOPE + MLA_V)
    k_nope, v_mla = kv[..., :MLA_NOPE], kv[..., MLA_NOPE:]
    k_r = jnp.broadcast_to(k_rope[:, :, None, :], (b, n, MLA_HEADS, MLA_ROPE))
    k_mla = rmsnorm(jnp.concatenate([k_nope, k_r], axis=-1), p['mla_k_g'])
    k_swa = rmsnorm(k_swa.reshape(b, n, SWA_KV_HEADS, SWA_HD), p['swa_k_g'])
    v_swa = v_swa.reshape(b, n, SWA_KV_HEADS, SWA_HD)
    if rope is not None:
        cos_m, sin_m, cos_s, sin_s = rope
        k_mla = jnp.concatenate([k_mla[..., :MLA_NOPE], apply_rope(k_mla[..., MLA_NOPE:], cos_m, sin_m)], axis=-1)
        k_swa = apply_rope(k_swa, cos_s, sin_s)
    return k_mla, v_mla, k_swa, v_swa


def q_side(h, p, rope):
    b, n = h.shape[:2]
    q_lat, q_swa, gate_a, gate_b = jnp.split(h @ p['w_in'][:, KV_COLS:], Q_OFFSETS, axis=-1)
    q_mla = (rmsnorm(q_lat, p['mla_q_a_g']) @ p['w_q_up']).reshape(b, n, MLA_HEADS, MLA_QK)
    q_mla = rmsnorm(q_mla, p['mla_q_g'])
    q_swa = rmsnorm(q_swa.reshape(b, n, SWA_HEADS, SWA_HD), p['swa_q_g'])
    if rope is not None:
        cos_m, sin_m, cos_s, sin_s = rope
        q_mla = jnp.concatenate([q_mla[..., :MLA_NOPE], apply_rope(q_mla[..., MLA_NOPE:], cos_m, sin_m)], axis=-1)
        q_swa = apply_rope(q_swa, cos_s, sin_s)
    return q_mla, q_swa, gate_a, gate_b


def dense_attend(q, k, v, scale):
    s = jnp.einsum('bqhd,bkhd->bhqk', q, k).astype(jnp.float32) * scale
    pr = jax.nn.softmax(s, axis=-1).astype(v.dtype)
    return jnp.einsum('bhqk,bkhd->bqhd', pr, v)


def mla_attend_latent(q, k_lat, v_lat, k_ctx, v_ctx):
    b, n = q.shape[:2]
    k = jnp.concatenate([k_lat, k_ctx], axis=1)
    v = jnp.concatenate([v_lat, v_ctx], axis=1)
    nb = n // Q_BLOCK
    qb = q.reshape(b, nb, Q_BLOCK, MLA_HEADS, MLA_QK).swapaxes(0, 1)
    o = lax.map(lambda qi: dense_attend(qi, k, v, MLA_QK ** -0.5), qb)
    return o.swapaxes(0, 1).reshape(b, n, MLA_HEADS * MLA_V)


def sink_logits(sink, lead_shape):
    s = sink.astype(jnp.float32).reshape(SWA_KV_HEADS, SWA_GROUP)[:, :, None, None]
    return jnp.broadcast_to(s, lead_shape + (1,))


def swa_attend_latent(q, k_lat, v_lat, k_ctx, v_ctx, sink):
    b, n = q.shape[:2]
    nb = n // BAND_BLOCK
    n_ctx = k_ctx.shape[1]
    nband = 3 * BAND_BLOCK
    qb = q.reshape(b, nb, BAND_BLOCK, SWA_KV_HEADS, SWA_GROUP, SWA_HD)

    def band(t):
        tp = jnp.pad(t.reshape(b, nb, BAND_BLOCK, SWA_KV_HEADS, SWA_HD), ((0, 0), (1, 1), (0, 0), (0, 0), (0, 0)))
        return jnp.concatenate([tp[:, :-2], tp[:, 1:-1], tp[:, 2:]], axis=2)

    kb, vb = band(k_lat), band(v_lat)
    scale = SWA_HD ** -0.5
    s_band = jnp.einsum('bnqkgd,bnjkd->bnkgqj', qb, kb).astype(jnp.float32) * scale
    qi = jnp.arange(BAND_BLOCK)[:, None]
    kj = jnp.arange(nband)[None, :]
    rel = qi + BAND_BLOCK - kj
    kpos = (jnp.arange(nb)[:, None, None] - 1) * BAND_BLOCK + kj[None]
    allowed = (jnp.abs(rel) <= WINDOW)[None] & (kpos >= 0) & (kpos < n)
    s_band = jnp.where(allowed[None, :, None, None], s_band, NEG_INF)
    s_ctx = jnp.einsum('bnqkgd,bckd->bnkgqc', qb, k_ctx).astype(jnp.float32) * scale
    s_all = jnp.concatenate([s_band, s_ctx, sink_logits(sink, s_band.shape[:-1])], axis=-1)
    pr = jax.nn.softmax(s_all, axis=-1).astype(v_lat.dtype)
    o = (jnp.einsum('bnkgqj,bnjkd->bnqkgd', pr[..., :nband], vb)
         + jnp.einsum('bnkgqc,bckd->bnqkgd', pr[..., nband:nband + n_ctx], v_ctx))
    return o.reshape(b, n, SWA_HEADS * SWA_HD)


def swa_attend_context(q, k_ctx, v_ctx, sink):
    b, n = q.shape[:2]
    qg = q.reshape(b, n, SWA_KV_HEADS, SWA_GROUP, SWA_HD)
    s = jnp.einsum('bqkgd,bckd->bkgqc', qg, k_ctx).astype(jnp.float32) * SWA_HD ** -0.5
    s_all = jnp.concatenate([s, sink_logits(sink, s.shape[:-1])], axis=-1)
    pr = jax.nn.softmax(s_all, axis=-1)[..., :n].astype(v_ctx.dtype)
    return jnp.einsum('bkgqc,bckd->bqkgd', pr, v_ctx).reshape(b, n, SWA_HEADS * SWA_HD)


def merge_branches(o_a, o_b, gate_a, gate_b, p):
    y = jax.nn.sigmoid(gate_a) * (o_a @ p['w_branch_a']) + jax.nn.sigmoid(gate_b) * (o_b @ p['w_branch_b'])
    return y @ p['w_out']


def clamped_swiglu(gu):
    x_glu, x_lin = gu[..., ::2], gu[..., 1::2]
    x_glu = jnp.minimum(x_glu, SWIGLU_LIMIT)
    x_lin = jnp.clip(x_lin, -SWIGLU_LIMIT, SWIGLU_LIMIT)
    return x_glu * jax.nn.sigmoid(SWIGLU_ALPHA * x_glu) * (x_lin + 1)


def moe(h, p):
    b, n, d = h.shape
    t = b * n
    a = t * TOP_K
    hf = h.reshape(t, d)
    logits = (hf @ p['w_router'] + p['b_router']).astype(jnp.float32)
    top_vals, top_idx = lax.top_k(logits, TOP_K)
    wts = jax.nn.softmax(top_vals, axis=-1).astype(h.dtype)
    expert_flat = top_idx.reshape(a).astype(jnp.int32)
    token_flat = jnp.arange(a, dtype=jnp.int32) // TOP_K
    w_flat = wts.reshape(a)
    order = jnp.argsort(expert_flat, stable=True)
    sorted_e = expert_flat[order]
    counts = jnp.bincount(expert_flat, length=N_EXPERTS)
    starts = jnp.cumsum(counts) - counts
    padded = (counts + MOE_BLOCK - 1) // MOE_BLOCK * MOE_BLOCK
    pends = jnp.cumsum(padded)
    pstarts = pends - padded
    dest = pstarts[sorted_e] + (jnp.arange(a, dtype=jnp.int32) - starts[sorted_e])
    n_slots = -(-(a + N_EXPERTS * (MOE_BLOCK - 1)) // MOE_BLOCK) * MOE_BLOCK
    nblk = n_slots // MOE_BLOCK
    slot_token = jnp.full((n_slots,), t, jnp.int32).at[dest].set(token_flat[order])
    slot_w = jnp.zeros((n_slots,), h.dtype).at[dest].set(w_flat[order])
    block_expert = jnp.clip(jnp.searchsorted(pends, jnp.arange(nblk) * MOE_BLOCK, side='right'), 0, N_EXPERTS - 1)
    x_pad = jnp.concatenate([hf, jnp.zeros((1, d), h.dtype)], axis=0)
    w_gu, b_gu, w_dn, b_dn = p['w_gate_up'], p['b_gate_up'], p['w_down'], p['b_down']

    def run_block(args):
        tok, e, w = args
        xb = x_pad[tok]
        act = clamped_swiglu(xb @ w_gu[e] + b_gu[e])
        return (act @ w_dn[e] + b_dn[e]) * w[:, None]

    out = lax.map(run_block, (slot_token.reshape(nblk, MOE_BLOCK), block_expert, slot_w.reshape(nblk, MOE_BLOCK)))
    y = jnp.zeros((t + 1, d), h.dtype).at[slot_token].add(out.reshape(n_slots, d))[:t]
    return y.reshape(b, n, d)


def hybrid_layer(x, ctx, mod, mod_ctx, p, rope, update_ctx):
    shift1, scale1, gate1, shift2, scale2, gate2 = mod
    cshift1, cscale1, cgate1, cshift2, cscale2, cgate2 = mod_ctx
    h = modulate(x, p['norm1_g'], shift1, scale1)
    hc = modulate(ctx, p['norm1_g'], cshift1, cscale1)
    k_mla_c, v_mla_c, k_swa_c, v_swa_c = kv_side(hc, p, None)
    k_mla, v_mla, k_swa, v_swa = kv_side(h, p, rope)
    q_mla, q_swa, gate_a, gate_b = q_side(h, p, rope)
    o_a = mla_attend_latent(q_mla, k_mla, v_mla, k_mla_c, v_mla_c)
    o_b = swa_attend_latent(q_swa, k_swa, v_swa, k_swa_c, v_swa_c, p['swa_sink'])
    x_new = x + gate1[..., None, :] * merge_branches(o_a, o_b, gate_a, gate_b, p)
    x_new = x_new + gate2[..., None, :] * moe(modulate(x_new, p['norm2_g'], shift2, scale2), p)
    if update_ctx:
        b, n_ctx = ctx.shape[:2]
        qc_mla, qc_swa, gca, gcb = q_side(hc, p, None)
        oc_a = dense_attend(qc_mla, k_mla_c, v_mla_c, MLA_QK ** -0.5).reshape(b, n_ctx, MLA_HEADS * MLA_V)
        oc_b = swa_attend_context(qc_swa, k_swa_c, v_swa_c, p['swa_sink'])
        ctx = ctx + cgate1[..., None, :] * merge_branches(oc_a, oc_b, gca, gcb, p)
        ctx = ctx + cgate2[..., None, :] * moe(modulate(ctx, p['norm2_g'], cshift2, cscale2), p)
    return x_new, ctx


def setup_inputs(seed: int = 0) -> dict:
    key = jax.random.key(seed)
    ks = jax.random.split(key, 32)
    f32 = jnp.float32

    def nrm(k, shape, scale):
        return jax.random.normal(k, shape, f32) * scale

    def gain(k, shape):
        return 1.0 + 0.02 * jax.random.normal(k, shape, f32)

    D, L, E, F = D_MODEL, DEPTH, N_EXPERTS, D_EXPERT
    return {
        "x": nrm(ks[0], (BATCH, SEQ, D), 1.0),
        "c": nrm(ks[1], (BATCH, D), 1.0),
        "ctx": nrm(ks[2], (BATCH, CTX_LEN, D), 1.0),
        "c_ctx": nrm(ks[3], (D,), 1.0),
        "w_ada": nrm(ks[4], (L, D, N_MOD * D), 0.5 * D ** -0.5),
        "b_ada": nrm(ks[5], (L, N_MOD * D), 0.02),
        "norm1_g": gain(ks[6], (L, D)),
        "norm2_g": gain(ks[7], (L, D)),
        "w_in": nrm(ks[8], (L, D, IN_COLS), D ** -0.5),
        "mla_q_a_g": gain(ks[9], (L, MLA_Q_RANK)),
        "mla_kv_a_g": gain(ks[10], (L, MLA_KV_RANK)),
        "w_q_up": nrm(ks[11], (L, MLA_Q_RANK, MLA_HEADS * MLA_QK), MLA_Q_RANK ** -0.5),
        "w_kv_up": nrm(ks[12], (L, MLA_KV_RANK, MLA_HEADS * (MLA_NOPE + MLA_V)), MLA_KV_RANK ** -0.5),
        "mla_q_g": gain(ks[13], (L, MLA_QK)),
        "mla_k_g": gain(ks[14], (L, MLA_QK)),
        "swa_q_g": gain(ks[15], (L, SWA_HD)),
        "swa_k_g": gain(ks[16], (L, SWA_HD)),
        "swa_sink": nrm(ks[17], (L, SWA_HEADS), 0.5),
        "w_branch_a": nrm(ks[18], (L, MLA_HEADS * MLA_V, D), (MLA_HEADS * MLA_V) ** -0.5),
        "w_branch_b": nrm(ks[19], (L, SWA_HEADS * SWA_HD, D), (SWA_HEADS * SWA_HD) ** -0.5),
        "w_out": nrm(ks[20], (L, D, D), D ** -0.5),
        "w_router": nrm(ks[21], (L, D, E), D ** -0.5),
        "b_router": nrm(ks[22], (L, E), 0.01),
        "w_gate_up": nrm(ks[23], (L, E, D, 2 * F), D ** -0.5),
        "b_gate_up": nrm(ks[24], (L, E, 2 * F), 0.02),
        "w_down": nrm(ks[25], (L, E, F, D), F ** -0.5),
        "b_down": nrm(ks[26], (L, E, D), 0.02),
    }


def reference(x, c, ctx, c_ctx, w_ada, b_ada, norm1_g, norm2_g, w_in, mla_q_a_g, mla_kv_a_g,
              w_q_up, w_kv_up, mla_q_g, mla_k_g, swa_q_g, swa_k_g, swa_sink, w_branch_a, w_branch_b,
              w_out, w_router, b_router, w_gate_up, b_gate_up, w_down, b_down):
    n = x.shape[1]
    ROWS = n // GRID_W
    row = jnp.repeat(jnp.arange(ROWS, dtype=jnp.int32), GRID_W)
    col = jnp.tile(jnp.arange(GRID_W, dtype=jnp.int32), ROWS)
    cos_m, sin_m = axial_rope_tables(row, col, MLA_ROPE, x.dtype)
    cos_s, sin_s = axial_rope_tables(row, col, SWA_HD, x.dtype)
    rope = (cos_m, sin_m, cos_s, sin_s)
    for l in range(DEPTH):
        p = dict(norm1_g=norm1_g[l], norm2_g=norm2_g[l], w_in=w_in[l], mla_q_a_g=mla_q_a_g[l],
                 mla_kv_a_g=mla_kv_a_g[l], w_q_up=w_q_up[l], w_kv_up=w_kv_up[l], mla_q_g=mla_q_g[l],
                 mla_k_g=mla_k_g[l], swa_q_g=swa_q_g[l], swa_k_g=swa_k_g[l], swa_sink=swa_sink[l],
                 w_branch_a=w_branch_a[l], w_branch_b=w_branch_b[l], w_out=w_out[l], w_router=w_router[l],
                 b_router=b_router[l], w_gate_up=w_gate_up[l], b_gate_up=b_gate_up[l], w_down=w_down[l],
                 b_down=b_down[l])
        mod = ada_params(c, w_ada[l], b_ada[l])
        mod_ctx = ada_params(c_ctx, w_ada[l], b_ada[l])
        x, ctx = hybrid_layer(x, ctx, mod, mod_ctx, p, rope, update_ctx=(l < DEPTH - 1))
    return x
```

```python
import functools

import jax
import jax.numpy as jnp
from jax import lax
from jax.experimental import pallas as pl
from jax.experimental.pallas import tpu as pltpu

F32 = jnp.float32
BF16 = jnp.bfloat16
I32 = jnp.int32

LANES = 128

D_MODEL = 1024
BATCH = 8
SEQ = 4096
CTX_LEN = 256
GRID_W = 64
ROPE_THETA = 10000.0
NORM_EPS = 1e-6
N_MOD = 6
NEG_INF = -1e30

MLA_HEADS = 8
MLA_NOPE = 64
MLA_ROPE = 32
MLA_QK = MLA_NOPE + MLA_ROPE
MLA_V = 64
MLA_Q_RANK = 384
MLA_KV_RANK = 256

SWA_HEADS = 8
SWA_KV_HEADS = 2
SWA_GROUP = SWA_HEADS // SWA_KV_HEADS
SWA_HD = 64
WINDOW = 128

N_EXPERTS = 32
TOP_K = 4
D_EXPERT = D_MODEL
SWIGLU_LIMIT = 7.0
SWIGLU_ALPHA = 1.702

N_TOK = BATCH * SEQ
MOD_ROWS = 16

ADA_TN = 1536
PROJ_TM = 256
MLA_TQ = 256
SWA_TQ = 256
SWA_KW = SWA_TQ + 2 * WINDOW
MERGE_TM = 256
MOE_BLK = 256
MOVE_TM = 256
N_SLOTS = -(-(N_TOK * TOP_K + N_EXPERTS * (MOE_BLK - 1)) // MOE_BLK) * MOE_BLK
N_BLK = N_SLOTS // MOE_BLK

VMEM_LIMIT = 56 * 1024 * 1024

C_KVLAT = 0
C_KSWA = C_KVLAT + MLA_KV_RANK
C_VSWA = C_KSWA + SWA_KV_HEADS * LANES
C_KROPE = C_VSWA + SWA_KV_HEADS * SWA_HD
C_KVEND = C_KROPE + LANES
C_QLAT = C_KVEND
C_QSWA = C_QLAT + MLA_Q_RANK
C_GA = C_QSWA + SWA_HEADS * LANES
C_GB = C_GA + D_MODEL
C_END = C_GB + D_MODEL


def _mm(a, b):
    return jnp.dot(a.astype(BF16), b.astype(BF16), preferred_element_type=F32)


def _mm_nt(a, b):
    return lax.dot_general(a.astype(BF16), b.astype(BF16), (((1,), (1,)), ((), ())),
                           preferred_element_type=F32)


def _split(a):
    hi = a.astype(BF16)
    lo = (a - hi.astype(F32)).astype(BF16)
    return hi, lo


def _rms(x):
    return x * lax.rsqrt(jnp.mean(x * x, axis=-1, keepdims=True) + NORM_EPS)


def _ada_kernel(c_ref, w_ref, b_ref, o_ref):
    c = c_ref[...]
    s = c * jax.nn.sigmoid(c)
    shi, slo = _split(s)
    whi, wlo = _split(w_ref[...])
    acc = _mm(shi, whi) + _mm(slo, whi) + _mm(shi, wlo)
    o_ref[...] = acc + b_ref[...]


def _ada(cond, w_ada, b_ada):
    n = w_ada.shape[1]
    return pl.pallas_call(
        _ada_kernel,
        grid=(n // ADA_TN,),
        in_specs=[
            pl.BlockSpec((MOD_ROWS, D_MODEL), lambda j: (0, 0)),
            pl.BlockSpec((D_MODEL, ADA_TN), lambda j: (0, j)),
            pl.BlockSpec((1, ADA_TN), lambda j: (0, j)),
        ],
        out_specs=pl.BlockSpec((MOD_ROWS, ADA_TN), lambda j: (0, j)),
        out_shape=jax.ShapeDtypeStruct((MOD_ROWS, n), F32),
        compiler_params=pltpu.CompilerParams(
            dimension_semantics=("arbitrary",), vmem_limit_bytes=VMEM_LIMIT),
        name="ada",
    )(cond, w_ada, b_ada)


def _head_norm(xraw, ind_ref, indt_ref, inv_dim):
    w = xraw.shape[1]
    ss = _mm(xraw * xraw, ind_ref[0:w, :])
    r = lax.rsqrt(ss * inv_dim + NORM_EPS)
    rhi, rlo = _split(r)
    scale = _mm(rhi, indt_ref[:, 0:w]) + _mm(rlo, indt_ref[:, 0:w])
    return xraw * scale


def _rope(xh, cos, sa, sb, quarter):
    return (xh * cos + pltpu.roll(xh, LANES - quarter, 1) * sa
            + pltpu.roll(xh, quarter, 1) * sb)


def _proj_kernel(*refs, with_q):
    (x_ref, sh_ref, sc_ref, g1_ref, win_ref, cm_ref, sam_ref, sbm_ref, cs_ref, sas_ref, sbs_ref,
     gkva_ref, wkn_ref, wv_ref, gk_ref, gks_ref, ind_ref, indt_ref) = refs[:18]
    if with_q:
        gqa_ref, wqup_ref, gq_ref, gqs_ref = refs[18:22]
        kmla_ref, vmla_ref, kswa_ref, vswa_ref, qmla_ref, qswa_ref, siga_ref, sigb_ref = refs[22:]
    else:
        kmla_ref, vmla_ref, kswa_ref, vswa_ref = refs[18:]

    x = x_ref[...]
    h = _rms(x) * g1_ref[...] * (1.0 + sc_ref[...]) + sh_ref[...]
    y = _mm(h, win_ref[...])

    cm, sam, sbm = cm_ref[...], sam_ref[...], sbm_ref[...]
    cs, sas, sbs = cs_ref[...], sas_ref[...], sbs_ref[...]

    kvn = _rms(y[:, C_KVLAT:C_KVLAT + MLA_KV_RANK]) * gkva_ref[...]
    kn = _mm(kvn, wkn_ref[...])
    vmla_ref[...] = _mm(kvn, wv_ref[...]).astype(BF16)
    kr = y[:, C_KROPE:C_KROPE + LANES]
    kraw = kn + jnp.concatenate([kr] * MLA_HEADS, axis=1)
    kfull = _head_norm(kraw, ind_ref, indt_ref, 1.0 / MLA_QK) * gk_ref[...]
    for hd in range(MLA_HEADS):
        sl = slice(hd * LANES, (hd + 1) * LANES)
        kmla_ref[:, sl] = _rope(kfull[:, sl], cm, sam, sbm, MLA_ROPE // 4).astype(BF16)

    ks = _head_norm(y[:, C_KSWA:C_KSWA + SWA_KV_HEADS * LANES], ind_ref, indt_ref,
                    1.0 / SWA_HD) * gks_ref[...]
    for hd in range(SWA_KV_HEADS):
        sl = slice(hd * LANES, (hd + 1) * LANES)
        kswa_ref[:, sl] = _rope(ks[:, sl], cs, sas, sbs, SWA_HD // 4).astype(BF16)
    vswa_ref[...] = y[:, C_VSWA:C_VSWA + SWA_KV_HEADS * SWA_HD].astype(BF16)

    if with_q:
        qn = _rms(y[:, C_QLAT:C_QLAT + MLA_Q_RANK]) * gqa_ref[...]
        qraw = _mm(qn, wqup_ref[...])
        qf = _head_norm(qraw, ind_ref, indt_ref, 1.0 / MLA_QK) * gq_ref[...]
        for hd in range(MLA_HEADS):
            sl = slice(hd * LANES, (hd + 1) * LANES)
            qmla_ref[:, sl] = _rope(qf[:, sl], cm, sam, sbm, MLA_ROPE // 4).astype(BF16)
        qs = _head_norm(y[:, C_QSWA:C_QSWA + SWA_HEADS * LANES], ind_ref, indt_ref,
                        1.0 / SWA_HD) * gqs_ref[...]
        for hd in range(SWA_HEADS):
            sl = slice(hd * LANES, (hd + 1) * LANES)
            qswa_ref[:, sl] = _rope(qs[:, sl], cs, sas, sbs, SWA_HD // 4).astype(BF16)
        siga_ref[...] = jax.nn.sigmoid(y[:, C_GA:C_GA + D_MODEL]).astype(BF16)
        sigb_ref[...] = jax.nn.sigmoid(y[:, C_GB:C_GB + D_MODEL]).astype(BF16)


def _proj(x2d, mod3, mod_row_fn, tab_row_fn, tabs, consts, q_consts, win, with_q, name):
    rows = x2d.shape[0]
    tm = PROJ_TM
    const = lambda shape: pl.BlockSpec(shape, lambda i: (0,) * len(shape))
    tab = pl.BlockSpec((tm, LANES), lambda i: (tab_row_fn(i), 0))
    g1, rest = consts[0], consts[1:]
    in_specs = [
        pl.BlockSpec((tm, D_MODEL), lambda i: (i, 0)),
        pl.BlockSpec((None, 1, D_MODEL), lambda i: (mod_row_fn(i), 0, 0)),
        pl.BlockSpec((None, 1, D_MODEL), lambda i: (mod_row_fn(i), 0, 1)),
        const(g1.shape),
        pl.BlockSpec(win.shape, lambda i: (0, 0), pipeline_mode=pl.Buffered(1)),
        tab, tab, tab, tab, tab, tab,
    ] + [const(c.shape) for c in rest]
    args = [x2d, mod3, mod3, g1, win] + list(tabs) + list(rest)
    out_w = [MLA_HEADS * LANES, MLA_HEADS * MLA_V, SWA_KV_HEADS * LANES, SWA_KV_HEADS * SWA_HD]
    if with_q:
        in_specs += [const(c.shape) for c in q_consts]
        args += list(q_consts)
        out_w += [MLA_HEADS * LANES, SWA_HEADS * LANES, D_MODEL, D_MODEL]
    return pl.pallas_call(
        functools.partial(_proj_kernel, with_q=with_q),
        grid=(rows // tm,),
        in_specs=in_specs,
        out_specs=[pl.BlockSpec((tm, w), lambda i: (i, 0)) for w in out_w],
        out_shape=[jax.ShapeDtypeStruct((rows, w), BF16) for w in out_w],
        compiler_params=pltpu.CompilerParams(
            dimension_semantics=("arbitrary",), vmem_limit_bytes=VMEM_LIMIT),
        name=name,
    )(*args)


def _mla_kernel(q_ref, kl_ref, vl_ref, kc_ref, vc_ref, o_ref):
    outs = []
    for hh in range(2):
        sl = slice(hh * LANES, (hh + 1) * LANES)
        q = q_ref[:, sl]
        s1 = _mm_nt(q, kl_ref[:, sl])
        s2 = _mm_nt(q, kc_ref[:, sl])
        m = jnp.maximum(jnp.max(s1, axis=-1, keepdims=True), jnp.max(s2, axis=-1, keepdims=True))
        p1 = jnp.exp(s1 - m)
        p2 = jnp.exp(s2 - m)
        l = jnp.sum(p1, axis=-1, keepdims=True) + jnp.sum(p2, axis=-1, keepdims=True)
        o = _mm(p1, vl_ref[...]) + _mm(p2, vc_ref[...])
        outs.append(o / l)
    lane = lax.broadcasted_iota(I32, outs[0].shape, 1)
    o_ref[...] = jnp.where(lane < MLA_V, outs[0], outs[1]).astype(BF16)


def _mla_attn(q, k_lat, v_lat, k_ctx, v_ctx):
    tq = MLA_TQ
    return pl.pallas_call(
        _mla_kernel,
        grid=(BATCH, MLA_HEADS // 2, SEQ // tq),
        in_specs=[
            pl.BlockSpec((None, tq, 2 * LANES), lambda b, hp, i: (b, i, hp)),
            pl.BlockSpec((None, SEQ, 2 * LANES), lambda b, hp, i: (b, 0, hp)),
            pl.BlockSpec((None, SEQ, 2 * MLA_V), lambda b, hp, i: (b, 0, hp)),
            pl.BlockSpec((None, CTX_LEN, 2 * LANES), lambda b, hp, i: (b, 0, hp)),
            pl.BlockSpec((None, CTX_LEN, 2 * MLA_V), lambda b, hp, i: (b, 0, hp)),
        ],
        out_specs=pl.BlockSpec((None, tq, 2 * MLA_V), lambda b, hp, i: (b, i, hp)),
        out_shape=jax.ShapeDtypeStruct((BATCH, SEQ, MLA_HEADS * MLA_V), BF16),
        compiler_params=pltpu.CompilerParams(
            dimension_semantics=("arbitrary", "arbitrary", "arbitrary"),
            vmem_limit_bytes=VMEM_LIMIT),
        name="mla_attn",
    )(q, k_lat, v_lat, k_ctx, v_ctx)


def _swa_kernel(sink_ref, q_ref, k_ref, v_ref, kc_ref, vc_ref, o_ref):
    tq = SWA_TQ
    i = pl.program_id(1)
    s0 = i * tq
    kstart = pl.multiple_of(jnp.clip(s0 - WINDOW, 0, SEQ - SWA_KW), LANES)
    kwin = k_ref[pl.ds(kstart, SWA_KW), :]
    vwin = v_ref[pl.ds(kstart, SWA_KW), :]
    kc = kc_ref[...]
    vc = vc_ref[...]
    row = lax.broadcasted_iota(I32, (SWA_GROUP * tq, SWA_KW), 0)
    qpos = s0 + (row & (tq - 1))
    kpos = kstart + lax.broadcasted_iota(I32, (SWA_GROUP * tq, SWA_KW), 1)
    allowed = jnp.abs(qpos - kpos) <= WINDOW
    lane = lax.broadcasted_iota(I32, (tq, LANES), 1)
    pieces = {}
    for kh in range(SWA_KV_HEADS):
        ksl = slice(kh * LANES, (kh + 1) * LANES)
        q4 = jnp.concatenate(
            [q_ref[:, (kh * SWA_GROUP + g) * LANES:(kh * SWA_GROUP + g + 1) * LANES]
             for g in range(SWA_GROUP)], axis=0)
        sb = jnp.where(allowed, _mm_nt(q4, kwin[:, ksl]), NEG_INF)
        sc = _mm_nt(q4, kc[:, ksl])
        sink = jnp.concatenate(
            [jnp.full((tq, 1), sink_ref[kh * SWA_GROUP + g], F32) for g in range(SWA_GROUP)], axis=0)
        m = jnp.maximum(jnp.maximum(jnp.max(sb, axis=-1, keepdims=True),
                                    jnp.max(sc, axis=-1, keepdims=True)), sink)
        pb = jnp.exp(sb - m)
        pc = jnp.exp(sc - m)
        l = (jnp.sum(pb, axis=-1, keepdims=True) + jnp.sum(pc, axis=-1, keepdims=True)
             + jnp.exp(sink - m))
        o = (_mm(pb, vwin) + _mm(pc, vc)) / l
        for g in range(SWA_GROUP):
            og = o[g * tq:(g + 1) * tq]
            if (g % 2) != kh:
                og = pltpu.roll(og, SWA_HD, 1)
            pieces[kh * SWA_GROUP + g] = og
    for c in range(SWA_HEADS // 2):
        o_ref[:, c * LANES:(c + 1) * LANES] = jnp.where(
            lane < SWA_HD, pieces[2 * c], pieces[2 * c + 1]).astype(BF16)


def _swa_attn(sink, q, k_lat, v_lat, k_ctx, v_ctx):
    tq = SWA_TQ
    return pl.pallas_call(
        _swa_kernel,
        grid=(BATCH, SEQ // tq),
        in_specs=[
            pl.BlockSpec(memory_space=pltpu.SMEM),
            pl.BlockSpec((None, tq, SWA_HEADS * LANES), lambda b, i: (b, i, 0)),
            pl.BlockSpec((None, SEQ, SWA_KV_HEADS * LANES), lambda b, i: (b, 0, 0)),
            pl.BlockSpec((None, SEQ, SWA_KV_HEADS * SWA_HD), lambda b, i: (b, 0, 0)),
            pl.BlockSpec((None, CTX_LEN, SWA_KV_HEADS * LANES), lambda b, i: (b, 0, 0)),
            pl.BlockSpec((None, CTX_LEN, SWA_KV_HEADS * SWA_HD), lambda b, i: (b, 0, 0)),
        ],
        out_specs=pl.BlockSpec((None, tq, SWA_HEADS * SWA_HD), lambda b, i: (b, i, 0)),
        out_shape=jax.ShapeDtypeStruct((BATCH, SEQ, SWA_HEADS * SWA_HD), BF16),
        compiler_params=pltpu.CompilerParams(
            dimension_semantics=("arbitrary", "arbitrary"), vmem_limit_bytes=VMEM_LIMIT),
        name="swa_attn",
    )(sink, q, k_lat, v_lat, k_ctx, v_ctx)


def _merge_kernel(oa_ref, ob_ref, sa_ref, sb_ref, x_ref, g1_ref, sh2_ref, sc2_ref, wba_ref, wbb_ref,
                  wout_ref, g2_ref, wr_ref, br_ref, xnew_ref, h2_ref, idx_ref, wts_ref):
    ya = _mm(oa_ref[...], wba_ref[...])
    yb = _mm(ob_ref[...], wbb_ref[...])
    y = sa_ref[...].astype(F32) * ya + sb_ref[...].astype(F32) * yb
    z = _mm(y, wout_ref[...])
    xn = x_ref[...] + g1_ref[...] * z
    xnew_ref[...] = xn
    h2 = _rms(xn) * g2_ref[...] * (1.0 + sc2_ref[...]) + sh2_ref[...]
    h2_ref[...] = h2

    hhi, hlo = _split(h2)
    whi, wlo = _split(wr_ref[...])
    lg = _mm_nt(whi, hhi) + _mm_nt(whi, hlo) + _mm_nt(wlo, hhi) + br_ref[...]
    eiota = lax.broadcasted_iota(I32, lg.shape, 0).astype(F32)
    vals = []
    for k in range(TOP_K):
        m = jnp.max(lg, axis=0, keepdims=True)
        ik = jnp.min(jnp.where(lg == m, eiota, float(N_EXPERTS)), axis=0, keepdims=True)
        idx_ref[k:k + 1, :] = ik.astype(I32)
        vals.append(m)
        lg = jnp.where(eiota == ik, -jnp.inf, lg)
    es = [jnp.exp(v - vals[0]) for v in vals]
    tot = es[0] + es[1] + es[2] + es[3]
    for k in range(TOP_K):
        wts_ref[k:k + 1, :] = es[k] / tot


def _merge(o_a, o_b, sig_a, sig_b, x2d, mod3, w_ba, w_bb, w_out, g2, w_rt, b_r):
    tm = MERGE_TM
    tiles_per_batch = SEQ // tm
    const = lambda shape: pl.BlockSpec(shape, lambda i: (0,) * len(shape))
    modspec = lambda j: pl.BlockSpec((None, 1, D_MODEL), lambda i: (i // tiles_per_batch, 0, j))
    row = lambda w: pl.BlockSpec((tm, w), lambda i: (i, 0))
    return pl.pallas_call(
        _merge_kernel,
        grid=(N_TOK // tm,),
        in_specs=[row(512), row(512), row(D_MODEL), row(D_MODEL), row(D_MODEL),
                  modspec(2), modspec(3), modspec(4),
                  const(w_ba.shape), const(w_bb.shape), const(w_out.shape), const(g2.shape),
                  const(w_rt.shape), const(b_r.shape)],
        out_specs=[row(D_MODEL), row(D_MODEL),
                   pl.BlockSpec((TOP_K, tm), lambda i: (0, i)),
                   pl.BlockSpec((TOP_K, tm), lambda i: (0, i))],
        out_shape=[jax.ShapeDtypeStruct((N_TOK, D_MODEL), F32),
                   jax.ShapeDtypeStruct((N_TOK, D_MODEL), F32),
                   jax.ShapeDtypeStruct((TOP_K, N_TOK), I32),
                   jax.ShapeDtypeStruct((TOP_K, N_TOK), F32)],
        compiler_params=pltpu.CompilerParams(
            dimension_semantics=("arbitrary",), vmem_limit_bytes=VMEM_LIMIT),
        name="merge",
    )(o_a, o_b, sig_a, sig_b, x2d, mod3, mod3, mod3, w_ba, w_bb, w_out, g2, w_rt, b_r)


def _row_copy(src, si, dst, di, sem):
    return pltpu.make_async_copy(src.at[pl.ds(si, 1)], dst.at[pl.ds(di, 1)], sem)


def _dispatch_kernel(dest_ref, h2_ref, xs_in_ref, xs_ref, idx_smem, sems):
    del xs_in_ref
    j = pl.program_id(0)
    n_idx = TOP_K * MOVE_TM
    cp = pltpu.make_async_copy(dest_ref.at[pl.ds(j * n_idx, n_idx)], idx_smem, sems.at[0])
    cp.start()
    cp.wait()

    def issue(t, carry):
        for k in range(TOP_K):
            _row_copy(h2_ref, t, xs_ref, idx_smem[k * MOVE_TM + t], sems.at[1]).start()
        return carry

    lax.fori_loop(0, MOVE_TM, issue, 0)

    def drain(t, carry):
        for k in range(TOP_K):
            _row_copy(h2_ref, t, xs_ref, idx_smem[k * MOVE_TM + t], sems.at[1]).wait()
        return carry

    lax.fori_loop(0, MOVE_TM, drain, 0)


def _dispatch(dest_tiles, h2, xs_init):
    return pl.pallas_call(
        _dispatch_kernel,
        grid=(N_TOK // MOVE_TM,),
        in_specs=[
            pl.BlockSpec(memory_space=pl.ANY),
            pl.BlockSpec((MOVE_TM, D_MODEL), lambda j: (j, 0)),
            pl.BlockSpec(memory_space=pl.ANY),
        ],
        out_specs=pl.BlockSpec(memory_space=pl.ANY),
        out_shape=jax.ShapeDtypeStruct((N_SLOTS, D_MODEL), F32),
        scratch_shapes=[pltpu.SMEM((TOP_K * MOVE_TM,), I32), pltpu.SemaphoreType.DMA((2,))],
        input_output_aliases={2: 0},
        compiler_params=pltpu.CompilerParams(
            dimension_semantics=("arbitrary",), vmem_limit_bytes=VMEM_LIMIT),
        name="dispatch",
    )(dest_tiles, h2, xs_init)


def _expert_kernel(be_ref, nused_ref, xs_ref, wg_ref, wl_ref, bg_ref, bl_ref, wd_ref, bd_ref, o_ref):
    del be_ref
    i = pl.program_id(0)

    @pl.when(i < nused_ref[0])
    def _():
        xb = xs_ref[...].astype(BF16)
        g = jnp.minimum(_mm(xb, wg_ref[...]) + bg_ref[...], SWIGLU_LIMIT)
        l = jnp.clip(_mm(xb, wl_ref[...]) + bl_ref[...], -SWIGLU_LIMIT, SWIGLU_LIMIT)
        act = g * jax.nn.sigmoid(SWIGLU_ALPHA * g) * (l + 1.0)
        o_ref[...] = _mm(act, wd_ref[...]) + bd_ref[...]

    @pl.when(i >= nused_ref[0])
    def _():
        o_ref[...] = jnp.zeros_like(o_ref)


def _experts(block_expert, nused, xs, wg, wl, bg, bl, wd, bd):
    wspec = pl.BlockSpec((None, D_MODEL, D_EXPERT), lambda i, be, nu: (be[i], 0, 0))
    bspec = pl.BlockSpec((None, 1, D_EXPERT), lambda i, be, nu: (be[i], 0, 0))
    grid_spec = pltpu.PrefetchScalarGridSpec(
        num_scalar_prefetch=2,
        grid=(N_BLK,),
        in_specs=[pl.BlockSpec((MOE_BLK, D_MODEL), lambda i, be, nu: (i, 0)),
                  wspec, wspec, bspec, bspec, wspec, bspec],
        out_specs=pl.BlockSpec((MOE_BLK, D_MODEL), lambda i, be, nu: (i, 0)),
    )
    return pl.pallas_call(
        _expert_kernel,
        grid_spec=grid_spec,
        out_shape=jax.ShapeDtypeStruct((N_SLOTS, D_MODEL), F32),
        compiler_params=pltpu.CompilerParams(
            dimension_semantics=("arbitrary",), vmem_limit_bytes=VMEM_LIMIT),
        name="experts",
    )(block_expert, nused, xs, wg, wl, bg, bl, wd, bd)


def _combine_kernel(dest_ref, eo_ref, xnew_ref, g2_ref, wt_ref, o_ref, gbuf, idx_smem, sems):
    j = pl.program_id(0)
    n_idx = TOP_K * MOVE_TM
    cp = pltpu.make_async_copy(dest_ref.at[pl.ds(j * n_idx, n_idx)], idx_smem, sems.at[0])
    cp.start()
    cp.wait()

    def issue(t, carry):
        for k in range(TOP_K):
            _row_copy(eo_ref, idx_smem[k * MOVE_TM + t], gbuf.at[k], t, sems.at[1]).start()
        return carry

    lax.fori_loop(0, MOVE_TM, issue, 0)

    def drain(t, carry):
        for k in range(TOP_K):
            _row_copy(eo_ref, idx_smem[k * MOVE_TM + t], gbuf.at[k], t, sems.at[1]).wait()
        return carry

    lax.fori_loop(0, MOVE_TM, drain, 0)

    wt = wt_ref[...]
    acc = gbuf[0] * wt[:, 0:1]
    for k in range(1, TOP_K):
        acc = acc + gbuf[k] * wt[:, k:k + 1]
    o_ref[...] = xnew_ref[...] + g2_ref[...] * acc


def _combine(dest_tiles, eo, x_new, mod3, wts_t):
    tiles_per_batch = SEQ // MOVE_TM
    return pl.pallas_call(
        _combine_kernel,
        grid=(N_TOK // MOVE_TM,),
        in_specs=[
            pl.BlockSpec(memory_space=pl.ANY),
            pl.BlockSpec(memory_space=pl.ANY),
            pl.BlockSpec((MOVE_TM, D_MODEL), lambda j: (j, 0)),
            pl.BlockSpec((None, 1, D_MODEL), lambda j: (j // tiles_per_batch, 0, 5)),
            pl.BlockSpec((MOVE_TM, TOP_K), lambda j: (j, 0)),
        ],
        out_specs=pl.BlockSpec((MOVE_TM, D_MODEL), lambda j: (j, 0)),
        out_shape=jax.ShapeDtypeStruct((N_TOK, D_MODEL), F32),
        scratch_shapes=[pltpu.VMEM((TOP_K, MOVE_TM, D_MODEL), F32),
                        pltpu.SMEM((TOP_K * MOVE_TM,), I32),
                        pltpu.SemaphoreType.DMA((2,))],
        compiler_params=pltpu.CompilerParams(
            dimension_semantics=("arbitrary",), vmem_limit_bytes=VMEM_LIMIT),
        name="combine",
    )(dest_tiles, eo, x_new, mod3, wts_t)


def _pad_heads(w, heads, dim):
    lead = w.shape[:-1]
    w = w.reshape(lead + (heads, dim))
    w = jnp.pad(w, [(0, 0)] * len(lead) + [(0, 0), (0, LANES - dim)])
    return w.reshape(lead + (heads * LANES,))


def _rope_tables(dim, offset):
    pos = jnp.arange(SEQ, dtype=jnp.int32)
    row = (pos // GRID_W).astype(F32)
    col = (pos % GRID_W).astype(F32)
    q = dim // 4
    freqs = ROPE_THETA ** (-jnp.arange(q, dtype=F32) / q)
    ang_r = row[:, None] * freqs
    ang_c = col[:, None] * freqs
    ang = jnp.concatenate([ang_r, ang_r, ang_c, ang_c], axis=-1)
    cos, sin = jnp.cos(ang), jnp.sin(ang)
    first = (jnp.arange(dim) % (2 * q)) < q
    sa = jnp.where(first, -sin, 0.0)
    sb = jnp.where(first, 0.0, sin)
    pad = lambda t, v: jnp.pad(t, ((0, 0), (offset, LANES - offset - dim)), constant_values=v)
    return pad(cos, 1.0), pad(sa, 0.0), pad(sb, 0.0)


def kernel(x, c, ctx, c_ctx, w_ada, b_ada, norm1_g, norm2_g, w_in, mla_q_a_g, mla_kv_a_g, w_q_up, w_kv_up,
           mla_q_g, mla_k_g, swa_q_g, swa_k_g, swa_sink, w_branch_a, w_branch_b, w_out, w_router, b_router,
           w_gate_up, b_gate_up, w_down, b_down):
    assert x.shape == (BATCH, SEQ, D_MODEL) and ctx.shape == (BATCH, CTX_LEN, D_MODEL)
    assert w_ada.shape[0] == 1, "single layer"

    cond = jnp.concatenate([c, c_ctx[None], jnp.zeros((MOD_ROWS - BATCH - 1, D_MODEL), F32)], axis=0)
    mod = _ada(cond, w_ada[0], b_ada[0][None])
    mod3 = mod.reshape(MOD_ROWS, 1, N_MOD * D_MODEL)

    wi = w_in[0]
    o1, o2, o3 = MLA_KV_RANK, MLA_KV_RANK + MLA_ROPE, MLA_KV_RANK + MLA_ROPE + SWA_KV_HEADS * SWA_HD
    kv_cols = o3 + SWA_KV_HEADS * SWA_HD
    q1 = kv_cols + MLA_Q_RANK
    q2 = q1 + SWA_HEADS * SWA_HD
    q3 = q2 + D_MODEL
    krope_grp = jnp.pad(wi[:, o1:o2], ((0, 0), (MLA_NOPE, LANES - MLA_QK)))
    win = jnp.concatenate([
        wi[:, 0:o1], _pad_heads(wi[:, o2:o3], SWA_KV_HEADS, SWA_HD), wi[:, o3:kv_cols], krope_grp,
        wi[:, kv_cols:q1], _pad_heads(wi[:, q1:q2], SWA_HEADS, SWA_HD), wi[:, q2:q3], wi[:, q3:]],
        axis=1).astype(BF16)
    assert win.shape == (D_MODEL, C_END)
    wkv = w_kv_up[0].reshape(MLA_KV_RANK, MLA_HEADS, MLA_NOPE + MLA_V)
    wkn = _pad_heads(wkv[:, :, :MLA_NOPE].reshape(MLA_KV_RANK, -1), MLA_HEADS, MLA_NOPE).astype(BF16)
    wv = wkv[:, :, MLA_NOPE:].reshape(MLA_KV_RANK, MLA_HEADS * MLA_V).astype(BF16)
    wqup = _pad_heads(w_q_up[0], MLA_HEADS, MLA_QK).astype(BF16)
    gk_t = _pad_heads(jnp.tile(mla_k_g[0], MLA_HEADS)[None], MLA_HEADS, MLA_QK)
    gq_t = _pad_heads(jnp.tile(mla_q_g[0] * MLA_QK ** -0.5, MLA_HEADS)[None], MLA_HEADS, MLA_QK)
    gks_t = _pad_heads(jnp.tile(swa_k_g[0], SWA_KV_HEADS)[None], SWA_KV_HEADS, SWA_HD)
    gqs_t = _pad_heads(jnp.tile(swa_q_g[0] * SWA_HD ** -0.5, SWA_HEADS)[None], SWA_HEADS, SWA_HD)
    head_of_lane = jnp.arange(MLA_HEADS * LANES) // LANES
    ind = (head_of_lane[:, None] == jnp.arange(LANES)[None, :]).astype(BF16)
    indt = ind.T
    consts = [norm1_g[0][None], mla_kv_a_g[0][None], wkn, wv, gk_t, gks_t, ind, indt]
    q_consts = [mla_q_a_g[0][None], wqup, gq_t, gqs_t]

    tabs_lat = _rope_tables(MLA_ROPE, MLA_NOPE) + _rope_tables(SWA_HD, 0)
    ones = jnp.ones((PROJ_TM, LANES), F32)
    zeros = jnp.zeros((PROJ_TM, LANES), F32)
    tabs_ctx = (ones, zeros, zeros, ones, zeros, zeros)

    x2d = x.reshape(N_TOK, D_MODEL)
    tiles_per_batch = SEQ // PROJ_TM
    k_mla, v_mla, k_swa, v_swa, q_mla, q_swa, sig_a, sig_b = _proj(
        x2d, mod3, lambda i: i // tiles_per_batch, lambda i: i % tiles_per_batch,
        tabs_lat, consts, q_consts, win, True, "proj_latent")
    kc_mla, vc_mla, kc_swa, vc_swa = _proj(
        ctx.reshape(BATCH * CTX_LEN, D_MODEL), mod3, lambda i: BATCH, lambda i: 0,
        tabs_ctx, consts, q_consts, win[:, :C_KVEND], False, "proj_ctx")

    b3 = lambda a, n: a.reshape(BATCH, n, a.shape[-1])
    o_a = _mla_attn(b3(q_mla, SEQ), b3(k_mla, SEQ), b3(v_mla, SEQ), b3(kc_mla, CTX_LEN), b3(vc_mla, CTX_LEN))
    o_b = _swa_attn(swa_sink[0], b3(q_swa, SEQ), b3(k_swa, SEQ), b3(v_swa, SEQ),
                    b3(kc_swa, CTX_LEN), b3(vc_swa, CTX_LEN))

    x_new, h2, top_idx, top_w = _merge(
        o_a.reshape(N_TOK, -1), o_b.reshape(N_TOK, -1), sig_a, sig_b, x2d, mod3,
        w_branch_a[0].astype(BF16), w_branch_b[0].astype(BF16), w_out[0].astype(BF16),
        norm2_g[0][None], w_router[0].T, b_router[0][:, None])

    e_flat = top_idx.reshape(-1)
    onehot = (e_flat[:, None] == jnp.arange(N_EXPERTS, dtype=I32)[None, :]).astype(I32)
    csum = jnp.cumsum(onehot, axis=0)
    rank = jnp.sum(csum * onehot, axis=1) - 1
    counts = csum[-1]
    padded = (counts + MOE_BLK - 1) // MOE_BLK * MOE_BLK
    pends = jnp.cumsum(padded)
    pstarts = pends - padded
    dest = jnp.sum(onehot * pstarts[None, :], axis=1) + rank
    nused = (pends[-1] // MOE_BLK).astype(I32)[None]
    block_expert = jnp.clip(
        jnp.searchsorted(pends, jnp.arange(N_BLK, dtype=I32) * MOE_BLK, side='right'),
        0, N_EXPERTS - 1).astype(I32)
    n_tiles = N_TOK // MOVE_TM
    dest_tiles = dest.astype(I32).reshape(TOP_K, n_tiles, MOVE_TM).transpose(1, 0, 2).reshape(-1)

    xs = _dispatch(dest_tiles, h2, jnp.zeros((N_SLOTS, D_MODEL), F32))
    wgu = w_gate_up[0]
    bgu = b_gate_up[0]
    eo = _experts(block_expert, nused, xs,
                  wgu[:, :, 0::2].astype(BF16), wgu[:, :, 1::2].astype(BF16),
                  bgu[:, None, 0::2], bgu[:, None, 1::2],
                  w_down[0].astype(BF16), b_down[0][:, None, :])
    out = _combine(dest_tiles, eo, x_new, mod3, top_w.T)
    return out.reshape(BATCH, SEQ, D_MODEL)
```

```python
import functools

import jax
import jax.numpy as jnp
from jax import lax
from jax.experimental import pallas as pl
from jax.experimental.pallas import tpu as pltpu

F32 = jnp.float32
BF16 = jnp.bfloat16
I32 = jnp.int32

LANES = 128

D_MODEL = 1024
BATCH = 8
SEQ = 4096
CTX_LEN = 256
GRID_W = 64
ROPE_THETA = 10000.0
NORM_EPS = 1e-6
N_MOD = 6
NEG_INF = -1e30

MLA_HEADS = 8
MLA_NOPE = 64
MLA_ROPE = 32
MLA_QK = MLA_NOPE + MLA_ROPE
MLA_V = 64
MLA_Q_RANK = 384
MLA_KV_RANK = 256

SWA_HEADS = 8
SWA_KV_HEADS = 2
SWA_GROUP = SWA_HEADS // SWA_KV_HEADS
SWA_HD = 64
WINDOW = 128

N_EXPERTS = 32
TOP_K = 4
D_EXPERT = D_MODEL
SWIGLU_LIMIT = 7.0
SWIGLU_ALPHA = 1.702

N_TOK = BATCH * SEQ
MOD_ROWS = 16

ADA_TN = 1536
PROJ_TM = 256
MLA_TQ = 256
SWA_TQ = 256
SWA_KW = SWA_TQ + 2 * WINDOW
MERGE_TM = 256
MOE_BLK = 256
MOVE_TM = 256
N_SLOTS = -(-(N_TOK * TOP_K + N_EXPERTS * (MOE_BLK - 1)) // MOE_BLK) * MOE_BLK
N_BLK = N_SLOTS // MOE_BLK

VMEM_LIMIT = 56 * 1024 * 1024

C_KVLAT = 0
C_KSWA = C_KVLAT + MLA_KV_RANK
C_VSWA = C_KSWA + SWA_KV_HEADS * LANES
C_KROPE = C_VSWA + SWA_KV_HEADS * SWA_HD
C_KVEND = C_KROPE + LANES
C_QLAT = C_KVEND
C_QSWA = C_QLAT + MLA_Q_RANK
C_GA = C_QSWA + SWA_HEADS * LANES
C_GB = C_GA + D_MODEL
C_END = C_GB + D_MODEL


def _mm(a, b):
    return jnp.dot(a.astype(BF16), b.astype(BF16), preferred_element_type=F32)


def _mm_nt(a, b):
    return lax.dot_general(a.astype(BF16), b.astype(BF16), (((1,), (1,)), ((), ())),
                           preferred_element_type=F32)


def _split(a):
    hi = a.astype(BF16)
    lo = (a - hi.astype(F32)).astype(BF16)
    return hi, lo


def _rms(x):
    return x * lax.rsqrt(jnp.mean(x * x, axis=-1, keepdims=True) + NORM_EPS)


def _ada_kernel(c_ref, w_ref, b_ref, o_ref):
    c = c_ref[...]
    s = c * jax.nn.sigmoid(c)
    shi, slo = _split(s)
    whi, wlo = _split(w_ref[...])
    acc = _mm(shi, whi) + _mm(slo, whi) + _mm(shi, wlo)
    o_ref[...] = acc + b_ref[...]


def _ada(cond, w_ada, b_ada):
    n = w_ada.shape[1]
    return pl.pallas_call(
        _ada_kernel,
        grid=(n // ADA_TN,),
        in_specs=[
            pl.BlockSpec((MOD_ROWS, D_MODEL), lambda j: (0, 0)),
            pl.BlockSpec((D_MODEL, ADA_TN), lambda j: (0, j)),
            pl.BlockSpec((1, ADA_TN), lambda j: (0, j)),
        ],
        out_specs=pl.BlockSpec((MOD_ROWS, ADA_TN), lambda j: (0, j)),
        out_shape=jax.ShapeDtypeStruct((MOD_ROWS, n), F32),
        compiler_params=pltpu.CompilerParams(
            dimension_semantics=("arbitrary",), vmem_limit_bytes=VMEM_LIMIT),
        name="ada",
    )(cond, w_ada, b_ada)


def _head_norm(xraw, ind_ref, indt_ref, inv_dim):
    w = xraw.shape[1]
    ss = _mm(xraw * xraw, ind_ref[0:w, :])
    r = lax.rsqrt(ss * inv_dim + NORM_EPS)
    rhi, rlo = _split(r)
    scale = _mm(rhi, indt_ref[:, 0:w]) + _mm(rlo, indt_ref[:, 0:w])
    return xraw * scale


def _rope(xh, cos, sa, sb, quarter):
    return (xh * cos + pltpu.roll(xh, LANES - quarter, 1) * sa
            + pltpu.roll(xh, quarter, 1) * sb)


def _proj_kernel(*refs, with_q):
    (x_ref, sh_ref, sc_ref, g1_ref, win_ref, cm_ref, sam_ref, sbm_ref, cs_ref, sas_ref, sbs_ref,
     gkva_ref, wkn_ref, wv_ref, gk_ref, gks_ref, ind_ref, indt_ref) = refs[:18]
    if with_q:
        gqa_ref, wqup_ref, gq_ref, gqs_ref = refs[18:22]
        kmla_ref, vmla_ref, kswa_ref, vswa_ref, qmla_ref, qswa_ref, siga_ref, sigb_ref = refs[22:]
    else:
        kmla_ref, vmla_ref, kswa_ref, vswa_ref = refs[18:]

    x = x_ref[...]
    h = _rms(x) * g1_ref[...] * (1.0 + sc_ref[...]) + sh_ref[...]
    y = _mm(h, win_ref[...])

    cm, sam, sbm = cm_ref[...], sam_ref[...], sbm_ref[...]
    cs, sas, sbs = cs_ref[...], sas_ref[...], sbs_ref[...]

    kvn = _rms(y[:, C_KVLAT:C_KVLAT + MLA_KV_RANK]) * gkva_ref[...]
    kn = _mm(kvn, wkn_ref[...])
    vmla_ref[...] = _mm(kvn, wv_ref[...]).astype(BF16)
    kr = y[:, C_KROPE:C_KROPE + LANES]
    kraw = kn + jnp.concatenate([kr] * MLA_HEADS, axis=1)
    kfull = _head_norm(kraw, ind_ref, indt_ref, 1.0 / MLA_QK) * gk_ref[...]
    for hd in range(MLA_HEADS):
        sl = slice(hd * LANES, (hd + 1) * LANES)
        kmla_ref[:, sl] = _rope(kfull[:, sl], cm, sam, sbm, MLA_ROPE // 4).astype(BF16)

    ks = _head_norm(y[:, C_KSWA:C_KSWA + SWA_KV_HEADS * LANES], ind_ref, indt_ref,
                    1.0 / SWA_HD) * gks_ref[...]
    for hd in range(SWA_KV_HEADS):
        sl = slice(hd * LANES, (hd + 1) * LANES)
        kswa_ref[:, sl] = _rope(ks[:, sl], cs, sas, sbs, SWA_HD // 4).astype(BF16)
    vswa_ref[...] = y[:, C_VSWA:C_VSWA + SWA_KV_HEADS * SWA_HD].astype(BF16)

    if with_q:
        qn = _rms(y[:, C_QLAT:C_QLAT + MLA_Q_RANK]) * gqa_ref[...]
        qraw = _mm(qn, wqup_ref[...])
        qf = _head_norm(qraw, ind_ref, indt_ref, 1.0 / MLA_QK) * gq_ref[...]
        for hd in range(MLA_HEADS):
            sl = slice(hd * LANES, (hd + 1) * LANES)
            qmla_ref[:, sl] = _rope(qf[:, sl], cm, sam, sbm, MLA_ROPE // 4).astype(BF16)
        qs = _head_norm(y[:, C_QSWA:C_QSWA + SWA_HEADS * LANES], ind_ref, indt_ref,
                        1.0 / SWA_HD) * gqs_ref[...]
        for hd in range(SWA_HEADS):
            sl = slice(hd * LANES, (hd + 1) * LANES)
            qswa_ref[:, sl] = _rope(qs[:, sl], cs, sas, sbs, SWA_HD // 4).astype(BF16)
        siga_ref[...] = jax.nn.sigmoid(y[:, C_GA:C_GA + D_MODEL]).astype(BF16)
        sigb_ref[...] = jax.nn.sigmoid(y[:, C_GB:C_GB + D_MODEL]).astype(BF16)


def _proj(x2d, mod3, mod_row_fn, tab_row_fn, tabs, consts, q_consts, win, with_q, name):
    rows = x2d.shape[0]
    tm = PROJ_TM
    const = lambda shape: pl.BlockSpec(shape, lambda i: (0,) * len(shape))
    tab = pl.BlockSpec((tm, LANES), lambda i: (tab_row_fn(i), 0))
    g1, rest = consts[0], consts[1:]
    in_specs = [
        pl.BlockSpec((tm, D_MODEL), lambda i: (i, 0)),
        pl.BlockSpec((None, 1, D_MODEL), lambda i: (mod_row_fn(i), 0, 0)),
        pl.BlockSpec((None, 1, D_MODEL), lambda i: (mod_row_fn(i), 0, 1)),
        const(g1.shape),
        pl.BlockSpec(win.shape, lambda i: (0, 0), pipeline_mode=pl.Buffered(1)),
        tab, tab, tab, tab, tab, tab,
    ] + [const(c.shape) for c in rest]
    args = [x2d, mod3, mod3, g1, win] + list(tabs) + list(rest)
    out_w = [MLA_HEADS * LANES, MLA_HEADS * MLA_V, SWA_KV_HEADS * LANES, SWA_KV_HEADS * SWA_HD]
    if with_q:
        in_specs += [const(c.shape) for c in q_consts]
        args += list(q_consts)
        out_w += [MLA_HEADS * LANES, SWA_HEADS * LANES, D_MODEL, D_MODEL]
    return pl.pallas_call(
        functools.partial(_proj_kernel, with_q=with_q),
        grid=(rows // tm,),
        in_specs=in_specs,
        out_specs=[pl.BlockSpec((tm, w), lambda i: (i, 0)) for w in out_w],
        out_shape=[jax.ShapeDtypeStruct((rows, w), BF16) for w in out_w],
        compiler_params=pltpu.CompilerParams(
            dimension_semantics=("arbitrary",), vmem_limit_bytes=VMEM_LIMIT),
        name=name,
    )(*args)


def _mla_kernel(q_ref, kl_ref, vl_ref, kc_ref, vc_ref, o_ref):
    outs = []
    for hh in range(2):
        sl = slice(hh * LANES, (hh + 1) * LANES)
        q = q_ref[:, sl]
        s1 = _mm_nt(q, kl_ref[:, sl])
        s2 = _mm_nt(q, kc_ref[:, sl])
        m = jnp.maximum(jnp.max(s1, axis=-1, keepdims=True), jnp.max(s2, axis=-1, keepdims=True))
        p1 = jnp.exp(s1 - m)
        p2 = jnp.exp(s2 - m)
        l = jnp.sum(p1, axis=-1, keepdims=True) + jnp.sum(p2, axis=-1, keepdims=True)
        o = _mm(p1, vl_ref[...]) + _mm(p2, vc_ref[...])
        outs.append(o / l)
    lane = lax.broadcasted_iota(I32, outs[0].shape, 1)
    o_ref[...] = jnp.where(lane < MLA_V, outs[0], outs[1]).astype(BF16)


def _mla_attn(q, k_lat, v_lat, k_ctx, v_ctx):
    tq = MLA_TQ
    return pl.pallas_call(
        _mla_kernel,
        grid=(BATCH, MLA_HEADS // 2, SEQ // tq),
        in_specs=[
            pl.BlockSpec((None, tq, 2 * LANES), lambda b, hp, i: (b, i, hp)),
            pl.BlockSpec((None, SEQ, 2 * LANES), lambda b, hp, i: (b, 0, hp)),
            pl.BlockSpec((None, SEQ, 2 * MLA_V), lambda b, hp, i: (b, 0, hp)),
            pl.BlockSpec((None, CTX_LEN, 2 * LANES), lambda b, hp, i: (b, 0, hp)),
            pl.BlockSpec((None, CTX_LEN, 2 * MLA_V), lambda b, hp, i: (b, 0, hp)),
        ],
        out_specs=pl.BlockSpec((None, tq, 2 * MLA_V), lambda b, hp, i: (b, i, hp)),
        out_shape=jax.ShapeDtypeStruct((BATCH, SEQ, MLA_HEADS * MLA_V), BF16),
        compiler_params=pltpu.CompilerParams(
            dimension_semantics=("arbitrary", "arbitrary", "arbitrary"),
            vmem_limit_bytes=VMEM_LIMIT),
        name="mla_attn",
    )(q, k_lat, v_lat, k_ctx, v_ctx)


def _swa_kernel(sink_ref, q_ref, k_ref, v_ref, kc_ref, vc_ref, o_ref):
    tq = SWA_TQ
    i = pl.program_id(1)
    s0 = i * tq
    kstart = pl.multiple_of(jnp.clip(s0 - WINDOW, 0, SEQ - SWA_KW), LANES)
    kwin = k_ref[pl.ds(kstart, SWA_KW), :]
    vwin = v_ref[pl.ds(kstart, SWA_KW), :]
    kc = kc_ref[...]
    vc = vc_ref[...]
    row = lax.broadcasted_iota(I32, (SWA_GROUP * tq, SWA_KW), 0)
    qpos = s0 + (row & (tq - 1))
    kpos = kstart + lax.broadcasted_iota(I32, (SWA_GROUP * tq, SWA_KW), 1)
    allowed = jnp.abs(qpos - kpos) <= WINDOW
    lane = lax.broadcasted_iota(I32, (tq, LANES), 1)
    pieces = {}
    for kh in range(SWA_KV_HEADS):
        ksl = slice(kh * LANES, (kh + 1) * LANES)
        q4 = jnp.concatenate(
            [q_ref[:, (kh * SWA_GROUP + g) * LANES:(kh * SWA_GROUP + g + 1) * LANES]
             for g in range(SWA_GROUP)], axis=0)
        sb = jnp.where(allowed, _mm_nt(q4, kwin[:, ksl]), NEG_INF)
        sc = _mm_nt(q4, kc[:, ksl])
        sink = jnp.concatenate(
            [jnp.full((tq, 1), sink_ref[kh * SWA_GROUP + g], F32) for g in range(SWA_GROUP)], axis=0)
        m = jnp.maximum(jnp.maximum(jnp.max(sb, axis=-1, keepdims=True),
                                    jnp.max(sc, axis=-1, keepdims=True)), sink)
        pb = jnp.exp(sb - m)
        pc = jnp.exp(sc - m)
        l = (jnp.sum(pb, axis=-1, keepdims=True) + jnp.sum(pc, axis=-1, keepdims=True)
             + jnp.exp(sink - m))
        o = (_mm(pb, vwin) + _mm(pc, vc)) / l
        for g in range(SWA_GROUP):
            og = o[g * tq:(g + 1) * tq]
            if (g % 2) != kh:
                og = pltpu.roll(og, SWA_HD, 1)
            pieces[kh * SWA_GROUP + g] = og
    for c in range(SWA_HEADS // 2):
        o_ref[:, c * LANES:(c + 1) * LANES] = jnp.where(
            lane < SWA_HD, pieces[2 * c], pieces[2 * c + 1]).astype(BF16)


def _swa_attn(sink, q, k_lat, v_lat, k_ctx, v_ctx):
    tq = SWA_TQ
    return pl.pallas_call(
        _swa_kernel,
        grid=(BATCH, SEQ // tq),
        in_specs=[
            pl.BlockSpec(memory_space=pltpu.SMEM),
            pl.BlockSpec((None, tq, SWA_HEADS * LANES), lambda b, i: (b, i, 0)),
            pl.BlockSpec((None, SEQ, SWA_KV_HEADS * LANES), lambda b, i: (b, 0, 0)),
            pl.BlockSpec((None, SEQ, SWA_KV_HEADS * SWA_HD), lambda b, i: (b, 0, 0)),
            pl.BlockSpec((None, CTX_LEN, SWA_KV_HEADS * LANES), lambda b, i: (b, 0, 0)),
            pl.BlockSpec((None, CTX_LEN, SWA_KV_HEADS * SWA_HD), lambda b, i: (b, 0, 0)),
        ],
        out_specs=pl.BlockSpec((None, tq, SWA_HEADS * SWA_HD), lambda b, i: (b, i, 0)),
        out_shape=jax.ShapeDtypeStruct((BATCH, SEQ, SWA_HEADS * SWA_HD), BF16),
        compiler_params=pltpu.CompilerParams(
            dimension_semantics=("arbitrary", "arbitrary"), vmem_limit_bytes=VMEM_LIMIT),
        name="swa_attn",
    )(sink, q, k_lat, v_lat, k_ctx, v_ctx)


def _merge_kernel(oa_ref, ob_ref, sa_ref, sb_ref, x_ref, g1_ref, sh2_ref, sc2_ref, wba_ref, wbb_ref,
                  wout_ref, g2_ref, wr_ref, br_ref, xnew_ref, h2_ref, idx_ref, wts_ref):
    ya = _mm(oa_ref[...], wba_ref[...])
    yb = _mm(ob_ref[...], wbb_ref[...])
    y = sa_ref[...].astype(F32) * ya + sb_ref[...].astype(F32) * yb
    z = _mm(y, wout_ref[...])
    xn = x_ref[...] + g1_ref[...] * z
    xnew_ref[...] = xn
    h2 = _rms(xn) * g2_ref[...] * (1.0 + sc2_ref[...]) + sh2_ref[...]
    h2_ref[...] = h2

    hhi, hlo = _split(h2)
    whi, wlo = _split(wr_ref[...])
    lg = _mm_nt(whi, hhi) + _mm_nt(whi, hlo) + _mm_nt(wlo, hhi) + br_ref[...]
    eiota = lax.broadcasted_iota(I32, lg.shape, 0).astype(F32)
    vals = []
    for k in range(TOP_K):
        m = jnp.max(lg, axis=0, keepdims=True)
        ik = jnp.min(jnp.where(lg == m, eiota, float(N_EXPERTS)), axis=0, keepdims=True)
        idx_ref[k:k + 1, :] = ik.astype(I32)
        vals.append(m)
        lg = jnp.where(eiota == ik, -jnp.inf, lg)
    es = [jnp.exp(v - vals[0]) for v in vals]
    tot = es[0] + es[1] + es[2] + es[3]
    for k in range(TOP_K):
        wts_ref[k:k + 1, :] = es[k] / tot


def _merge(o_a, o_b, sig_a, sig_b, x2d, mod3, w_ba, w_bb, w_out, g2, w_rt, b_r):
    tm = MERGE_TM
    tiles_per_batch = SEQ // tm
    const = lambda shape: pl.BlockSpec(shape, lambda i: (0,) * len(shape))
    modspec = lambda j: pl.BlockSpec((None, 1, D_MODEL), lambda i: (i // tiles_per_batch, 0, j))
    row = lambda w: pl.BlockSpec((tm, w), lambda i: (i, 0))
    return pl.pallas_call(
        _merge_kernel,
        grid=(N_TOK // tm,),
        in_specs=[row(512), row(512), row(D_MODEL), row(D_MODEL), row(D_MODEL),
                  modspec(2), modspec(3), modspec(4),
                  const(w_ba.shape), const(w_bb.shape), const(w_out.shape), const(g2.shape),
                  const(w_rt.shape), const(b_r.shape)],
        out_specs=[row(D_MODEL), row(D_MODEL),
                   pl.BlockSpec((TOP_K, tm), lambda i: (0, i)),
                   pl.BlockSpec((TOP_K, tm), lambda i: (0, i))],
        out_shape=[jax.ShapeDtypeStruct((N_TOK, D_MODEL), F32),
                   jax.ShapeDtypeStruct((N_TOK, D_MODEL), F32),
                   jax.ShapeDtypeStruct((TOP_K, N_TOK), I32),
                   jax.ShapeDtypeStruct((TOP_K, N_TOK), F32)],
        compiler_params=pltpu.CompilerParams(
            dimension_semantics=("arbitrary",), vmem_limit_bytes=VMEM_LIMIT),
        name="merge",
    )(o_a, o_b, sig_a, sig_b, x2d, mod3, mod3, mod3, w_ba, w_bb, w_out, g2, w_rt, b_r)


def _row_copy(src, si, dst, di, sem):
    return pltpu.make_async_copy(src.at[pl.ds(si, 1)], dst.at[pl.ds(di, 1)], sem)


def _dispatch_kernel(dest_ref, h2_ref, xs_in_ref, xs_ref, idx_smem, sems):
    del xs_in_ref
    j = pl.program_id(0)
    n_idx = TOP_K * MOVE_TM
    cp = pltpu.make_async_copy(dest_ref.at[pl.ds(j * n_idx, n_idx)], idx_smem, sems.at[0])
    cp.start()
    cp.wait()

    def issue(t, carry):
        for k in range(TOP_K):
            _row_copy(h2_ref, t, xs_ref, idx_smem[k * MOVE_TM + t], sems.at[1]).start()
        return carry

    lax.fori_loop(0, MOVE_TM, issue, 0)

    def drain(t, carry):
        for k in range(TOP_K):
            _row_copy(h2_ref, t, xs_ref, idx_smem[k * MOVE_TM + t], sems.at[1]).wait()
        return carry

    lax.fori_loop(0, MOVE_TM, drain, 0)


def _dispatch(dest_tiles, h2, xs_init):
    return pl.pallas_call(
        _dispatch_kernel,
        grid=(N_TOK // MOVE_TM,),
        in_specs=[
            pl.BlockSpec(memory_space=pl.ANY),
            pl.BlockSpec((MOVE_TM, D_MODEL), lambda j: (j, 0)),
            pl.BlockSpec(memory_space=pl.ANY),
        ],
        out_specs=pl.BlockSpec(memory_space=pl.ANY),
        out_shape=jax.ShapeDtypeStruct((N_SLOTS, D_MODEL), F32),
        scratch_shapes=[pltpu.SMEM((TOP_K * MOVE_TM,), I32), pltpu.SemaphoreType.DMA((2,))],
        input_output_aliases={2: 0},
        compiler_params=pltpu.CompilerParams(
            dimension_semantics=("arbitrary",), vmem_limit_bytes=VMEM_LIMIT),
        name="dispatch",
    )(dest_tiles, h2, xs_init)


SPLIT_TN = 1024
SPLIT_SUB = 256


def _split_kernel(w_ref, p_ref, wg_ref, wl_ref):
    half = SPLIT_SUB // 2
    for s in range(SPLIT_TN // SPLIT_SUB):
        r = _mm(w_ref[:, s * SPLIT_SUB:(s + 1) * SPLIT_SUB], p_ref[...])
        wg_ref[:, s * half:(s + 1) * half] = r[:, :half].astype(BF16)
        wl_ref[:, s * half:(s + 1) * half] = r[:, half:].astype(BF16)


def _split_gate_up(w_gu):
    col = jnp.arange(SPLIT_SUB)
    src = jnp.where(col < SPLIT_SUB // 2, 2 * col, 2 * (col - SPLIT_SUB // 2) + 1)
    perm = (jnp.arange(SPLIT_SUB)[:, None] == src[None, :]).astype(BF16)
    out = jax.ShapeDtypeStruct((N_EXPERTS, D_MODEL, D_EXPERT), BF16)
    return pl.pallas_call(
        _split_kernel,
        grid=(N_EXPERTS, 2 * D_EXPERT // SPLIT_TN),
        in_specs=[pl.BlockSpec((None, D_MODEL, SPLIT_TN), lambda e, t: (e, 0, t)),
                  pl.BlockSpec((SPLIT_SUB, SPLIT_SUB), lambda e, t: (0, 0))],
        out_specs=[pl.BlockSpec((None, D_MODEL, SPLIT_TN // 2), lambda e, t: (e, 0, t)),
                   pl.BlockSpec((None, D_MODEL, SPLIT_TN // 2), lambda e, t: (e, 0, t))],
        out_shape=[out, out],
        compiler_params=pltpu.CompilerParams(
            dimension_semantics=("arbitrary", "arbitrary"), vmem_limit_bytes=VMEM_LIMIT),
        name="split_gate_up",
    )(w_gu, perm)


def _expert_kernel(be_ref, nused_ref, xs_ref, wg_ref, wl_ref, bg_ref, bl_ref, wd_ref, bd_ref, o_ref):
    del be_ref
    i = pl.program_id(0)

    @pl.when(i < nused_ref[0])
    def _():
        xb = xs_ref[...].astype(BF16)
        g = jnp.minimum(_mm(xb, wg_ref[...]) + bg_ref[...], SWIGLU_LIMIT)
        l = jnp.clip(_mm(xb, wl_ref[...]) + bl_ref[...], -SWIGLU_LIMIT, SWIGLU_LIMIT)
        act = g * jax.nn.sigmoid(SWIGLU_ALPHA * g) * (l + 1.0)
        o_ref[...] = _mm(act, wd_ref[...]) + bd_ref[...]

    @pl.when(i >= nused_ref[0])
    def _():
        o_ref[...] = jnp.zeros_like(o_ref)


def _experts(block_expert, nused, xs, wg, wl, bg, bl, wd, bd):
    wspec = pl.BlockSpec((None, D_MODEL, D_EXPERT), lambda i, be, nu: (be[i], 0, 0))
    bspec = pl.BlockSpec((None, 1, D_EXPERT), lambda i, be, nu: (be[i], 0, 0))
    grid_spec = pltpu.PrefetchScalarGridSpec(
        num_scalar_prefetch=2,
        grid=(N_BLK,),
        in_specs=[pl.BlockSpec((MOE_BLK, D_MODEL), lambda i, be, nu: (i, 0)),
                  wspec, wspec, bspec, bspec, wspec, bspec],
        out_specs=pl.BlockSpec((MOE_BLK, D_MODEL), lambda i, be, nu: (i, 0)),
    )
    return pl.pallas_call(
        _expert_kernel,
        grid_spec=grid_spec,
        out_shape=jax.ShapeDtypeStruct((N_SLOTS, D_MODEL), F32),
        compiler_params=pltpu.CompilerParams(
            dimension_semantics=("arbitrary",), vmem_limit_bytes=VMEM_LIMIT),
        name="experts",
    )(block_expert, nused, xs, wg, wl, bg, bl, wd, bd)


def _combine_kernel(dest_ref, eo_ref, xnew_ref, g2_ref, wt_ref, o_ref, gbuf, idx_smem, sems):
    j = pl.program_id(0)
    n_idx = TOP_K * MOVE_TM
    cp = pltpu.make_async_copy(dest_ref.at[pl.ds(j * n_idx, n_idx)], idx_smem, sems.at[0])
    cp.start()
    cp.wait()

    def issue(t, carry):
        for k in range(TOP_K):
            _row_copy(eo_ref, idx_smem[k * MOVE_TM + t], gbuf.at[k], t, sems.at[1]).start()
        return carry

    lax.fori_loop(0, MOVE_TM, issue, 0)

    def drain(t, carry):
        for k in range(TOP_K):
            _row_copy(eo_ref, idx_smem[k * MOVE_TM + t], gbuf.at[k], t, sems.at[1]).wait()
        return carry

    lax.fori_loop(0, MOVE_TM, drain, 0)

    wt = wt_ref[...]
    acc = gbuf[0] * wt[:, 0:1]
    for k in range(1, TOP_K):
        acc = acc + gbuf[k] * wt[:, k:k + 1]
    o_ref[...] = xnew_ref[...] + g2_ref[...] * acc


def _combine(dest_tiles, eo, x_new, mod3, wts_t):
    tiles_per_batch = SEQ // MOVE_TM
    return pl.pallas_call(
        _combine_kernel,
        grid=(N_TOK // MOVE_TM,),
        in_specs=[
            pl.BlockSpec(memory_space=pl.ANY),
            pl.BlockSpec(memory_space=pl.ANY),
            pl.BlockSpec((MOVE_TM, D_MODEL), lambda j: (j, 0)),
            pl.BlockSpec((None, 1, D_MODEL), lambda j: (j // tiles_per_batch, 0, 5)),
            pl.BlockSpec((MOVE_TM, TOP_K), lambda j: (j, 0)),
        ],
        out_specs=pl.BlockSpec((MOVE_TM, D_MODEL), lambda j: (j, 0)),
        out_shape=jax.ShapeDtypeStruct((N_TOK, D_MODEL), F32),
        scratch_shapes=[pltpu.VMEM((TOP_K, MOVE_TM, D_MODEL), F32),
                        pltpu.SMEM((TOP_K * MOVE_TM,), I32),
                        pltpu.SemaphoreType.DMA((2,))],
        compiler_params=pltpu.CompilerParams(
            dimension_semantics=("arbitrary",), vmem_limit_bytes=VMEM_LIMIT),
        name="combine",
    )(dest_tiles, eo, x_new, mod3, wts_t)


def _pad_heads(w, heads, dim):
    lead = w.shape[:-1]
    w = w.reshape(lead + (heads, dim))
    w = jnp.pad(w, [(0, 0)] * len(lead) + [(0, 0), (0, LANES - dim)])
    return w.reshape(lead + (heads * LANES,))


def _rope_tables(dim, offset):
    pos = jnp.arange(SEQ, dtype=jnp.int32)
    row = (pos // GRID_W).astype(F32)
    col = (pos % GRID_W).astype(F32)
    q = dim // 4
    freqs = ROPE_THETA ** (-jnp.arange(q, dtype=F32) / q)
    ang_r = row[:, None] * freqs
    ang_c = col[:, None] * freqs
    ang = jnp.concatenate([ang_r, ang_r, ang_c, ang_c], axis=-1)
    cos, sin = jnp.cos(ang), jnp.sin(ang)
    first = (jnp.arange(dim) % (2 * q)) < q
    sa = jnp.where(first, -sin, 0.0)
    sb = jnp.where(first, 0.0, sin)
    pad = lambda t, v: jnp.pad(t, ((0, 0), (offset, LANES - offset - dim)), constant_values=v)
    return pad(cos, 1.0), pad(sa, 0.0), pad(sb, 0.0)


def kernel(x, c, ctx, c_ctx, w_ada, b_ada, norm1_g, norm2_g, w_in, mla_q_a_g, mla_kv_a_g, w_q_up, w_kv_up,
           mla_q_g, mla_k_g, swa_q_g, swa_k_g, swa_sink, w_branch_a, w_branch_b, w_out, w_router, b_router,
           w_gate_up, b_gate_up, w_down, b_down):
    assert x.shape == (BATCH, SEQ, D_MODEL) and ctx.shape == (BATCH, CTX_LEN, D_MODEL)
    assert w_ada.shape[0] == 1, "single layer"

    cond = jnp.concatenate([c, c_ctx[None], jnp.zeros((MOD_ROWS - BATCH - 1, D_MODEL), F32)], axis=0)
    mod = _ada(cond, w_ada[0], b_ada[0][None])
    mod3 = mod.reshape(MOD_ROWS, 1, N_MOD * D_MODEL)

    wi = w_in[0]
    o1, o2, o3 = MLA_KV_RANK, MLA_KV_RANK + MLA_ROPE, MLA_KV_RANK + MLA_ROPE + SWA_KV_HEADS * SWA_HD
    kv_cols = o3 + SWA_KV_HEADS * SWA_HD
    q1 = kv_cols + MLA_Q_RANK
    q2 = q1 + SWA_HEADS * SWA_HD
    q3 = q2 + D_MODEL
    krope_grp = jnp.pad(wi[:, o1:o2], ((0, 0), (MLA_NOPE, LANES - MLA_QK)))
    win = jnp.concatenate([
        wi[:, 0:o1], _pad_heads(wi[:, o2:o3], SWA_KV_HEADS, SWA_HD), wi[:, o3:kv_cols], krope_grp,
        wi[:, kv_cols:q1], _pad_heads(wi[:, q1:q2], SWA_HEADS, SWA_HD), wi[:, q2:q3], wi[:, q3:]],
        axis=1).astype(BF16)
    assert win.shape == (D_MODEL, C_END)
    wkv = w_kv_up[0].reshape(MLA_KV_RANK, MLA_HEADS, MLA_NOPE + MLA_V)
    wkn = _pad_heads(wkv[:, :, :MLA_NOPE].reshape(MLA_KV_RANK, -1), MLA_HEADS, MLA_NOPE).astype(BF16)
    wv = wkv[:, :, MLA_NOPE:].reshape(MLA_KV_RANK, MLA_HEADS * MLA_V).astype(BF16)
    wqup = _pad_heads(w_q_up[0], MLA_HEADS, MLA_QK).astype(BF16)
    gk_t = _pad_heads(jnp.tile(mla_k_g[0], MLA_HEADS)[None], MLA_HEADS, MLA_QK)
    gq_t = _pad_heads(jnp.tile(mla_q_g[0] * MLA_QK ** -0.5, MLA_HEADS)[None], MLA_HEADS, MLA_QK)
    gks_t = _pad_heads(jnp.tile(swa_k_g[0], SWA_KV_HEADS)[None], SWA_KV_HEADS, SWA_HD)
    gqs_t = _pad_heads(jnp.tile(swa_q_g[0] * SWA_HD ** -0.5, SWA_HEADS)[None], SWA_HEADS, SWA_HD)
    head_of_lane = jnp.arange(MLA_HEADS * LANES) // LANES
    ind = (head_of_lane[:, None] == jnp.arange(LANES)[None, :]).astype(BF16)
    indt = ind.T
    consts = [norm1_g[0][None], mla_kv_a_g[0][None], wkn, wv, gk_t, gks_t, ind, indt]
    q_consts = [mla_q_a_g[0][None], wqup, gq_t, gqs_t]

    tabs_lat = _rope_tables(MLA_ROPE, MLA_NOPE) + _rope_tables(SWA_HD, 0)
    ones = jnp.ones((PROJ_TM, LANES), F32)
    zeros = jnp.zeros((PROJ_TM, LANES), F32)
    tabs_ctx = (ones, zeros, zeros, ones, zeros, zeros)

    x2d = x.reshape(N_TOK, D_MODEL)
    tiles_per_batch = SEQ // PROJ_TM
    k_mla, v_mla, k_swa, v_swa, q_mla, q_swa, sig_a, sig_b = _proj(
        x2d, mod3, lambda i: i // tiles_per_batch, lambda i: i % tiles_per_batch,
        tabs_lat, consts, q_consts, win, True, "proj_latent")
    kc_mla, vc_mla, kc_swa, vc_swa = _proj(
        ctx.reshape(BATCH * CTX_LEN, D_MODEL), mod3, lambda i: BATCH, lambda i: 0,
        tabs_ctx, consts, q_consts, win[:, :C_KVEND], False, "proj_ctx")

    b3 = lambda a, n: a.reshape(BATCH, n, a.shape[-1])
    o_a = _mla_attn(b3(q_mla, SEQ), b3(k_mla, SEQ), b3(v_mla, SEQ), b3(kc_mla, CTX_LEN), b3(vc_mla, CTX_LEN))
    o_b = _swa_attn(swa_sink[0], b3(q_swa, SEQ), b3(k_swa, SEQ), b3(v_swa, SEQ),
                    b3(kc_swa, CTX_LEN), b3(vc_swa, CTX_LEN))

    x_new, h2, top_idx, top_w = _merge(
        o_a.reshape(N_TOK, -1), o_b.reshape(N_TOK, -1), sig_a, sig_b, x2d, mod3,
        w_branch_a[0].astype(BF16), w_branch_b[0].astype(BF16), w_out[0].astype(BF16),
        norm2_g[0][None], w_router[0].T, b_router[0][:, None])

    e_flat = top_idx.reshape(-1)
    onehot = (e_flat[:, None] == jnp.arange(N_EXPERTS, dtype=I32)[None, :]).astype(I32)
    csum = jnp.cumsum(onehot, axis=0)
    rank = jnp.sum(csum * onehot, axis=1) - 1
    counts = csum[-1]
    padded = (counts + MOE_BLK - 1) // MOE_BLK * MOE_BLK
    pends = jnp.cumsum(padded)
    pstarts = pends - padded
    dest = jnp.sum(onehot * pstarts[None, :], axis=1) + rank
    nused = (pends[-1] // MOE_BLK).astype(I32)[None]
    block_expert = jnp.clip(
        jnp.searchsorted(pends, jnp.arange(N_BLK, dtype=I32) * MOE_BLK, side='right'),
        0, N_EXPERTS - 1).astype(I32)
    n_tiles = N_TOK // MOVE_TM
    dest_tiles = dest.astype(I32).reshape(TOP_K, n_tiles, MOVE_TM).transpose(1, 0, 2).reshape(-1)

    xs = _dispatch(dest_tiles, h2, jnp.zeros((N_SLOTS, D_MODEL), F32))
    wg, wl = _split_gate_up(w_gate_up[0])
    bgu = b_gate_up[0]
    eo = _experts(block_expert, nused, xs,
                  wg, wl, bgu[:, None, 0::2], bgu[:, None, 1::2],
                  w_down[0].astype(BF16), b_down[0][:, None, :])
    out = _combine(dest_tiles, eo, x_new, mod3, top_w.T)
    return out.reshape(BATCH, SEQ, D_MODEL)
```

```python
import functools

import jax
import jax.numpy as jnp
from jax import lax
from jax.experimental import pallas as pl
from jax.experimental.pallas import tpu as pltpu

F32 = jnp.float32
BF16 = jnp.bfloat16
I32 = jnp.int32

LANES = 128
SUBLANES = 8

D_MODEL = 1024
BATCH = 8
SEQ = 4096
CTX_LEN = 256
GRID_W = 64
ROPE_THETA = 10000.0
NORM_EPS = 1e-6
N_MOD = 6
NEG_INF = -1e30

MLA_HEADS = 8
MLA_NOPE = 64
MLA_ROPE = 32
MLA_QK = MLA_NOPE + MLA_ROPE
MLA_V = 64
MLA_Q_RANK = 384
MLA_KV_RANK = 256

SWA_HEADS = 8
SWA_KV_HEADS = 2
SWA_GROUP = SWA_HEADS // SWA_KV_HEADS
SWA_HD = 64
WINDOW = 128

N_EXPERTS = 32
TOP_K = 4
D_EXPERT = D_MODEL
SWIGLU_LIMIT = 7.0
SWIGLU_ALPHA = 1.702

N_TOK = BATCH * SEQ
MOD_ROWS = 16

ADA_TN = 1536
PROJ_TM = 256
MLA_TQ = 256
SWA_TQ = 256
SWA_KW = SWA_TQ + 2 * WINDOW
MERGE_TM = 256
MOE_BLK = 256
SEG_ALIGN = SUBLANES
N_TILES = N_TOK // MERGE_TM
LROWS = -(-(MERGE_TM * TOP_K + N_EXPERTS * (SEG_ALIGN - 1)) // LANES) * LANES
SEG_BITS = tuple(range(SEG_ALIGN.bit_length() - 1, (MERGE_TM * TOP_K).bit_length()))
SEG_W = 128
N_SLOTS = -(-(N_TOK * TOP_K + N_TILES * N_EXPERTS * (SEG_ALIGN - 1) + N_EXPERTS * (MOE_BLK - 1))
            // MOE_BLK) * MOE_BLK
N_BLK = N_SLOTS // MOE_BLK

VMEM_LIMIT = 56 * 1024 * 1024

C_KVLAT = 0
C_KSWA = C_KVLAT + MLA_KV_RANK
C_VSWA = C_KSWA + SWA_KV_HEADS * LANES
C_KROPE = C_VSWA + SWA_KV_HEADS * SWA_HD
C_KVEND = C_KROPE + LANES
C_QLAT = C_KVEND
C_QSWA = C_QLAT + MLA_Q_RANK
C_GA = C_QSWA + SWA_HEADS * LANES
C_GB = C_GA + D_MODEL
C_END = C_GB + D_MODEL


def _mm(a, b):
    return jnp.dot(a.astype(BF16), b.astype(BF16), preferred_element_type=F32)


def _mm_nt(a, b):
    return lax.dot_general(a.astype(BF16), b.astype(BF16), (((1,), (1,)), ((), ())),
                           preferred_element_type=F32)


def _split(a):
    hi = a.astype(BF16)
    lo = (a - hi.astype(F32)).astype(BF16)
    return hi, lo


def _rms(x):
    return x * lax.rsqrt(jnp.mean(x * x, axis=-1, keepdims=True) + NORM_EPS)


def _ada_kernel(c_ref, w_ref, b_ref, o_ref):
    c = c_ref[...]
    s = c * jax.nn.sigmoid(c)
    shi, slo = _split(s)
    whi, wlo = _split(w_ref[...])
    acc = _mm(shi, whi) + _mm(slo, whi) + _mm(shi, wlo)
    o_ref[...] = acc + b_ref[...]


def _ada(cond, w_ada, b_ada):
    n = w_ada.shape[1]
    return pl.pallas_call(
        _ada_kernel,
        grid=(n // ADA_TN,),
        in_specs=[
            pl.BlockSpec((MOD_ROWS, D_MODEL), lambda j: (0, 0)),
            pl.BlockSpec((D_MODEL, ADA_TN), lambda j: (0, j)),
            pl.BlockSpec((1, ADA_TN), lambda j: (0, j)),
        ],
        out_specs=pl.BlockSpec((MOD_ROWS, ADA_TN), lambda j: (0, j)),
        out_shape=jax.ShapeDtypeStruct((MOD_ROWS, n), F32),
        compiler_params=pltpu.CompilerParams(
            dimension_semantics=("arbitrary",), vmem_limit_bytes=VMEM_LIMIT),
        name="ada",
    )(cond, w_ada, b_ada)


def _head_norm(xraw, ind_ref, indt_ref, inv_dim):
    w = xraw.shape[1]
    ss = _mm(xraw * xraw, ind_ref[0:w, :])
    r = lax.rsqrt(ss * inv_dim + NORM_EPS)
    rhi, rlo = _split(r)
    scale = _mm(rhi, indt_ref[:, 0:w]) + _mm(rlo, indt_ref[:, 0:w])
    return xraw * scale


def _rope(xh, cos, sa, sb, quarter):
    return (xh * cos + pltpu.roll(xh, LANES - quarter, 1) * sa
            + pltpu.roll(xh, quarter, 1) * sb)


def _proj_kernel(*refs, with_q):
    (x_ref, sh_ref, sc_ref, g1_ref, win_ref, cm_ref, sam_ref, sbm_ref, cs_ref, sas_ref, sbs_ref,
     gkva_ref, wkn_ref, wv_ref, gk_ref, gks_ref, ind_ref, indt_ref) = refs[:18]
    if with_q:
        gqa_ref, wqup_ref, gq_ref, gqs_ref = refs[18:22]
        kmla_ref, vmla_ref, kswa_ref, vswa_ref, qmla_ref, qswa_ref, siga_ref, sigb_ref = refs[22:]
    else:
        kmla_ref, vmla_ref, kswa_ref, vswa_ref = refs[18:]

    x = x_ref[...]
    h = _rms(x) * g1_ref[...] * (1.0 + sc_ref[...]) + sh_ref[...]
    y = _mm(h, win_ref[...])

    cm, sam, sbm = cm_ref[...], sam_ref[...], sbm_ref[...]
    cs, sas, sbs = cs_ref[...], sas_ref[...], sbs_ref[...]

    kvn = _rms(y[:, C_KVLAT:C_KVLAT + MLA_KV_RANK]) * gkva_ref[...]
    kn = _mm(kvn, wkn_ref[...])
    vmla_ref[...] = _mm(kvn, wv_ref[...]).astype(BF16)
    kr = y[:, C_KROPE:C_KROPE + LANES]
    kraw = kn + jnp.concatenate([kr] * MLA_HEADS, axis=1)
    kfull = _head_norm(kraw, ind_ref, indt_ref, 1.0 / MLA_QK) * gk_ref[...]
    for hd in range(MLA_HEADS):
        sl = slice(hd * LANES, (hd + 1) * LANES)
        kmla_ref[:, sl] = _rope(kfull[:, sl], cm, sam, sbm, MLA_ROPE // 4).astype(BF16)

    ks = _head_norm(y[:, C_KSWA:C_KSWA + SWA_KV_HEADS * LANES], ind_ref, indt_ref,
                    1.0 / SWA_HD) * gks_ref[...]
    for hd in range(SWA_KV_HEADS):
        sl = slice(hd * LANES, (hd + 1) * LANES)
        kswa_ref[:, sl] = _rope(ks[:, sl], cs, sas, sbs, SWA_HD // 4).astype(BF16)
    vswa_ref[...] = y[:, C_VSWA:C_VSWA + SWA_KV_HEADS * SWA_HD].astype(BF16)

    if with_q:
        qn = _rms(y[:, C_QLAT:C_QLAT + MLA_Q_RANK]) * gqa_ref[...]
        qraw = _mm(qn, wqup_ref[...])
        qf = _head_norm(qraw, ind_ref, indt_ref, 1.0 / MLA_QK) * gq_ref[...]
        for hd in range(MLA_HEADS):
            sl = slice(hd * LANES, (hd + 1) * LANES)
            qmla_ref[:, sl] = _rope(qf[:, sl], cm, sam, sbm, MLA_ROPE // 4).astype(BF16)
        qs = _head_norm(y[:, C_QSWA:C_QSWA + SWA_HEADS * LANES], ind_ref, indt_ref,
                        1.0 / SWA_HD) * gqs_ref[...]
        for hd in range(SWA_HEADS):
            sl = slice(hd * LANES, (hd + 1) * LANES)
            qswa_ref[:, sl] = _rope(qs[:, sl], cs, sas, sbs, SWA_HD // 4).astype(BF16)
        siga_ref[...] = jax.nn.sigmoid(y[:, C_GA:C_GA + D_MODEL]).astype(BF16)
        sigb_ref[...] = jax.nn.sigmoid(y[:, C_GB:C_GB + D_MODEL]).astype(BF16)


def _proj(x2d, mod3, mod_row_fn, tab_row_fn, tabs, consts, q_consts, win, with_q, name):
    rows = x2d.shape[0]
    tm = PROJ_TM
    const = lambda shape: pl.BlockSpec(shape, lambda i: (0,) * len(shape))
    tab = pl.BlockSpec((tm, LANES), lambda i: (tab_row_fn(i), 0))
    g1, rest = consts[0], consts[1:]
    in_specs = [
        pl.BlockSpec((tm, D_MODEL), lambda i: (i, 0)),
        pl.BlockSpec((None, 1, D_MODEL), lambda i: (mod_row_fn(i), 0, 0)),
        pl.BlockSpec((None, 1, D_MODEL), lambda i: (mod_row_fn(i), 0, 1)),
        const(g1.shape),
        pl.BlockSpec(win.shape, lambda i: (0, 0), pipeline_mode=pl.Buffered(1)),
        tab, tab, tab, tab, tab, tab,
    ] + [const(c.shape) for c in rest]
    args = [x2d, mod3, mod3, g1, win] + list(tabs) + list(rest)
    out_w = [MLA_HEADS * LANES, MLA_HEADS * MLA_V, SWA_KV_HEADS * LANES, SWA_KV_HEADS * SWA_HD]
    if with_q:
        in_specs += [const(c.shape) for c in q_consts]
        args += list(q_consts)
        out_w += [MLA_HEADS * LANES, SWA_HEADS * LANES, D_MODEL, D_MODEL]
    return pl.pallas_call(
        functools.partial(_proj_kernel, with_q=with_q),
        grid=(rows // tm,),
        in_specs=in_specs,
        out_specs=[pl.BlockSpec((tm, w), lambda i: (i, 0)) for w in out_w],
        out_shape=[jax.ShapeDtypeStruct((rows, w), BF16) for w in out_w],
        compiler_params=pltpu.CompilerParams(
            dimension_semantics=("arbitrary",), vmem_limit_bytes=VMEM_LIMIT),
        name=name,
    )(*args)


def _mla_kernel(q_ref, kl_ref, vl_ref, kc_ref, vc_ref, o_ref):
    outs = []
    for hh in range(2):
        sl = slice(hh * LANES, (hh + 1) * LANES)
        q = q_ref[:, sl]
        s1 = _mm_nt(q, kl_ref[:, sl])
        s2 = _mm_nt(q, kc_ref[:, sl])
        m = jnp.maximum(jnp.max(s1, axis=-1, keepdims=True), jnp.max(s2, axis=-1, keepdims=True))
        p1 = jnp.exp(s1 - m)
        p2 = jnp.exp(s2 - m)
        l = jnp.sum(p1, axis=-1, keepdims=True) + jnp.sum(p2, axis=-1, keepdims=True)
        o = _mm(p1, vl_ref[...]) + _mm(p2, vc_ref[...])
        outs.append(o / l)
    lane = lax.broadcasted_iota(I32, outs[0].shape, 1)
    o_ref[...] = jnp.where(lane < MLA_V, outs[0], outs[1]).astype(BF16)


def _mla_attn(q, k_lat, v_lat, k_ctx, v_ctx):
    tq = MLA_TQ
    return pl.pallas_call(
        _mla_kernel,
        grid=(BATCH, MLA_HEADS // 2, SEQ // tq),
        in_specs=[
            pl.BlockSpec((None, tq, 2 * LANES), lambda b, hp, i: (b, i, hp)),
            pl.BlockSpec((None, SEQ, 2 * LANES), lambda b, hp, i: (b, 0, hp)),
            pl.BlockSpec((None, SEQ, 2 * MLA_V), lambda b, hp, i: (b, 0, hp)),
            pl.BlockSpec((None, CTX_LEN, 2 * LANES), lambda b, hp, i: (b, 0, hp)),
            pl.BlockSpec((None, CTX_LEN, 2 * MLA_V), lambda b, hp, i: (b, 0, hp)),
        ],
        out_specs=pl.BlockSpec((None, tq, 2 * MLA_V), lambda b, hp, i: (b, i, hp)),
        out_shape=jax.ShapeDtypeStruct((BATCH, SEQ, MLA_HEADS * MLA_V), BF16),
        compiler_params=pltpu.CompilerParams(
            dimension_semantics=("arbitrary", "arbitrary", "arbitrary"),
            vmem_limit_bytes=VMEM_LIMIT),
        name="mla_attn",
    )(q, k_lat, v_lat, k_ctx, v_ctx)


def _swa_kernel(sink_ref, q_ref, k_ref, v_ref, kc_ref, vc_ref, o_ref):
    tq = SWA_TQ
    i = pl.program_id(1)
    s0 = i * tq
    kstart = pl.multiple_of(jnp.clip(s0 - WINDOW, 0, SEQ - SWA_KW), LANES)
    kwin = k_ref[pl.ds(kstart, SWA_KW), :]
    vwin = v_ref[pl.ds(kstart, SWA_KW), :]
    kc = kc_ref[...]
    vc = vc_ref[...]
    row = lax.broadcasted_iota(I32, (SWA_GROUP * tq, SWA_KW), 0)
    qpos = s0 + (row & (tq - 1))
    kpos = kstart + lax.broadcasted_iota(I32, (SWA_GROUP * tq, SWA_KW), 1)
    allowed = jnp.abs(qpos - kpos) <= WINDOW
    lane = lax.broadcasted_iota(I32, (tq, LANES), 1)
    pieces = {}
    for kh in range(SWA_KV_HEADS):
        ksl = slice(kh * LANES, (kh + 1) * LANES)
        q4 = jnp.concatenate(
            [q_ref[:, (kh * SWA_GROUP + g) * LANES:(kh * SWA_GROUP + g + 1) * LANES]
             for g in range(SWA_GROUP)], axis=0)
        sb = jnp.where(allowed, _mm_nt(q4, kwin[:, ksl]), NEG_INF)
        sc = _mm_nt(q4, kc[:, ksl])
        sink = jnp.concatenate(
            [jnp.full((tq, 1), sink_ref[kh * SWA_GROUP + g], F32) for g in range(SWA_GROUP)], axis=0)
        m = jnp.maximum(jnp.maximum(jnp.max(sb, axis=-1, keepdims=True),
                                    jnp.max(sc, axis=-1, keepdims=True)), sink)
        pb = jnp.exp(sb - m)
        pc = jnp.exp(sc - m)
        l = (jnp.sum(pb, axis=-1, keepdims=True) + jnp.sum(pc, axis=-1, keepdims=True)
             + jnp.exp(sink - m))
        o = (_mm(pb, vwin) + _mm(pc, vc)) / l
        for g in range(SWA_GROUP):
            og = o[g * tq:(g + 1) * tq]
            if (g % 2) != kh:
                og = pltpu.roll(og, SWA_HD, 1)
            pieces[kh * SWA_GROUP + g] = og
    for c in range(SWA_HEADS // 2):
        o_ref[:, c * LANES:(c + 1) * LANES] = jnp.where(
            lane < SWA_HD, pieces[2 * c], pieces[2 * c + 1]).astype(BF16)


def _swa_attn(sink, q, k_lat, v_lat, k_ctx, v_ctx):
    tq = SWA_TQ
    return pl.pallas_call(
        _swa_kernel,
        grid=(BATCH, SEQ // tq),
        in_specs=[
            pl.BlockSpec(memory_space=pltpu.SMEM),
            pl.BlockSpec((None, tq, SWA_HEADS * LANES), lambda b, i: (b, i, 0)),
            pl.BlockSpec((None, SEQ, SWA_KV_HEADS * LANES), lambda b, i: (b, 0, 0)),
            pl.BlockSpec((None, SEQ, SWA_KV_HEADS * SWA_HD), lambda b, i: (b, 0, 0)),
            pl.BlockSpec((None, CTX_LEN, SWA_KV_HEADS * LANES), lambda b, i: (b, 0, 0)),
            pl.BlockSpec((None, CTX_LEN, SWA_KV_HEADS * SWA_HD), lambda b, i: (b, 0, 0)),
        ],
        out_specs=pl.BlockSpec((None, tq, SWA_HEADS * SWA_HD), lambda b, i: (b, i, 0)),
        out_shape=jax.ShapeDtypeStruct((BATCH, SEQ, SWA_HEADS * SWA_HD), BF16),
        compiler_params=pltpu.CompilerParams(
            dimension_semantics=("arbitrary", "arbitrary"), vmem_limit_bytes=VMEM_LIMIT),
        name="swa_attn",
    )(sink, q, k_lat, v_lat, k_ctx, v_ctx)


def _merge_kernel(oa_ref, ob_ref, sa_ref, sb_ref, x_ref, g1_ref, sh2_ref, sc2_ref, wba_ref, wbb_ref,
                  wout_ref, g2_ref, wr_ref, br_ref, utri_ref, ltri_ref,
                  xnew_ref, h2_ref, pos_ref, wts_ref, seg_ref, carry_ref):
    i = pl.program_id(0)

    @pl.when(i == 0)
    def _():
        carry_ref[...] = jnp.zeros_like(carry_ref)

    ya = _mm(oa_ref[...], wba_ref[...])
    yb = _mm(ob_ref[...], wbb_ref[...])
    y = sa_ref[...].astype(F32) * ya + sb_ref[...].astype(F32) * yb
    z = _mm(y, wout_ref[...])
    xn = x_ref[...] + g1_ref[...] * z
    xnew_ref[...] = xn
    h2 = _rms(xn) * g2_ref[...] * (1.0 + sc2_ref[...]) + sh2_ref[...]
    h2_ref[...] = h2.astype(BF16)

    hhi, hlo = _split(h2)
    whi, wlo = _split(wr_ref[...])
    lg = _mm_nt(whi, hhi) + _mm_nt(whi, hlo) + _mm_nt(wlo, hhi) + br_ref[...]
    eiota = lax.broadcasted_iota(I32, lg.shape, 0).astype(F32)
    vals, onehots = [], []
    for k in range(TOP_K):
        m = jnp.max(lg, axis=0, keepdims=True)
        ik = jnp.min(jnp.where(lg == m, eiota, float(N_EXPERTS)), axis=0, keepdims=True)
        hit = eiota == ik
        vals.append(m)
        onehots.append(hit.astype(F32))
        lg = jnp.where(hit, -jnp.inf, lg)
    es = [jnp.exp(v - vals[0]) for v in vals]
    tot = es[0] + es[1] + es[2] + es[3]
    for k in range(TOP_K):
        wts_ref[k:k + 1, :] = es[k] / tot

    tots = [jnp.sum(oh, axis=1, keepdims=True) for oh in onehots]
    n = tots[0] + tots[1] + tots[2] + tots[3]
    m_al = jnp.floor((n + (SEG_ALIGN - 1)) * (1.0 / SEG_ALIGN))
    m_b = jnp.broadcast_to(m_al, (N_EXPERTS, LANES))
    lstart = _mm(ltri_ref[...], m_b) * float(SEG_ALIGN)
    off = jnp.zeros_like(n)
    for k in range(TOP_K):
        prefix = _mm(onehots[k], utri_ref[...])
        lp = jnp.sum(onehots[k] * (lstart[:, 0:1] + off + prefix), axis=0, keepdims=True)
        pos_ref[k:k + 1, :] = lp.astype(I32)
        off = off + tots[k]
    eye = (lax.broadcasted_iota(I32, (N_EXPERTS, LANES), 0)
           == lax.broadcasted_iota(I32, (N_EXPERTS, LANES), 1)).astype(F32)
    to_row = lambda v: jnp.sum(v * eye, axis=0, keepdims=True).astype(I32)
    carry = carry_ref[...]
    seg_ref[...] = jnp.zeros_like(seg_ref)
    seg_ref[0:1, :] = to_row(lstart)
    seg_ref[1:2, :] = to_row(m_b * float(SEG_ALIGN))
    seg_ref[2:3, :] = to_row(carry)
    carry_ref[...] = carry + m_b * float(SEG_ALIGN)


def _merge(o_a, o_b, sig_a, sig_b, x2d, mod3, w_ba, w_bb, w_out, g2, w_rt, b_r):
    tm = MERGE_TM
    tiles_per_batch = SEQ // tm
    const = lambda shape: pl.BlockSpec(shape, lambda i: (0,) * len(shape))
    modspec = lambda j: pl.BlockSpec((None, 1, D_MODEL), lambda i: (i // tiles_per_batch, 0, j))
    row = lambda w: pl.BlockSpec((tm, w), lambda i: (i, 0))
    utri = (jnp.arange(tm)[:, None] < jnp.arange(tm)[None, :]).astype(BF16)
    ltri = (jnp.arange(N_EXPERTS)[None, :] < jnp.arange(N_EXPERTS)[:, None]).astype(BF16)
    return pl.pallas_call(
        _merge_kernel,
        grid=(N_TILES,),
        in_specs=[row(512), row(512), row(D_MODEL), row(D_MODEL), row(D_MODEL),
                  modspec(2), modspec(3), modspec(4),
                  const(w_ba.shape), const(w_bb.shape), const(w_out.shape), const(g2.shape),
                  const(w_rt.shape), const(b_r.shape), const(utri.shape), const(ltri.shape)],
        out_specs=[row(D_MODEL), row(D_MODEL),
                   pl.BlockSpec((TOP_K, tm), lambda i: (0, i)),
                   pl.BlockSpec((TOP_K, tm), lambda i: (0, i)),
                   pl.BlockSpec((SUBLANES, LANES), lambda i: (i, 0))],
        out_shape=[jax.ShapeDtypeStruct((N_TOK, D_MODEL), F32),
                   jax.ShapeDtypeStruct((N_TOK, D_MODEL), BF16),
                   jax.ShapeDtypeStruct((TOP_K, N_TOK), I32),
                   jax.ShapeDtypeStruct((TOP_K, N_TOK), F32),
                   jax.ShapeDtypeStruct((N_TILES * SUBLANES, LANES), I32)],
        scratch_shapes=[pltpu.VMEM((N_EXPERTS, LANES), F32)],
        compiler_params=pltpu.CompilerParams(
            dimension_semantics=("arbitrary",), vmem_limit_bytes=VMEM_LIMIT),
        name="merge",
    )(o_a, o_b, sig_a, sig_b, x2d, mod3, mod3, mod3, w_ba, w_bb, w_out, g2, w_rt, b_r, utri, ltri)


def _load_seg_table(seg_hbm, tile, seg_smem, slot, sem):
    cp = pltpu.make_async_copy(seg_hbm.at[pl.ds(tile * SEG_W, SEG_W)],
                               seg_smem.at[pl.ds(slot * SEG_W, SEG_W)], sem)
    cp.start()
    cp.wait()


def _seg_copies(seg_smem, slot, hbm, buf, to_hbm, sem, action):
    base = slot * SEG_W

    def per_expert(e, carry):
        src = seg_smem[base + e]
        rows = seg_smem[base + N_EXPERTS + e]
        dst = seg_smem[base + 2 * N_EXPERTS + e]
        for b in reversed(SEG_BITS):
            size = 1 << b
            done = rows & (-2 * size)

            @pl.when((rows & size) != 0)
            def _():
                vm = buf.at[pl.ds(pl.multiple_of(src + done, SEG_ALIGN), size)]
                hb = hbm.at[pl.ds(pl.multiple_of(dst + done, SEG_ALIGN), size)]
                cp = (pltpu.make_async_copy(vm, hb, sem) if to_hbm
                      else pltpu.make_async_copy(hb, vm, sem))
                getattr(cp, action)()
        return carry

    lax.fori_loop(0, N_EXPERTS, per_expert, 0)


def _dispatch_kernel(pend_ref, padded_ref, seg_hbm, pos_ref, h2_ref, xs_ref, staged, seg_smem, sems):
    j = pl.program_id(0)
    slot = lax.rem(j, 2)

    @pl.when(j == 0)
    def _():
        staged[0, 0:MOE_BLK, :] = jnp.zeros((MOE_BLK, D_MODEL), F32)

        def tail(action):
            def body(e, carry):
                @pl.when(padded_ref[e] > 0)
                def _():
                    start = pl.multiple_of(pend_ref[e] - MOE_BLK, MOE_BLK)
                    cp = pltpu.make_async_copy(staged.at[0, pl.ds(0, MOE_BLK)],
                                               xs_ref.at[pl.ds(start, MOE_BLK)], sems.at[1])
                    getattr(cp, action)()
                return carry
            lax.fori_loop(0, N_EXPERTS, body, 0)

        def unused(action):
            def body(b, carry):
                cp = pltpu.make_async_copy(
                    staged.at[0, pl.ds(0, MOE_BLK)],
                    xs_ref.at[pl.ds(pl.multiple_of(b * MOE_BLK, MOE_BLK), MOE_BLK)], sems.at[1])
                getattr(cp, action)()
                return carry
            lax.fori_loop(pend_ref[N_EXPERTS - 1] // MOE_BLK, N_BLK, body, 0)

        tail("start")
        unused("start")
        tail("wait")
        unused("wait")

    _load_seg_table(seg_hbm, j, seg_smem, slot, sems.at[0])

    pos = pos_ref[...]
    piota = lax.broadcasted_iota(I32, (LROWS, MERGE_TM), 0)
    sel = (piota == pos[0:1, :]).astype(F32)
    for k in range(1, TOP_K):
        sel = sel + (piota == pos[k:k + 1, :]).astype(F32)
    staged[slot] = _mm(sel, h2_ref[...])

    _seg_copies(seg_smem, slot, xs_ref, staged.at[slot], True, sems.at[2 + slot], "start")

    @pl.when(j > 0)
    def _():
        _seg_copies(seg_smem, 1 - slot, xs_ref, staged.at[1 - slot], True, sems.at[3 - slot], "wait")

    @pl.when(j == N_TILES - 1)
    def _():
        _seg_copies(seg_smem, slot, xs_ref, staged.at[slot], True, sems.at[2 + slot], "wait")


def _dispatch(pends, padded, segtab, pos, h2):
    grid_spec = pltpu.PrefetchScalarGridSpec(
        num_scalar_prefetch=2,
        grid=(N_TILES,),
        in_specs=[
            pl.BlockSpec(memory_space=pl.ANY),
            pl.BlockSpec((TOP_K, MERGE_TM), lambda j, pe, pa: (0, j)),
            pl.BlockSpec((MERGE_TM, D_MODEL), lambda j, pe, pa: (j, 0)),
        ],
        out_specs=pl.BlockSpec(memory_space=pl.ANY),
        scratch_shapes=[pltpu.VMEM((2, LROWS, D_MODEL), F32),
                        pltpu.SMEM((2 * SEG_W,), I32),
                        pltpu.SemaphoreType.DMA((4,))],
    )
    return pl.pallas_call(
        _dispatch_kernel,
        grid_spec=grid_spec,
        out_shape=jax.ShapeDtypeStruct((N_SLOTS, D_MODEL), F32),
        compiler_params=pltpu.CompilerParams(
            dimension_semantics=("arbitrary",), vmem_limit_bytes=VMEM_LIMIT),
        name="dispatch",
    )(pends, padded, segtab, pos, h2)


SPLIT_TN = 1024
SPLIT_SUB = 256


def _split_kernel(w_ref, p_ref, wg_ref, wl_ref):
    half = SPLIT_SUB // 2
    for s in range(SPLIT_TN // SPLIT_SUB):
        r = _mm(w_ref[:, s * SPLIT_SUB:(s + 1) * SPLIT_SUB], p_ref[...])
        wg_ref[:, s * half:(s + 1) * half] = r[:, :half].astype(BF16)
        wl_ref[:, s * half:(s + 1) * half] = r[:, half:].astype(BF16)


def _split_gate_up(w_gu):
    col = jnp.arange(SPLIT_SUB)
    src = jnp.where(col < SPLIT_SUB // 2, 2 * col, 2 * (col - SPLIT_SUB // 2) + 1)
    perm = (jnp.arange(SPLIT_SUB)[:, None] == src[None, :]).astype(BF16)
    out = jax.ShapeDtypeStruct((N_EXPERTS, D_MODEL, D_EXPERT), BF16)
    return pl.pallas_call(
        _split_kernel,
        grid=(N_EXPERTS, 2 * D_EXPERT // SPLIT_TN),
        in_specs=[pl.BlockSpec((None, D_MODEL, SPLIT_TN), lambda e, t: (e, 0, t)),
                  pl.BlockSpec((SPLIT_SUB, SPLIT_SUB), lambda e, t: (0, 0))],
        out_specs=[pl.BlockSpec((None, D_MODEL, SPLIT_TN // 2), lambda e, t: (e, 0, t)),
                   pl.BlockSpec((None, D_MODEL, SPLIT_TN // 2), lambda e, t: (e, 0, t))],
        out_shape=[out, out],
        compiler_params=pltpu.CompilerParams(
            dimension_semantics=("arbitrary", "arbitrary"), vmem_limit_bytes=VMEM_LIMIT),
        name="split_gate_up",
    )(w_gu, perm)


def _expert_kernel(be_ref, nused_ref, xs_ref, wg_ref, wl_ref, bg_ref, bl_ref, wd_ref, bd_ref, o_ref):
    del be_ref
    i = pl.program_id(0)

    @pl.when(i < nused_ref[0])
    def _():
        xb = xs_ref[...].astype(BF16)
        g = jnp.minimum(_mm(xb, wg_ref[...]) + bg_ref[...], SWIGLU_LIMIT)
        l = jnp.clip(_mm(xb, wl_ref[...]) + bl_ref[...], -SWIGLU_LIMIT, SWIGLU_LIMIT)
        act = g * jax.nn.sigmoid(SWIGLU_ALPHA * g) * (l + 1.0)
        o_ref[...] = _mm(act, wd_ref[...]) + bd_ref[...]

    @pl.when(i >= nused_ref[0])
    def _():
        o_ref[...] = jnp.zeros_like(o_ref)


def _experts(block_expert, nused, xs, wg, wl, bg, bl, wd, bd):
    blk = lambda i, be, nu: (jnp.minimum(i, nu[0] - 1), 0)
    wspec = pl.BlockSpec((None, D_MODEL, D_EXPERT), lambda i, be, nu: (be[i], 0, 0))
    bspec = pl.BlockSpec((None, 1, D_EXPERT), lambda i, be, nu: (be[i], 0, 0))
    grid_spec = pltpu.PrefetchScalarGridSpec(
        num_scalar_prefetch=2,
        grid=(N_BLK,),
        in_specs=[pl.BlockSpec((MOE_BLK, D_MODEL), blk),
                  wspec, wspec, bspec, bspec, wspec, bspec],
        out_specs=pl.BlockSpec((MOE_BLK, D_MODEL), lambda i, be, nu: (i, 0)),
    )
    return pl.pallas_call(
        _expert_kernel,
        grid_spec=grid_spec,
        out_shape=jax.ShapeDtypeStruct((N_SLOTS, D_MODEL), F32),
        compiler_params=pltpu.CompilerParams(
            dimension_semantics=("arbitrary",), vmem_limit_bytes=VMEM_LIMIT),
        name="experts",
    )(block_expert, nused, xs, wg, wl, bg, bl, wd, bd)


def _combine_kernel(seg_hbm, eo_ref, xnew_ref, g2_ref, pos_ref, wt_ref, o_ref, gbuf, seg_smem, sems):
    j = pl.program_id(0)
    slot = lax.rem(j, 2)

    def fetch(tile, sl):
        _load_seg_table(seg_hbm, tile, seg_smem, sl, sems.at[0])
        _seg_copies(seg_smem, sl, eo_ref, gbuf.at[sl], False, sems.at[1 + sl], "start")

    @pl.when(j == 0)
    def _():
        gbuf[...] = jnp.zeros_like(gbuf)
        fetch(0, 0)

    @pl.when(j + 1 < N_TILES)
    def _():
        fetch(j + 1, 1 - slot)

    _seg_copies(seg_smem, slot, eo_ref, gbuf.at[slot], False, sems.at[1 + slot], "wait")

    pos = pos_ref[...]
    wt = wt_ref[...]
    lane = lax.broadcasted_iota(I32, (MERGE_TM, LROWS), 1)
    wsel = jnp.where(lane == pos[:, 0:1], wt[:, 0:1], 0.0)
    for k in range(1, TOP_K):
        wsel = wsel + jnp.where(lane == pos[:, k:k + 1], wt[:, k:k + 1], 0.0)
    whi, wlo = _split(wsel)
    rows = gbuf[slot].astype(BF16)
    acc = _mm(whi, rows) + _mm(wlo, rows)
    o_ref[...] = xnew_ref[...] + g2_ref[...] * acc


def _combine(segtab, eo, x_new, mod3, pos_t, wts_t):
    tiles_per_batch = SEQ // MERGE_TM
    return pl.pallas_call(
        _combine_kernel,
        grid=(N_TILES,),
        in_specs=[
            pl.BlockSpec(memory_space=pl.ANY),
            pl.BlockSpec(memory_space=pl.ANY),
            pl.BlockSpec((MERGE_TM, D_MODEL), lambda j: (j, 0)),
            pl.BlockSpec((None, 1, D_MODEL), lambda j: (j // tiles_per_batch, 0, 5)),
            pl.BlockSpec((MERGE_TM, TOP_K), lambda j: (j, 0)),
            pl.BlockSpec((MERGE_TM, TOP_K), lambda j: (j, 0)),
        ],
        out_specs=pl.BlockSpec((MERGE_TM, D_MODEL), lambda j: (j, 0)),
        out_shape=jax.ShapeDtypeStruct((N_TOK, D_MODEL), F32),
        scratch_shapes=[pltpu.VMEM((2, LROWS, D_MODEL), F32),
                        pltpu.SMEM((2 * SEG_W,), I32),
                        pltpu.SemaphoreType.DMA((3,))],
        compiler_params=pltpu.CompilerParams(
            dimension_semantics=("arbitrary",), vmem_limit_bytes=VMEM_LIMIT),
        name="combine",
    )(segtab, eo, x_new, mod3, pos_t, wts_t)


def _pad_heads(w, heads, dim):
    lead = w.shape[:-1]
    w = w.reshape(lead + (heads, dim))
    w = jnp.pad(w, [(0, 0)] * len(lead) + [(0, 0), (0, LANES - dim)])
    return w.reshape(lead + (heads * LANES,))


def _rope_tables(dim, offset):
    pos = jnp.arange(SEQ, dtype=jnp.int32)
    row = (pos // GRID_W).astype(F32)
    col = (pos % GRID_W).astype(F32)
    q = dim // 4
    freqs = ROPE_THETA ** (-jnp.arange(q, dtype=F32) / q)
    ang_r = row[:, None] * freqs
    ang_c = col[:, None] * freqs
    ang = jnp.concatenate([ang_r, ang_r, ang_c, ang_c], axis=-1)
    cos, sin = jnp.cos(ang), jnp.sin(ang)
    first = (jnp.arange(dim) % (2 * q)) < q
    sa = jnp.where(first, -sin, 0.0)
    sb = jnp.where(first, 0.0, sin)
    pad = lambda t, v: jnp.pad(t, ((0, 0), (offset, LANES - offset - dim)), constant_values=v)
    return pad(cos, 1.0), pad(sa, 0.0), pad(sb, 0.0)


def kernel(x, c, ctx, c_ctx, w_ada, b_ada, norm1_g, norm2_g, w_in, mla_q_a_g, mla_kv_a_g, w_q_up, w_kv_up,
           mla_q_g, mla_k_g, swa_q_g, swa_k_g, swa_sink, w_branch_a, w_branch_b, w_out, w_router, b_router,
           w_gate_up, b_gate_up, w_down, b_down):
    assert x.shape == (BATCH, SEQ, D_MODEL) and ctx.shape == (BATCH, CTX_LEN, D_MODEL)
    assert w_ada.shape[0] == 1, "single layer"

    cond = jnp.concatenate([c, c_ctx[None], jnp.zeros((MOD_ROWS - BATCH - 1, D_MODEL), F32)], axis=0)
    mod = _ada(cond, w_ada[0], b_ada[0][None])
    mod3 = mod.reshape(MOD_ROWS, 1, N_MOD * D_MODEL)

    wi = w_in[0]
    o1, o2, o3 = MLA_KV_RANK, MLA_KV_RANK + MLA_ROPE, MLA_KV_RANK + MLA_ROPE + SWA_KV_HEADS * SWA_HD
    kv_cols = o3 + SWA_KV_HEADS * SWA_HD
    q1 = kv_cols + MLA_Q_RANK
    q2 = q1 + SWA_HEADS * SWA_HD
    q3 = q2 + D_MODEL
    krope_grp = jnp.pad(wi[:, o1:o2], ((0, 0), (MLA_NOPE, LANES - MLA_QK)))
    win = jnp.concatenate([
        wi[:, 0:o1], _pad_heads(wi[:, o2:o3], SWA_KV_HEADS, SWA_HD), wi[:, o3:kv_cols], krope_grp,
        wi[:, kv_cols:q1], _pad_heads(wi[:, q1:q2], SWA_HEADS, SWA_HD), wi[:, q2:q3], wi[:, q3:]],
        axis=1).astype(BF16)
    assert win.shape == (D_MODEL, C_END)
    wkv = w_kv_up[0].reshape(MLA_KV_RANK, MLA_HEADS, MLA_NOPE + MLA_V)
    wkn = _pad_heads(wkv[:, :, :MLA_NOPE].reshape(MLA_KV_RANK, -1), MLA_HEADS, MLA_NOPE).astype(BF16)
    wv = wkv[:, :, MLA_NOPE:].reshape(MLA_KV_RANK, MLA_HEADS * MLA_V).astype(BF16)
    wqup = _pad_heads(w_q_up[0], MLA_HEADS, MLA_QK).astype(BF16)
    gk_t = _pad_heads(jnp.tile(mla_k_g[0], MLA_HEADS)[None], MLA_HEADS, MLA_QK)
    gq_t = _pad_heads(jnp.tile(mla_q_g[0] * MLA_QK ** -0.5, MLA_HEADS)[None], MLA_HEADS, MLA_QK)
    gks_t = _pad_heads(jnp.tile(swa_k_g[0], SWA_KV_HEADS)[None], SWA_KV_HEADS, SWA_HD)
    gqs_t = _pad_heads(jnp.tile(swa_q_g[0] * SWA_HD ** -0.5, SWA_HEADS)[None], SWA_HEADS, SWA_HD)
    head_of_lane = jnp.arange(MLA_HEADS * LANES) // LANES
    ind = (head_of_lane[:, None] == jnp.arange(LANES)[None, :]).astype(BF16)
    indt = ind.T
    consts = [norm1_g[0][None], mla_kv_a_g[0][None], wkn, wv, gk_t, gks_t, ind, indt]
    q_consts = [mla_q_a_g[0][None], wqup, gq_t, gqs_t]

    tabs_lat = _rope_tables(MLA_ROPE, MLA_NOPE) + _rope_tables(SWA_HD, 0)
    ones = jnp.ones((PROJ_TM, LANES), F32)
    zeros = jnp.zeros((PROJ_TM, LANES), F32)
    tabs_ctx = (ones, zeros, zeros, ones, zeros, zeros)

    x2d = x.reshape(N_TOK, D_MODEL)
    tiles_per_batch = SEQ // PROJ_TM
    k_mla, v_mla, k_swa, v_swa, q_mla, q_swa, sig_a, sig_b = _proj(
        x2d, mod3, lambda i: i // tiles_per_batch, lambda i: i % tiles_per_batch,
        tabs_lat, consts, q_consts, win, True, "proj_latent")
    kc_mla, vc_mla, kc_swa, vc_swa = _proj(
        ctx.reshape(BATCH * CTX_LEN, D_MODEL), mod3, lambda i: BATCH, lambda i: 0,
        tabs_ctx, consts, q_consts, win[:, :C_KVEND], False, "proj_ctx")

    b3 = lambda a, n: a.reshape(BATCH, n, a.shape[-1])
    o_a = _mla_attn(b3(q_mla, SEQ), b3(k_mla, SEQ), b3(v_mla, SEQ), b3(kc_mla, CTX_LEN), b3(vc_mla, CTX_LEN))
    o_b = _swa_attn(swa_sink[0], b3(q_swa, SEQ), b3(k_swa, SEQ), b3(v_swa, SEQ),
                    b3(kc_swa, CTX_LEN), b3(vc_swa, CTX_LEN))

    x_new, h2, pos, top_w, seg = _merge(
        o_a.reshape(N_TOK, -1), o_b.reshape(N_TOK, -1), sig_a, sig_b, x2d, mod3,
        w_branch_a[0].astype(BF16), w_branch_b[0].astype(BF16), w_out[0].astype(BF16),
        norm2_g[0][None], w_router[0].T, b_router[0][:, None])

    seg3 = seg.reshape(N_TILES, SUBLANES, LANES)
    lstart, rows, gcarry = (seg3[:, r, :N_EXPERTS] for r in range(3))
    total = gcarry[-1] + rows[-1]
    padded = (total + MOE_BLK - 1) // MOE_BLK * MOE_BLK
    pends = jnp.cumsum(padded).astype(I32)
    pstarts = pends - padded
    nused = (pends[-1] // MOE_BLK).astype(I32)[None]
    block_expert = jnp.clip(
        jnp.searchsorted(pends, jnp.arange(N_BLK, dtype=I32) * MOE_BLK, side='right'),
        0, N_EXPERTS - 1).astype(I32)
    segtab = jnp.concatenate(
        [lstart, rows, pstarts[None, :] + gcarry, jnp.zeros_like(rows)], axis=1).reshape(-1).astype(I32)

    xs = _dispatch(pends, padded.astype(I32), segtab, pos, h2)
    wg, wl = _split_gate_up(w_gate_up[0])
    bgu = b_gate_up[0]
    eo = _experts(block_expert, nused, xs,
                  wg, wl, bgu[:, None, 0::2], bgu[:, None, 1::2],
                  w_down[0].astype(BF16), b_down[0][:, None, :])
    out = _combine(segtab, eo, x_new, mod3, pos.T, top_w.T)
    return out.reshape(BATCH, SEQ, D_MODEL)
```

```python
import functools

import jax
import jax.numpy as jnp
from jax import lax
from jax.experimental import pallas as pl
from jax.experimental.pallas import tpu as pltpu

F32 = jnp.float32
BF16 = jnp.bfloat16
I32 = jnp.int32

LANES = 128
SUBLANES = 8

D_MODEL = 1024
BATCH = 8
SEQ = 4096
CTX_LEN = 256
GRID_W = 64
ROPE_THETA = 10000.0
NORM_EPS = 1e-6
N_MOD = 6
NEG_INF = -1e30
LOG2E = 1.4426950408889634

MLA_HEADS = 8
MLA_NOPE = 64
MLA_ROPE = 32
MLA_QK = MLA_NOPE + MLA_ROPE
MLA_V = 64
MLA_Q_RANK = 384
MLA_KV_RANK = 256

SWA_HEADS = 8
SWA_KV_HEADS = 2
SWA_GROUP = SWA_HEADS // SWA_KV_HEADS
SWA_HD = 64
WINDOW = 128

N_EXPERTS = 32
TOP_K = 4
D_EXPERT = D_MODEL
SWIGLU_LIMIT = 7.0
SWIGLU_ALPHA = 1.702

N_TOK = BATCH * SEQ
MOD_ROWS = 16

ADA_TN = 1536
PROJ_TM = 256
MLA_TQ = 256
SWA_TQ = 256
SWA_KW = SWA_TQ + 2 * WINDOW
MERGE_TM = 256
MOE_BLK = 256
SEG_ALIGN = SUBLANES
N_TILES = N_TOK // MERGE_TM
LROWS = -(-(MERGE_TM * TOP_K + N_EXPERTS * (SEG_ALIGN - 1)) // LANES) * LANES
SEG_BITS = tuple(range(SEG_ALIGN.bit_length() - 1, (MERGE_TM * TOP_K).bit_length()))
SEG_W = 128
N_SLOTS = -(-(N_TOK * TOP_K + N_TILES * N_EXPERTS * (SEG_ALIGN - 1) + N_EXPERTS * (MOE_BLK - 1))
            // MOE_BLK) * MOE_BLK
N_BLK = N_SLOTS // MOE_BLK

VMEM_LIMIT = 56 * 1024 * 1024

C_KVLAT = 0
C_KSWA = C_KVLAT + MLA_KV_RANK
C_VSWA = C_KSWA + SWA_KV_HEADS * LANES
C_KROPE = C_VSWA + SWA_KV_HEADS * SWA_HD
C_KVEND = C_KROPE + LANES
C_QLAT = C_KVEND
C_QSWA = C_QLAT + MLA_Q_RANK
C_GA = C_QSWA + SWA_HEADS * LANES
C_GB = C_GA + D_MODEL
C_END = C_GB + D_MODEL


def _mm(a, b):
    return jnp.dot(a.astype(BF16), b.astype(BF16), preferred_element_type=F32)


def _mm_nt(a, b):
    return lax.dot_general(a.astype(BF16), b.astype(BF16), (((1,), (1,)), ((), ())),
                           preferred_element_type=F32)


def _split(a):
    hi = a.astype(BF16)
    lo = (a - hi.astype(F32)).astype(BF16)
    return hi, lo


def _rms(x):
    return x * lax.rsqrt(jnp.mean(x * x, axis=-1, keepdims=True) + NORM_EPS)


def _ada_kernel(c_ref, w_ref, b_ref, o_ref):
    c = c_ref[...]
    s = c * jax.nn.sigmoid(c)
    shi, slo = _split(s)
    whi, wlo = _split(w_ref[...])
    acc = _mm(shi, whi) + _mm(slo, whi) + _mm(shi, wlo)
    o_ref[...] = acc + b_ref[...]


def _ada(cond, w_ada, b_ada):
    n = w_ada.shape[1]
    return pl.pallas_call(
        _ada_kernel,
        grid=(n // ADA_TN,),
        in_specs=[
            pl.BlockSpec((MOD_ROWS, D_MODEL), lambda j: (0, 0)),
            pl.BlockSpec((D_MODEL, ADA_TN), lambda j: (0, j)),
            pl.BlockSpec((1, ADA_TN), lambda j: (0, j)),
        ],
        out_specs=pl.BlockSpec((MOD_ROWS, ADA_TN), lambda j: (0, j)),
        out_shape=jax.ShapeDtypeStruct((MOD_ROWS, n), F32),
        compiler_params=pltpu.CompilerParams(
            dimension_semantics=("arbitrary",), vmem_limit_bytes=VMEM_LIMIT),
        name="ada",
    )(cond, w_ada, b_ada)


def _head_norm(xraw, ind_ref, indt_ref, inv_dim):
    w = xraw.shape[1]
    ss = _mm(xraw * xraw, ind_ref[0:w, :])
    r = lax.rsqrt(ss * inv_dim + NORM_EPS)
    rhi, rlo = _split(r)
    scale = _mm(rhi, indt_ref[:, 0:w]) + _mm(rlo, indt_ref[:, 0:w])
    return xraw * scale


def _rope(xh, cos, sa, sb, quarter):
    return (xh * cos + pltpu.roll(xh, LANES - quarter, 1) * sa
            + pltpu.roll(xh, quarter, 1) * sb)


def _proj_kernel(*refs, with_q):
    (x_ref, sh_ref, sc_ref, g1_ref, win_ref, cm_ref, sam_ref, sbm_ref, cs_ref, sas_ref, sbs_ref,
     gkva_ref, wkn_ref, wv_ref, gk_ref, gks_ref, ind_ref, indt_ref) = refs[:18]
    if with_q:
        gqa_ref, wqup_ref, gq_ref, gqs_ref = refs[18:22]
        kmla_ref, vmla_ref, kswa_ref, vswa_ref, qmla_ref, qswa_ref, siga_ref, sigb_ref = refs[22:]
    else:
        kmla_ref, vmla_ref, kswa_ref, vswa_ref = refs[18:]

    x = x_ref[...]
    h = _rms(x) * g1_ref[...] * (1.0 + sc_ref[...]) + sh_ref[...]
    y = _mm(h, win_ref[...])

    cm, sam, sbm = cm_ref[...], sam_ref[...], sbm_ref[...]
    cs, sas, sbs = cs_ref[...], sas_ref[...], sbs_ref[...]

    kvn = _rms(y[:, C_KVLAT:C_KVLAT + MLA_KV_RANK]) * gkva_ref[...]
    kn = _mm(kvn, wkn_ref[...])
    vmla_ref[...] = _mm(kvn, wv_ref[...]).astype(BF16)
    kr = y[:, C_KROPE:C_KROPE + LANES]
    kraw = kn + jnp.concatenate([kr] * MLA_HEADS, axis=1)
    kfull = _head_norm(kraw, ind_ref, indt_ref, 1.0 / MLA_QK) * gk_ref[...]
    for hd in range(MLA_HEADS):
        sl = slice(hd * LANES, (hd + 1) * LANES)
        kmla_ref[:, sl] = _rope(kfull[:, sl], cm, sam, sbm, MLA_ROPE // 4).astype(BF16)

    ks = _head_norm(y[:, C_KSWA:C_KSWA + SWA_KV_HEADS * LANES], ind_ref, indt_ref,
                    1.0 / SWA_HD) * gks_ref[...]
    for hd in range(SWA_KV_HEADS):
        sl = slice(hd * LANES, (hd + 1) * LANES)
        kswa_ref[:, sl] = _rope(ks[:, sl], cs, sas, sbs, SWA_HD // 4).astype(BF16)
    vswa_ref[...] = y[:, C_VSWA:C_VSWA + SWA_KV_HEADS * SWA_HD].astype(BF16)

    if with_q:
        qn = _rms(y[:, C_QLAT:C_QLAT + MLA_Q_RANK]) * gqa_ref[...]
        qraw = _mm(qn, wqup_ref[...])
        qf = _head_norm(qraw, ind_ref, indt_ref, 1.0 / MLA_QK) * gq_ref[...]
        for hd in range(MLA_HEADS):
            sl = slice(hd * LANES, (hd + 1) * LANES)
            qmla_ref[:, sl] = _rope(qf[:, sl], cm, sam, sbm, MLA_ROPE // 4).astype(BF16)
        qs = _head_norm(y[:, C_QSWA:C_QSWA + SWA_HEADS * LANES], ind_ref, indt_ref,
                        1.0 / SWA_HD) * gqs_ref[...]
        for hd in range(SWA_HEADS):
            sl = slice(hd * LANES, (hd + 1) * LANES)
            qswa_ref[:, sl] = _rope(qs[:, sl], cs, sas, sbs, SWA_HD // 4).astype(BF16)
        siga_ref[...] = jax.nn.sigmoid(y[:, C_GA:C_GA + D_MODEL]).astype(BF16)
        sigb_ref[...] = jax.nn.sigmoid(y[:, C_GB:C_GB + D_MODEL]).astype(BF16)


def _proj(x2d, mod3, mod_row_fn, tab_row_fn, tabs, consts, q_consts, win, with_q, name):
    rows = x2d.shape[0]
    tm = PROJ_TM
    const = lambda shape: pl.BlockSpec(shape, lambda i: (0,) * len(shape))
    tab = pl.BlockSpec((tm, LANES), lambda i: (tab_row_fn(i), 0))
    g1, rest = consts[0], consts[1:]
    in_specs = [
        pl.BlockSpec((tm, D_MODEL), lambda i: (i, 0)),
        pl.BlockSpec((None, 1, D_MODEL), lambda i: (mod_row_fn(i), 0, 0)),
        pl.BlockSpec((None, 1, D_MODEL), lambda i: (mod_row_fn(i), 0, 1)),
        const(g1.shape),
        pl.BlockSpec(win.shape, lambda i: (0, 0), pipeline_mode=pl.Buffered(1)),
        tab, tab, tab, tab, tab, tab,
    ] + [const(c.shape) for c in rest]
    args = [x2d, mod3, mod3, g1, win] + list(tabs) + list(rest)
    out_w = [MLA_HEADS * LANES, MLA_HEADS * MLA_V, SWA_KV_HEADS * LANES, SWA_KV_HEADS * SWA_HD]
    if with_q:
        in_specs += [const(c.shape) for c in q_consts]
        args += list(q_consts)
        out_w += [MLA_HEADS * LANES, SWA_HEADS * LANES, D_MODEL, D_MODEL]
    return pl.pallas_call(
        functools.partial(_proj_kernel, with_q=with_q),
        grid=(rows // tm,),
        in_specs=in_specs,
        out_specs=[pl.BlockSpec((tm, w), lambda i: (i, 0)) for w in out_w],
        out_shape=[jax.ShapeDtypeStruct((rows, w), BF16) for w in out_w],
        compiler_params=pltpu.CompilerParams(
            dimension_semantics=("arbitrary",), vmem_limit_bytes=VMEM_LIMIT),
        name=name,
    )(*args)


def _mla_step(q_ref, kl_ref, vl_ref, kc_ref, vc_ref, o_ref, cur, prev):
    s_lat_c, s_ctx_c, m_c = cur
    s_lat_p, s_ctx_p, m_p = prev
    outs = []
    for hh in range(2):
        sl = slice(hh * LANES, (hh + 1) * LANES)
        q = q_ref[:, sl]
        s1 = _mm_nt(q, kl_ref[:, sl])
        s2 = _mm_nt(q, kc_ref[:, sl])
        s_lat_c[hh] = s1
        s_ctx_c[hh] = s2
        m_c[hh] = jnp.maximum(jnp.max(s1, axis=-1, keepdims=True),
                              jnp.max(s2, axis=-1, keepdims=True))

        m = m_p[hh]
        p1 = jnp.exp2(s_lat_p[hh] - m)
        p2 = jnp.exp2(s_ctx_p[hh] - m)
        l = jnp.sum(p1, axis=-1, keepdims=True) + jnp.sum(p2, axis=-1, keepdims=True)
        o = _mm(p1, vl_ref[...]) + _mm(p2, vc_ref[...])
        outs.append(o / l)
    lane = lax.broadcasted_iota(I32, outs[0].shape, 1)
    o_ref[...] = jnp.where(lane < MLA_V, outs[0], outs[1]).astype(BF16)


def _mla_kernel(q_ref, kl_ref, vl_ref, kc_ref, vc_ref, o_ref, *scratch):
    set0, set1 = scratch[:3], scratch[3:]
    i = pl.program_id(2)

    @pl.when(i == 0)
    def _():
        for ref in set1:
            ref[...] = jnp.zeros(ref.shape, F32)

    @pl.when(lax.rem(i, 2) == 0)
    def _():
        _mla_step(q_ref, kl_ref, vl_ref, kc_ref, vc_ref, o_ref, set0, set1)

    @pl.when(lax.rem(i, 2) == 1)
    def _():
        _mla_step(q_ref, kl_ref, vl_ref, kc_ref, vc_ref, o_ref, set1, set0)


def _mla_attn(q, k_lat, v_lat, k_ctx, v_ctx):
    tq = MLA_TQ
    nq = SEQ // tq
    return pl.pallas_call(
        _mla_kernel,
        grid=(BATCH, MLA_HEADS // 2, nq + 1),
        in_specs=[
            pl.BlockSpec((None, tq, 2 * LANES), lambda b, hp, i: (b, jnp.minimum(i, nq - 1), hp)),
            pl.BlockSpec((None, SEQ, 2 * LANES), lambda b, hp, i: (b, 0, hp)),
            pl.BlockSpec((None, SEQ, 2 * MLA_V), lambda b, hp, i: (b, 0, hp)),
            pl.BlockSpec((None, CTX_LEN, 2 * LANES), lambda b, hp, i: (b, 0, hp)),
            pl.BlockSpec((None, CTX_LEN, 2 * MLA_V), lambda b, hp, i: (b, 0, hp)),
        ],
        out_specs=pl.BlockSpec((None, tq, 2 * MLA_V), lambda b, hp, i: (b, jnp.maximum(i - 1, 0), hp)),
        out_shape=jax.ShapeDtypeStruct((BATCH, SEQ, MLA_HEADS * MLA_V), BF16),
        scratch_shapes=2 * [pltpu.VMEM((2, tq, SEQ), F32),
                            pltpu.VMEM((2, tq, CTX_LEN), F32),
                            pltpu.VMEM((2, tq, 1), F32)],
        compiler_params=pltpu.CompilerParams(
            dimension_semantics=("arbitrary", "arbitrary", "arbitrary"),
            vmem_limit_bytes=VMEM_LIMIT),
        name="mla_attn",
    )(q, k_lat, v_lat, k_ctx, v_ctx)


def _swa_kernel(sink_ref, q_ref, k_ref, v_ref, kc_ref, vc_ref, o_ref):
    tq = SWA_TQ
    i = pl.program_id(1)
    s0 = i * tq
    kstart = pl.multiple_of(jnp.clip(s0 - WINDOW, 0, SEQ - SWA_KW), LANES)
    kwin = k_ref[pl.ds(kstart, SWA_KW), :]
    vwin = v_ref[pl.ds(kstart, SWA_KW), :]
    kc = kc_ref[...]
    vc = vc_ref[...]
    row = lax.broadcasted_iota(I32, (SWA_GROUP * tq, SWA_KW), 0)
    qpos = s0 + (row & (tq - 1))
    kpos = kstart + lax.broadcasted_iota(I32, (SWA_GROUP * tq, SWA_KW), 1)
    allowed = jnp.abs(qpos - kpos) <= WINDOW
    lane = lax.broadcasted_iota(I32, (tq, LANES), 1)
    pieces = {}
    for kh in range(SWA_KV_HEADS):
        ksl = slice(kh * LANES, (kh + 1) * LANES)
        q4 = jnp.concatenate(
            [q_ref[:, (kh * SWA_GROUP + g) * LANES:(kh * SWA_GROUP + g + 1) * LANES]
             for g in range(SWA_GROUP)], axis=0)
        sb = jnp.where(allowed, _mm_nt(q4, kwin[:, ksl]), NEG_INF)
        sc = _mm_nt(q4, kc[:, ksl])
        sink = jnp.concatenate(
            [jnp.full((tq, 1), sink_ref[kh * SWA_GROUP + g], F32) for g in range(SWA_GROUP)], axis=0)
        m = jnp.maximum(jnp.maximum(jnp.max(sb, axis=-1, keepdims=True),
                                    jnp.max(sc, axis=-1, keepdims=True)), sink)
        pb = jnp.exp(sb - m)
        pc = jnp.exp(sc - m)
        l = (jnp.sum(pb, axis=-1, keepdims=True) + jnp.sum(pc, axis=-1, keepdims=True)
             + jnp.exp(sink - m))
        o = (_mm(pb, vwin) + _mm(pc, vc)) / l
        for g in range(SWA_GROUP):
            og = o[g * tq:(g + 1) * tq]
            if (g % 2) != kh:
                og = pltpu.roll(og, SWA_HD, 1)
            pieces[kh * SWA_GROUP + g] = og
    for c in range(SWA_HEADS // 2):
        o_ref[:, c * LANES:(c + 1) * LANES] = jnp.where(
            lane < SWA_HD, pieces[2 * c], pieces[2 * c + 1]).astype(BF16)


def _swa_attn(sink, q, k_lat, v_lat, k_ctx, v_ctx):
    tq = SWA_TQ
    return pl.pallas_call(
        _swa_kernel,
        grid=(BATCH, SEQ // tq),
        in_specs=[
            pl.BlockSpec(memory_space=pltpu.SMEM),
            pl.BlockSpec((None, tq, SWA_HEADS * LANES), lambda b, i: (b, i, 0)),
            pl.BlockSpec((None, SEQ, SWA_KV_HEADS * LANES), lambda b, i: (b, 0, 0)),
            pl.BlockSpec((None, SEQ, SWA_KV_HEADS * SWA_HD), lambda b, i: (b, 0, 0)),
            pl.BlockSpec((None, CTX_LEN, SWA_KV_HEADS * LANES), lambda b, i: (b, 0, 0)),
            pl.BlockSpec((None, CTX_LEN, SWA_KV_HEADS * SWA_HD), lambda b, i: (b, 0, 0)),
        ],
        out_specs=pl.BlockSpec((None, tq, SWA_HEADS * SWA_HD), lambda b, i: (b, i, 0)),
        out_shape=jax.ShapeDtypeStruct((BATCH, SEQ, SWA_HEADS * SWA_HD), BF16),
        compiler_params=pltpu.CompilerParams(
            dimension_semantics=("arbitrary", "arbitrary"), vmem_limit_bytes=VMEM_LIMIT),
        name="swa_attn",
    )(sink, q, k_lat, v_lat, k_ctx, v_ctx)


def _merge_kernel(oa_ref, ob_ref, sa_ref, sb_ref, x_ref, g1_ref, sh2_ref, sc2_ref, wba_ref, wbb_ref,
                  wout_ref, g2_ref, wr_ref, br_ref, utri_ref, ltri_ref,
                  xnew_ref, h2_ref, pos_ref, wts_ref, seg_ref, carry_ref):
    i = pl.program_id(0)

    @pl.when(i == 0)
    def _():
        carry_ref[...] = jnp.zeros_like(carry_ref)

    ya = _mm(oa_ref[...], wba_ref[...])
    yb = _mm(ob_ref[...], wbb_ref[...])
    y = sa_ref[...].astype(F32) * ya + sb_ref[...].astype(F32) * yb
    z = _mm(y, wout_ref[...])
    xn = x_ref[...] + g1_ref[...] * z
    xnew_ref[...] = xn
    h2 = _rms(xn) * g2_ref[...] * (1.0 + sc2_ref[...]) + sh2_ref[...]
    h2_ref[...] = h2.astype(BF16)

    hhi, hlo = _split(h2)
    whi, wlo = _split(wr_ref[...])
    lg = _mm_nt(whi, hhi) + _mm_nt(whi, hlo) + _mm_nt(wlo, hhi) + br_ref[...]
    eiota = lax.broadcasted_iota(I32, lg.shape, 0).astype(F32)
    vals, onehots = [], []
    for k in range(TOP_K):
        m = jnp.max(lg, axis=0, keepdims=True)
        ik = jnp.min(jnp.where(lg == m, eiota, float(N_EXPERTS)), axis=0, keepdims=True)
        hit = eiota == ik
        vals.append(m)
        onehots.append(hit.astype(F32))
        lg = jnp.where(hit, -jnp.inf, lg)
    es = [jnp.exp(v - vals[0]) for v in vals]
    tot = es[0] + es[1] + es[2] + es[3]
    for k in range(TOP_K):
        wts_ref[k:k + 1, :] = es[k] / tot

    tots = [jnp.sum(oh, axis=1, keepdims=True) for oh in onehots]
    n = tots[0] + tots[1] + tots[2] + tots[3]
    m_al = jnp.floor((n + (SEG_ALIGN - 1)) * (1.0 / SEG_ALIGN))
    m_b = jnp.broadcast_to(m_al, (N_EXPERTS, LANES))
    lstart = _mm(ltri_ref[...], m_b) * float(SEG_ALIGN)
    off = jnp.zeros_like(n)
    for k in range(TOP_K):
        prefix = _mm(onehots[k], utri_ref[...])
        lp = jnp.sum(onehots[k] * (lstart[:, 0:1] + off + prefix), axis=0, keepdims=True)
        pos_ref[k:k + 1, :] = lp.astype(I32)
        off = off + tots[k]
    eye = (lax.broadcasted_iota(I32, (N_EXPERTS, LANES), 0)
           == lax.broadcasted_iota(I32, (N_EXPERTS, LANES), 1)).astype(F32)
    to_row = lambda v: jnp.sum(v * eye, axis=0, keepdims=True).astype(I32)
    carry = carry_ref[...]
    seg_ref[...] = jnp.zeros_like(seg_ref)
    seg_ref[0:1, :] = to_row(lstart)
    seg_ref[1:2, :] = to_row(m_b * float(SEG_ALIGN))
    seg_ref[2:3, :] = to_row(carry)
    carry_ref[...] = carry + m_b * float(SEG_ALIGN)


def _merge(o_a, o_b, sig_a, sig_b, x2d, mod3, w_ba, w_bb, w_out, g2, w_rt, b_r):
    tm = MERGE_TM
    tiles_per_batch = SEQ // tm
    const = lambda shape: pl.BlockSpec(shape, lambda i: (0,) * len(shape))
    modspec = lambda j: pl.BlockSpec((None, 1, D_MODEL), lambda i: (i // tiles_per_batch, 0, j))
    row = lambda w: pl.BlockSpec((tm, w), lambda i: (i, 0))
    utri = (jnp.arange(tm)[:, None] < jnp.arange(tm)[None, :]).astype(BF16)
    ltri = (jnp.arange(N_EXPERTS)[None, :] < jnp.arange(N_EXPERTS)[:, None]).astype(BF16)
    return pl.pallas_call(
        _merge_kernel,
        grid=(N_TILES,),
        in_specs=[row(512), row(512), row(D_MODEL), row(D_MODEL), row(D_MODEL),
                  modspec(2), modspec(3), modspec(4),
                  const(w_ba.shape), const(w_bb.shape), const(w_out.shape), const(g2.shape),
                  const(w_rt.shape), const(b_r.shape), const(utri.shape), const(ltri.shape)],
        out_specs=[row(D_MODEL), row(D_MODEL),
                   pl.BlockSpec((TOP_K, tm), lambda i: (0, i)),
                   pl.BlockSpec((TOP_K, tm), lambda i: (0, i)),
                   pl.BlockSpec((SUBLANES, LANES), lambda i: (i, 0))],
        out_shape=[jax.ShapeDtypeStruct((N_TOK, D_MODEL), F32),
                   jax.ShapeDtypeStruct((N_TOK, D_MODEL), BF16),
                   jax.ShapeDtypeStruct((TOP_K, N_TOK), I32),
                   jax.ShapeDtypeStruct((TOP_K, N_TOK), F32),
                   jax.ShapeDtypeStruct((N_TILES * SUBLANES, LANES), I32)],
        scratch_shapes=[pltpu.VMEM((N_EXPERTS, LANES), F32)],
        compiler_params=pltpu.CompilerParams(
            dimension_semantics=("arbitrary",), vmem_limit_bytes=VMEM_LIMIT),
        name="merge",
    )(o_a, o_b, sig_a, sig_b, x2d, mod3, mod3, mod3, w_ba, w_bb, w_out, g2, w_rt, b_r, utri, ltri)


def _seg_copies(seg_smem, tile, hbm, buf, to_hbm, sem, action):
    base = tile * SEG_W

    def per_expert(e, carry):
        src = seg_smem[base + e]
        rows = seg_smem[base + N_EXPERTS + e]
        dst = seg_smem[base + 2 * N_EXPERTS + e]
        for b in reversed(SEG_BITS):
            size = 1 << b
            done = rows & (-2 * size)

            @pl.when((rows & size) != 0)
            def _():
                vm = buf.at[pl.ds(pl.multiple_of(src + done, SEG_ALIGN), size)]
                hb = hbm.at[pl.ds(pl.multiple_of(dst + done, SEG_ALIGN), size)]
                cp = (pltpu.make_async_copy(vm, hb, sem) if to_hbm
                      else pltpu.make_async_copy(hb, vm, sem))
                getattr(cp, action)()
        return carry

    lax.fori_loop(0, N_EXPERTS, per_expert, 0)


def _dispatch_kernel(pend_ref, padded_ref, seg_ref, pos_ref, h2_ref, xs_ref, staged, sems):
    j = pl.program_id(0)
    slot = lax.rem(j, 2)

    @pl.when(j == 0)
    def _():
        staged[0, 0:MOE_BLK, :] = jnp.zeros((MOE_BLK, D_MODEL), F32)

        def tail(action):
            def body(e, carry):
                @pl.when(padded_ref[e] > 0)
                def _():
                    start = pl.multiple_of(pend_ref[e] - MOE_BLK, MOE_BLK)
                    cp = pltpu.make_async_copy(staged.at[0, pl.ds(0, MOE_BLK)],
                                               xs_ref.at[pl.ds(start, MOE_BLK)], sems.at[0])
                    getattr(cp, action)()
                return carry
            lax.fori_loop(0, N_EXPERTS, body, 0)

        def unused(action):
            def body(b, carry):
                cp = pltpu.make_async_copy(
                    staged.at[0, pl.ds(0, MOE_BLK)],
                    xs_ref.at[pl.ds(pl.multiple_of(b * MOE_BLK, MOE_BLK), MOE_BLK)], sems.at[0])
                getattr(cp, action)()
                return carry
            lax.fori_loop(pend_ref[N_EXPERTS - 1] // MOE_BLK, N_BLK, body, 0)

        tail("start")
        unused("start")
        tail("wait")
        unused("wait")

    pos = pos_ref[...]
    piota = lax.broadcasted_iota(I32, (LROWS, MERGE_TM), 0)
    sel = (piota == pos[0:1, :]).astype(F32)
    for k in range(1, TOP_K):
        sel = sel + (piota == pos[k:k + 1, :]).astype(F32)
    staged[slot] = _mm(sel, h2_ref[...])

    _seg_copies(seg_ref, j, xs_ref, staged.at[slot], True, sems.at[1 + slot], "start")

    @pl.when(j > 0)
    def _():
        _seg_copies(seg_ref, j - 1, xs_ref, staged.at[1 - slot], True, sems.at[2 - slot], "wait")

    @pl.when(j == N_TILES - 1)
    def _():
        _seg_copies(seg_ref, j, xs_ref, staged.at[slot], True, sems.at[1 + slot], "wait")


def _dispatch(pends, padded, segtab, pos, h2):
    grid_spec = pltpu.PrefetchScalarGridSpec(
        num_scalar_prefetch=3,
        grid=(N_TILES,),
        in_specs=[
            pl.BlockSpec((TOP_K, MERGE_TM), lambda j, pe, pa, sg: (0, j)),
            pl.BlockSpec((MERGE_TM, D_MODEL), lambda j, pe, pa, sg: (j, 0)),
        ],
        out_specs=pl.BlockSpec(memory_space=pl.ANY),
        scratch_shapes=[pltpu.VMEM((2, LROWS, D_MODEL), F32),
                        pltpu.SemaphoreType.DMA((3,))],
    )
    return pl.pallas_call(
        _dispatch_kernel,
        grid_spec=grid_spec,
        out_shape=jax.ShapeDtypeStruct((N_SLOTS, D_MODEL), F32),
        compiler_params=pltpu.CompilerParams(
            dimension_semantics=("arbitrary",), vmem_limit_bytes=VMEM_LIMIT),
        name="dispatch",
    )(pends, padded, segtab, pos, h2)


SPLIT_TN = 1024
SPLIT_SUB = 256


def _split_kernel(w_ref, p_ref, wg_ref, wl_ref):
    half = SPLIT_SUB // 2
    for s in range(SPLIT_TN // SPLIT_SUB):
        r = _mm(w_ref[:, s * SPLIT_SUB:(s + 1) * SPLIT_SUB], p_ref[...])
        wg_ref[:, s * half:(s + 1) * half] = r[:, :half].astype(BF16)
        wl_ref[:, s * half:(s + 1) * half] = r[:, half:].astype(BF16)


def _split_gate_up(w_gu):
    col = jnp.arange(SPLIT_SUB)
    src = jnp.where(col < SPLIT_SUB // 2, 2 * col, 2 * (col - SPLIT_SUB // 2) + 1)
    perm = (jnp.arange(SPLIT_SUB)[:, None] == src[None, :]).astype(BF16)
    out = jax.ShapeDtypeStruct((N_EXPERTS, D_MODEL, D_EXPERT), BF16)
    return pl.pallas_call(
        _split_kernel,
        grid=(N_EXPERTS, 2 * D_EXPERT // SPLIT_TN),
        in_specs=[pl.BlockSpec((None, D_MODEL, SPLIT_TN), lambda e, t: (e, 0, t)),
                  pl.BlockSpec((SPLIT_SUB, SPLIT_SUB), lambda e, t: (0, 0))],
        out_specs=[pl.BlockSpec((None, D_MODEL, SPLIT_TN // 2), lambda e, t: (e, 0, t)),
                   pl.BlockSpec((None, D_MODEL, SPLIT_TN // 2), lambda e, t: (e, 0, t))],
        out_shape=[out, out],
        compiler_params=pltpu.CompilerParams(
            dimension_semantics=("arbitrary", "arbitrary"), vmem_limit_bytes=VMEM_LIMIT),
        name="split_gate_up",
    )(w_gu, perm)


def _expert_kernel(be_ref, nused_ref, xs_ref, wg_ref, wl_ref, bg_ref, bl_ref, wd_ref, bd_ref, o_ref):
    del be_ref
    i = pl.program_id(0)

    @pl.when(i < nused_ref[0])
    def _():
        xb = xs_ref[...].astype(BF16)
        g = jnp.minimum(_mm(xb, wg_ref[...]) + bg_ref[...], SWIGLU_LIMIT)
        l = jnp.clip(_mm(xb, wl_ref[...]) + bl_ref[...], -SWIGLU_LIMIT, SWIGLU_LIMIT)
        act = g * jax.nn.sigmoid(SWIGLU_ALPHA * g) * (l + 1.0)
        o_ref[...] = _mm(act, wd_ref[...]) + bd_ref[...]

    @pl.when(i >= nused_ref[0])
    def _():
        o_ref[...] = jnp.zeros_like(o_ref)


def _experts(block_expert, nused, xs, wg, wl, bg, bl, wd, bd):
    blk = lambda i, be, nu: (jnp.minimum(i, nu[0] - 1), 0)
    wspec = pl.BlockSpec((None, D_MODEL, D_EXPERT), lambda i, be, nu: (be[i], 0, 0))
    bspec = pl.BlockSpec((None, 1, D_EXPERT), lambda i, be, nu: (be[i], 0, 0))
    grid_spec = pltpu.PrefetchScalarGridSpec(
        num_scalar_prefetch=2,
        grid=(N_BLK,),
        in_specs=[pl.BlockSpec((MOE_BLK, D_MODEL), blk),
                  wspec, wspec, bspec, bspec, wspec, bspec],
        out_specs=pl.BlockSpec((MOE_BLK, D_MODEL), lambda i, be, nu: (i, 0)),
    )
    return pl.pallas_call(
        _expert_kernel,
        grid_spec=grid_spec,
        out_shape=jax.ShapeDtypeStruct((N_SLOTS, D_MODEL), F32),
        compiler_params=pltpu.CompilerParams(
            dimension_semantics=("arbitrary",), vmem_limit_bytes=VMEM_LIMIT),
        name="experts",
    )(block_expert, nused, xs, wg, wl, bg, bl, wd, bd)


def _combine_kernel(seg_ref, eo_ref, xnew_ref, g2_ref, pos_ref, wt_ref, o_ref, gbuf, sems):
    j = pl.program_id(0)
    slot = lax.rem(j, 2)

    def fetch(tile, sl):
        _seg_copies(seg_ref, tile, eo_ref, gbuf.at[sl], False, sems.at[sl], "start")

    @pl.when(j == 0)
    def _():
        gbuf[...] = jnp.zeros_like(gbuf)
        fetch(0, 0)

    @pl.when(j + 1 < N_TILES)
    def _():
        fetch(j + 1, 1 - slot)

    _seg_copies(seg_ref, j, eo_ref, gbuf.at[slot], False, sems.at[slot], "wait")

    pos = pos_ref[...]
    wt = wt_ref[...]
    lane = lax.broadcasted_iota(I32, (MERGE_TM, LROWS), 1)
    wsel = jnp.where(lane == pos[:, 0:1], wt[:, 0:1], 0.0)
    for k in range(1, TOP_K):
        wsel = wsel + jnp.where(lane == pos[:, k:k + 1], wt[:, k:k + 1], 0.0)
    acc = _mm(wsel, gbuf[slot])
    o_ref[...] = xnew_ref[...] + g2_ref[...] * acc


def _combine(segtab, eo, x_new, mod3, pos_t, wts_t):
    tiles_per_batch = SEQ // MERGE_TM
    grid_spec = pltpu.PrefetchScalarGridSpec(
        num_scalar_prefetch=1,
        grid=(N_TILES,),
        in_specs=[
            pl.BlockSpec(memory_space=pl.ANY),
            pl.BlockSpec((MERGE_TM, D_MODEL), lambda j, sg: (j, 0)),
            pl.BlockSpec((None, 1, D_MODEL), lambda j, sg: (j // tiles_per_batch, 0, 5)),
            pl.BlockSpec((MERGE_TM, TOP_K), lambda j, sg: (j, 0)),
            pl.BlockSpec((MERGE_TM, TOP_K), lambda j, sg: (j, 0)),
        ],
        out_specs=pl.BlockSpec((MERGE_TM, D_MODEL), lambda j, sg: (j, 0)),
        scratch_shapes=[pltpu.VMEM((2, LROWS, D_MODEL), F32),
                        pltpu.SemaphoreType.DMA((2,))],
    )
    return pl.pallas_call(
        _combine_kernel,
        grid_spec=grid_spec,
        out_shape=jax.ShapeDtypeStruct((N_TOK, D_MODEL), F32),
        compiler_params=pltpu.CompilerParams(
            dimension_semantics=("arbitrary",), vmem_limit_bytes=VMEM_LIMIT),
        name="combine",
    )(segtab, eo, x_new, mod3, pos_t, wts_t)


def _pad_heads(w, heads, dim):
    lead = w.shape[:-1]
    w = w.reshape(lead + (heads, dim))
    w = jnp.pad(w, [(0, 0)] * len(lead) + [(0, 0), (0, LANES - dim)])
    return w.reshape(lead + (heads * LANES,))


def _rope_tables(dim, offset):
    pos = jnp.arange(SEQ, dtype=jnp.int32)
    row = (pos // GRID_W).astype(F32)
    col = (pos % GRID_W).astype(F32)
    q = dim // 4
    freqs = ROPE_THETA ** (-jnp.arange(q, dtype=F32) / q)
    ang_r = row[:, None] * freqs
    ang_c = col[:, None] * freqs
    ang = jnp.concatenate([ang_r, ang_r, ang_c, ang_c], axis=-1)
    cos, sin = jnp.cos(ang), jnp.sin(ang)
    first = (jnp.arange(dim) % (2 * q)) < q
    sa = jnp.where(first, -sin, 0.0)
    sb = jnp.where(first, 0.0, sin)
    pad = lambda t, v: jnp.pad(t, ((0, 0), (offset, LANES - offset - dim)), constant_values=v)
    return pad(cos, 1.0), pad(sa, 0.0), pad(sb, 0.0)


def kernel(x, c, ctx, c_ctx, w_ada, b_ada, norm1_g, norm2_g, w_in, mla_q_a_g, mla_kv_a_g, w_q_up, w_kv_up,
           mla_q_g, mla_k_g, swa_q_g, swa_k_g, swa_sink, w_branch_a, w_branch_b, w_out, w_router, b_router,
           w_gate_up, b_gate_up, w_down, b_down):
    assert x.shape == (BATCH, SEQ, D_MODEL) and ctx.shape == (BATCH, CTX_LEN, D_MODEL)
    assert w_ada.shape[0] == 1, "single layer"

    cond = jnp.concatenate([c, c_ctx[None], jnp.zeros((MOD_ROWS - BATCH - 1, D_MODEL), F32)], axis=0)
    mod = _ada(cond, w_ada[0], b_ada[0][None])
    mod3 = mod.reshape(MOD_ROWS, 1, N_MOD * D_MODEL)

    wi = w_in[0]
    o1, o2, o3 = MLA_KV_RANK, MLA_KV_RANK + MLA_ROPE, MLA_KV_RANK + MLA_ROPE + SWA_KV_HEADS * SWA_HD
    kv_cols = o3 + SWA_KV_HEADS * SWA_HD
    q1 = kv_cols + MLA_Q_RANK
    q2 = q1 + SWA_HEADS * SWA_HD
    q3 = q2 + D_MODEL
    krope_grp = jnp.pad(wi[:, o1:o2], ((0, 0), (MLA_NOPE, LANES - MLA_QK)))
    win = jnp.concatenate([
        wi[:, 0:o1], _pad_heads(wi[:, o2:o3], SWA_KV_HEADS, SWA_HD), wi[:, o3:kv_cols], krope_grp,
        wi[:, kv_cols:q1], _pad_heads(wi[:, q1:q2], SWA_HEADS, SWA_HD), wi[:, q2:q3], wi[:, q3:]],
        axis=1).astype(BF16)
    assert win.shape == (D_MODEL, C_END)
    wkv = w_kv_up[0].reshape(MLA_KV_RANK, MLA_HEADS, MLA_NOPE + MLA_V)
    wkn = _pad_heads(wkv[:, :, :MLA_NOPE].reshape(MLA_KV_RANK, -1), MLA_HEADS, MLA_NOPE).astype(BF16)
    wv = wkv[:, :, MLA_NOPE:].reshape(MLA_KV_RANK, MLA_HEADS * MLA_V).astype(BF16)
    wqup = _pad_heads(w_q_up[0], MLA_HEADS, MLA_QK).astype(BF16)
    gk_t = _pad_heads(jnp.tile(mla_k_g[0], MLA_HEADS)[None], MLA_HEADS, MLA_QK)
    gq_t = _pad_heads(jnp.tile(mla_q_g[0] * (MLA_QK ** -0.5 * LOG2E), MLA_HEADS)[None], MLA_HEADS, MLA_QK)
    gks_t = _pad_heads(jnp.tile(swa_k_g[0], SWA_KV_HEADS)[None], SWA_KV_HEADS, SWA_HD)
    gqs_t = _pad_heads(jnp.tile(swa_q_g[0] * SWA_HD ** -0.5, SWA_HEADS)[None], SWA_HEADS, SWA_HD)
    head_of_lane = jnp.arange(MLA_HEADS * LANES) // LANES
    ind = (head_of_lane[:, None] == jnp.arange(LANES)[None, :]).astype(BF16)
    indt = ind.T
    consts = [norm1_g[0][None], mla_kv_a_g[0][None], wkn, wv, gk_t, gks_t, ind, indt]
    q_consts = [mla_q_a_g[0][None], wqup, gq_t, gqs_t]

    tabs_lat = _rope_tables(MLA_ROPE, MLA_NOPE) + _rope_tables(SWA_HD, 0)
    ones = jnp.ones((PROJ_TM, LANES), F32)
    zeros = jnp.zeros((PROJ_TM, LANES), F32)
    tabs_ctx = (ones, zeros, zeros, ones, zeros, zeros)

    x2d = x.reshape(N_TOK, D_MODEL)
    tiles_per_batch = SEQ // PROJ_TM
    k_mla, v_mla, k_swa, v_swa, q_mla, q_swa, sig_a, sig_b = _proj(
        x2d, mod3, lambda i: i // tiles_per_batch, lambda i: i % tiles_per_batch,
        tabs_lat, consts, q_consts, win, True, "proj_latent")
    kc_mla, vc_mla, kc_swa, vc_swa = _proj(
        ctx.reshape(BATCH * CTX_LEN, D_MODEL), mod3, lambda i: BATCH, lambda i: 0,
        tabs_ctx, consts, q_consts, win[:, :C_KVEND], False, "proj_ctx")

    b3 = lambda a, n: a.reshape(BATCH, n, a.shape[-1])
    o_a = _mla_attn(b3(q_mla, SEQ), b3(k_mla, SEQ), b3(v_mla, SEQ), b3(kc_mla, CTX_LEN), b3(vc_mla, CTX_LEN))
    o_b = _swa_attn(swa_sink[0], b3(q_swa, SEQ), b3(k_swa, SEQ), b3(v_swa, SEQ),
                    b3(kc_swa, CTX_LEN), b3(vc_swa, CTX_LEN))

    x_new, h2, pos, top_w, seg = _merge(
        o_a.reshape(N_TOK, -1), o_b.reshape(N_TOK, -1), sig_a, sig_b, x2d, mod3,
        w_branch_a[0].astype(BF16), w_branch_b[0].astype(BF16), w_out[0].astype(BF16),
        norm2_g[0][None], w_router[0].T, b_router[0][:, None])

    seg3 = seg.reshape(N_TILES, SUBLANES, LANES)
    lstart, rows, gcarry = (seg3[:, r, :N_EXPERTS] for r in range(3))
    total = gcarry[-1] + rows[-1]
    padded = (total + MOE_BLK - 1) // MOE_BLK * MOE_BLK
    pends = jnp.cumsum(padded).astype(I32)
    pstarts = pends - padded
    nused = (pends[-1] // MOE_BLK).astype(I32)[None]
    blk_start = jnp.arange(N_BLK, dtype=I32) * MOE_BLK
    block_expert = jnp.minimum(
        jnp.sum((blk_start[:, None] >= pends[None, :]).astype(I32), axis=1), N_EXPERTS - 1)
    segtab = jnp.concatenate(
        [lstart, rows, pstarts[None, :] + gcarry, jnp.zeros_like(rows)], axis=1).reshape(-1).astype(I32)

    xs = _dispatch(pends, padded.astype(I32), segtab, pos, h2)
    wg, wl = _split_gate_up(w_gate_up[0])
    bgu = b_gate_up[0]
    eo = _experts(block_expert, nused, xs,
                  wg, wl, bgu[:, None, 0::2], bgu[:, None, 1::2],
                  w_down[0].astype(BF16), b_down[0][:, None, :])
    out = _combine(segtab, eo, x_new, mod3, pos.T, top_w.T)
    return out.reshape(BATCH, SEQ, D_MODEL)
```

```python
import functools

import jax
import jax.numpy as jnp
from jax import lax
from jax.experimental import pallas as pl
from jax.experimental.pallas import tpu as pltpu

F32 = jnp.float32
BF16 = jnp.bfloat16
I32 = jnp.int32

LANES = 128
SUBLANES = 8

D_MODEL = 1024
BATCH = 8
SEQ = 4096
CTX_LEN = 256
GRID_W = 64
ROPE_THETA = 10000.0
NORM_EPS = 1e-6
N_MOD = 6
NEG_INF = -1e30
LOG2E = 1.4426950408889634

MLA_HEADS = 8
MLA_NOPE = 64
MLA_ROPE = 32
MLA_QK = MLA_NOPE + MLA_ROPE
MLA_V = 64
MLA_Q_RANK = 384
MLA_KV_RANK = 256

SWA_HEADS = 8
SWA_KV_HEADS = 2
SWA_GROUP = SWA_HEADS // SWA_KV_HEADS
SWA_HD = 64
WINDOW = 128

N_EXPERTS = 32
TOP_K = 4
D_EXPERT = D_MODEL
SWIGLU_LIMIT = 7.0
SWIGLU_ALPHA = 1.702

N_TOK = BATCH * SEQ
MOD_ROWS = 16

ADA_TN = 1536
PROJ_TM = 256
MLA_TQ = 256
SWA_TQ = 256
SWA_KW = SWA_TQ + 2 * WINDOW
MERGE_TM = 256
MOE_BLK = 256
SEG_ALIGN = SUBLANES
N_TILES = N_TOK // MERGE_TM
LROWS = -(-(MERGE_TM * TOP_K + N_EXPERTS * (SEG_ALIGN - 1)) // LANES) * LANES
SEG_BITS = tuple(range(SEG_ALIGN.bit_length() - 1, (MERGE_TM * TOP_K).bit_length()))
SEG_W = 128
N_SLOTS = -(-(N_TOK * TOP_K + N_TILES * N_EXPERTS * (SEG_ALIGN - 1) + N_EXPERTS * (MOE_BLK - 1))
            // MOE_BLK) * MOE_BLK
N_BLK = N_SLOTS // MOE_BLK

VMEM_LIMIT = 56 * 1024 * 1024

C_KVLAT = 0
C_KSWA = C_KVLAT + MLA_KV_RANK
C_KROPE = C_KSWA + SWA_KV_HEADS * LANES
C_KVEND = C_KROPE + LANES
C_QLAT = C_KVEND
C_QSWA = C_QLAT + MLA_Q_RANK
C_GA = C_QSWA + SWA_HEADS * LANES
C_GB = C_GA + D_MODEL
C_END = C_GB + D_MODEL


def _mm(a, b):
    return jnp.dot(a.astype(BF16), b.astype(BF16), preferred_element_type=F32)


def _mm_nt(a, b):
    return lax.dot_general(a.astype(BF16), b.astype(BF16), (((1,), (1,)), ((), ())),
                           preferred_element_type=F32)


def _mm_tn(a, b):
    return lax.dot_general(a.astype(BF16), b.astype(BF16), (((0,), (0,)), ((), ())),
                           preferred_element_type=F32)


def _split(a):
    hi = a.astype(BF16)
    lo = (a - hi.astype(F32)).astype(BF16)
    return hi, lo


def _rms(x):
    return x * lax.rsqrt(jnp.mean(x * x, axis=-1, keepdims=True) + NORM_EPS)


def _ada_kernel(c_ref, w_ref, b_ref, o_ref):
    c = c_ref[...]
    s = c * jax.nn.sigmoid(c)
    shi, slo = _split(s)
    whi, wlo = _split(w_ref[...])
    acc = _mm(shi, whi) + _mm(slo, whi) + _mm(shi, wlo)
    o_ref[...] = acc + b_ref[...]


def _ada(cond, w_ada, b_ada):
    n = w_ada.shape[1]
    return pl.pallas_call(
        _ada_kernel,
        grid=(n // ADA_TN,),
        in_specs=[
            pl.BlockSpec((MOD_ROWS, D_MODEL), lambda j: (0, 0)),
            pl.BlockSpec((D_MODEL, ADA_TN), lambda j: (0, j)),
            pl.BlockSpec((1, ADA_TN), lambda j: (0, j)),
        ],
        out_specs=pl.BlockSpec((MOD_ROWS, ADA_TN), lambda j: (0, j)),
        out_shape=jax.ShapeDtypeStruct((MOD_ROWS, n), F32),
        compiler_params=pltpu.CompilerParams(
            dimension_semantics=("arbitrary",), vmem_limit_bytes=VMEM_LIMIT),
        name="ada",
    )(cond, w_ada, b_ada)


def _head_norm(xraw, ind_ref, indt_ref, inv_dim):
    w = xraw.shape[1]
    ss = _mm(xraw * xraw, ind_ref[0:w, :])
    r = lax.rsqrt(ss * inv_dim + NORM_EPS)
    rhi, rlo = _split(r)
    scale = _mm(rhi, indt_ref[:, 0:w]) + _mm(rlo, indt_ref[:, 0:w])
    return xraw * scale


def _rope(xh, cos, sa, sb, quarter):
    return (xh * cos + pltpu.roll(xh, LANES - quarter, 1) * sa
            + pltpu.roll(xh, quarter, 1) * sb)


def _proj_kernel(*refs, with_q):
    (x_ref, sh_ref, sc_ref, g1_ref, win_ref, cm_ref, sam_ref, sbm_ref, cs_ref, sas_ref, sbs_ref,
     gkva_ref, wkn_ref, wvt_ref, wvst_ref, gk_ref, gks_ref, ind_ref, indt_ref) = refs[:19]
    if with_q:
        gqa_ref, wqup_ref, gq_ref, gqs_ref = refs[19:23]
        kmla_ref, vmla_ref, kswa_ref, vswa_ref, qmla_ref, qswa_ref, siga_ref, sigb_ref = refs[23:]
    else:
        kmla_ref, vmla_ref, kswa_ref, vswa_ref = refs[19:]

    x = x_ref[...]
    h = _rms(x) * g1_ref[...] * (1.0 + sc_ref[...]) + sh_ref[...]
    y = _mm(h, win_ref[...])

    cm, sam, sbm = cm_ref[...], sam_ref[...], sbm_ref[...]
    cs, sas, sbs = cs_ref[...], sas_ref[...], sbs_ref[...]

    kvn = _rms(y[:, C_KVLAT:C_KVLAT + MLA_KV_RANK]) * gkva_ref[...]
    kn = _mm(kvn, wkn_ref[...])
    vmla_ref[...] = _mm_nt(wvt_ref[...], kvn).astype(BF16)
    vswa_ref[...] = _mm_nt(wvst_ref[...], h).astype(BF16)
    kr = y[:, C_KROPE:C_KROPE + LANES]
    kraw = kn + jnp.concatenate([kr] * MLA_HEADS, axis=1)
    kfull = _head_norm(kraw, ind_ref, indt_ref, 1.0 / MLA_QK) * gk_ref[...]
    for hd in range(MLA_HEADS):
        sl = slice(hd * LANES, (hd + 1) * LANES)
        kmla_ref[:, sl] = _rope(kfull[:, sl], cm, sam, sbm, MLA_ROPE // 4).astype(BF16)

    ks = _head_norm(y[:, C_KSWA:C_KSWA + SWA_KV_HEADS * LANES], ind_ref, indt_ref,
                    1.0 / SWA_HD) * gks_ref[...]
    for hd in range(SWA_KV_HEADS):
        sl = slice(hd * LANES, (hd + 1) * LANES)
        kswa_ref[:, sl] = _rope(ks[:, sl], cs, sas, sbs, SWA_HD // 4).astype(BF16)

    if with_q:
        qn = _rms(y[:, C_QLAT:C_QLAT + MLA_Q_RANK]) * gqa_ref[...]
        qraw = _mm(qn, wqup_ref[...])
        qf = _head_norm(qraw, ind_ref, indt_ref, 1.0 / MLA_QK) * gq_ref[...]
        for hd in range(MLA_HEADS):
            sl = slice(hd * LANES, (hd + 1) * LANES)
            qmla_ref[:, sl] = _rope(qf[:, sl], cm, sam, sbm, MLA_ROPE // 4).astype(BF16)
        qs = _head_norm(y[:, C_QSWA:C_QSWA + SWA_HEADS * LANES], ind_ref, indt_ref,
                        1.0 / SWA_HD) * gqs_ref[...]
        for hd in range(SWA_HEADS):
            sl = slice(hd * LANES, (hd + 1) * LANES)
            qswa_ref[:, sl] = _rope(qs[:, sl], cs, sas, sbs, SWA_HD // 4).astype(BF16)
        siga_ref[...] = jax.nn.sigmoid(y[:, C_GA:C_GA + D_MODEL]).astype(BF16)
        sigb_ref[...] = jax.nn.sigmoid(y[:, C_GB:C_GB + D_MODEL]).astype(BF16)


def _proj(x2d, mod3, mod_row_fn, tab_row_fn, tabs, consts, q_consts, win, with_q, name):
    rows = x2d.shape[0]
    tm = PROJ_TM
    const = lambda shape: pl.BlockSpec(shape, lambda i: (0,) * len(shape))
    tab = pl.BlockSpec((tm, LANES), lambda i: (tab_row_fn(i), 0))
    g1, rest = consts[0], consts[1:]
    in_specs = [
        pl.BlockSpec((tm, D_MODEL), lambda i: (i, 0)),
        pl.BlockSpec((None, 1, D_MODEL), lambda i: (mod_row_fn(i), 0, 0)),
        pl.BlockSpec((None, 1, D_MODEL), lambda i: (mod_row_fn(i), 0, 1)),
        const(g1.shape),
        pl.BlockSpec(win.shape, lambda i: (0, 0), pipeline_mode=pl.Buffered(1)),
        tab, tab, tab, tab, tab, tab,
    ] + [const(c.shape) for c in rest]
    args = [x2d, mod3, mod3, g1, win] + list(tabs) + list(rest)
    outs = [(MLA_HEADS * LANES, False), (MLA_HEADS * MLA_V, True),
            (SWA_KV_HEADS * LANES, False), (SWA_KV_HEADS * SWA_HD, True)]
    if with_q:
        in_specs += [const(c.shape) for c in q_consts]
        args += list(q_consts)
        outs += [(MLA_HEADS * LANES, False), (SWA_HEADS * LANES, False), (D_MODEL, False), (D_MODEL, False)]
    return pl.pallas_call(
        functools.partial(_proj_kernel, with_q=with_q),
        grid=(rows // tm,),
        in_specs=in_specs,
        out_specs=[pl.BlockSpec((w, tm), lambda i: (0, i)) if t else pl.BlockSpec((tm, w), lambda i: (i, 0))
                   for w, t in outs],
        out_shape=[jax.ShapeDtypeStruct((w, rows) if t else (rows, w), BF16) for w, t in outs],
        compiler_params=pltpu.CompilerParams(
            dimension_semantics=("arbitrary",), vmem_limit_bytes=VMEM_LIMIT),
        name=name,
    )(*args)


def _mla_step(q_ref, kl_ref, vlt_ref, kc_ref, vct_ref, o_ref, cur, prev):
    s_lat_c, s_ctx_c, m_c = cur
    s_lat_p, s_ctx_p, m_p = prev
    outs = []
    for hh in range(2):
        sl = slice(hh * LANES, (hh + 1) * LANES)
        vrows = slice(hh * MLA_V, (hh + 1) * MLA_V)
        q = q_ref[:, sl]
        s1 = _mm_nt(kl_ref[:, sl], q)
        s2 = _mm_nt(kc_ref[:, sl], q)
        s_lat_c[hh] = s1
        s_ctx_c[hh] = s2
        m_c[hh] = jnp.maximum(jnp.max(s1, axis=0, keepdims=True),
                              jnp.max(s2, axis=0, keepdims=True))

        m = m_p[hh]
        p1 = jnp.exp2(s_lat_p[hh] - m)
        p2 = jnp.exp2(s_ctx_p[hh] - m)
        l = jnp.sum(p1, axis=0, keepdims=True) + jnp.sum(p2, axis=0, keepdims=True)
        o = _mm(vlt_ref[vrows, :], p1) + _mm(vct_ref[vrows, :], p2)
        outs.append(o / l)
    o_ref[...] = jnp.concatenate(outs, axis=0).astype(BF16)


def _mla_kernel(q_ref, kl_ref, vl_ref, kc_ref, vc_ref, o_ref, *scratch):
    set0, set1 = scratch[:3], scratch[3:]
    i = pl.program_id(2)

    @pl.when(i == 0)
    def _():
        for ref in set1:
            ref[...] = jnp.zeros(ref.shape, F32)

    @pl.when(lax.rem(i, 2) == 0)
    def _():
        _mla_step(q_ref, kl_ref, vl_ref, kc_ref, vc_ref, o_ref, set0, set1)

    @pl.when(lax.rem(i, 2) == 1)
    def _():
        _mla_step(q_ref, kl_ref, vl_ref, kc_ref, vc_ref, o_ref, set1, set0)


def _mla_attn(q, k_lat, vt_lat, k_ctx, vt_ctx):
    tq = MLA_TQ
    nq = SEQ // tq
    return pl.pallas_call(
        _mla_kernel,
        grid=(BATCH, MLA_HEADS // 2, nq + 1),
        in_specs=[
            pl.BlockSpec((tq, 2 * LANES), lambda b, hp, i: (b * nq + jnp.minimum(i, nq - 1), hp)),
            pl.BlockSpec((SEQ, 2 * LANES), lambda b, hp, i: (b, hp)),
            pl.BlockSpec((2 * MLA_V, SEQ), lambda b, hp, i: (hp, b)),
            pl.BlockSpec((CTX_LEN, 2 * LANES), lambda b, hp, i: (b, hp)),
            pl.BlockSpec((2 * MLA_V, CTX_LEN), lambda b, hp, i: (hp, b)),
        ],
        out_specs=pl.BlockSpec((2 * MLA_V, tq), lambda b, hp, i: (hp, b * nq + jnp.maximum(i - 1, 0))),
        out_shape=jax.ShapeDtypeStruct((MLA_HEADS * MLA_V, N_TOK), BF16),
        scratch_shapes=2 * [pltpu.VMEM((2, SEQ, tq), F32),
                            pltpu.VMEM((2, CTX_LEN, tq), F32),
                            pltpu.VMEM((2, 1, tq), F32)],
        compiler_params=pltpu.CompilerParams(
            dimension_semantics=("arbitrary", "arbitrary", "arbitrary"),
            vmem_limit_bytes=VMEM_LIMIT),
        name="mla_attn",
    )(q, k_lat, vt_lat, k_ctx, vt_ctx)


def _swa_kernel(sink_ref, q_ref, k_ref, vt_ref, kc_ref, vct_ref, o_ref):
    tq = SWA_TQ
    i = pl.program_id(1)
    s0 = i * tq
    kstart = pl.multiple_of(jnp.clip(s0 - WINDOW, 0, SEQ - SWA_KW), LANES)
    kwin = k_ref[pl.ds(kstart, SWA_KW), :]
    vtwin = vt_ref[:, pl.ds(kstart, SWA_KW)]
    kc = kc_ref[...]
    vct = vct_ref[...]
    col = lax.broadcasted_iota(I32, (SWA_KW, SWA_GROUP * tq), 1)
    qpos = s0 + (col & (tq - 1))
    kpos = kstart + lax.broadcasted_iota(I32, (SWA_KW, SWA_GROUP * tq), 0)
    allowed = jnp.abs(qpos - kpos) <= WINDOW
    for kh in range(SWA_KV_HEADS):
        ksl = slice(kh * LANES, (kh + 1) * LANES)
        vrows = slice(kh * SWA_HD, (kh + 1) * SWA_HD)
        q4 = jnp.concatenate(
            [q_ref[:, (kh * SWA_GROUP + g) * LANES:(kh * SWA_GROUP + g + 1) * LANES]
             for g in range(SWA_GROUP)], axis=0)
        sb = jnp.where(allowed, _mm_nt(kwin[:, ksl], q4), NEG_INF)
        sc = _mm_nt(kc[:, ksl], q4)
        sink = jnp.concatenate(
            [jnp.full((1, tq), sink_ref[kh * SWA_GROUP + g] * LOG2E, F32) for g in range(SWA_GROUP)],
            axis=1)
        m = jnp.maximum(jnp.maximum(jnp.max(sb, axis=0, keepdims=True),
                                    jnp.max(sc, axis=0, keepdims=True)), sink)
        pb = jnp.exp2(sb - m)
        pc = jnp.exp2(sc - m)
        l = (jnp.sum(pb, axis=0, keepdims=True) + jnp.sum(pc, axis=0, keepdims=True)
             + jnp.exp2(sink - m))
        o = (_mm(vtwin[vrows, :], pb) + _mm(vct[vrows, :], pc)) / l
        for g in range(SWA_GROUP):
            hd = kh * SWA_GROUP + g
            o_ref[hd * SWA_HD:(hd + 1) * SWA_HD, :] = o[:, g * tq:(g + 1) * tq].astype(BF16)


def _swa_attn(sink, q, k_lat, vt_lat, k_ctx, vt_ctx):
    tq = SWA_TQ
    nq = SEQ // tq
    return pl.pallas_call(
        _swa_kernel,
        grid=(BATCH, nq),
        in_specs=[
            pl.BlockSpec(memory_space=pltpu.SMEM),
            pl.BlockSpec((tq, SWA_HEADS * LANES), lambda b, i: (b * nq + i, 0)),
            pl.BlockSpec((SEQ, SWA_KV_HEADS * LANES), lambda b, i: (b, 0)),
            pl.BlockSpec((SWA_KV_HEADS * SWA_HD, SEQ), lambda b, i: (0, b)),
            pl.BlockSpec((CTX_LEN, SWA_KV_HEADS * LANES), lambda b, i: (b, 0)),
            pl.BlockSpec((SWA_KV_HEADS * SWA_HD, CTX_LEN), lambda b, i: (0, b)),
        ],
        out_specs=pl.BlockSpec((SWA_HEADS * SWA_HD, tq), lambda b, i: (0, b * nq + i)),
        out_shape=jax.ShapeDtypeStruct((SWA_HEADS * SWA_HD, N_TOK), BF16),
        compiler_params=pltpu.CompilerParams(
            dimension_semantics=("arbitrary", "arbitrary"), vmem_limit_bytes=VMEM_LIMIT),
        name="swa_attn",
    )(sink, q, k_lat, vt_lat, k_ctx, vt_ctx)


def _merge_kernel(oa_ref, ob_ref, sa_ref, sb_ref, x_ref, g1_ref, sh2_ref, sc2_ref, wba_ref, wbb_ref,
                  wout_ref, g2_ref, wr_ref, br_ref, utri_ref, ltri_ref,
                  xnew_ref, h2_ref, pos_ref, wts_ref, seg_ref, carry_ref):
    i = pl.program_id(0)

    @pl.when(i == 0)
    def _():
        carry_ref[...] = jnp.zeros_like(carry_ref)

    ya = _mm_tn(oa_ref[...], wba_ref[...])
    yb = _mm_tn(ob_ref[...], wbb_ref[...])
    y = sa_ref[...].astype(F32) * ya + sb_ref[...].astype(F32) * yb
    z = _mm(y, wout_ref[...])
    xn = x_ref[...] + g1_ref[...] * z
    xnew_ref[...] = xn
    h2 = _rms(xn) * g2_ref[...] * (1.0 + sc2_ref[...]) + sh2_ref[...]
    h2_ref[...] = h2.astype(BF16)

    hhi, hlo = _split(h2)
    whi, wlo = _split(wr_ref[...])
    lg = _mm_nt(whi, hhi) + _mm_nt(whi, hlo) + _mm_nt(wlo, hhi) + br_ref[...]
    eiota = lax.broadcasted_iota(I32, lg.shape, 0).astype(F32)
    vals, onehots = [], []
    for k in range(TOP_K):
        m = jnp.max(lg, axis=0, keepdims=True)
        ik = jnp.min(jnp.where(lg == m, eiota, float(N_EXPERTS)), axis=0, keepdims=True)
        hit = eiota == ik
        vals.append(m)
        onehots.append(hit.astype(F32))
        lg = jnp.where(hit, -jnp.inf, lg)
    es = [jnp.exp(v - vals[0]) for v in vals]
    tot = es[0] + es[1] + es[2] + es[3]
    for k in range(TOP_K):
        wts_ref[k:k + 1, :] = es[k] / tot

    tots = [jnp.sum(oh, axis=1, keepdims=True) for oh in onehots]
    n = tots[0] + tots[1] + tots[2] + tots[3]
    m_al = jnp.floor((n + (SEG_ALIGN - 1)) * (1.0 / SEG_ALIGN))
    m_b = jnp.broadcast_to(m_al, (N_EXPERTS, LANES))
    lstart = _mm(ltri_ref[...], m_b) * float(SEG_ALIGN)
    off = jnp.zeros_like(n)
    for k in range(TOP_K):
        prefix = _mm(onehots[k], utri_ref[...])
        lp = jnp.sum(onehots[k] * (lstart[:, 0:1] + off + prefix), axis=0, keepdims=True)
        pos_ref[k:k + 1, :] = lp.astype(I32)
        off = off + tots[k]
    eye = (lax.broadcasted_iota(I32, (N_EXPERTS, LANES), 0)
           == lax.broadcasted_iota(I32, (N_EXPERTS, LANES), 1)).astype(F32)
    to_row = lambda v: jnp.sum(v * eye, axis=0, keepdims=True).astype(I32)
    carry = carry_ref[...]
    seg_ref[...] = jnp.zeros_like(seg_ref)
    seg_ref[0:1, :] = to_row(lstart)
    seg_ref[1:2, :] = to_row(m_b * float(SEG_ALIGN))
    seg_ref[2:3, :] = to_row(carry)
    carry_ref[...] = carry + m_b * float(SEG_ALIGN)


def _merge(o_a, o_b, sig_a, sig_b, x2d, mod3, w_ba, w_bb, w_out, g2, w_rt, b_r):
    tm = MERGE_TM
    tiles_per_batch = SEQ // tm
    const = lambda shape: pl.BlockSpec(shape, lambda i: (0,) * len(shape))
    modspec = lambda j: pl.BlockSpec((None, 1, D_MODEL), lambda i: (i // tiles_per_batch, 0, j))
    row = lambda w: pl.BlockSpec((tm, w), lambda i: (i, 0))
    utri = (jnp.arange(tm)[:, None] < jnp.arange(tm)[None, :]).astype(BF16)
    ltri = (jnp.arange(N_EXPERTS)[None, :] < jnp.arange(N_EXPERTS)[:, None]).astype(BF16)
    return pl.pallas_call(
        _merge_kernel,
        grid=(N_TILES,),
        in_specs=[pl.BlockSpec((512, tm), lambda i: (0, i)), pl.BlockSpec((512, tm), lambda i: (0, i)),
                  row(D_MODEL), row(D_MODEL), row(D_MODEL),
                  modspec(2), modspec(3), modspec(4),
                  const(w_ba.shape), const(w_bb.shape), const(w_out.shape), const(g2.shape),
                  const(w_rt.shape), const(b_r.shape), const(utri.shape), const(ltri.shape)],
        out_specs=[row(D_MODEL), row(D_MODEL),
                   pl.BlockSpec((TOP_K, tm), lambda i: (0, i)),
                   pl.BlockSpec((TOP_K, tm), lambda i: (0, i)),
                   pl.BlockSpec((SUBLANES, LANES), lambda i: (i, 0))],
        out_shape=[jax.ShapeDtypeStruct((N_TOK, D_MODEL), F32),
                   jax.ShapeDtypeStruct((N_TOK, D_MODEL), BF16),
                   jax.ShapeDtypeStruct((TOP_K, N_TOK), I32),
                   jax.ShapeDtypeStruct((TOP_K, N_TOK), F32),
                   jax.ShapeDtypeStruct((N_TILES * SUBLANES, LANES), I32)],
        scratch_shapes=[pltpu.VMEM((N_EXPERTS, LANES), F32)],
        compiler_params=pltpu.CompilerParams(
            dimension_semantics=("arbitrary",), vmem_limit_bytes=VMEM_LIMIT),
        name="merge",
    )(o_a, o_b, sig_a, sig_b, x2d, mod3, mod3, mod3, w_ba, w_bb, w_out, g2, w_rt, b_r, utri, ltri)


def _seg_copies(seg_smem, tile, hbm, buf, to_hbm, sem, action):
    base = tile * SEG_W

    def per_expert(e, carry):
        src = seg_smem[base + e]
        rows = seg_smem[base + N_EXPERTS + e]
        dst = seg_smem[base + 2 * N_EXPERTS + e]
        for b in reversed(SEG_BITS):
            size = 1 << b
            done = rows & (-2 * size)

            @pl.when((rows & size) != 0)
            def _():
                vm = buf.at[pl.ds(pl.multiple_of(src + done, SEG_ALIGN), size)]
                hb = hbm.at[pl.ds(pl.multiple_of(dst + done, SEG_ALIGN), size)]
                cp = (pltpu.make_async_copy(vm, hb, sem) if to_hbm
                      else pltpu.make_async_copy(hb, vm, sem))
                getattr(cp, action)()
        return carry

    lax.fori_loop(0, N_EXPERTS, per_expert, 0)


def _dispatch_kernel(pend_ref, padded_ref, seg_ref, pos_ref, h2_ref, xs_ref, staged, sems):
    j = pl.program_id(0)
    slot = lax.rem(j, 2)

    @pl.when(j == 0)
    def _():
        staged[0, 0:MOE_BLK, :] = jnp.zeros((MOE_BLK, D_MODEL), F32)

        def tail(action):
            def body(e, carry):
                @pl.when(padded_ref[e] > 0)
                def _():
                    start = pl.multiple_of(pend_ref[e] - MOE_BLK, MOE_BLK)
                    cp = pltpu.make_async_copy(staged.at[0, pl.ds(0, MOE_BLK)],
                                               xs_ref.at[pl.ds(start, MOE_BLK)], sems.at[0])
                    getattr(cp, action)()
                return carry
            lax.fori_loop(0, N_EXPERTS, body, 0)

        def unused(action):
            def body(b, carry):
                cp = pltpu.make_async_copy(
                    staged.at[0, pl.ds(0, MOE_BLK)],
                    xs_ref.at[pl.ds(pl.multiple_of(b * MOE_BLK, MOE_BLK), MOE_BLK)], sems.at[0])
                getattr(cp, action)()
                return carry
            lax.fori_loop(pend_ref[N_EXPERTS - 1] // MOE_BLK, N_BLK, body, 0)

        tail("start")
        unused("start")
        tail("wait")
        unused("wait")

    pos = pos_ref[...]
    piota = lax.broadcasted_iota(I32, (LROWS, MERGE_TM), 0)
    sel = (piota == pos[0:1, :]).astype(F32)
    for k in range(1, TOP_K):
        sel = sel + (piota == pos[k:k + 1, :]).astype(F32)
    staged[slot] = _mm(sel, h2_ref[...])

    _seg_copies(seg_ref, j, xs_ref, staged.at[slot], True, sems.at[1 + slot], "start")

    @pl.when(j > 0)
    def _():
        _seg_copies(seg_ref, j - 1, xs_ref, staged.at[1 - slot], True, sems.at[2 - slot], "wait")

    @pl.when(j == N_TILES - 1)
    def _():
        _seg_copies(seg_ref, j, xs_ref, staged.at[slot], True, sems.at[1 + slot], "wait")


def _dispatch(pends, padded, segtab, pos, h2):
    grid_spec = pltpu.PrefetchScalarGridSpec(
        num_scalar_prefetch=3,
        grid=(N_TILES,),
        in_specs=[
            pl.BlockSpec((TOP_K, MERGE_TM), lambda j, pe, pa, sg: (0, j)),
            pl.BlockSpec((MERGE_TM, D_MODEL), lambda j, pe, pa, sg: (j, 0)),
        ],
        out_specs=pl.BlockSpec(memory_space=pl.ANY),
        scratch_shapes=[pltpu.VMEM((2, LROWS, D_MODEL), F32),
                        pltpu.SemaphoreType.DMA((3,))],
    )
    return pl.pallas_call(
        _dispatch_kernel,
        grid_spec=grid_spec,
        out_shape=jax.ShapeDtypeStruct((N_SLOTS, D_MODEL), F32),
        compiler_params=pltpu.CompilerParams(
            dimension_semantics=("arbitrary",), vmem_limit_bytes=VMEM_LIMIT),
        name="dispatch",
    )(pends, padded, segtab, pos, h2)


SPLIT_TN = 1024
SPLIT_SUB = 256


def _split_kernel(w_ref, p_ref, wg_ref, wl_ref):
    half = SPLIT_SUB // 2
    for s in range(SPLIT_TN // SPLIT_SUB):
        r = _mm(w_ref[:, s * SPLIT_SUB:(s + 1) * SPLIT_SUB], p_ref[...])
        wg_ref[:, s * half:(s + 1) * half] = r[:, :half].astype(BF16)
        wl_ref[:, s * half:(s + 1) * half] = r[:, half:].astype(BF16)


def _split_gate_up(w_gu):
    col = jnp.arange(SPLIT_SUB)
    src = jnp.where(col < SPLIT_SUB // 2, 2 * col, 2 * (col - SPLIT_SUB // 2) + 1)
    perm = (jnp.arange(SPLIT_SUB)[:, None] == src[None, :]).astype(BF16)
    out = jax.ShapeDtypeStruct((N_EXPERTS, D_MODEL, D_EXPERT), BF16)
    return pl.pallas_call(
        _split_kernel,
        grid=(N_EXPERTS, 2 * D_EXPERT // SPLIT_TN),
        in_specs=[pl.BlockSpec((None, D_MODEL, SPLIT_TN), lambda e, t: (e, 0, t)),
                  pl.BlockSpec((SPLIT_SUB, SPLIT_SUB), lambda e, t: (0, 0))],
        out_specs=[pl.BlockSpec((None, D_MODEL, SPLIT_TN // 2), lambda e, t: (e, 0, t)),
                   pl.BlockSpec((None, D_MODEL, SPLIT_TN // 2), lambda e, t: (e, 0, t))],
        out_shape=[out, out],
        compiler_params=pltpu.CompilerParams(
            dimension_semantics=("arbitrary", "arbitrary"), vmem_limit_bytes=VMEM_LIMIT),
        name="split_gate_up",
    )(w_gu, perm)


def _expert_kernel(be_ref, nused_ref, xs_ref, wg_ref, wl_ref, bg_ref, bl_ref, wd_ref, bd_ref, o_ref):
    del be_ref
    i = pl.program_id(0)

    @pl.when(i < nused_ref[0])
    def _():
        xb = xs_ref[...].astype(BF16)
        g = jnp.minimum(_mm(xb, wg_ref[...]) + bg_ref[...], SWIGLU_LIMIT)
        l = jnp.clip(_mm(xb, wl_ref[...]) + bl_ref[...], -SWIGLU_LIMIT, SWIGLU_LIMIT)
        act = g * jax.nn.sigmoid(SWIGLU_ALPHA * g) * (l + 1.0)
        o_ref[...] = _mm(act, wd_ref[...]) + bd_ref[...]

    @pl.when(i >= nused_ref[0])
    def _():
        o_ref[...] = jnp.zeros_like(o_ref)


def _experts(block_expert, nused, xs, wg, wl, bg, bl, wd, bd):
    blk = lambda i, be, nu: (jnp.minimum(i, nu[0] - 1), 0)
    wspec = pl.BlockSpec((None, D_MODEL, D_EXPERT), lambda i, be, nu: (be[i], 0, 0))
    bspec = pl.BlockSpec((None, 1, D_EXPERT), lambda i, be, nu: (be[i], 0, 0))
    grid_spec = pltpu.PrefetchScalarGridSpec(
        num_scalar_prefetch=2,
        grid=(N_BLK,),
        in_specs=[pl.BlockSpec((MOE_BLK, D_MODEL), blk),
                  wspec, wspec, bspec, bspec, wspec, bspec],
        out_specs=pl.BlockSpec((MOE_BLK, D_MODEL), lambda i, be, nu: (i, 0)),
    )
    return pl.pallas_call(
        _expert_kernel,
        grid_spec=grid_spec,
        out_shape=jax.ShapeDtypeStruct((N_SLOTS, D_MODEL), F32),
        compiler_params=pltpu.CompilerParams(
            dimension_semantics=("arbitrary",), vmem_limit_bytes=VMEM_LIMIT),
        name="experts",
    )(block_expert, nused, xs, wg, wl, bg, bl, wd, bd)


def _combine_kernel(seg_ref, eo_ref, xnew_ref, g2_ref, pos_ref, wt_ref, o_ref, gbuf, sems):
    j = pl.program_id(0)
    slot = lax.rem(j, 2)

    def fetch(tile, sl):
        _seg_copies(seg_ref, tile, eo_ref, gbuf.at[sl], False, sems.at[sl], "start")

    @pl.when(j == 0)
    def _():
        gbuf[...] = jnp.zeros_like(gbuf)
        fetch(0, 0)

    @pl.when(j + 1 < N_TILES)
    def _():
        fetch(j + 1, 1 - slot)

    _seg_copies(seg_ref, j, eo_ref, gbuf.at[slot], False, sems.at[slot], "wait")

    pos = pos_ref[...]
    wt = wt_ref[...]
    lane = lax.broadcasted_iota(I32, (MERGE_TM, LROWS), 1)
    wsel = jnp.where(lane == pos[:, 0:1], wt[:, 0:1], 0.0)
    for k in range(1, TOP_K):
        wsel = wsel + jnp.where(lane == pos[:, k:k + 1], wt[:, k:k + 1], 0.0)
    acc = _mm(wsel, gbuf[slot])
    o_ref[...] = xnew_ref[...] + g2_ref[...] * acc


def _combine(segtab, eo, x_new, mod3, pos_t, wts_t):
    tiles_per_batch = SEQ // MERGE_TM
    grid_spec = pltpu.PrefetchScalarGridSpec(
        num_scalar_prefetch=1,
        grid=(N_TILES,),
        in_specs=[
            pl.BlockSpec(memory_space=pl.ANY),
            pl.BlockSpec((MERGE_TM, D_MODEL), lambda j, sg: (j, 0)),
            pl.BlockSpec((None, 1, D_MODEL), lambda j, sg: (j // tiles_per_batch, 0, 5)),
            pl.BlockSpec((MERGE_TM, TOP_K), lambda j, sg: (j, 0)),
            pl.BlockSpec((MERGE_TM, TOP_K), lambda j, sg: (j, 0)),
        ],
        out_specs=pl.BlockSpec((MERGE_TM, D_MODEL), lambda j, sg: (j, 0)),
        scratch_shapes=[pltpu.VMEM((2, LROWS, D_MODEL), F32),
                        pltpu.SemaphoreType.DMA((2,))],
    )
    return pl.pallas_call(
        _combine_kernel,
        grid_spec=grid_spec,
        out_shape=jax.ShapeDtypeStruct((N_TOK, D_MODEL), F32),
        compiler_params=pltpu.CompilerParams(
            dimension_semantics=("arbitrary",), vmem_limit_bytes=VMEM_LIMIT),
        name="combine",
    )(segtab, eo, x_new, mod3, pos_t, wts_t)


def _pad_heads(w, heads, dim):
    lead = w.shape[:-1]
    w = w.reshape(lead + (heads, dim))
    w = jnp.pad(w, [(0, 0)] * len(lead) + [(0, 0), (0, LANES - dim)])
    return w.reshape(lead + (heads * LANES,))


def _rope_tables(dim, offset):
    pos = jnp.arange(SEQ, dtype=jnp.int32)
    row = (pos // GRID_W).astype(F32)
    col = (pos % GRID_W).astype(F32)
    q = dim // 4
    freqs = ROPE_THETA ** (-jnp.arange(q, dtype=F32) / q)
    ang_r = row[:, None] * freqs
    ang_c = col[:, None] * freqs
    ang = jnp.concatenate([ang_r, ang_r, ang_c, ang_c], axis=-1)
    cos, sin = jnp.cos(ang), jnp.sin(ang)
    first = (jnp.arange(dim) % (2 * q)) < q
    sa = jnp.where(first, -sin, 0.0)
    sb = jnp.where(first, 0.0, sin)
    pad = lambda t, v: jnp.pad(t, ((0, 0), (offset, LANES - offset - dim)), constant_values=v)
    return pad(cos, 1.0), pad(sa, 0.0), pad(sb, 0.0)


def kernel(x, c, ctx, c_ctx, w_ada, b_ada, norm1_g, norm2_g, w_in, mla_q_a_g, mla_kv_a_g, w_q_up, w_kv_up,
           mla_q_g, mla_k_g, swa_q_g, swa_k_g, swa_sink, w_branch_a, w_branch_b, w_out, w_router, b_router,
           w_gate_up, b_gate_up, w_down, b_down):
    assert x.shape == (BATCH, SEQ, D_MODEL) and ctx.shape == (BATCH, CTX_LEN, D_MODEL)
    assert w_ada.shape[0] == 1, "single layer"

    cond = jnp.concatenate([c, c_ctx[None], jnp.zeros((MOD_ROWS - BATCH - 1, D_MODEL), F32)], axis=0)
    mod = _ada(cond, w_ada[0], b_ada[0][None])
    mod3 = mod.reshape(MOD_ROWS, 1, N_MOD * D_MODEL)

    wi = w_in[0]
    o1, o2, o3 = MLA_KV_RANK, MLA_KV_RANK + MLA_ROPE, MLA_KV_RANK + MLA_ROPE + SWA_KV_HEADS * SWA_HD
    kv_cols = o3 + SWA_KV_HEADS * SWA_HD
    q1 = kv_cols + MLA_Q_RANK
    q2 = q1 + SWA_HEADS * SWA_HD
    q3 = q2 + D_MODEL
    krope_grp = jnp.pad(wi[:, o1:o2], ((0, 0), (MLA_NOPE, LANES - MLA_QK)))
    win = jnp.concatenate([
        wi[:, 0:o1], _pad_heads(wi[:, o2:o3], SWA_KV_HEADS, SWA_HD), krope_grp,
        wi[:, kv_cols:q1], _pad_heads(wi[:, q1:q2], SWA_HEADS, SWA_HD), wi[:, q2:q3], wi[:, q3:]],
        axis=1).astype(BF16)
    assert win.shape == (D_MODEL, C_END)
    wvst = wi[:, o3:kv_cols].T.astype(BF16)
    wkv = w_kv_up[0].reshape(MLA_KV_RANK, MLA_HEADS, MLA_NOPE + MLA_V)
    wkn = _pad_heads(wkv[:, :, :MLA_NOPE].reshape(MLA_KV_RANK, -1), MLA_HEADS, MLA_NOPE).astype(BF16)
    wvt = wkv[:, :, MLA_NOPE:].reshape(MLA_KV_RANK, MLA_HEADS * MLA_V).T.astype(BF16)
    wqup = _pad_heads(w_q_up[0], MLA_HEADS, MLA_QK).astype(BF16)
    gk_t = _pad_heads(jnp.tile(mla_k_g[0], MLA_HEADS)[None], MLA_HEADS, MLA_QK)
    gq_t = _pad_heads(jnp.tile(mla_q_g[0] * (MLA_QK ** -0.5 * LOG2E), MLA_HEADS)[None], MLA_HEADS, MLA_QK)
    gks_t = _pad_heads(jnp.tile(swa_k_g[0], SWA_KV_HEADS)[None], SWA_KV_HEADS, SWA_HD)
    gqs_t = _pad_heads(jnp.tile(swa_q_g[0] * (SWA_HD ** -0.5 * LOG2E), SWA_HEADS)[None], SWA_HEADS, SWA_HD)
    head_of_lane = jnp.arange(MLA_HEADS * LANES) // LANES
    ind = (head_of_lane[:, None] == jnp.arange(LANES)[None, :]).astype(BF16)
    indt = ind.T
    consts = [norm1_g[0][None], mla_kv_a_g[0][None], wkn, wvt, wvst, gk_t, gks_t, ind, indt]
    q_consts = [mla_q_a_g[0][None], wqup, gq_t, gqs_t]

    tabs_lat = _rope_tables(MLA_ROPE, MLA_NOPE) + _rope_tables(SWA_HD, 0)
    ones = jnp.ones((PROJ_TM, LANES), F32)
    zeros = jnp.zeros((PROJ_TM, LANES), F32)
    tabs_ctx = (ones, zeros, zeros, ones, zeros, zeros)

    x2d = x.reshape(N_TOK, D_MODEL)
    tiles_per_batch = SEQ // PROJ_TM
    k_mla, vt_mla, k_swa, vt_swa, q_mla, q_swa, sig_a, sig_b = _proj(
        x2d, mod3, lambda i: i // tiles_per_batch, lambda i: i % tiles_per_batch,
        tabs_lat, consts, q_consts, win, True, "proj_latent")
    kc_mla, vct_mla, kc_swa, vct_swa = _proj(
        ctx.reshape(BATCH * CTX_LEN, D_MODEL), mod3, lambda i: BATCH, lambda i: 0,
        tabs_ctx, consts, q_consts, win[:, :C_KVEND], False, "proj_ctx")

    o_a = _mla_attn(q_mla, k_mla, vt_mla, kc_mla, vct_mla)
    o_b = _swa_attn(swa_sink[0], q_swa, k_swa, vt_swa, kc_swa, vct_swa)

    x_new, h2, pos, top_w, seg = _merge(
        o_a, o_b, sig_a, sig_b, x2d, mod3,
        w_branch_a[0].astype(BF16), w_branch_b[0].astype(BF16), w_out[0].astype(BF16),
        norm2_g[0][None], w_router[0].T, b_router[0][:, None])

    seg3 = seg.reshape(N_TILES, SUBLANES, LANES)
    lstart, rows, gcarry = (seg3[:, r, :N_EXPERTS] for r in range(3))
    total = gcarry[-1] + rows[-1]
    padded = (total + MOE_BLK - 1) // MOE_BLK * MOE_BLK
    pends = jnp.cumsum(padded).astype(I32)
    pstarts = pends - padded
    nused = (pends[-1] // MOE_BLK).astype(I32)[None]
    blk_start = jnp.arange(N_BLK, dtype=I32) * MOE_BLK
    block_expert = jnp.minimum(
        jnp.sum((blk_start[:, None] >= pends[None, :]).astype(I32), axis=1), N_EXPERTS - 1)
    segtab = jnp.concatenate(
        [lstart, rows, pstarts[None, :] + gcarry, jnp.zeros_like(rows)], axis=1).reshape(-1).astype(I32)

    xs = _dispatch(pends, padded.astype(I32), segtab, pos, h2)
    wg, wl = _split_gate_up(w_gate_up[0])
    bgu = b_gate_up[0]
    eo = _experts(block_expert, nused, xs,
                  wg, wl, bgu[:, None, 0::2], bgu[:, None, 1::2],
                  w_down[0].astype(BF16), b_down[0][:, None, :])
    out = _combine(segtab, eo, x_new, mod3, pos.T, top_w.T)
    return out.reshape(BATCH, SEQ, D_MODEL)
```

```python
import functools

import jax
import jax.numpy as jnp
from jax import lax
from jax.experimental import pallas as pl
from jax.experimental.pallas import tpu as pltpu

F32 = jnp.float32
BF16 = jnp.bfloat16
I32 = jnp.int32

LANES = 128
SUBLANES = 8

D_MODEL = 1024
BATCH = 8
SEQ = 4096
CTX_LEN = 256
GRID_W = 64
ROPE_THETA = 10000.0
NORM_EPS = 1e-6
N_MOD = 6
NEG_INF = -1e30
LOG2E = 1.4426950408889634

MLA_HEADS = 8
MLA_NOPE = 64
MLA_ROPE = 32
MLA_QK = MLA_NOPE + MLA_ROPE
MLA_V = 64
MLA_Q_RANK = 384
MLA_KV_RANK = 256

SWA_HEADS = 8
SWA_KV_HEADS = 2
SWA_GROUP = SWA_HEADS // SWA_KV_HEADS
SWA_HD = 64
WINDOW = 128

N_EXPERTS = 32
TOP_K = 4
D_EXPERT = D_MODEL
SWIGLU_LIMIT = 7.0
SWIGLU_ALPHA = 1.702

N_TOK = BATCH * SEQ
MOD_ROWS = 16

ADA_TN = 1536
PROJ_TM = 256
MLA_TQ = 256
SWA_TQ = 256
SWA_KW = SWA_TQ + 2 * WINDOW
MERGE_TM = 256
MOE_BLK = 512
SEG_ALIGN = SUBLANES
N_TILES = N_TOK // MERGE_TM
LROWS = -(-(MERGE_TM * TOP_K + N_EXPERTS * (SEG_ALIGN - 1)) // LANES) * LANES
SEG_BITS = tuple(range(SEG_ALIGN.bit_length() - 1, (MERGE_TM * TOP_K).bit_length()))
SEG_W = 128
SEG_TOTAL = 3 * N_EXPERTS
SEG_BIG = 128
N_SLOTS = -(-(N_TOK * TOP_K + N_TILES * N_EXPERTS * (SEG_ALIGN - 1) + N_EXPERTS * (MOE_BLK - 1))
            // MOE_BLK) * MOE_BLK
N_BLK = N_SLOTS // MOE_BLK

VMEM_LIMIT = 56 * 1024 * 1024

C_KVLAT = 0
C_KSWA = C_KVLAT + MLA_KV_RANK
C_KROPE = C_KSWA + SWA_KV_HEADS * LANES
C_KVEND = C_KROPE + LANES
C_QLAT = C_KVEND
C_QSWA = C_QLAT + MLA_Q_RANK
C_GA = C_QSWA + SWA_HEADS * LANES
C_GB = C_GA + D_MODEL
C_END = C_GB + D_MODEL


def _mm(a, b):
    return jnp.dot(a.astype(BF16), b.astype(BF16), preferred_element_type=F32)


def _mm_nt(a, b):
    return lax.dot_general(a.astype(BF16), b.astype(BF16), (((1,), (1,)), ((), ())),
                           preferred_element_type=F32)


def _mm_tn(a, b):
    return lax.dot_general(a.astype(BF16), b.astype(BF16), (((0,), (0,)), ((), ())),
                           preferred_element_type=F32)


def _split(a):
    hi = a.astype(BF16)
    lo = (a - hi.astype(F32)).astype(BF16)
    return hi, lo


def _rms(x):
    return x * lax.rsqrt(jnp.mean(x * x, axis=-1, keepdims=True) + NORM_EPS)


def _ada_kernel(c_ref, w_ref, b_ref, o_ref):
    c = c_ref[...]
    s = c * jax.nn.sigmoid(c)
    shi, slo = _split(s)
    whi, wlo = _split(w_ref[...])
    acc = _mm(shi, whi) + _mm(slo, whi) + _mm(shi, wlo)
    o_ref[...] = acc + b_ref[...]


def _ada(cond, w_ada, b_ada):
    n = w_ada.shape[1]
    return pl.pallas_call(
        _ada_kernel,
        grid=(n // ADA_TN,),
        in_specs=[
            pl.BlockSpec((MOD_ROWS, D_MODEL), lambda j: (0, 0)),
            pl.BlockSpec((D_MODEL, ADA_TN), lambda j: (0, j)),
            pl.BlockSpec((1, ADA_TN), lambda j: (0, j)),
        ],
        out_specs=pl.BlockSpec((MOD_ROWS, ADA_TN), lambda j: (0, j)),
        out_shape=jax.ShapeDtypeStruct((MOD_ROWS, n), F32),
        compiler_params=pltpu.CompilerParams(
            dimension_semantics=("arbitrary",), vmem_limit_bytes=VMEM_LIMIT),
        name="ada",
    )(cond, w_ada, b_ada)


def _head_norm(xraw, ind_ref, indt_ref, inv_dim):
    w = xraw.shape[1]
    ss = _mm(xraw * xraw, ind_ref[0:w, :])
    r = lax.rsqrt(ss * inv_dim + NORM_EPS)
    scale = _mm(jnp.concatenate(_split(r), axis=1), indt_ref[:, 0:w])
    return xraw * scale


def _rope(xh, cos, sa, sb, quarter):
    return (xh * cos + pltpu.roll(xh, LANES - quarter, 1) * sa
            + pltpu.roll(xh, quarter, 1) * sb)


def _proj_kernel(*refs, with_q):
    (x_ref, sh_ref, sc_ref, g1_ref, win_ref, cm_ref, sam_ref, sbm_ref, cs_ref, sas_ref, sbs_ref,
     gkva_ref, wkn_ref, wvt_ref, wvst_ref, gk_ref, gks_ref, ind_ref, indt_ref) = refs[:19]
    if with_q:
        gqa_ref, wqup_ref, gq_ref, gqs_ref = refs[19:23]
        kmla_ref, vmla_ref, kswa_ref, vswa_ref, qmla_ref, qswa_ref, siga_ref, sigb_ref = refs[23:]
    else:
        kmla_ref, vmla_ref, kswa_ref, vswa_ref = refs[19:]

    x = x_ref[...]
    h = _rms(x) * g1_ref[...] * (1.0 + sc_ref[...]) + sh_ref[...]
    y = _mm(h, win_ref[...])

    cm, sam, sbm = cm_ref[...], sam_ref[...], sbm_ref[...]
    cs, sas, sbs = cs_ref[...], sas_ref[...], sbs_ref[...]

    kvn = _rms(y[:, C_KVLAT:C_KVLAT + MLA_KV_RANK]) * gkva_ref[...]
    kn = _mm(kvn, wkn_ref[...])
    vmla_ref[...] = _mm_nt(wvt_ref[...], kvn).astype(BF16)
    vswa_ref[...] = _mm_nt(wvst_ref[...], h).astype(BF16)
    kr = y[:, C_KROPE:C_KROPE + LANES]
    kraw = kn + jnp.concatenate([kr] * MLA_HEADS, axis=1)
    kfull = _head_norm(kraw, ind_ref, indt_ref, 1.0 / MLA_QK) * gk_ref[...]
    for hd in range(MLA_HEADS):
        sl = slice(hd * LANES, (hd + 1) * LANES)
        kmla_ref[:, sl] = _rope(kfull[:, sl], cm, sam, sbm, MLA_ROPE // 4).astype(BF16)

    ks = _head_norm(y[:, C_KSWA:C_KSWA + SWA_KV_HEADS * LANES], ind_ref, indt_ref,
                    1.0 / SWA_HD) * gks_ref[...]
    for hd in range(SWA_KV_HEADS):
        sl = slice(hd * LANES, (hd + 1) * LANES)
        kswa_ref[:, sl] = _rope(ks[:, sl], cs, sas, sbs, SWA_HD // 4).astype(BF16)

    if with_q:
        qn = _rms(y[:, C_QLAT:C_QLAT + MLA_Q_RANK]) * gqa_ref[...]
        qraw = _mm(qn, wqup_ref[...])
        qf = _head_norm(qraw, ind_ref, indt_ref, 1.0 / MLA_QK) * gq_ref[...]
        for hd in range(MLA_HEADS):
            sl = slice(hd * LANES, (hd + 1) * LANES)
            qmla_ref[:, sl] = _rope(qf[:, sl], cm, sam, sbm, MLA_ROPE // 4).astype(BF16)
        qs = _head_norm(y[:, C_QSWA:C_QSWA + SWA_HEADS * LANES], ind_ref, indt_ref,
                        1.0 / SWA_HD) * gqs_ref[...]
        for hd in range(SWA_HEADS):
            sl = slice(hd * LANES, (hd + 1) * LANES)
            qswa_ref[:, sl] = _rope(qs[:, sl], cs, sas, sbs, SWA_HD // 4).astype(BF16)
        siga_ref[...] = jax.nn.sigmoid(y[:, C_GA:C_GA + D_MODEL]).astype(BF16)
        sigb_ref[...] = jax.nn.sigmoid(y[:, C_GB:C_GB + D_MODEL]).astype(BF16)


def _proj(x2d, mod3, mod_row_fn, tab_row_fn, tabs, consts, q_consts, win, with_q, name):
    rows = x2d.shape[0]
    tm = PROJ_TM
    const = lambda shape: pl.BlockSpec(shape, lambda i: (0,) * len(shape))
    tab = pl.BlockSpec((tm, LANES), lambda i: (tab_row_fn(i), 0))
    g1, rest = consts[0], consts[1:]
    in_specs = [
        pl.BlockSpec((tm, D_MODEL), lambda i: (i, 0)),
        pl.BlockSpec((None, 1, D_MODEL), lambda i: (mod_row_fn(i), 0, 0)),
        pl.BlockSpec((None, 1, D_MODEL), lambda i: (mod_row_fn(i), 0, 1)),
        const(g1.shape),
        pl.BlockSpec(win.shape, lambda i: (0, 0), pipeline_mode=pl.Buffered(1)),
        tab, tab, tab, tab, tab, tab,
    ] + [const(c.shape) for c in rest]
    args = [x2d, mod3, mod3, g1, win] + list(tabs) + list(rest)
    outs = [(MLA_HEADS * LANES, False), (MLA_HEADS * MLA_V, True),
            (SWA_KV_HEADS * LANES, False), (SWA_KV_HEADS * SWA_HD, True)]
    if with_q:
        in_specs += [const(c.shape) for c in q_consts]
        args += list(q_consts)
        outs += [(MLA_HEADS * LANES, False), (SWA_HEADS * LANES, False), (D_MODEL, False), (D_MODEL, False)]
    return pl.pallas_call(
        functools.partial(_proj_kernel, with_q=with_q),
        grid=(rows // tm,),
        in_specs=in_specs,
        out_specs=[pl.BlockSpec((w, tm), lambda i: (0, i)) if t else pl.BlockSpec((tm, w), lambda i: (i, 0))
                   for w, t in outs],
        out_shape=[jax.ShapeDtypeStruct((w, rows) if t else (rows, w), BF16) for w, t in outs],
        compiler_params=pltpu.CompilerParams(
            dimension_semantics=("arbitrary",), vmem_limit_bytes=VMEM_LIMIT),
        name=name,
    )(*args)


def _mla_step(q_ref, kl_ref, vlt_ref, kc_ref, vct_ref, o_ref, cur, prev):
    s_lat_c, s_ctx_c, m_c = cur
    s_lat_p, s_ctx_p, m_p = prev
    outs = []
    for hh in range(2):
        sl = slice(hh * LANES, (hh + 1) * LANES)
        vrows = slice(hh * MLA_V, (hh + 1) * MLA_V)
        q = q_ref[:, sl]
        s1 = _mm_nt(kl_ref[:, sl], q)
        s2 = _mm_nt(kc_ref[:, sl], q)
        s_lat_c[hh] = s1
        s_ctx_c[hh] = s2
        m_c[hh] = jnp.maximum(jnp.max(s1, axis=0, keepdims=True),
                              jnp.max(s2, axis=0, keepdims=True))

        m = m_p[hh]
        p1 = jnp.exp2(s_lat_p[hh] - m)
        p2 = jnp.exp2(s_ctx_p[hh] - m)
        l = jnp.sum(p1, axis=0, keepdims=True) + jnp.sum(p2, axis=0, keepdims=True)
        o = _mm(vlt_ref[vrows, :], p1) + _mm(vct_ref[vrows, :], p2)
        outs.append(o / l)
    o_ref[...] = jnp.concatenate(outs, axis=0).astype(BF16)


def _mla_kernel(q_ref, kl_ref, vl_ref, kc_ref, vc_ref, o_ref, *scratch):
    set0, set1 = scratch[:3], scratch[3:]
    i = pl.program_id(2)

    @pl.when(i == 0)
    def _():
        for ref in set1:
            ref[...] = jnp.zeros(ref.shape, F32)

    @pl.when(lax.rem(i, 2) == 0)
    def _():
        _mla_step(q_ref, kl_ref, vl_ref, kc_ref, vc_ref, o_ref, set0, set1)

    @pl.when(lax.rem(i, 2) == 1)
    def _():
        _mla_step(q_ref, kl_ref, vl_ref, kc_ref, vc_ref, o_ref, set1, set0)


def _mla_attn(q, k_lat, vt_lat, k_ctx, vt_ctx):
    tq = MLA_TQ
    nq = SEQ // tq
    return pl.pallas_call(
        _mla_kernel,
        grid=(BATCH, MLA_HEADS // 2, nq + 1),
        in_specs=[
            pl.BlockSpec((tq, 2 * LANES), lambda b, hp, i: (b * nq + jnp.minimum(i, nq - 1), hp)),
            pl.BlockSpec((SEQ, 2 * LANES), lambda b, hp, i: (b, hp)),
            pl.BlockSpec((2 * MLA_V, SEQ), lambda b, hp, i: (hp, b)),
            pl.BlockSpec((CTX_LEN, 2 * LANES), lambda b, hp, i: (b, hp)),
            pl.BlockSpec((2 * MLA_V, CTX_LEN), lambda b, hp, i: (hp, b)),
        ],
        out_specs=pl.BlockSpec((2 * MLA_V, tq), lambda b, hp, i: (hp, b * nq + jnp.maximum(i - 1, 0))),
        out_shape=jax.ShapeDtypeStruct((MLA_HEADS * MLA_V, N_TOK), BF16),
        scratch_shapes=2 * [pltpu.VMEM((2, SEQ, tq), F32),
                            pltpu.VMEM((2, CTX_LEN, tq), F32),
                            pltpu.VMEM((2, 1, tq), F32)],
        compiler_params=pltpu.CompilerParams(
            dimension_semantics=("arbitrary", "arbitrary", "arbitrary"),
            vmem_limit_bytes=VMEM_LIMIT),
        name="mla_attn",
    )(q, k_lat, vt_lat, k_ctx, vt_ctx)


def _swa_kernel(sink_ref, q_ref, k_ref, vt_ref, kc_ref, vct_ref, o_ref):
    tq = SWA_TQ
    i = pl.program_id(1)
    s0 = i * tq
    kstart = pl.multiple_of(jnp.clip(s0 - WINDOW, 0, SEQ - SWA_KW), LANES)
    kwin = k_ref[pl.ds(kstart, SWA_KW), :]
    vtwin = vt_ref[:, pl.ds(kstart, SWA_KW)]
    kc = kc_ref[...]
    vct = vct_ref[...]
    col = lax.broadcasted_iota(I32, (SWA_KW, SWA_GROUP * tq), 1)
    qpos = s0 + (col & (tq - 1))
    kpos = kstart + lax.broadcasted_iota(I32, (SWA_KW, SWA_GROUP * tq), 0)
    allowed = jnp.abs(qpos - kpos) <= WINDOW
    for kh in range(SWA_KV_HEADS):
        ksl = slice(kh * LANES, (kh + 1) * LANES)
        vrows = slice(kh * SWA_HD, (kh + 1) * SWA_HD)
        q4 = jnp.concatenate(
            [q_ref[:, (kh * SWA_GROUP + g) * LANES:(kh * SWA_GROUP + g + 1) * LANES]
             for g in range(SWA_GROUP)], axis=0)
        sb = jnp.where(allowed, _mm_nt(kwin[:, ksl], q4), NEG_INF)
        sc = _mm_nt(kc[:, ksl], q4)
        sink = jnp.concatenate(
            [jnp.full((1, tq), sink_ref[kh * SWA_GROUP + g] * LOG2E, F32) for g in range(SWA_GROUP)],
            axis=1)
        m = jnp.maximum(jnp.maximum(jnp.max(sb, axis=0, keepdims=True),
                                    jnp.max(sc, axis=0, keepdims=True)), sink)
        pb = jnp.exp2(sb - m)
        pc = jnp.exp2(sc - m)
        l = (jnp.sum(pb, axis=0, keepdims=True) + jnp.sum(pc, axis=0, keepdims=True)
             + jnp.exp2(sink - m))
        o = (_mm(vtwin[vrows, :], pb) + _mm(vct[vrows, :], pc)) / l
        for g in range(SWA_GROUP):
            hd = kh * SWA_GROUP + g
            o_ref[hd * SWA_HD:(hd + 1) * SWA_HD, :] = o[:, g * tq:(g + 1) * tq].astype(BF16)


def _swa_attn(sink, q, k_lat, vt_lat, k_ctx, vt_ctx):
    tq = SWA_TQ
    nq = SEQ // tq
    return pl.pallas_call(
        _swa_kernel,
        grid=(BATCH, nq),
        in_specs=[
            pl.BlockSpec(memory_space=pltpu.SMEM),
            pl.BlockSpec((tq, SWA_HEADS * LANES), lambda b, i: (b * nq + i, 0)),
            pl.BlockSpec((SEQ, SWA_KV_HEADS * LANES), lambda b, i: (b, 0)),
            pl.BlockSpec((SWA_KV_HEADS * SWA_HD, SEQ), lambda b, i: (0, b)),
            pl.BlockSpec((CTX_LEN, SWA_KV_HEADS * LANES), lambda b, i: (b, 0)),
            pl.BlockSpec((SWA_KV_HEADS * SWA_HD, CTX_LEN), lambda b, i: (0, b)),
        ],
        out_specs=pl.BlockSpec((SWA_HEADS * SWA_HD, tq), lambda b, i: (0, b * nq + i)),
        out_shape=jax.ShapeDtypeStruct((SWA_HEADS * SWA_HD, N_TOK), BF16),
        compiler_params=pltpu.CompilerParams(
            dimension_semantics=("arbitrary", "arbitrary"), vmem_limit_bytes=VMEM_LIMIT),
        name="swa_attn",
    )(sink, q, k_lat, vt_lat, k_ctx, vt_ctx)


def _merge_kernel(oa_ref, ob_ref, sa_ref, sb_ref, x_ref, g1_ref, sh2_ref, sc2_ref, wba_ref, wbb_ref,
                  wout_ref, g2_ref, wr_ref, br_ref, utri_ref, ltri_ref,
                  xnew_ref, h2_ref, pos_ref, wts_ref, seg_ref, carry_ref):
    i = pl.program_id(0)

    @pl.when(i == 0)
    def _():
        carry_ref[...] = jnp.zeros_like(carry_ref)

    ya = _mm_tn(oa_ref[...], wba_ref[...])
    yb = _mm_tn(ob_ref[...], wbb_ref[...])
    y = sa_ref[...].astype(F32) * ya + sb_ref[...].astype(F32) * yb
    z = _mm(y, wout_ref[...])
    xn = x_ref[...] + g1_ref[...] * z
    xnew_ref[...] = xn
    h2 = _rms(xn) * g2_ref[...] * (1.0 + sc2_ref[...]) + sh2_ref[...]
    h2_ref[...] = h2.astype(BF16)

    hhi, hlo = _split(h2)
    whi, wlo = _split(wr_ref[...])
    lg = _mm_nt(whi, hhi) + _mm_nt(whi, hlo) + _mm_nt(wlo, hhi) + br_ref[...]
    eiota = lax.broadcasted_iota(I32, lg.shape, 0).astype(F32)
    vals, onehots = [], []
    for k in range(TOP_K):
        m = jnp.max(lg, axis=0, keepdims=True)
        ik = jnp.min(jnp.where(lg == m, eiota, float(N_EXPERTS)), axis=0, keepdims=True)
        hit = eiota == ik
        vals.append(m)
        onehots.append(hit.astype(F32))
        lg = jnp.where(hit, -jnp.inf, lg)
    es = [jnp.exp(v - vals[0]) for v in vals]
    tot = es[0] + es[1] + es[2] + es[3]
    for k in range(TOP_K):
        wts_ref[k:k + 1, :] = es[k] / tot

    tots = [jnp.sum(oh, axis=1, keepdims=True) for oh in onehots]
    n = tots[0] + tots[1] + tots[2] + tots[3]
    m_al = jnp.floor((n + (SEG_ALIGN - 1)) * (1.0 / SEG_ALIGN))
    m_b = jnp.broadcast_to(m_al, (N_EXPERTS, LANES))
    lstart = _mm(ltri_ref[...], m_b) * float(SEG_ALIGN)
    off = jnp.zeros_like(n)
    for k in range(TOP_K):
        prefix = _mm(onehots[k], utri_ref[...])
        lp = jnp.sum(onehots[k] * (lstart[:, 0:1] + off + prefix), axis=0, keepdims=True)
        pos_ref[k:k + 1, :] = lp.astype(I32)
        off = off + tots[k]
    eye = (lax.broadcasted_iota(I32, (N_EXPERTS, LANES), 0)
           == lax.broadcasted_iota(I32, (N_EXPERTS, LANES), 1)).astype(F32)
    to_row = lambda v: jnp.sum(v * eye, axis=0, keepdims=True).astype(I32)
    carry = carry_ref[...]
    seg_ref[...] = jnp.zeros_like(seg_ref)
    seg_ref[0:1, :] = to_row(lstart)
    seg_ref[1:2, :] = to_row(m_b * float(SEG_ALIGN))
    seg_ref[2:3, :] = to_row(carry)
    carry_ref[...] = carry + m_b * float(SEG_ALIGN)


def _merge(o_a, o_b, sig_a, sig_b, x2d, mod3, w_ba, w_bb, w_out, g2, w_rt, b_r):
    tm = MERGE_TM
    tiles_per_batch = SEQ // tm
    const = lambda shape: pl.BlockSpec(shape, lambda i: (0,) * len(shape))
    modspec = lambda j: pl.BlockSpec((None, 1, D_MODEL), lambda i: (i // tiles_per_batch, 0, j))
    row = lambda w: pl.BlockSpec((tm, w), lambda i: (i, 0))
    utri = (jnp.arange(tm)[:, None] < jnp.arange(tm)[None, :]).astype(BF16)
    ltri = (jnp.arange(N_EXPERTS)[None, :] < jnp.arange(N_EXPERTS)[:, None]).astype(BF16)
    return pl.pallas_call(
        _merge_kernel,
        grid=(N_TILES,),
        in_specs=[pl.BlockSpec((512, tm), lambda i: (0, i)), pl.BlockSpec((512, tm), lambda i: (0, i)),
                  row(D_MODEL), row(D_MODEL), row(D_MODEL),
                  modspec(2), modspec(3), modspec(4),
                  const(w_ba.shape), const(w_bb.shape), const(w_out.shape), const(g2.shape),
                  const(w_rt.shape), const(b_r.shape), const(utri.shape), const(ltri.shape)],
        out_specs=[row(D_MODEL), row(D_MODEL),
                   pl.BlockSpec((TOP_K, tm), lambda i: (0, i)),
                   pl.BlockSpec((TOP_K, tm), lambda i: (0, i)),
                   pl.BlockSpec((SUBLANES, LANES), lambda i: (i, 0))],
        out_shape=[jax.ShapeDtypeStruct((N_TOK, D_MODEL), F32),
                   jax.ShapeDtypeStruct((N_TOK, D_MODEL), BF16),
                   jax.ShapeDtypeStruct((TOP_K, N_TOK), I32),
                   jax.ShapeDtypeStruct((TOP_K, N_TOK), F32),
                   jax.ShapeDtypeStruct((N_TILES * SUBLANES, LANES), I32)],
        scratch_shapes=[pltpu.VMEM((N_EXPERTS, LANES), F32)],
        compiler_params=pltpu.CompilerParams(
            dimension_semantics=("arbitrary",), vmem_limit_bytes=VMEM_LIMIT),
        name="merge",
    )(o_a, o_b, sig_a, sig_b, x2d, mod3, mod3, mod3, w_ba, w_bb, w_out, g2, w_rt, b_r, utri, ltri)


def _chunk_copy(hbm, hbm_row, buf, buf_row, size, to_hbm, sem):
    vm = buf.at[pl.ds(pl.multiple_of(buf_row, SEG_ALIGN), size)]
    hb = hbm.at[pl.ds(pl.multiple_of(hbm_row, SEG_ALIGN), size)]
    return pltpu.make_async_copy(vm, hb, sem) if to_hbm else pltpu.make_async_copy(hb, vm, sem)


def _seg_start(seg_smem, tile, hbm, buf, to_hbm, sem):
    base = tile * SEG_W

    def chunks(src, rows, dst, bits):
        for b in bits:
            size = 1 << b
            done = rows & (-2 * size)

            @pl.when((rows & size) != 0)
            def _():
                _chunk_copy(hbm, dst + done, buf, src + done, size, to_hbm, sem).start()

    def per_expert(e, carry):
        src = seg_smem[base + e]
        rows = seg_smem[base + N_EXPERTS + e]
        dst = seg_smem[base + 2 * N_EXPERTS + e]

        @pl.when(rows >= SEG_BIG)
        def _():
            chunks(src, rows, dst, [b for b in SEG_BITS if (1 << b) >= SEG_BIG])

        chunks(src, rows, dst, [b for b in SEG_BITS if (1 << b) < SEG_BIG])
        return carry

    lax.fori_loop(0, N_EXPERTS, per_expert, 0)


def _seg_wait(seg_smem, tile, hbm, buf, to_hbm, sem):
    assert LROWS < 2 << SEG_BITS[-1]
    total = seg_smem[tile * SEG_W + SEG_TOTAL]
    for b in SEG_BITS:
        size = 1 << b

        @pl.when((total & size) != 0)
        def _():
            _chunk_copy(hbm, 0, buf, 0, size, to_hbm, sem).wait()


def _dispatch_kernel(pend_ref, padded_ref, seg_ref, pos_ref, h2_ref, xs_ref, staged, sems):
    j = pl.program_id(0)
    slot = lax.rem(j, 2)

    @pl.when(j == 0)
    def _():
        staged[0, 0:MOE_BLK, :] = jnp.zeros((MOE_BLK, D_MODEL), F32)

        def tail(action):
            def body(e, carry):
                @pl.when(padded_ref[e] > 0)
                def _():
                    start = pl.multiple_of(pend_ref[e] - MOE_BLK, MOE_BLK)
                    cp = pltpu.make_async_copy(staged.at[0, pl.ds(0, MOE_BLK)],
                                               xs_ref.at[pl.ds(start, MOE_BLK)], sems.at[0])
                    getattr(cp, action)()
                return carry
            lax.fori_loop(0, N_EXPERTS, body, 0)

        def unused(action):
            def body(b, carry):
                cp = pltpu.make_async_copy(
                    staged.at[0, pl.ds(0, MOE_BLK)],
                    xs_ref.at[pl.ds(pl.multiple_of(b * MOE_BLK, MOE_BLK), MOE_BLK)], sems.at[0])
                getattr(cp, action)()
                return carry
            lax.fori_loop(pend_ref[N_EXPERTS - 1] // MOE_BLK, N_BLK, body, 0)

        tail("start")
        unused("start")
        tail("wait")
        unused("wait")

    pos = pos_ref[...]
    piota = lax.broadcasted_iota(I32, (LROWS, MERGE_TM), 0)
    sel = (piota == pos[0:1, :]).astype(F32)
    for k in range(1, TOP_K):
        sel = sel + (piota == pos[k:k + 1, :]).astype(F32)
    staged[slot] = _mm(sel, h2_ref[...])

    _seg_start(seg_ref, j, xs_ref, staged.at[slot], True, sems.at[1 + slot])

    @pl.when(j > 0)
    def _():
        _seg_wait(seg_ref, j - 1, xs_ref, staged.at[1 - slot], True, sems.at[2 - slot])

    @pl.when(j == N_TILES - 1)
    def _():
        _seg_wait(seg_ref, j, xs_ref, staged.at[slot], True, sems.at[1 + slot])


def _dispatch(pends, padded, segtab, pos, h2):
    grid_spec = pltpu.PrefetchScalarGridSpec(
        num_scalar_prefetch=3,
        grid=(N_TILES,),
        in_specs=[
            pl.BlockSpec((TOP_K, MERGE_TM), lambda j, pe, pa, sg: (0, j)),
            pl.BlockSpec((MERGE_TM, D_MODEL), lambda j, pe, pa, sg: (j, 0)),
        ],
        out_specs=pl.BlockSpec(memory_space=pl.ANY),
        scratch_shapes=[pltpu.VMEM((2, LROWS, D_MODEL), F32),
                        pltpu.SemaphoreType.DMA((3,))],
    )
    return pl.pallas_call(
        _dispatch_kernel,
        grid_spec=grid_spec,
        out_shape=jax.ShapeDtypeStruct((N_SLOTS, D_MODEL), F32),
        compiler_params=pltpu.CompilerParams(
            dimension_semantics=("arbitrary",), vmem_limit_bytes=VMEM_LIMIT),
        name="dispatch",
    )(pends, padded, segtab, pos, h2)


SPLIT_TN = 1024
SPLIT_SUB = 256


def _split_kernel(w_ref, p_ref, wg_ref, wl_ref):
    half = SPLIT_SUB // 2
    for s in range(SPLIT_TN // SPLIT_SUB):
        r = _mm(w_ref[:, s * SPLIT_SUB:(s + 1) * SPLIT_SUB], p_ref[...])
        wg_ref[:, s * half:(s + 1) * half] = r[:, :half].astype(BF16)
        wl_ref[:, s * half:(s + 1) * half] = r[:, half:].astype(BF16)


def _split_gate_up(w_gu):
    col = jnp.arange(SPLIT_SUB)
    src = jnp.where(col < SPLIT_SUB // 2, 2 * col, 2 * (col - SPLIT_SUB // 2) + 1)
    perm = (jnp.arange(SPLIT_SUB)[:, None] == src[None, :]).astype(BF16)
    out = jax.ShapeDtypeStruct((N_EXPERTS, D_MODEL, D_EXPERT), BF16)
    return pl.pallas_call(
        _split_kernel,
        grid=(N_EXPERTS, 2 * D_EXPERT // SPLIT_TN),
        in_specs=[pl.BlockSpec((None, D_MODEL, SPLIT_TN), lambda e, t: (e, 0, t)),
                  pl.BlockSpec((SPLIT_SUB, SPLIT_SUB), lambda e, t: (0, 0))],
        out_specs=[pl.BlockSpec((None, D_MODEL, SPLIT_TN // 2), lambda e, t: (e, 0, t)),
                   pl.BlockSpec((None, D_MODEL, SPLIT_TN // 2), lambda e, t: (e, 0, t))],
        out_shape=[out, out],
        compiler_params=pltpu.CompilerParams(
            dimension_semantics=("arbitrary", "arbitrary"), vmem_limit_bytes=VMEM_LIMIT),
        name="split_gate_up",
    )(w_gu, perm)


def _expert_kernel(be_ref, nused_ref, xs_ref, wg_ref, wl_ref, bg_ref, bl_ref, wd_ref, bd_ref, o_ref):
    del be_ref
    i = pl.program_id(0)

    @pl.when(i < nused_ref[0])
    def _():
        xb = xs_ref[...].astype(BF16)
        g = jnp.minimum(_mm(xb, wg_ref[...]) + bg_ref[...], SWIGLU_LIMIT)
        l = jnp.clip(_mm(xb, wl_ref[...]) + bl_ref[...], -SWIGLU_LIMIT, SWIGLU_LIMIT)
        act = g * jax.nn.sigmoid(SWIGLU_ALPHA * g) * (l + 1.0)
        o_ref[...] = _mm(act, wd_ref[...]) + bd_ref[...]

    @pl.when(i >= nused_ref[0])
    def _():
        o_ref[...] = jnp.zeros_like(o_ref)


def _experts(block_expert, nused, xs, wg, wl, bg, bl, wd, bd):
    blk = lambda i, be, nu: (jnp.minimum(i, nu[0] - 1), 0)
    wspec = pl.BlockSpec((None, D_MODEL, D_EXPERT), lambda i, be, nu: (be[i], 0, 0))
    bspec = pl.BlockSpec((None, 1, D_EXPERT), lambda i, be, nu: (be[i], 0, 0))
    grid_spec = pltpu.PrefetchScalarGridSpec(
        num_scalar_prefetch=2,
        grid=(N_BLK,),
        in_specs=[pl.BlockSpec((MOE_BLK, D_MODEL), blk),
                  wspec, wspec, bspec, bspec, wspec, bspec],
        out_specs=pl.BlockSpec((MOE_BLK, D_MODEL), lambda i, be, nu: (i, 0)),
    )
    return pl.pallas_call(
        _expert_kernel,
        grid_spec=grid_spec,
        out_shape=jax.ShapeDtypeStruct((N_SLOTS, D_MODEL), F32),
        compiler_params=pltpu.CompilerParams(
            dimension_semantics=("arbitrary",), vmem_limit_bytes=VMEM_LIMIT),
        name="experts",
    )(block_expert, nused, xs, wg, wl, bg, bl, wd, bd)


def _combine_kernel(seg_ref, eo_ref, xnew_ref, g2_ref, pos_ref, wt_ref, o_ref, gbuf, sems):
    j = pl.program_id(0)
    slot = lax.rem(j, 2)

    def fetch(tile, sl):
        _seg_start(seg_ref, tile, eo_ref, gbuf.at[sl], False, sems.at[sl])

    @pl.when(j == 0)
    def _():
        gbuf[...] = jnp.zeros_like(gbuf)
        fetch(0, 0)

    @pl.when(j + 1 < N_TILES)
    def _():
        fetch(j + 1, 1 - slot)

    _seg_wait(seg_ref, j, eo_ref, gbuf.at[slot], False, sems.at[slot])

    pos = pos_ref[...]
    wt = wt_ref[...]
    lane = lax.broadcasted_iota(I32, (MERGE_TM, LROWS), 1)
    wsel = jnp.where(lane == pos[:, 0:1], wt[:, 0:1], 0.0)
    for k in range(1, TOP_K):
        wsel = wsel + jnp.where(lane == pos[:, k:k + 1], wt[:, k:k + 1], 0.0)
    acc = _mm(wsel, gbuf[slot])
    o_ref[...] = xnew_ref[...] + g2_ref[...] * acc


def _combine(segtab, eo, x_new, mod3, pos_t, wts_t):
    tiles_per_batch = SEQ // MERGE_TM
    grid_spec = pltpu.PrefetchScalarGridSpec(
        num_scalar_prefetch=1,
        grid=(N_TILES,),
        in_specs=[
            pl.BlockSpec(memory_space=pl.ANY),
            pl.BlockSpec((MERGE_TM, D_MODEL), lambda j, sg: (j, 0)),
            pl.BlockSpec((None, 1, D_MODEL), lambda j, sg: (j // tiles_per_batch, 0, 5)),
            pl.BlockSpec((MERGE_TM, TOP_K), lambda j, sg: (j, 0)),
            pl.BlockSpec((MERGE_TM, TOP_K), lambda j, sg: (j, 0)),
        ],
        out_specs=pl.BlockSpec((MERGE_TM, D_MODEL), lambda j, sg: (j, 0)),
        scratch_shapes=[pltpu.VMEM((2, LROWS, D_MODEL), F32),
                        pltpu.SemaphoreType.DMA((2,))],
    )
    return pl.pallas_call(
        _combine_kernel,
        grid_spec=grid_spec,
        out_shape=jax.ShapeDtypeStruct((N_TOK, D_MODEL), F32),
        compiler_params=pltpu.CompilerParams(
            dimension_semantics=("arbitrary",), vmem_limit_bytes=VMEM_LIMIT),
        name="combine",
    )(segtab, eo, x_new, mod3, pos_t, wts_t)


def _pad_heads(w, heads, dim):
    lead = w.shape[:-1]
    w = w.reshape(lead + (heads, dim))
    w = jnp.pad(w, [(0, 0)] * len(lead) + [(0, 0), (0, LANES - dim)])
    return w.reshape(lead + (heads * LANES,))


def _rope_tables(dim, offset):
    pos = jnp.arange(SEQ, dtype=jnp.int32)
    row = (pos // GRID_W).astype(F32)
    col = (pos % GRID_W).astype(F32)
    q = dim // 4
    freqs = ROPE_THETA ** (-jnp.arange(q, dtype=F32) / q)
    ang_r = row[:, None] * freqs
    ang_c = col[:, None] * freqs
    ang = jnp.concatenate([ang_r, ang_r, ang_c, ang_c], axis=-1)
    cos, sin = jnp.cos(ang), jnp.sin(ang)
    first = (jnp.arange(dim) % (2 * q)) < q
    sa = jnp.where(first, -sin, 0.0)
    sb = jnp.where(first, 0.0, sin)
    pad = lambda t, v: jnp.pad(t, ((0, 0), (offset, LANES - offset - dim)), constant_values=v)
    return pad(cos, 1.0), pad(sa, 0.0), pad(sb, 0.0)


def kernel(x, c, ctx, c_ctx, w_ada, b_ada, norm1_g, norm2_g, w_in, mla_q_a_g, mla_kv_a_g, w_q_up, w_kv_up,
           mla_q_g, mla_k_g, swa_q_g, swa_k_g, swa_sink, w_branch_a, w_branch_b, w_out, w_router, b_router,
           w_gate_up, b_gate_up, w_down, b_down):
    assert x.shape == (BATCH, SEQ, D_MODEL) and ctx.shape == (BATCH, CTX_LEN, D_MODEL)
    assert w_ada.shape[0] == 1, "single layer"

    cond = jnp.concatenate([c, c_ctx[None], jnp.zeros((MOD_ROWS - BATCH - 1, D_MODEL), F32)], axis=0)
    mod = _ada(cond, w_ada[0], b_ada[0][None])
    mod3 = mod.reshape(MOD_ROWS, 1, N_MOD * D_MODEL)

    wi = w_in[0]
    o1, o2, o3 = MLA_KV_RANK, MLA_KV_RANK + MLA_ROPE, MLA_KV_RANK + MLA_ROPE + SWA_KV_HEADS * SWA_HD
    kv_cols = o3 + SWA_KV_HEADS * SWA_HD
    q1 = kv_cols + MLA_Q_RANK
    q2 = q1 + SWA_HEADS * SWA_HD
    q3 = q2 + D_MODEL
    krope_grp = jnp.pad(wi[:, o1:o2], ((0, 0), (MLA_NOPE, LANES - MLA_QK)))
    win = jnp.concatenate([
        wi[:, 0:o1], _pad_heads(wi[:, o2:o3], SWA_KV_HEADS, SWA_HD), krope_grp,
        wi[:, kv_cols:q1], _pad_heads(wi[:, q1:q2], SWA_HEADS, SWA_HD), wi[:, q2:q3], wi[:, q3:]],
        axis=1).astype(BF16)
    assert win.shape == (D_MODEL, C_END)
    wvst = wi[:, o3:kv_cols].T.astype(BF16)
    wkv = w_kv_up[0].reshape(MLA_KV_RANK, MLA_HEADS, MLA_NOPE + MLA_V)
    wkn = _pad_heads(wkv[:, :, :MLA_NOPE].reshape(MLA_KV_RANK, -1), MLA_HEADS, MLA_NOPE).astype(BF16)
    wvt = wkv[:, :, MLA_NOPE:].reshape(MLA_KV_RANK, MLA_HEADS * MLA_V).T.astype(BF16)
    wqup = _pad_heads(w_q_up[0], MLA_HEADS, MLA_QK).astype(BF16)
    gk_t = _pad_heads(jnp.tile(mla_k_g[0], MLA_HEADS)[None], MLA_HEADS, MLA_QK)
    gq_t = _pad_heads(jnp.tile(mla_q_g[0] * (MLA_QK ** -0.5 * LOG2E), MLA_HEADS)[None], MLA_HEADS, MLA_QK)
    gks_t = _pad_heads(jnp.tile(swa_k_g[0], SWA_KV_HEADS)[None], SWA_KV_HEADS, SWA_HD)
    gqs_t = _pad_heads(jnp.tile(swa_q_g[0] * (SWA_HD ** -0.5 * LOG2E), SWA_HEADS)[None], SWA_HEADS, SWA_HD)
    head_of_lane = jnp.arange(MLA_HEADS * LANES) // LANES
    ind = (head_of_lane[:, None] == jnp.arange(LANES)[None, :]).astype(BF16)
    indt = jnp.concatenate([ind.T, ind.T], axis=0)
    consts = [norm1_g[0][None], mla_kv_a_g[0][None], wkn, wvt, wvst, gk_t, gks_t, ind, indt]
    q_consts = [mla_q_a_g[0][None], wqup, gq_t, gqs_t]

    tabs_lat = _rope_tables(MLA_ROPE, MLA_NOPE) + _rope_tables(SWA_HD, 0)
    ones = jnp.ones((PROJ_TM, LANES), F32)
    zeros = jnp.zeros((PROJ_TM, LANES), F32)
    tabs_ctx = (ones, zeros, zeros, ones, zeros, zeros)

    x2d = x.reshape(N_TOK, D_MODEL)
    tiles_per_batch = SEQ // PROJ_TM
    k_mla, vt_mla, k_swa, vt_swa, q_mla, q_swa, sig_a, sig_b = _proj(
        x2d, mod3, lambda i: i // tiles_per_batch, lambda i: i % tiles_per_batch,
        tabs_lat, consts, q_consts, win, True, "proj_latent")
    kc_mla, vct_mla, kc_swa, vct_swa = _proj(
        ctx.reshape(BATCH * CTX_LEN, D_MODEL), mod3, lambda i: BATCH, lambda i: 0,
        tabs_ctx, consts, q_consts, win[:, :C_KVEND], False, "proj_ctx")

    o_a = _mla_attn(q_mla, k_mla, vt_mla, kc_mla, vct_mla)
    o_b = _swa_attn(swa_sink[0], q_swa, k_swa, vt_swa, kc_swa, vct_swa)

    x_new, h2, pos, top_w, seg = _merge(
        o_a, o_b, sig_a, sig_b, x2d, mod3,
        w_branch_a[0].astype(BF16), w_branch_b[0].astype(BF16), w_out[0].astype(BF16),
        norm2_g[0][None], w_router[0].T, b_router[0][:, None])

    seg3 = seg.reshape(N_TILES, SUBLANES, LANES)
    lstart, rows, gcarry = (seg3[:, r, :N_EXPERTS] for r in range(3))
    total = gcarry[-1] + rows[-1]
    padded = (total + MOE_BLK - 1) // MOE_BLK * MOE_BLK
    pends = jnp.cumsum(padded).astype(I32)
    pstarts = pends - padded
    nused = (pends[-1] // MOE_BLK).astype(I32)[None]
    blk_start = jnp.arange(N_BLK, dtype=I32) * MOE_BLK
    block_expert = jnp.minimum(
        jnp.sum((blk_start[:, None] >= pends[None, :]).astype(I32), axis=1), N_EXPERTS - 1)
    segtab = jnp.concatenate(
        [lstart, rows, pstarts[None, :] + gcarry,
         jnp.broadcast_to(jnp.sum(rows, axis=1, keepdims=True), rows.shape)], axis=1).reshape(-1).astype(I32)

    xs = _dispatch(pends, padded.astype(I32), segtab, pos, h2)
    wg, wl = _split_gate_up(w_gate_up[0])
    bgu = b_gate_up[0]
    eo = _experts(block_expert, nused, xs,
                  wg, wl, bgu[:, None, 0::2], bgu[:, None, 1::2],
                  w_down[0].astype(BF16), b_down[0][:, None, :])
    out = _combine(segtab, eo, x_new, mod3, pos.T, top_w.T)
    return out.reshape(BATCH, SEQ, D_MODEL)
```

```python
import functools

import jax
import jax.numpy as jnp
from jax import lax
from jax.experimental import pallas as pl
from jax.experimental.pallas import tpu as pltpu

F32 = jnp.float32
BF16 = jnp.bfloat16
I32 = jnp.int32
U32 = jnp.uint32

LANES = 128
SUBLANES = 8

D_MODEL = 1024
BATCH = 8
SEQ = 4096
CTX_LEN = 256
GRID_W = 64
ROPE_THETA = 10000.0
NORM_EPS = 1e-6
N_MOD = 6
NEG_INF = -1e30
LOG2E = 1.4426950408889634

MLA_HEADS = 8
MLA_NOPE = 64
MLA_ROPE = 32
MLA_QK = MLA_NOPE + MLA_ROPE
MLA_V = 64
MLA_Q_RANK = 384
MLA_KV_RANK = 256

SWA_HEADS = 8
SWA_KV_HEADS = 2
SWA_GROUP = SWA_HEADS // SWA_KV_HEADS
SWA_HD = 64
WINDOW = 128

N_EXPERTS = 32
TOP_K = 4
D_EXPERT = D_MODEL
SWIGLU_LIMIT = 7.0
SWIGLU_ALPHA = 1.702

N_TOK = BATCH * SEQ
MOD_ROWS = 16

ADA_TN = 1536
PROJ_TM = 256
MLA_TQ = 256
SWA_TQ = 256
SWA_KW = SWA_TQ + 2 * WINDOW
MERGE_TM = 256
MOE_BLK = 512
SEG_ALIGN = SUBLANES
N_TILES = N_TOK // MERGE_TM
LROWS = -(-(MERGE_TM * TOP_K + N_EXPERTS * (SEG_ALIGN - 1)) // LANES) * LANES
SEG_BITS = tuple(range(SEG_ALIGN.bit_length() - 1, (MERGE_TM * TOP_K).bit_length()))
SEG_W = 128
SEG_TOTAL = 3 * N_EXPERTS
SEG_BIG = 128
N_SLOTS = -(-(N_TOK * TOP_K + N_TILES * N_EXPERTS * (SEG_ALIGN - 1) + N_EXPERTS * (MOE_BLK - 1))
            // MOE_BLK) * MOE_BLK
N_BLK = N_SLOTS // MOE_BLK

PACK_W = D_MODEL // 2

VMEM_LIMIT = 56 * 1024 * 1024

C_KVLAT = 0
C_KSWA = C_KVLAT + MLA_KV_RANK
C_KROPE = C_KSWA + SWA_KV_HEADS * LANES
C_KVEND = C_KROPE + LANES
C_QLAT = C_KVEND
C_QSWA = C_QLAT + MLA_Q_RANK
C_GA = C_QSWA + SWA_HEADS * LANES
C_GB = C_GA + D_MODEL
C_END = C_GB + D_MODEL


def _mm(a, b):
    return jnp.dot(a.astype(BF16), b.astype(BF16), preferred_element_type=F32)


def _mm_nt(a, b):
    return lax.dot_general(a.astype(BF16), b.astype(BF16), (((1,), (1,)), ((), ())),
                           preferred_element_type=F32)


def _mm_tn(a, b):
    return lax.dot_general(a.astype(BF16), b.astype(BF16), (((0,), (0,)), ((), ())),
                           preferred_element_type=F32)


def _split(a):
    hi = a.astype(BF16)
    lo = (a - hi.astype(F32)).astype(BF16)
    return hi, lo


def _pack_rows(a):
    half = a.shape[1] // 2
    bits = lambda v: lax.bitcast_convert_type(v.astype(BF16).astype(F32), U32)
    return bits(a[:, half:]) | (bits(a[:, :half]) >> 16)


def _unpack_rows(p):
    lo = lax.bitcast_convert_type(p << 16, F32)
    hi = lax.bitcast_convert_type(p & jnp.uint32(0xFFFF0000), F32)
    return jnp.concatenate([lo, hi], axis=1).astype(BF16)


def _rms(x):
    return x * lax.rsqrt(jnp.mean(x * x, axis=-1, keepdims=True) + NORM_EPS)


def _ada_kernel(c_ref, w_ref, b_ref, o_ref):
    c = c_ref[...]
    s = c * jax.nn.sigmoid(c)
    shi, slo = _split(s)
    whi, wlo = _split(w_ref[...])
    acc = _mm(shi, whi) + _mm(slo, whi) + _mm(shi, wlo)
    o_ref[...] = acc + b_ref[...]


def _ada(cond, w_ada, b_ada):
    n = w_ada.shape[1]
    return pl.pallas_call(
        _ada_kernel,
        grid=(n // ADA_TN,),
        in_specs=[
            pl.BlockSpec((MOD_ROWS, D_MODEL), lambda j: (0, 0)),
            pl.BlockSpec((D_MODEL, ADA_TN), lambda j: (0, j)),
            pl.BlockSpec((1, ADA_TN), lambda j: (0, j)),
        ],
        out_specs=pl.BlockSpec((MOD_ROWS, ADA_TN), lambda j: (0, j)),
        out_shape=jax.ShapeDtypeStruct((MOD_ROWS, n), F32),
        compiler_params=pltpu.CompilerParams(
            dimension_semantics=("arbitrary",), vmem_limit_bytes=VMEM_LIMIT),
        name="ada",
    )(cond, w_ada, b_ada)


def _head_norm(xraw, ind_ref, indt_ref, inv_dim):
    w = xraw.shape[1]
    ss = _mm(xraw * xraw, ind_ref[0:w, :])
    r = lax.rsqrt(ss * inv_dim + NORM_EPS)
    scale = _mm(jnp.concatenate(_split(r), axis=1), indt_ref[:, 0:w])
    return xraw * scale


def _rope(xh, cos, sa, sb, quarter):
    return (xh * cos + pltpu.roll(xh, LANES - quarter, 1) * sa
            + pltpu.roll(xh, quarter, 1) * sb)


def _proj_kernel(*refs, with_q):
    (x_ref, sh_ref, sc_ref, g1_ref, win_ref, cm_ref, sam_ref, sbm_ref, cs_ref, sas_ref, sbs_ref,
     gkva_ref, wkn_ref, wvt_ref, wvst_ref, gk_ref, gks_ref, ind_ref, indt_ref) = refs[:19]
    if with_q:
        gqa_ref, wqup_ref, gq_ref, gqs_ref = refs[19:23]
        kmla_ref, vmla_ref, kswa_ref, vswa_ref, qmla_ref, qswa_ref, siga_ref, sigb_ref = refs[23:]
    else:
        kmla_ref, vmla_ref, kswa_ref, vswa_ref = refs[19:]

    x = x_ref[...]
    h = _rms(x) * g1_ref[...] * (1.0 + sc_ref[...]) + sh_ref[...]
    y = _mm(h, win_ref[...])

    cm, sam, sbm = cm_ref[...], sam_ref[...], sbm_ref[...]
    cs, sas, sbs = cs_ref[...], sas_ref[...], sbs_ref[...]

    kvn = _rms(y[:, C_KVLAT:C_KVLAT + MLA_KV_RANK]) * gkva_ref[...]
    kn = _mm(kvn, wkn_ref[...])
    vmla_ref[...] = _mm_nt(wvt_ref[...], kvn).astype(BF16)
    vswa_ref[...] = _mm_nt(wvst_ref[...], h).astype(BF16)
    kr = y[:, C_KROPE:C_KROPE + LANES]
    kraw = kn + jnp.concatenate([kr] * MLA_HEADS, axis=1)
    kfull = _head_norm(kraw, ind_ref, indt_ref, 1.0 / MLA_QK) * gk_ref[...]
    for hd in range(MLA_HEADS):
        sl = slice(hd * LANES, (hd + 1) * LANES)
        kmla_ref[:, sl] = _rope(kfull[:, sl], cm, sam, sbm, MLA_ROPE // 4).astype(BF16)

    ks = _head_norm(y[:, C_KSWA:C_KSWA + SWA_KV_HEADS * LANES], ind_ref, indt_ref,
                    1.0 / SWA_HD) * gks_ref[...]
    for hd in range(SWA_KV_HEADS):
        sl = slice(hd * LANES, (hd + 1) * LANES)
        kswa_ref[:, sl] = _rope(ks[:, sl], cs, sas, sbs, SWA_HD // 4).astype(BF16)

    if with_q:
        qn = _rms(y[:, C_QLAT:C_QLAT + MLA_Q_RANK]) * gqa_ref[...]
        qraw = _mm(qn, wqup_ref[...])
        qf = _head_norm(qraw, ind_ref, indt_ref, 1.0 / MLA_QK) * gq_ref[...]
        for hd in range(MLA_HEADS):
            sl = slice(hd * LANES, (hd + 1) * LANES)
            qmla_ref[:, sl] = _rope(qf[:, sl], cm, sam, sbm, MLA_ROPE // 4).astype(BF16)
        qs = _head_norm(y[:, C_QSWA:C_QSWA + SWA_HEADS * LANES], ind_ref, indt_ref,
                        1.0 / SWA_HD) * gqs_ref[...]
        for hd in range(SWA_HEADS):
            sl = slice(hd * LANES, (hd + 1) * LANES)
            qswa_ref[:, sl] = _rope(qs[:, sl], cs, sas, sbs, SWA_HD // 4).astype(BF16)
        siga_ref[...] = jax.nn.sigmoid(y[:, C_GA:C_GA + D_MODEL]).astype(BF16)
        sigb_ref[...] = jax.nn.sigmoid(y[:, C_GB:C_GB + D_MODEL]).astype(BF16)


def _proj(x2d, mod3, mod_row_fn, tab_row_fn, tabs, consts, q_consts, win, with_q, name):
    rows = x2d.shape[0]
    tm = PROJ_TM
    const = lambda shape: pl.BlockSpec(shape, lambda i: (0,) * len(shape))
    tab = pl.BlockSpec((tm, LANES), lambda i: (tab_row_fn(i), 0))
    g1, rest = consts[0], consts[1:]
    in_specs = [
        pl.BlockSpec((tm, D_MODEL), lambda i: (i, 0)),
        pl.BlockSpec((None, 1, D_MODEL), lambda i: (mod_row_fn(i), 0, 0)),
        pl.BlockSpec((None, 1, D_MODEL), lambda i: (mod_row_fn(i), 0, 1)),
        const(g1.shape),
        pl.BlockSpec(win.shape, lambda i: (0, 0), pipeline_mode=pl.Buffered(1)),
        tab, tab, tab, tab, tab, tab,
    ] + [const(c.shape) for c in rest]
    args = [x2d, mod3, mod3, g1, win] + list(tabs) + list(rest)
    outs = [(MLA_HEADS * LANES, False), (MLA_HEADS * MLA_V, True),
            (SWA_KV_HEADS * LANES, False), (SWA_KV_HEADS * SWA_HD, True)]
    if with_q:
        in_specs += [const(c.shape) for c in q_consts]
        args += list(q_consts)
        outs += [(MLA_HEADS * LANES, False), (SWA_HEADS * LANES, False), (D_MODEL, False), (D_MODEL, False)]
    return pl.pallas_call(
        functools.partial(_proj_kernel, with_q=with_q),
        grid=(rows // tm,),
        in_specs=in_specs,
        out_specs=[pl.BlockSpec((w, tm), lambda i: (0, i)) if t else pl.BlockSpec((tm, w), lambda i: (i, 0))
                   for w, t in outs],
        out_shape=[jax.ShapeDtypeStruct((w, rows) if t else (rows, w), BF16) for w, t in outs],
        compiler_params=pltpu.CompilerParams(
            dimension_semantics=("arbitrary",), vmem_limit_bytes=VMEM_LIMIT),
        name=name,
    )(*args)


def _mla_step(q_ref, kl_ref, vlt_ref, kc_ref, vct_ref, o_ref, cur, prev):
    s_lat_c, s_ctx_c, m_c = cur
    s_lat_p, s_ctx_p, m_p = prev
    outs = []
    for hh in range(2):
        sl = slice(hh * LANES, (hh + 1) * LANES)
        vrows = slice(hh * MLA_V, (hh + 1) * MLA_V)
        q = q_ref[:, sl]
        s1 = _mm_nt(kl_ref[:, sl], q)
        s2 = _mm_nt(kc_ref[:, sl], q)
        s_lat_c[hh] = s1
        s_ctx_c[hh] = s2
        m_c[hh] = jnp.maximum(jnp.max(s1, axis=0, keepdims=True),
                              jnp.max(s2, axis=0, keepdims=True))

        m = m_p[hh]
        p1 = jnp.exp2(s_lat_p[hh] - m)
        p2 = jnp.exp2(s_ctx_p[hh] - m)
        l = jnp.sum(p1, axis=0, keepdims=True) + jnp.sum(p2, axis=0, keepdims=True)
        o = _mm(vlt_ref[vrows, :], p1) + _mm(vct_ref[vrows, :], p2)
        outs.append(o / l)
    o_ref[...] = jnp.concatenate(outs, axis=0).astype(BF16)


def _mla_kernel(q_ref, kl_ref, vl_ref, kc_ref, vc_ref, o_ref, *scratch):
    set0, set1 = scratch[:3], scratch[3:]
    i = pl.program_id(2)

    @pl.when(i == 0)
    def _():
        for ref in set1:
            ref[...] = jnp.zeros(ref.shape, F32)

    @pl.when(lax.rem(i, 2) == 0)
    def _():
        _mla_step(q_ref, kl_ref, vl_ref, kc_ref, vc_ref, o_ref, set0, set1)

    @pl.when(lax.rem(i, 2) == 1)
    def _():
        _mla_step(q_ref, kl_ref, vl_ref, kc_ref, vc_ref, o_ref, set1, set0)


def _mla_attn(q, k_lat, vt_lat, k_ctx, vt_ctx):
    tq = MLA_TQ
    nq = SEQ // tq
    return pl.pallas_call(
        _mla_kernel,
        grid=(BATCH, MLA_HEADS // 2, nq + 1),
        in_specs=[
            pl.BlockSpec((tq, 2 * LANES), lambda b, hp, i: (b * nq + jnp.minimum(i, nq - 1), hp)),
            pl.BlockSpec((SEQ, 2 * LANES), lambda b, hp, i: (b, hp)),
            pl.BlockSpec((2 * MLA_V, SEQ), lambda b, hp, i: (hp, b)),
            pl.BlockSpec((CTX_LEN, 2 * LANES), lambda b, hp, i: (b, hp)),
            pl.BlockSpec((2 * MLA_V, CTX_LEN), lambda b, hp, i: (hp, b)),
        ],
        out_specs=pl.BlockSpec((2 * MLA_V, tq), lambda b, hp, i: (hp, b * nq + jnp.maximum(i - 1, 0))),
        out_shape=jax.ShapeDtypeStruct((MLA_HEADS * MLA_V, N_TOK), BF16),
        scratch_shapes=2 * [pltpu.VMEM((2, SEQ, tq), F32),
                            pltpu.VMEM((2, CTX_LEN, tq), F32),
                            pltpu.VMEM((2, 1, tq), F32)],
        compiler_params=pltpu.CompilerParams(
            dimension_semantics=("arbitrary", "arbitrary", "arbitrary"),
            vmem_limit_bytes=VMEM_LIMIT),
        name="mla_attn",
    )(q, k_lat, vt_lat, k_ctx, vt_ctx)


def _swa_kernel(sink_ref, q_ref, k_ref, vt_ref, kc_ref, vct_ref, o_ref):
    tq = SWA_TQ
    i = pl.program_id(1)
    s0 = i * tq
    kstart = pl.multiple_of(jnp.clip(s0 - WINDOW, 0, SEQ - SWA_KW), LANES)
    kwin = k_ref[pl.ds(kstart, SWA_KW), :]
    vtwin = vt_ref[:, pl.ds(kstart, SWA_KW)]
    kc = kc_ref[...]
    vct = vct_ref[...]
    col = lax.broadcasted_iota(I32, (SWA_KW, SWA_GROUP * tq), 1)
    qpos = s0 + (col & (tq - 1))
    kpos = kstart + lax.broadcasted_iota(I32, (SWA_KW, SWA_GROUP * tq), 0)
    allowed = jnp.abs(qpos - kpos) <= WINDOW
    for kh in range(SWA_KV_HEADS):
        ksl = slice(kh * LANES, (kh + 1) * LANES)
        vrows = slice(kh * SWA_HD, (kh + 1) * SWA_HD)
        q4 = jnp.concatenate(
            [q_ref[:, (kh * SWA_GROUP + g) * LANES:(kh * SWA_GROUP + g + 1) * LANES]
             for g in range(SWA_GROUP)], axis=0)
        sb = jnp.where(allowed, _mm_nt(kwin[:, ksl], q4), NEG_INF)
        sc = _mm_nt(kc[:, ksl], q4)
        sink = jnp.concatenate(
            [jnp.full((1, tq), sink_ref[kh * SWA_GROUP + g] * LOG2E, F32) for g in range(SWA_GROUP)],
            axis=1)
        m = jnp.maximum(jnp.maximum(jnp.max(sb, axis=0, keepdims=True),
                                    jnp.max(sc, axis=0, keepdims=True)), sink)
        pb = jnp.exp2(sb - m)
        pc = jnp.exp2(sc - m)
        l = (jnp.sum(pb, axis=0, keepdims=True) + jnp.sum(pc, axis=0, keepdims=True)
             + jnp.exp2(sink - m))
        o = (_mm(vtwin[vrows, :], pb) + _mm(vct[vrows, :], pc)) / l
        for g in range(SWA_GROUP):
            hd = kh * SWA_GROUP + g
            o_ref[hd * SWA_HD:(hd + 1) * SWA_HD, :] = o[:, g * tq:(g + 1) * tq].astype(BF16)


def _swa_attn(sink, q, k_lat, vt_lat, k_ctx, vt_ctx):
    tq = SWA_TQ
    nq = SEQ // tq
    return pl.pallas_call(
        _swa_kernel,
        grid=(BATCH, nq),
        in_specs=[
            pl.BlockSpec(memory_space=pltpu.SMEM),
            pl.BlockSpec((tq, SWA_HEADS * LANES), lambda b, i: (b * nq + i, 0)),
            pl.BlockSpec((SEQ, SWA_KV_HEADS * LANES), lambda b, i: (b, 0)),
            pl.BlockSpec((SWA_KV_HEADS * SWA_HD, SEQ), lambda b, i: (0, b)),
            pl.BlockSpec((CTX_LEN, SWA_KV_HEADS * LANES), lambda b, i: (b, 0)),
            pl.BlockSpec((SWA_KV_HEADS * SWA_HD, CTX_LEN), lambda b, i: (0, b)),
        ],
        out_specs=pl.BlockSpec((SWA_HEADS * SWA_HD, tq), lambda b, i: (0, b * nq + i)),
        out_shape=jax.ShapeDtypeStruct((SWA_HEADS * SWA_HD, N_TOK), BF16),
        compiler_params=pltpu.CompilerParams(
            dimension_semantics=("arbitrary", "arbitrary"), vmem_limit_bytes=VMEM_LIMIT),
        name="swa_attn",
    )(sink, q, k_lat, vt_lat, k_ctx, vt_ctx)


def _merge_kernel(oa_ref, ob_ref, sa_ref, sb_ref, x_ref, g1_ref, sh2_ref, sc2_ref, wba_ref, wbb_ref,
                  wout_ref, g2_ref, wr_ref, br_ref, utri_ref, ltri_ref,
                  xnew_ref, h2_ref, pos_ref, wts_ref, seg_ref, carry_ref):
    i = pl.program_id(0)

    @pl.when(i == 0)
    def _():
        carry_ref[...] = jnp.zeros_like(carry_ref)

    ya = _mm_tn(oa_ref[...], wba_ref[...])
    yb = _mm_tn(ob_ref[...], wbb_ref[...])
    y = sa_ref[...].astype(F32) * ya + sb_ref[...].astype(F32) * yb
    z = _mm(y, wout_ref[...])
    xn = x_ref[...] + g1_ref[...] * z
    xnew_ref[...] = xn
    h2 = _rms(xn) * g2_ref[...] * (1.0 + sc2_ref[...]) + sh2_ref[...]
    h2_ref[...] = h2.astype(BF16)

    hhi, hlo = _split(h2)
    whi, wlo = _split(wr_ref[...])
    lg = _mm_nt(whi, hhi) + _mm_nt(whi, hlo) + _mm_nt(wlo, hhi) + br_ref[...]
    eiota = lax.broadcasted_iota(I32, lg.shape, 0).astype(F32)
    vals, onehots = [], []
    for k in range(TOP_K):
        m = jnp.max(lg, axis=0, keepdims=True)
        ik = jnp.min(jnp.where(lg == m, eiota, float(N_EXPERTS)), axis=0, keepdims=True)
        hit = eiota == ik
        vals.append(m)
        onehots.append(hit.astype(F32))
        lg = jnp.where(hit, -jnp.inf, lg)
    es = [jnp.exp(v - vals[0]) for v in vals]
    tot = es[0] + es[1] + es[2] + es[3]
    for k in range(TOP_K):
        wts_ref[k:k + 1, :] = es[k] / tot

    tots = [jnp.sum(oh, axis=1, keepdims=True) for oh in onehots]
    n = tots[0] + tots[1] + tots[2] + tots[3]
    m_al = jnp.floor((n + (SEG_ALIGN - 1)) * (1.0 / SEG_ALIGN))
    m_b = jnp.broadcast_to(m_al, (N_EXPERTS, LANES))
    lstart = _mm(ltri_ref[...], m_b) * float(SEG_ALIGN)
    off = jnp.zeros_like(n)
    for k in range(TOP_K):
        prefix = _mm(onehots[k], utri_ref[...])
        lp = jnp.sum(onehots[k] * (lstart[:, 0:1] + off + prefix), axis=0, keepdims=True)
        pos_ref[k:k + 1, :] = lp.astype(I32)
        off = off + tots[k]
    eye = (lax.broadcasted_iota(I32, (N_EXPERTS, LANES), 0)
           == lax.broadcasted_iota(I32, (N_EXPERTS, LANES), 1)).astype(F32)
    to_row = lambda v: jnp.sum(v * eye, axis=0, keepdims=True).astype(I32)
    carry = carry_ref[...]
    seg_ref[...] = jnp.zeros_like(seg_ref)
    seg_ref[0:1, :] = to_row(lstart)
    seg_ref[1:2, :] = to_row(m_b * float(SEG_ALIGN))
    seg_ref[2:3, :] = to_row(carry)
    carry_ref[...] = carry + m_b * float(SEG_ALIGN)


def _merge(o_a, o_b, sig_a, sig_b, x2d, mod3, w_ba, w_bb, w_out, g2, w_rt, b_r):
    tm = MERGE_TM
    tiles_per_batch = SEQ // tm
    const = lambda shape: pl.BlockSpec(shape, lambda i: (0,) * len(shape))
    modspec = lambda j: pl.BlockSpec((None, 1, D_MODEL), lambda i: (i // tiles_per_batch, 0, j))
    row = lambda w: pl.BlockSpec((tm, w), lambda i: (i, 0))
    utri = (jnp.arange(tm)[:, None] < jnp.arange(tm)[None, :]).astype(BF16)
    ltri = (jnp.arange(N_EXPERTS)[None, :] < jnp.arange(N_EXPERTS)[:, None]).astype(BF16)
    return pl.pallas_call(
        _merge_kernel,
        grid=(N_TILES,),
        in_specs=[pl.BlockSpec((512, tm), lambda i: (0, i)), pl.BlockSpec((512, tm), lambda i: (0, i)),
                  row(D_MODEL), row(D_MODEL), row(D_MODEL),
                  modspec(2), modspec(3), modspec(4),
                  const(w_ba.shape), const(w_bb.shape), const(w_out.shape), const(g2.shape),
                  const(w_rt.shape), const(b_r.shape), const(utri.shape), const(ltri.shape)],
        out_specs=[row(D_MODEL), row(D_MODEL),
                   pl.BlockSpec((TOP_K, tm), lambda i: (0, i)),
                   pl.BlockSpec((TOP_K, tm), lambda i: (0, i)),
                   pl.BlockSpec((SUBLANES, LANES), lambda i: (i, 0))],
        out_shape=[jax.ShapeDtypeStruct((N_TOK, D_MODEL), F32),
                   jax.ShapeDtypeStruct((N_TOK, D_MODEL), BF16),
                   jax.ShapeDtypeStruct((TOP_K, N_TOK), I32),
                   jax.ShapeDtypeStruct((TOP_K, N_TOK), F32),
                   jax.ShapeDtypeStruct((N_TILES * SUBLANES, LANES), I32)],
        scratch_shapes=[pltpu.VMEM((N_EXPERTS, LANES), F32)],
        compiler_params=pltpu.CompilerParams(
            dimension_semantics=("arbitrary",), vmem_limit_bytes=VMEM_LIMIT),
        name="merge",
    )(o_a, o_b, sig_a, sig_b, x2d, mod3, mod3, mod3, w_ba, w_bb, w_out, g2, w_rt, b_r, utri, ltri)


def _chunk_copy(hbm, hbm_row, buf, buf_row, size, to_hbm, sem):
    vm = buf.at[pl.ds(pl.multiple_of(buf_row, SEG_ALIGN), size)]
    hb = hbm.at[pl.ds(pl.multiple_of(hbm_row, SEG_ALIGN), size)]
    return pltpu.make_async_copy(vm, hb, sem) if to_hbm else pltpu.make_async_copy(hb, vm, sem)


def _seg_start(seg_smem, tile, hbm, buf, to_hbm, sem):
    base = tile * SEG_W

    def chunks(src, rows, dst, bits):
        for b in bits:
            size = 1 << b
            done = rows & (-2 * size)

            @pl.when((rows & size) != 0)
            def _():
                _chunk_copy(hbm, dst + done, buf, src + done, size, to_hbm, sem).start()

    def per_expert(e, carry):
        src = seg_smem[base + e]
        rows = seg_smem[base + N_EXPERTS + e]
        dst = seg_smem[base + 2 * N_EXPERTS + e]

        @pl.when(rows >= SEG_BIG)
        def _():
            chunks(src, rows, dst, [b for b in SEG_BITS if (1 << b) >= SEG_BIG])

        chunks(src, rows, dst, [b for b in SEG_BITS if (1 << b) < SEG_BIG])
        return carry

    lax.fori_loop(0, N_EXPERTS, per_expert, 0)


def _seg_wait(seg_smem, tile, hbm, buf, to_hbm, sem):
    assert LROWS < 2 << SEG_BITS[-1]
    total = seg_smem[tile * SEG_W + SEG_TOTAL]
    for b in SEG_BITS:
        size = 1 << b

        @pl.when((total & size) != 0)
        def _():
            _chunk_copy(hbm, 0, buf, 0, size, to_hbm, sem).wait()


def _dispatch_kernel(pend_ref, padded_ref, seg_ref, pos_ref, h2_ref, xs_ref, staged, sems):
    j = pl.program_id(0)
    slot = lax.rem(j, 2)

    @pl.when(j == 0)
    def _():
        staged[0, 0:MOE_BLK, :] = jnp.zeros((MOE_BLK, PACK_W), U32)

        def tail(action):
            def body(e, carry):
                @pl.when(padded_ref[e] > 0)
                def _():
                    start = pl.multiple_of(pend_ref[e] - MOE_BLK, MOE_BLK)
                    cp = pltpu.make_async_copy(staged.at[0, pl.ds(0, MOE_BLK)],
                                               xs_ref.at[pl.ds(start, MOE_BLK)], sems.at[0])
                    getattr(cp, action)()
                return carry
            lax.fori_loop(0, N_EXPERTS, body, 0)

        def unused(action):
            def body(b, carry):
                cp = pltpu.make_async_copy(
                    staged.at[0, pl.ds(0, MOE_BLK)],
                    xs_ref.at[pl.ds(pl.multiple_of(b * MOE_BLK, MOE_BLK), MOE_BLK)], sems.at[0])
                getattr(cp, action)()
                return carry
            lax.fori_loop(pend_ref[N_EXPERTS - 1] // MOE_BLK, N_BLK, body, 0)

        tail("start")
        unused("start")
        tail("wait")
        unused("wait")

    pos = pos_ref[...]
    piota = lax.broadcasted_iota(I32, (LROWS, MERGE_TM), 0)
    sel = (piota == pos[0:1, :]).astype(F32)
    for k in range(1, TOP_K):
        sel = sel + (piota == pos[k:k + 1, :]).astype(F32)
    staged[slot] = _pack_rows(_mm(sel, h2_ref[...]))

    _seg_start(seg_ref, j, xs_ref, staged.at[slot], True, sems.at[1 + slot])

    @pl.when(j > 0)
    def _():
        _seg_wait(seg_ref, j - 1, xs_ref, staged.at[1 - slot], True, sems.at[2 - slot])

    @pl.when(j == N_TILES - 1)
    def _():
        _seg_wait(seg_ref, j, xs_ref, staged.at[slot], True, sems.at[1 + slot])


def _dispatch(pends, padded, segtab, pos, h2):
    grid_spec = pltpu.PrefetchScalarGridSpec(
        num_scalar_prefetch=3,
        grid=(N_TILES,),
        in_specs=[
            pl.BlockSpec((TOP_K, MERGE_TM), lambda j, pe, pa, sg: (0, j)),
            pl.BlockSpec((MERGE_TM, D_MODEL), lambda j, pe, pa, sg: (j, 0)),
        ],
        out_specs=pl.BlockSpec(memory_space=pl.ANY),
        scratch_shapes=[pltpu.VMEM((2, LROWS, PACK_W), U32),
                        pltpu.SemaphoreType.DMA((3,))],
    )
    return pl.pallas_call(
        _dispatch_kernel,
        grid_spec=grid_spec,
        out_shape=jax.ShapeDtypeStruct((N_SLOTS, PACK_W), U32),
        compiler_params=pltpu.CompilerParams(
            dimension_semantics=("arbitrary",), vmem_limit_bytes=VMEM_LIMIT),
        name="dispatch",
    )(pends, padded, segtab, pos, h2)


SPLIT_SUB = 256


def _expert_kernel(be_ref, nused_ref, xs_ref, wgu_ref, bg_ref, bl_ref, wd_ref, bd_ref, perm_ref, o_ref,
                   wg_s, wl_s, wd_s):
    i = pl.program_id(0)
    used = i < nused_ref[0]
    fresh = jnp.logical_or(i == 0, be_ref[i] != be_ref[jnp.maximum(i - 1, 0)])

    @pl.when(jnp.logical_and(used, fresh))
    def _():
        half = SPLIT_SUB // 2
        for s in range(2 * D_EXPERT // SPLIT_SUB):
            r = _mm(wgu_ref[:, s * SPLIT_SUB:(s + 1) * SPLIT_SUB], perm_ref[...])
            wg_s[:, s * half:(s + 1) * half] = r[:, :half].astype(BF16)
            wl_s[:, s * half:(s + 1) * half] = r[:, half:].astype(BF16)
        wd_s[...] = wd_ref[...].astype(BF16)

    @pl.when(used)
    def _():
        xb = _unpack_rows(xs_ref[...])
        g = jnp.minimum(_mm(xb, wg_s[...]) + bg_ref[...], SWIGLU_LIMIT)
        l = jnp.clip(_mm(xb, wl_s[...]) + bl_ref[...], -SWIGLU_LIMIT, SWIGLU_LIMIT)
        act = g * jax.nn.sigmoid(SWIGLU_ALPHA * g) * (l + 1.0)
        o_ref[...] = _pack_rows(_mm(act, wd_s[...]) + bd_ref[...])

    @pl.when(jnp.logical_not(used))
    def _():
        o_ref[...] = jnp.zeros_like(o_ref)


def _experts(block_expert, nused, xs, w_gu, bg, bl, w_dn, bd):
    col = jnp.arange(SPLIT_SUB)
    src = jnp.where(col < SPLIT_SUB // 2, 2 * col, 2 * (col - SPLIT_SUB // 2) + 1)
    perm = (jnp.arange(SPLIT_SUB)[:, None] == src[None, :]).astype(BF16)
    blk = lambda i, be, nu: (jnp.minimum(i, nu[0] - 1), 0)
    per_expert = lambda shape: pl.BlockSpec((None,) + shape, lambda i, be, nu: (be[i], 0, 0))
    grid_spec = pltpu.PrefetchScalarGridSpec(
        num_scalar_prefetch=2,
        grid=(N_BLK,),
        in_specs=[pl.BlockSpec((MOE_BLK, PACK_W), blk),
                  per_expert((D_MODEL, 2 * D_EXPERT)), per_expert((1, D_EXPERT)), per_expert((1, D_EXPERT)),
                  per_expert((D_EXPERT, D_MODEL)), per_expert((1, D_MODEL)),
                  pl.BlockSpec((SPLIT_SUB, SPLIT_SUB), lambda i, be, nu: (0, 0))],
        out_specs=pl.BlockSpec((MOE_BLK, PACK_W), lambda i, be, nu: (i, 0)),
        scratch_shapes=[pltpu.VMEM((D_MODEL, D_EXPERT), BF16), pltpu.VMEM((D_MODEL, D_EXPERT), BF16),
                        pltpu.VMEM((D_EXPERT, D_MODEL), BF16)],
    )
    return pl.pallas_call(
        _expert_kernel,
        grid_spec=grid_spec,
        out_shape=jax.ShapeDtypeStruct((N_SLOTS, PACK_W), U32),
        compiler_params=pltpu.CompilerParams(
            dimension_semantics=("arbitrary",), vmem_limit_bytes=VMEM_LIMIT),
        name="experts",
    )(block_expert, nused, xs, w_gu, bg, bl, w_dn, bd, perm)


def _combine_kernel(seg_ref, eo_ref, xnew_ref, g2_ref, pos_ref, wt_ref, o_ref, gbuf, sems):
    j = pl.program_id(0)
    slot = lax.rem(j, 2)

    def fetch(tile, sl):
        _seg_start(seg_ref, tile, eo_ref, gbuf.at[sl], False, sems.at[sl])

    @pl.when(j == 0)
    def _():
        gbuf[...] = jnp.zeros_like(gbuf)
        fetch(0, 0)

    @pl.when(j + 1 < N_TILES)
    def _():
        fetch(j + 1, 1 - slot)

    _seg_wait(seg_ref, j, eo_ref, gbuf.at[slot], False, sems.at[slot])

    pos = pos_ref[...]
    wt = wt_ref[...]
    lane = lax.broadcasted_iota(I32, (MERGE_TM, LROWS), 1)
    wsel = jnp.where(lane == pos[:, 0:1], wt[:, 0:1], 0.0)
    for k in range(1, TOP_K):
        wsel = wsel + jnp.where(lane == pos[:, k:k + 1], wt[:, k:k + 1], 0.0)
    acc = _mm(wsel, _unpack_rows(gbuf[slot]))
    o_ref[...] = xnew_ref[...] + g2_ref[...] * acc


def _combine(segtab, eo, x_new, mod3, pos_t, wts_t):
    tiles_per_batch = SEQ // MERGE_TM
    grid_spec = pltpu.PrefetchScalarGridSpec(
        num_scalar_prefetch=1,
        grid=(N_TILES,),
        in_specs=[
            pl.BlockSpec(memory_space=pl.ANY),
            pl.BlockSpec((MERGE_TM, D_MODEL), lambda j, sg: (j, 0)),
            pl.BlockSpec((None, 1, D_MODEL), lambda j, sg: (j // tiles_per_batch, 0, 5)),
            pl.BlockSpec((MERGE_TM, TOP_K), lambda j, sg: (j, 0)),
            pl.BlockSpec((MERGE_TM, TOP_K), lambda j, sg: (j, 0)),
        ],
        out_specs=pl.BlockSpec((MERGE_TM, D_MODEL), lambda j, sg: (j, 0)),
        scratch_shapes=[pltpu.VMEM((2, LROWS, PACK_W), U32),
                        pltpu.SemaphoreType.DMA((2,))],
    )
    return pl.pallas_call(
        _combine_kernel,
        grid_spec=grid_spec,
        out_shape=jax.ShapeDtypeStruct((N_TOK, D_MODEL), F32),
        compiler_params=pltpu.CompilerParams(
            dimension_semantics=("arbitrary",), vmem_limit_bytes=VMEM_LIMIT),
        name="combine",
    )(segtab, eo, x_new, mod3, pos_t, wts_t)


def _pad_heads(w, heads, dim):
    lead = w.shape[:-1]
    w = w.reshape(lead + (heads, dim))
    w = jnp.pad(w, [(0, 0)] * len(lead) + [(0, 0), (0, LANES - dim)])
    return w.reshape(lead + (heads * LANES,))


def _rope_tables(dim, offset):
    pos = jnp.arange(SEQ, dtype=jnp.int32)
    row = (pos // GRID_W).astype(F32)
    col = (pos % GRID_W).astype(F32)
    q = dim // 4
    freqs = ROPE_THETA ** (-jnp.arange(q, dtype=F32) / q)
    ang_r = row[:, None] * freqs
    ang_c = col[:, None] * freqs
    ang = jnp.concatenate([ang_r, ang_r, ang_c, ang_c], axis=-1)
    cos, sin = jnp.cos(ang), jnp.sin(ang)
    first = (jnp.arange(dim) % (2 * q)) < q
    sa = jnp.where(first, -sin, 0.0)
    sb = jnp.where(first, 0.0, sin)
    pad = lambda t, v: jnp.pad(t, ((0, 0), (offset, LANES - offset - dim)), constant_values=v)
    return pad(cos, 1.0), pad(sa, 0.0), pad(sb, 0.0)


def kernel(x, c, ctx, c_ctx, w_ada, b_ada, norm1_g, norm2_g, w_in, mla_q_a_g, mla_kv_a_g, w_q_up, w_kv_up,
           mla_q_g, mla_k_g, swa_q_g, swa_k_g, swa_sink, w_branch_a, w_branch_b, w_out, w_router, b_router,
           w_gate_up, b_gate_up, w_down, b_down):
    assert x.shape == (BATCH, SEQ, D_MODEL) and ctx.shape == (BATCH, CTX_LEN, D_MODEL)
    assert w_ada.shape[0] == 1, "single layer"

    cond = jnp.concatenate([c, c_ctx[None], jnp.zeros((MOD_ROWS - BATCH - 1, D_MODEL), F32)], axis=0)
    mod = _ada(cond, w_ada[0], b_ada[0][None])
    mod3 = mod.reshape(MOD_ROWS, 1, N_MOD * D_MODEL)

    wi = w_in[0]
    o1, o2, o3 = MLA_KV_RANK, MLA_KV_RANK + MLA_ROPE, MLA_KV_RANK + MLA_ROPE + SWA_KV_HEADS * SWA_HD
    kv_cols = o3 + SWA_KV_HEADS * SWA_HD
    q1 = kv_cols + MLA_Q_RANK
    q2 = q1 + SWA_HEADS * SWA_HD
    q3 = q2 + D_MODEL
    krope_grp = jnp.pad(wi[:, o1:o2], ((0, 0), (MLA_NOPE, LANES - MLA_QK)))
    win = jnp.concatenate([
        wi[:, 0:o1], _pad_heads(wi[:, o2:o3], SWA_KV_HEADS, SWA_HD), krope_grp,
        wi[:, kv_cols:q1], _pad_heads(wi[:, q1:q2], SWA_HEADS, SWA_HD), wi[:, q2:q3], wi[:, q3:]],
        axis=1).astype(BF16)
    assert win.shape == (D_MODEL, C_END)
    wvst = wi[:, o3:kv_cols].T.astype(BF16)
    wkv = w_kv_up[0].reshape(MLA_KV_RANK, MLA_HEADS, MLA_NOPE + MLA_V)
    wkn = _pad_heads(wkv[:, :, :MLA_NOPE].reshape(MLA_KV_RANK, -1), MLA_HEADS, MLA_NOPE).astype(BF16)
    wvt = wkv[:, :, MLA_NOPE:].reshape(MLA_KV_RANK, MLA_HEADS * MLA_V).T.astype(BF16)
    wqup = _pad_heads(w_q_up[0], MLA_HEADS, MLA_QK).astype(BF16)
    gk_t = _pad_heads(jnp.tile(mla_k_g[0], MLA_HEADS)[None], MLA_HEADS, MLA_QK)
    gq_t = _pad_heads(jnp.tile(mla_q_g[0] * (MLA_QK ** -0.5 * LOG2E), MLA_HEADS)[None], MLA_HEADS, MLA_QK)
    gks_t = _pad_heads(jnp.tile(swa_k_g[0], SWA_KV_HEADS)[None], SWA_KV_HEADS, SWA_HD)
    gqs_t = _pad_heads(jnp.tile(swa_q_g[0] * (SWA_HD ** -0.5 * LOG2E), SWA_HEADS)[None], SWA_HEADS, SWA_HD)
    head_of_lane = jnp.arange(MLA_HEADS * LANES) // LANES
    ind = (head_of_lane[:, None] == jnp.arange(LANES)[None, :]).astype(BF16)
    indt = jnp.concatenate([ind.T, ind.T], axis=0)
    consts = [norm1_g[0][None], mla_kv_a_g[0][None], wkn, wvt, wvst, gk_t, gks_t, ind, indt]
    q_consts = [mla_q_a_g[0][None], wqup, gq_t, gqs_t]

    tabs_lat = _rope_tables(MLA_ROPE, MLA_NOPE) + _rope_tables(SWA_HD, 0)
    ones = jnp.ones((PROJ_TM, LANES), F32)
    zeros = jnp.zeros((PROJ_TM, LANES), F32)
    tabs_ctx = (ones, zeros, zeros, ones, zeros, zeros)

    x2d = x.reshape(N_TOK, D_MODEL)
    tiles_per_batch = SEQ // PROJ_TM
    k_mla, vt_mla, k_swa, vt_swa, q_mla, q_swa, sig_a, sig_b = _proj(
        x2d, mod3, lambda i: i // tiles_per_batch, lambda i: i % tiles_per_batch,
        tabs_lat, consts, q_consts, win, True, "proj_latent")
    kc_mla, vct_mla, kc_swa, vct_swa = _proj(
        ctx.reshape(BATCH * CTX_LEN, D_MODEL), mod3, lambda i: BATCH, lambda i: 0,
        tabs_ctx, consts, q_consts, win[:, :C_KVEND], False, "proj_ctx")

    o_a = _mla_attn(q_mla, k_mla, vt_mla, kc_mla, vct_mla)
    o_b = _swa_attn(swa_sink[0], q_swa, k_swa, vt_swa, kc_swa, vct_swa)

    x_new, h2, pos, top_w, seg = _merge(
        o_a, o_b, sig_a, sig_b, x2d, mod3,
        w_branch_a[0].astype(BF16), w_branch_b[0].astype(BF16), w_out[0].astype(BF16),
        norm2_g[0][None], w_router[0].T, b_router[0][:, None])

    seg3 = seg.reshape(N_TILES, SUBLANES, LANES)
    lstart, rows, gcarry = (seg3[:, r, :N_EXPERTS] for r in range(3))
    total = gcarry[-1] + rows[-1]
    padded = (total + MOE_BLK - 1) // MOE_BLK * MOE_BLK
    pends = jnp.cumsum(padded).astype(I32)
    pstarts = pends - padded
    nused = (pends[-1] // MOE_BLK).astype(I32)[None]
    blk_start = jnp.arange(N_BLK, dtype=I32) * MOE_BLK
    block_expert = jnp.minimum(
        jnp.sum((blk_start[:, None] >= pends[None, :]).astype(I32), axis=1), N_EXPERTS - 1)
    segtab = jnp.concatenate(
        [lstart, rows, pstarts[None, :] + gcarry,
         jnp.broadcast_to(jnp.sum(rows, axis=1, keepdims=True), rows.shape)], axis=1).reshape(-1).astype(I32)

    xs = _dispatch(pends, padded.astype(I32), segtab, pos, h2)
    bgu = b_gate_up[0]
    eo = _experts(block_expert, nused, xs, w_gate_up[0], bgu[:, None, 0::2], bgu[:, None, 1::2],
                  w_down[0], b_down[0][:, None, :])
    out = _combine(segtab, eo, x_new, mod3, pos.T, top_w.T)
    return out.reshape(BATCH, SEQ, D_MODEL)
```

```python
import functools

import jax
import jax.numpy as jnp
from jax import lax
from jax.experimental import pallas as pl
from jax.experimental.pallas import tpu as pltpu

F32 = jnp.float32
BF16 = jnp.bfloat16
I32 = jnp.int32
U32 = jnp.uint32

LANES = 128
SUBLANES = 8

D_MODEL = 1024
BATCH = 8
SEQ = 4096
CTX_LEN = 256
GRID_W = 64
ROPE_THETA = 10000.0
NORM_EPS = 1e-6
N_MOD = 6
NEG_INF = -1e30
LOG2E = 1.4426950408889634

MLA_HEADS = 8
MLA_NOPE = 64
MLA_ROPE = 32
MLA_QK = MLA_NOPE + MLA_ROPE
MLA_V = 64
MLA_Q_RANK = 384
MLA_KV_RANK = 256

SWA_HEADS = 8
SWA_KV_HEADS = 2
SWA_GROUP = SWA_HEADS // SWA_KV_HEADS
SWA_HD = 64
WINDOW = 128

N_EXPERTS = 32
TOP_K = 4
D_EXPERT = D_MODEL
SWIGLU_LIMIT = 7.0
SWIGLU_ALPHA = 1.702

N_TOK = BATCH * SEQ
MOD_ROWS = 16

ADA_TN = 1536
PROJ_TM = 256
MLA_TQ = 256
SWA_TQ = 256
SWA_KW = SWA_TQ + 2 * WINDOW
MERGE_TM = 256
MERGE_SUB = 2
MOE_BLK = 512
SEG_ALIGN = SUBLANES
N_TILES = N_TOK // MERGE_TM
LROWS = -(-(MERGE_TM * TOP_K + N_EXPERTS * (SEG_ALIGN - 1)) // LANES) * LANES
SEG_BITS = tuple(range(SEG_ALIGN.bit_length() - 1, (MERGE_TM * TOP_K).bit_length()))
SEG_W = 128
SEG_TOTAL = 3 * N_EXPERTS
SEG_BIG = 128
N_SLOTS = -(-(N_TOK * TOP_K + N_TILES * N_EXPERTS * (SEG_ALIGN - 1) + N_EXPERTS * (MOE_BLK - 1))
            // MOE_BLK) * MOE_BLK
N_BLK = N_SLOTS // MOE_BLK

PACK_W = D_MODEL // 2

VMEM_LIMIT = 56 * 1024 * 1024

C_KVLAT = 0
C_KSWA = C_KVLAT + MLA_KV_RANK
C_KROPE = C_KSWA + SWA_KV_HEADS * LANES
C_KVEND = C_KROPE + LANES
C_QLAT = C_KVEND
C_QSWA = C_QLAT + MLA_Q_RANK
C_GA = C_QSWA + SWA_HEADS * LANES
C_GB = C_GA + D_MODEL
C_END = C_GB + D_MODEL


def _mm(a, b):
    return jnp.dot(a.astype(BF16), b.astype(BF16), preferred_element_type=F32)


def _mm_nt(a, b):
    return lax.dot_general(a.astype(BF16), b.astype(BF16), (((1,), (1,)), ((), ())),
                           preferred_element_type=F32)


def _mm_tn(a, b):
    return lax.dot_general(a.astype(BF16), b.astype(BF16), (((0,), (0,)), ((), ())),
                           preferred_element_type=F32)


def _split(a):
    hi = a.astype(BF16)
    lo = (a - hi.astype(F32)).astype(BF16)
    return hi, lo


def _pack_rows(a):
    half = a.shape[1] // 2
    bits = lambda v: lax.bitcast_convert_type(v.astype(BF16).astype(F32), U32)
    return bits(a[:, half:]) | (bits(a[:, :half]) >> 16)


def _unpack_rows(p):
    lo = lax.bitcast_convert_type(p << 16, F32)
    hi = lax.bitcast_convert_type(p & jnp.uint32(0xFFFF0000), F32)
    return jnp.concatenate([lo, hi], axis=1).astype(BF16)


def _rms(x):
    return x * lax.rsqrt(jnp.mean(x * x, axis=-1, keepdims=True) + NORM_EPS)


def _ada_kernel(c_ref, w_ref, b_ref, o_ref):
    c = c_ref[...]
    s = c * jax.nn.sigmoid(c)
    shi, slo = _split(s)
    whi, wlo = _split(w_ref[...])
    acc = _mm(shi, whi) + _mm(slo, whi) + _mm(shi, wlo)
    o_ref[...] = acc + b_ref[...]


def _ada(cond, w_ada, b_ada):
    n = w_ada.shape[1]
    return pl.pallas_call(
        _ada_kernel,
        grid=(n // ADA_TN,),
        in_specs=[
            pl.BlockSpec((MOD_ROWS, D_MODEL), lambda j: (0, 0)),
            pl.BlockSpec((D_MODEL, ADA_TN), lambda j: (0, j)),
            pl.BlockSpec((1, ADA_TN), lambda j: (0, j)),
        ],
        out_specs=pl.BlockSpec((MOD_ROWS, ADA_TN), lambda j: (0, j)),
        out_shape=jax.ShapeDtypeStruct((MOD_ROWS, n), F32),
        compiler_params=pltpu.CompilerParams(
            dimension_semantics=("arbitrary",), vmem_limit_bytes=VMEM_LIMIT),
        name="ada",
    )(cond, w_ada, b_ada)


def _head_norm(xraw, ind_ref, indt_ref, inv_dim):
    w = xraw.shape[1]
    ss = _mm(xraw * xraw, ind_ref[0:w, :])
    r = lax.rsqrt(ss * inv_dim + NORM_EPS)
    scale = _mm(jnp.concatenate(_split(r), axis=1), indt_ref[:, 0:w])
    return xraw * scale


def _rope(xh, cos, sa, sb, quarter):
    return (xh * cos + pltpu.roll(xh, LANES - quarter, 1) * sa
            + pltpu.roll(xh, quarter, 1) * sb)


def _proj_kernel(*refs, with_q):
    (x_ref, sh_ref, sc_ref, g1_ref, win_ref, cm_ref, sam_ref, sbm_ref, cs_ref, sas_ref, sbs_ref,
     gkva_ref, wkn_ref, wvt_ref, wvst_ref, gk_ref, gks_ref, ind_ref, indt_ref) = refs[:19]
    if with_q:
        gqa_ref, wqup_ref, gq_ref, gqs_ref = refs[19:23]
        kmla_ref, vmla_ref, kswa_ref, vswa_ref, qmla_ref, qswa_ref, siga_ref, sigb_ref = refs[23:]
    else:
        kmla_ref, vmla_ref, kswa_ref, vswa_ref = refs[19:]

    x = x_ref[...]
    h = _rms(x) * g1_ref[...] * (1.0 + sc_ref[...]) + sh_ref[...]
    y = _mm(h, win_ref[...])

    cm, sam, sbm = cm_ref[...], sam_ref[...], sbm_ref[...]
    cs, sas, sbs = cs_ref[...], sas_ref[...], sbs_ref[...]

    kvn = _rms(y[:, C_KVLAT:C_KVLAT + MLA_KV_RANK]) * gkva_ref[...]
    kn = _mm(kvn, wkn_ref[...])
    vmla_ref[...] = _mm_nt(wvt_ref[...], kvn).astype(BF16)
    vswa_ref[...] = _mm_nt(wvst_ref[...], h).astype(BF16)
    kr = y[:, C_KROPE:C_KROPE + LANES]
    kraw = kn + jnp.concatenate([kr] * MLA_HEADS, axis=1)
    kfull = _head_norm(kraw, ind_ref, indt_ref, 1.0 / MLA_QK) * gk_ref[...]
    for hd in range(MLA_HEADS):
        sl = slice(hd * LANES, (hd + 1) * LANES)
        kmla_ref[:, sl] = _rope(kfull[:, sl], cm, sam, sbm, MLA_ROPE // 4).astype(BF16)

    ks = _head_norm(y[:, C_KSWA:C_KSWA + SWA_KV_HEADS * LANES], ind_ref, indt_ref,
                    1.0 / SWA_HD) * gks_ref[...]
    for hd in range(SWA_KV_HEADS):
        sl = slice(hd * LANES, (hd + 1) * LANES)
        kswa_ref[:, sl] = _rope(ks[:, sl], cs, sas, sbs, SWA_HD // 4).astype(BF16)

    if with_q:
        qn = _rms(y[:, C_QLAT:C_QLAT + MLA_Q_RANK]) * gqa_ref[...]
        qraw = _mm(qn, wqup_ref[...])
        qf = _head_norm(qraw, ind_ref, indt_ref, 1.0 / MLA_QK) * gq_ref[...]
        for hd in range(MLA_HEADS):
            sl = slice(hd * LANES, (hd + 1) * LANES)
            qmla_ref[:, sl] = _rope(qf[:, sl], cm, sam, sbm, MLA_ROPE // 4).astype(BF16)
        qs = _head_norm(y[:, C_QSWA:C_QSWA + SWA_HEADS * LANES], ind_ref, indt_ref,
                        1.0 / SWA_HD) * gqs_ref[...]
        for hd in range(SWA_HEADS):
            sl = slice(hd * LANES, (hd + 1) * LANES)
            qswa_ref[:, sl] = _rope(qs[:, sl], cs, sas, sbs, SWA_HD // 4).astype(BF16)
        siga_ref[...] = jax.nn.sigmoid(y[:, C_GA:C_GA + D_MODEL]).astype(BF16)
        sigb_ref[...] = jax.nn.sigmoid(y[:, C_GB:C_GB + D_MODEL]).astype(BF16)


def _proj(x2d, mod3, mod_row_fn, tab_row_fn, tabs, consts, q_consts, win, with_q, name):
    rows = x2d.shape[0]
    tm = PROJ_TM
    const = lambda shape: pl.BlockSpec(shape, lambda i: (0,) * len(shape))
    tab = pl.BlockSpec((tm, LANES), lambda i: (tab_row_fn(i), 0))
    g1, rest = consts[0], consts[1:]
    in_specs = [
        pl.BlockSpec((tm, D_MODEL), lambda i: (i, 0)),
        pl.BlockSpec((None, 1, D_MODEL), lambda i: (mod_row_fn(i), 0, 0)),
        pl.BlockSpec((None, 1, D_MODEL), lambda i: (mod_row_fn(i), 0, 1)),
        const(g1.shape),
        pl.BlockSpec(win.shape, lambda i: (0, 0), pipeline_mode=pl.Buffered(1)),
        tab, tab, tab, tab, tab, tab,
    ] + [const(c.shape) for c in rest]
    args = [x2d, mod3, mod3, g1, win] + list(tabs) + list(rest)
    outs = [(MLA_HEADS * LANES, False), (MLA_HEADS * MLA_V, True),
            (SWA_KV_HEADS * LANES, False), (SWA_KV_HEADS * SWA_HD, True)]
    if with_q:
        in_specs += [const(c.shape) for c in q_consts]
        args += list(q_consts)
        outs += [(MLA_HEADS * LANES, False), (SWA_HEADS * LANES, False), (D_MODEL, False), (D_MODEL, False)]
    return pl.pallas_call(
        functools.partial(_proj_kernel, with_q=with_q),
        grid=(rows // tm,),
        in_specs=in_specs,
        out_specs=[pl.BlockSpec((w, tm), lambda i: (0, i)) if t else pl.BlockSpec((tm, w), lambda i: (i, 0))
                   for w, t in outs],
        out_shape=[jax.ShapeDtypeStruct((w, rows) if t else (rows, w), BF16) for w, t in outs],
        compiler_params=pltpu.CompilerParams(
            dimension_semantics=("arbitrary",), vmem_limit_bytes=VMEM_LIMIT),
        name=name,
    )(*args)


def _mla_step(q_ref, kl_ref, vlt_ref, kc_ref, vct_ref, o_ref, cur, prev):
    s_lat_c, s_ctx_c, m_c = cur
    s_lat_p, s_ctx_p, m_p = prev
    outs = []
    for hh in range(2):
        sl = slice(hh * LANES, (hh + 1) * LANES)
        vrows = slice(hh * MLA_V, (hh + 1) * MLA_V)
        q = q_ref[:, sl]
        s1 = _mm_nt(kl_ref[:, sl], q)
        s2 = _mm_nt(kc_ref[:, sl], q)
        s_lat_c[hh] = s1
        s_ctx_c[hh] = s2
        m_c[hh] = jnp.maximum(jnp.max(s1, axis=0, keepdims=True),
                              jnp.max(s2, axis=0, keepdims=True))

        m = m_p[hh]
        p1 = jnp.exp2(s_lat_p[hh] - m)
        p2 = jnp.exp2(s_ctx_p[hh] - m)
        l = jnp.sum(p1, axis=0, keepdims=True) + jnp.sum(p2, axis=0, keepdims=True)
        o = _mm(vlt_ref[vrows, :], p1) + _mm(vct_ref[vrows, :], p2)
        outs.append(o / l)
    o_ref[...] = jnp.concatenate(outs, axis=0).astype(BF16)


def _mla_kernel(q_ref, kl_ref, vl_ref, kc_ref, vc_ref, o_ref, *scratch):
    set0, set1 = scratch[:3], scratch[3:]
    i = pl.program_id(0)

    @pl.when(i == 0)
    def _():
        for ref in set1:
            ref[...] = jnp.zeros(ref.shape, F32)

    @pl.when(lax.rem(i, 2) == 0)
    def _():
        _mla_step(q_ref, kl_ref, vl_ref, kc_ref, vc_ref, o_ref, set0, set1)

    @pl.when(lax.rem(i, 2) == 1)
    def _():
        _mla_step(q_ref, kl_ref, vl_ref, kc_ref, vc_ref, o_ref, set1, set0)


def _mla_attn(q, k_lat, vt_lat, k_ctx, vt_ctx):
    tq = MLA_TQ
    nq = SEQ // tq
    pairs = MLA_HEADS // 2
    n_items = BATCH * pairs * nq

    def item(s):
        return s // (pairs * nq), (s // nq) % pairs, s % nq

    def scored(s):
        return item(jnp.minimum(s, n_items - 1))

    def finished(s):
        return item(jnp.maximum(s - 1, 0))

    def q_map(s):
        b, hp, i = scored(s)
        return b * nq + i, hp

    def k_map(s):
        b, hp, _ = scored(s)
        return b, hp

    def v_map(s):
        b, hp, _ = finished(s)
        return hp, b

    def o_map(s):
        b, hp, i = finished(s)
        return hp, b * nq + i

    return pl.pallas_call(
        _mla_kernel,
        grid=(n_items + 1,),
        in_specs=[
            pl.BlockSpec((tq, 2 * LANES), q_map),
            pl.BlockSpec((SEQ, 2 * LANES), k_map),
            pl.BlockSpec((2 * MLA_V, SEQ), v_map),
            pl.BlockSpec((CTX_LEN, 2 * LANES), k_map),
            pl.BlockSpec((2 * MLA_V, CTX_LEN), v_map),
        ],
        out_specs=pl.BlockSpec((2 * MLA_V, tq), o_map),
        out_shape=jax.ShapeDtypeStruct((MLA_HEADS * MLA_V, N_TOK), BF16),
        scratch_shapes=2 * [pltpu.VMEM((2, SEQ, tq), F32),
                            pltpu.VMEM((2, CTX_LEN, tq), F32),
                            pltpu.VMEM((2, 1, tq), F32)],
        compiler_params=pltpu.CompilerParams(
            dimension_semantics=("arbitrary",),
            vmem_limit_bytes=VMEM_LIMIT),
        name="mla_attn",
    )(q, k_lat, vt_lat, k_ctx, vt_ctx)


def _swa_kernel(sink_ref, q_ref, k_ref, vt_ref, kc_ref, vct_ref, o_ref):
    tq = SWA_TQ
    i = pl.program_id(1)
    s0 = i * tq
    kstart = pl.multiple_of(jnp.clip(s0 - WINDOW, 0, SEQ - SWA_KW), LANES)
    kwin = k_ref[pl.ds(kstart, SWA_KW), :]
    vtwin = vt_ref[:, pl.ds(kstart, SWA_KW)]
    kc = kc_ref[...]
    vct = vct_ref[...]
    col = lax.broadcasted_iota(I32, (SWA_KW, SWA_GROUP * tq), 1)
    qpos = s0 + (col & (tq - 1))
    kpos = kstart + lax.broadcasted_iota(I32, (SWA_KW, SWA_GROUP * tq), 0)
    allowed = jnp.abs(qpos - kpos) <= WINDOW
    for kh in range(SWA_KV_HEADS):
        ksl = slice(kh * LANES, (kh + 1) * LANES)
        vrows = slice(kh * SWA_HD, (kh + 1) * SWA_HD)
        q4 = jnp.concatenate(
            [q_ref[:, (kh * SWA_GROUP + g) * LANES:(kh * SWA_GROUP + g + 1) * LANES]
             for g in range(SWA_GROUP)], axis=0)
        sb = jnp.where(allowed, _mm_nt(kwin[:, ksl], q4), NEG_INF)
        sc = _mm_nt(kc[:, ksl], q4)
        sink = jnp.concatenate(
            [jnp.full((1, tq), sink_ref[kh * SWA_GROUP + g] * LOG2E, F32) for g in range(SWA_GROUP)],
            axis=1)
        m = jnp.maximum(jnp.maximum(jnp.max(sb, axis=0, keepdims=True),
                                    jnp.max(sc, axis=0, keepdims=True)), sink)
        pb = jnp.exp2(sb - m)
        pc = jnp.exp2(sc - m)
        l = (jnp.sum(pb, axis=0, keepdims=True) + jnp.sum(pc, axis=0, keepdims=True)
             + jnp.exp2(sink - m))
        o = (_mm(vtwin[vrows, :], pb) + _mm(vct[vrows, :], pc)) / l
        for g in range(SWA_GROUP):
            hd = kh * SWA_GROUP + g
            o_ref[hd * SWA_HD:(hd + 1) * SWA_HD, :] = o[:, g * tq:(g + 1) * tq].astype(BF16)


def _swa_attn(sink, q, k_lat, vt_lat, k_ctx, vt_ctx):
    tq = SWA_TQ
    nq = SEQ // tq
    return pl.pallas_call(
        _swa_kernel,
        grid=(BATCH, nq),
        in_specs=[
            pl.BlockSpec(memory_space=pltpu.SMEM),
            pl.BlockSpec((tq, SWA_HEADS * LANES), lambda b, i: (b * nq + i, 0)),
            pl.BlockSpec((SEQ, SWA_KV_HEADS * LANES), lambda b, i: (b, 0)),
            pl.BlockSpec((SWA_KV_HEADS * SWA_HD, SEQ), lambda b, i: (0, b)),
            pl.BlockSpec((CTX_LEN, SWA_KV_HEADS * LANES), lambda b, i: (b, 0)),
            pl.BlockSpec((SWA_KV_HEADS * SWA_HD, CTX_LEN), lambda b, i: (0, b)),
        ],
        out_specs=pl.BlockSpec((SWA_HEADS * SWA_HD, tq), lambda b, i: (0, b * nq + i)),
        out_shape=jax.ShapeDtypeStruct((SWA_HEADS * SWA_HD, N_TOK), BF16),
        compiler_params=pltpu.CompilerParams(
            dimension_semantics=("arbitrary", "arbitrary"), vmem_limit_bytes=VMEM_LIMIT),
        name="swa_attn",
    )(sink, q, k_lat, vt_lat, k_ctx, vt_ctx)


def _merge_kernel(oa_ref, ob_ref, sa_ref, sb_ref, x_ref, g1_ref, sh2_ref, sc2_ref, wba_ref, wbb_ref,
                  wout_ref, g2_ref, wr_ref, br_ref, utri_ref, ltri_ref,
                  xnew_ref, h2_ref, pos_ref, wts_ref, seg_ref, carry_ref):
    i = pl.program_id(0)

    @pl.when(i == 0)
    def _():
        carry_ref[...] = jnp.zeros_like(carry_ref)

    for sub in range(MERGE_SUB):
        _merge_tile(sub, oa_ref, ob_ref, sa_ref, sb_ref, x_ref, g1_ref, sh2_ref, sc2_ref, wba_ref,
                    wbb_ref, wout_ref, g2_ref, wr_ref, br_ref, utri_ref, ltri_ref,
                    xnew_ref, h2_ref, pos_ref, wts_ref, seg_ref, carry_ref)


def _merge_tile(sub, oa_ref, ob_ref, sa_ref, sb_ref, x_ref, g1_ref, sh2_ref, sc2_ref, wba_ref, wbb_ref,
                wout_ref, g2_ref, wr_ref, br_ref, utri_ref, ltri_ref,
                xnew_ref, h2_ref, pos_ref, wts_ref, seg_ref, carry_ref):
    tok = slice(sub * MERGE_TM, (sub + 1) * MERGE_TM)
    segrows = slice(sub * SUBLANES, (sub + 1) * SUBLANES)
    ya = _mm_tn(oa_ref[:, tok], wba_ref[...])
    yb = _mm_tn(ob_ref[:, tok], wbb_ref[...])
    y = sa_ref[tok, :].astype(F32) * ya + sb_ref[tok, :].astype(F32) * yb
    z = _mm(y, wout_ref[...])
    xn = x_ref[tok, :] + g1_ref[...] * z
    xnew_ref[tok, :] = xn
    h2 = _rms(xn) * g2_ref[...] * (1.0 + sc2_ref[...]) + sh2_ref[...]
    h2_ref[tok, :] = h2.astype(BF16)

    hhi, hlo = _split(h2)
    whi, wlo = _split(wr_ref[...])
    lg = _mm_nt(whi, hhi) + _mm_nt(whi, hlo) + _mm_nt(wlo, hhi) + br_ref[...]
    eiota = lax.broadcasted_iota(I32, lg.shape, 0).astype(F32)
    vals, onehots = [], []
    for k in range(TOP_K):
        m = jnp.max(lg, axis=0, keepdims=True)
        ik = jnp.min(jnp.where(lg == m, eiota, float(N_EXPERTS)), axis=0, keepdims=True)
        hit = eiota == ik
        vals.append(m)
        onehots.append(hit.astype(F32))
        lg = jnp.where(hit, -jnp.inf, lg)
    es = [jnp.exp(v - vals[0]) for v in vals]
    tot = es[0] + es[1] + es[2] + es[3]
    for k in range(TOP_K):
        wts_ref[k:k + 1, tok] = es[k] / tot

    tots = [jnp.sum(oh, axis=1, keepdims=True) for oh in onehots]
    n = tots[0] + tots[1] + tots[2] + tots[3]
    m_al = jnp.floor((n + (SEG_ALIGN - 1)) * (1.0 / SEG_ALIGN))
    m_b = jnp.broadcast_to(m_al, (N_EXPERTS, LANES))
    lstart = _mm(ltri_ref[...], m_b) * float(SEG_ALIGN)
    off = jnp.zeros_like(n)
    for k in range(TOP_K):
        prefix = _mm(onehots[k], utri_ref[...])
        lp = jnp.sum(onehots[k] * (lstart[:, 0:1] + off + prefix), axis=0, keepdims=True)
        pos_ref[k:k + 1, tok] = lp.astype(I32)
        off = off + tots[k]
    eye = (lax.broadcasted_iota(I32, (N_EXPERTS, LANES), 0)
           == lax.broadcasted_iota(I32, (N_EXPERTS, LANES), 1)).astype(F32)
    to_row = lambda v: jnp.sum(v * eye, axis=0, keepdims=True).astype(I32)
    carry = carry_ref[...]
    seg_ref[segrows, :] = jnp.concatenate(
        [to_row(lstart), to_row(m_b * float(SEG_ALIGN)), to_row(carry),
         jnp.zeros((SUBLANES - 3, LANES), I32)], axis=0)
    carry_ref[...] = carry + m_b * float(SEG_ALIGN)


def _merge(o_a, o_b, sig_a, sig_b, x2d, mod3, w_ba, w_bb, w_out, g2, w_rt, b_r):
    tm = MERGE_TM * MERGE_SUB
    tiles_per_batch = SEQ // tm
    const = lambda shape: pl.BlockSpec(shape, lambda i: (0,) * len(shape))
    modspec = lambda j: pl.BlockSpec((None, 1, D_MODEL), lambda i: (i // tiles_per_batch, 0, j))
    row = lambda w: pl.BlockSpec((tm, w), lambda i: (i, 0))
    utri = (jnp.arange(MERGE_TM)[:, None] < jnp.arange(MERGE_TM)[None, :]).astype(BF16)
    ltri = (jnp.arange(N_EXPERTS)[None, :] < jnp.arange(N_EXPERTS)[:, None]).astype(BF16)
    return pl.pallas_call(
        _merge_kernel,
        grid=(N_TILES // MERGE_SUB,),
        in_specs=[pl.BlockSpec((512, tm), lambda i: (0, i)), pl.BlockSpec((512, tm), lambda i: (0, i)),
                  row(D_MODEL), row(D_MODEL), row(D_MODEL),
                  modspec(2), modspec(3), modspec(4),
                  const(w_ba.shape), const(w_bb.shape), const(w_out.shape), const(g2.shape),
                  const(w_rt.shape), const(b_r.shape), const(utri.shape), const(ltri.shape)],
        out_specs=[row(D_MODEL), row(D_MODEL),
                   pl.BlockSpec((TOP_K, tm), lambda i: (0, i)),
                   pl.BlockSpec((TOP_K, tm), lambda i: (0, i)),
                   pl.BlockSpec((MERGE_SUB * SUBLANES, LANES), lambda i: (i, 0))],
        out_shape=[jax.ShapeDtypeStruct((N_TOK, D_MODEL), F32),
                   jax.ShapeDtypeStruct((N_TOK, D_MODEL), BF16),
                   jax.ShapeDtypeStruct((TOP_K, N_TOK), I32),
                   jax.ShapeDtypeStruct((TOP_K, N_TOK), F32),
                   jax.ShapeDtypeStruct((N_TILES * SUBLANES, LANES), I32)],
        scratch_shapes=[pltpu.VMEM((N_EXPERTS, LANES), F32)],
        compiler_params=pltpu.CompilerParams(
            dimension_semantics=("arbitrary",), vmem_limit_bytes=VMEM_LIMIT),
        name="merge",
    )(o_a, o_b, sig_a, sig_b, x2d, mod3, mod3, mod3, w_ba, w_bb, w_out, g2, w_rt, b_r, utri, ltri)


def _chunk_copy(hbm, hbm_row, buf, buf_row, size, to_hbm, sem):
    vm = buf.at[pl.ds(pl.multiple_of(buf_row, SEG_ALIGN), size)]
    hb = hbm.at[pl.ds(pl.multiple_of(hbm_row, SEG_ALIGN), size)]
    return pltpu.make_async_copy(vm, hb, sem) if to_hbm else pltpu.make_async_copy(hb, vm, sem)


def _seg_start(seg_smem, tile, hbm, buf, to_hbm, sem):
    base = tile * SEG_W

    def chunks(src, rows, dst, bits):
        for b in bits:
            size = 1 << b
            done = rows & (-2 * size)

            @pl.when((rows & size) != 0)
            def _():
                _chunk_copy(hbm, dst + done, buf, src + done, size, to_hbm, sem).start()

    def per_expert(e, carry):
        src = seg_smem[base + e]
        rows = seg_smem[base + N_EXPERTS + e]
        dst = seg_smem[base + 2 * N_EXPERTS + e]

        @pl.when(rows >= SEG_BIG)
        def _():
            chunks(src, rows, dst, [b for b in SEG_BITS if (1 << b) >= SEG_BIG])

        chunks(src, rows, dst, [b for b in SEG_BITS if (1 << b) < SEG_BIG])
        return carry

    lax.fori_loop(0, N_EXPERTS, per_expert, 0)


def _seg_wait(seg_smem, tile, hbm, buf, to_hbm, sem):
    assert LROWS < 2 << SEG_BITS[-1]
    total = seg_smem[tile * SEG_W + SEG_TOTAL]
    for b in SEG_BITS:
        size = 1 << b

        @pl.when((total & size) != 0)
        def _():
            _chunk_copy(hbm, 0, buf, 0, size, to_hbm, sem).wait()


def _dispatch_kernel(pend_ref, padded_ref, seg_ref, pos_ref, h2_ref, xs_ref, staged, sems):
    j = pl.program_id(0)
    slot = lax.rem(j, 2)

    @pl.when(j == 0)
    def _():
        staged[0, 0:MOE_BLK, :] = jnp.zeros((MOE_BLK, PACK_W), U32)

        def tail(action):
            def body(e, carry):
                @pl.when(padded_ref[e] > 0)
                def _():
                    start = pl.multiple_of(pend_ref[e] - MOE_BLK, MOE_BLK)
                    cp = pltpu.make_async_copy(staged.at[0, pl.ds(0, MOE_BLK)],
                                               xs_ref.at[pl.ds(start, MOE_BLK)], sems.at[0])
                    getattr(cp, action)()
                return carry
            lax.fori_loop(0, N_EXPERTS, body, 0)

        def unused(action):
            def body(b, carry):
                cp = pltpu.make_async_copy(
                    staged.at[0, pl.ds(0, MOE_BLK)],
                    xs_ref.at[pl.ds(pl.multiple_of(b * MOE_BLK, MOE_BLK), MOE_BLK)], sems.at[0])
                getattr(cp, action)()
                return carry
            lax.fori_loop(pend_ref[N_EXPERTS - 1] // MOE_BLK, N_BLK, body, 0)

        tail("start")
        unused("start")
        tail("wait")
        unused("wait")

    pos = pos_ref[...]
    piota = lax.broadcasted_iota(I32, (LROWS, MERGE_TM), 0)
    sel = (piota == pos[0:1, :]).astype(F32)
    for k in range(1, TOP_K):
        sel = sel + (piota == pos[k:k + 1, :]).astype(F32)
    staged[slot] = _pack_rows(_mm(sel, h2_ref[...]))

    _seg_start(seg_ref, j, xs_ref, staged.at[slot], True, sems.at[1 + slot])

    @pl.when(j > 0)
    def _():
        _seg_wait(seg_ref, j - 1, xs_ref, staged.at[1 - slot], True, sems.at[2 - slot])

    @pl.when(j == N_TILES - 1)
    def _():
        _seg_wait(seg_ref, j, xs_ref, staged.at[slot], True, sems.at[1 + slot])


def _dispatch(pends, padded, segtab, pos, h2):
    grid_spec = pltpu.PrefetchScalarGridSpec(
        num_scalar_prefetch=3,
        grid=(N_TILES,),
        in_specs=[
            pl.BlockSpec((TOP_K, MERGE_TM), lambda j, pe, pa, sg: (0, j)),
            pl.BlockSpec((MERGE_TM, D_MODEL), lambda j, pe, pa, sg: (j, 0)),
        ],
        out_specs=pl.BlockSpec(memory_space=pl.ANY),
        scratch_shapes=[pltpu.VMEM((2, LROWS, PACK_W), U32),
                        pltpu.SemaphoreType.DMA((3,))],
    )
    return pl.pallas_call(
        _dispatch_kernel,
        grid_spec=grid_spec,
        out_shape=jax.ShapeDtypeStruct((N_SLOTS, PACK_W), U32),
        compiler_params=pltpu.CompilerParams(
            dimension_semantics=("arbitrary",), vmem_limit_bytes=VMEM_LIMIT),
        name="dispatch",
    )(pends, padded, segtab, pos, h2)


SPLIT_SUB = 256


def _expert_kernel(be_ref, nused_ref, xs_ref, wgu_ref, bg_ref, bl_ref, wd_ref, bd_ref, perm_ref, o_ref,
                   wg_s, wl_s, wd_s):
    i = pl.program_id(0)
    used = i < nused_ref[0]
    fresh = jnp.logical_or(i == 0, be_ref[i] != be_ref[jnp.maximum(i - 1, 0)])

    @pl.when(jnp.logical_and(used, fresh))
    def _():
        half = SPLIT_SUB // 2
        for s in range(2 * D_EXPERT // SPLIT_SUB):
            r = _mm(wgu_ref[:, s * SPLIT_SUB:(s + 1) * SPLIT_SUB], perm_ref[...])
            wg_s[:, s * half:(s + 1) * half] = r[:, :half].astype(BF16)
            wl_s[:, s * half:(s + 1) * half] = r[:, half:].astype(BF16)
        wd_s[...] = wd_ref[...].astype(BF16)

    @pl.when(used)
    def _():
        xb = _unpack_rows(xs_ref[...])
        g = jnp.minimum(_mm(xb, wg_s[...]) + bg_ref[...], SWIGLU_LIMIT)
        l = jnp.clip(_mm(xb, wl_s[...]) + bl_ref[...], -SWIGLU_LIMIT, SWIGLU_LIMIT)
        act = g * jax.nn.sigmoid(SWIGLU_ALPHA * g) * (l + 1.0)
        o_ref[...] = _pack_rows(_mm(act, wd_s[...]) + bd_ref[...])

    @pl.when(jnp.logical_not(used))
    def _():
        o_ref[...] = jnp.zeros_like(o_ref)


def _experts(block_expert, nused, xs, w_gu, bg, bl, w_dn, bd):
    col = jnp.arange(SPLIT_SUB)
    src = jnp.where(col < SPLIT_SUB // 2, 2 * col, 2 * (col - SPLIT_SUB // 2) + 1)
    perm = (jnp.arange(SPLIT_SUB)[:, None] == src[None, :]).astype(BF16)
    blk = lambda i, be, nu: (jnp.minimum(i, nu[0] - 1), 0)
    per_expert = lambda shape: pl.BlockSpec((None,) + shape, lambda i, be, nu: (be[i], 0, 0))
    grid_spec = pltpu.PrefetchScalarGridSpec(
        num_scalar_prefetch=2,
        grid=(N_BLK,),
        in_specs=[pl.BlockSpec((MOE_BLK, PACK_W), blk),
                  per_expert((D_MODEL, 2 * D_EXPERT)), per_expert((1, D_EXPERT)), per_expert((1, D_EXPERT)),
                  per_expert((D_EXPERT, D_MODEL)), per_expert((1, D_MODEL)),
                  pl.BlockSpec((SPLIT_SUB, SPLIT_SUB), lambda i, be, nu: (0, 0))],
        out_specs=pl.BlockSpec((MOE_BLK, PACK_W), lambda i, be, nu: (i, 0)),
        scratch_shapes=[pltpu.VMEM((D_MODEL, D_EXPERT), BF16), pltpu.VMEM((D_MODEL, D_EXPERT), BF16),
                        pltpu.VMEM((D_EXPERT, D_MODEL), BF16)],
    )
    return pl.pallas_call(
        _expert_kernel,
        grid_spec=grid_spec,
        out_shape=jax.ShapeDtypeStruct((N_SLOTS, PACK_W), U32),
        compiler_params=pltpu.CompilerParams(
            dimension_semantics=("arbitrary",), vmem_limit_bytes=VMEM_LIMIT),
        name="experts",
    )(block_expert, nused, xs, w_gu, bg, bl, w_dn, bd, perm)


def _combine_kernel(seg_ref, eo_ref, xnew_ref, g2_ref, pos_ref, wt_ref, o_ref, gbuf, sems):
    j = pl.program_id(0)
    slot = lax.rem(j, 2)

    def fetch(tile, sl):
        _seg_start(seg_ref, tile, eo_ref, gbuf.at[sl], False, sems.at[sl])

    @pl.when(j == 0)
    def _():
        gbuf[...] = jnp.zeros_like(gbuf)
        fetch(0, 0)

    @pl.when(j + 1 < N_TILES)
    def _():
        fetch(j + 1, 1 - slot)

    _seg_wait(seg_ref, j, eo_ref, gbuf.at[slot], False, sems.at[slot])

    pos = pos_ref[...]
    wt = wt_ref[...]
    lane = lax.broadcasted_iota(I32, (MERGE_TM, LROWS), 1)
    wsel = jnp.where(lane == pos[:, 0:1], wt[:, 0:1], 0.0)
    for k in range(1, TOP_K):
        wsel = wsel + jnp.where(lane == pos[:, k:k + 1], wt[:, k:k + 1], 0.0)
    acc = _mm(wsel, _unpack_rows(gbuf[slot]))
    o_ref[...] = xnew_ref[...] + g2_ref[...] * acc


def _combine(segtab, eo, x_new, mod3, pos_t, wts_t):
    tiles_per_batch = SEQ // MERGE_TM
    grid_spec = pltpu.PrefetchScalarGridSpec(
        num_scalar_prefetch=1,
        grid=(N_TILES,),
        in_specs=[
            pl.BlockSpec(memory_space=pl.ANY),
            pl.BlockSpec((MERGE_TM, D_MODEL), lambda j, sg: (j, 0)),
            pl.BlockSpec((None, 1, D_MODEL), lambda j, sg: (j // tiles_per_batch, 0, 5)),
            pl.BlockSpec((MERGE_TM, TOP_K), lambda j, sg: (j, 0)),
            pl.BlockSpec((MERGE_TM, TOP_K), lambda j, sg: (j, 0)),
        ],
        out_specs=pl.BlockSpec((MERGE_TM, D_MODEL), lambda j, sg: (j, 0)),
        scratch_shapes=[pltpu.VMEM((2, LROWS, PACK_W), U32),
                        pltpu.SemaphoreType.DMA((2,))],
    )
    return pl.pallas_call(
        _combine_kernel,
        grid_spec=grid_spec,
        out_shape=jax.ShapeDtypeStruct((N_TOK, D_MODEL), F32),
        compiler_params=pltpu.CompilerParams(
            dimension_semantics=("arbitrary",), vmem_limit_bytes=VMEM_LIMIT),
        name="combine",
    )(segtab, eo, x_new, mod3, pos_t, wts_t)


def _pad_heads(w, heads, dim):
    lead = w.shape[:-1]
    w = w.reshape(lead + (heads, dim))
    w = jnp.pad(w, [(0, 0)] * len(lead) + [(0, 0), (0, LANES - dim)])
    return w.reshape(lead + (heads * LANES,))


def _rope_tables(dim, offset):
    pos = jnp.arange(SEQ, dtype=jnp.int32)
    row = (pos // GRID_W).astype(F32)
    col = (pos % GRID_W).astype(F32)
    q = dim // 4
    freqs = ROPE_THETA ** (-jnp.arange(q, dtype=F32) / q)
    ang_r = row[:, None] * freqs
    ang_c = col[:, None] * freqs
    ang = jnp.concatenate([ang_r, ang_r, ang_c, ang_c], axis=-1)
    cos, sin = jnp.cos(ang), jnp.sin(ang)
    first = (jnp.arange(dim) % (2 * q)) < q
    sa = jnp.where(first, -sin, 0.0)
    sb = jnp.where(first, 0.0, sin)
    pad = lambda t, v: jnp.pad(t, ((0, 0), (offset, LANES - offset - dim)), constant_values=v)
    return pad(cos, 1.0), pad(sa, 0.0), pad(sb, 0.0)


def kernel(x, c, ctx, c_ctx, w_ada, b_ada, norm1_g, norm2_g, w_in, mla_q_a_g, mla_kv_a_g, w_q_up, w_kv_up,
           mla_q_g, mla_k_g, swa_q_g, swa_k_g, swa_sink, w_branch_a, w_branch_b, w_out, w_router, b_router,
           w_gate_up, b_gate_up, w_down, b_down):
    assert x.shape == (BATCH, SEQ, D_MODEL) and ctx.shape == (BATCH, CTX_LEN, D_MODEL)
    assert w_ada.shape[0] == 1, "single layer"

    cond = jnp.concatenate([c, c_ctx[None], jnp.zeros((MOD_ROWS - BATCH - 1, D_MODEL), F32)], axis=0)
    mod = _ada(cond, w_ada[0], b_ada[0][None])
    mod3 = mod.reshape(MOD_ROWS, 1, N_MOD * D_MODEL)

    wi = w_in[0]
    o1, o2, o3 = MLA_KV_RANK, MLA_KV_RANK + MLA_ROPE, MLA_KV_RANK + MLA_ROPE + SWA_KV_HEADS * SWA_HD
    kv_cols = o3 + SWA_KV_HEADS * SWA_HD
    q1 = kv_cols + MLA_Q_RANK
    q2 = q1 + SWA_HEADS * SWA_HD
    q3 = q2 + D_MODEL
    krope_grp = jnp.pad(wi[:, o1:o2], ((0, 0), (MLA_NOPE, LANES - MLA_QK)))
    win = jnp.concatenate([
        wi[:, 0:o1], _pad_heads(wi[:, o2:o3], SWA_KV_HEADS, SWA_HD), krope_grp,
        wi[:, kv_cols:q1], _pad_heads(wi[:, q1:q2], SWA_HEADS, SWA_HD), wi[:, q2:q3], wi[:, q3:]],
        axis=1).astype(BF16)
    assert win.shape == (D_MODEL, C_END)
    wvst = wi[:, o3:kv_cols].T.astype(BF16)
    wkv = w_kv_up[0].reshape(MLA_KV_RANK, MLA_HEADS, MLA_NOPE + MLA_V)
    wkn = _pad_heads(wkv[:, :, :MLA_NOPE].reshape(MLA_KV_RANK, -1), MLA_HEADS, MLA_NOPE).astype(BF16)
    wvt = wkv[:, :, MLA_NOPE:].reshape(MLA_KV_RANK, MLA_HEADS * MLA_V).T.astype(BF16)
    wqup = _pad_heads(w_q_up[0], MLA_HEADS, MLA_QK).astype(BF16)
    gk_t = _pad_heads(jnp.tile(mla_k_g[0], MLA_HEADS)[None], MLA_HEADS, MLA_QK)
    gq_t = _pad_heads(jnp.tile(mla_q_g[0] * (MLA_QK ** -0.5 * LOG2E), MLA_HEADS)[None], MLA_HEADS, MLA_QK)
    gks_t = _pad_heads(jnp.tile(swa_k_g[0], SWA_KV_HEADS)[None], SWA_KV_HEADS, SWA_HD)
    gqs_t = _pad_heads(jnp.tile(swa_q_g[0] * (SWA_HD ** -0.5 * LOG2E), SWA_HEADS)[None], SWA_HEADS, SWA_HD)
    head_of_lane = jnp.arange(MLA_HEADS * LANES) // LANES
    ind = (head_of_lane[:, None] == jnp.arange(LANES)[None, :]).astype(BF16)
    indt = jnp.concatenate([ind.T, ind.T], axis=0)
    consts = [norm1_g[0][None], mla_kv_a_g[0][None], wkn, wvt, wvst, gk_t, gks_t, ind, indt]
    q_consts = [mla_q_a_g[0][None], wqup, gq_t, gqs_t]

    tabs_lat = _rope_tables(MLA_ROPE, MLA_NOPE) + _rope_tables(SWA_HD, 0)
    ones = jnp.ones((PROJ_TM, LANES), F32)
    zeros = jnp.zeros((PROJ_TM, LANES), F32)
    tabs_ctx = (ones, zeros, zeros, ones, zeros, zeros)

    x2d = x.reshape(N_TOK, D_MODEL)
    tiles_per_batch = SEQ // PROJ_TM
    k_mla, vt_mla, k_swa, vt_swa, q_mla, q_swa, sig_a, sig_b = _proj(
        x2d, mod3, lambda i: i // tiles_per_batch, lambda i: i % tiles_per_batch,
        tabs_lat, consts, q_consts, win, True, "proj_latent")
    kc_mla, vct_mla, kc_swa, vct_swa = _proj(
        ctx.reshape(BATCH * CTX_LEN, D_MODEL), mod3, lambda i: BATCH, lambda i: 0,
        tabs_ctx, consts, q_consts, win[:, :C_KVEND], False, "proj_ctx")

    o_a = _mla_attn(q_mla, k_mla, vt_mla, kc_mla, vct_mla)
    o_b = _swa_attn(swa_sink[0], q_swa, k_swa, vt_swa, kc_swa, vct_swa)

    x_new, h2, pos, top_w, seg = _merge(
        o_a, o_b, sig_a, sig_b, x2d, mod3,
        w_branch_a[0].astype(BF16), w_branch_b[0].astype(BF16), w_out[0].astype(BF16),
        norm2_g[0][None], w_router[0].T, b_router[0][:, None])

    seg3 = seg.reshape(N_TILES, SUBLANES, LANES)
    lstart, rows, gcarry = (seg3[:, r, :N_EXPERTS] for r in range(3))
    total = gcarry[-1] + rows[-1]
    padded = (total + MOE_BLK - 1) // MOE_BLK * MOE_BLK
    pends = jnp.cumsum(padded).astype(I32)
    pstarts = pends - padded
    nused = (pends[-1] // MOE_BLK).astype(I32)[None]
    blk_start = jnp.arange(N_BLK, dtype=I32) * MOE_BLK
    block_expert = jnp.minimum(
        jnp.sum((blk_start[:, None] >= pends[None, :]).astype(I32), axis=1), N_EXPERTS - 1)
    segtab = jnp.concatenate(
        [lstart, rows, pstarts[None, :] + gcarry,
         jnp.broadcast_to(jnp.sum(rows, axis=1, keepdims=True), rows.shape)], axis=1).reshape(-1).astype(I32)

    xs = _dispatch(pends, padded.astype(I32), segtab, pos, h2)
    bgu = b_gate_up[0]
    eo = _experts(block_expert, nused, xs, w_gate_up[0], bgu[:, None, 0::2], bgu[:, None, 1::2],
                  w_down[0], b_down[0][:, None, :])
    out = _combine(segtab, eo, x_new, mod3, pos.T, top_w.T)
    return out.reshape(BATCH, SEQ, D_MODEL)
```

```python
import functools

import jax
import jax.numpy as jnp
from jax import lax
from jax.experimental import pallas as pl
from jax.experimental.pallas import tpu as pltpu

F32 = jnp.float32
BF16 = jnp.bfloat16
I32 = jnp.int32
U32 = jnp.uint32

LANES = 128
SUBLANES = 8

D_MODEL = 1024
BATCH = 8
SEQ = 4096
CTX_LEN = 256
GRID_W = 64
ROPE_THETA = 10000.0
NORM_EPS = 1e-6
N_MOD = 6
NEG_INF = -1e30
LOG2E = 1.4426950408889634

MLA_HEADS = 8
MLA_NOPE = 64
MLA_ROPE = 32
MLA_QK = MLA_NOPE + MLA_ROPE
MLA_V = 64
MLA_Q_RANK = 384
MLA_KV_RANK = 256

SWA_HEADS = 8
SWA_KV_HEADS = 2
SWA_GROUP = SWA_HEADS // SWA_KV_HEADS
SWA_HD = 64
WINDOW = 128

N_EXPERTS = 32
TOP_K = 4
D_EXPERT = D_MODEL
SWIGLU_LIMIT = 7.0
SWIGLU_ALPHA = 1.702

N_TOK = BATCH * SEQ
MOD_ROWS = 16

ADA_TN = 1536
PROJ_TM = 256
MLA_TQ = 256
SWA_TQ = 256
SWA_KW = SWA_TQ + 2 * WINDOW
MERGE_TM = 256
MERGE_SUB = 2
MOE_BLK = 512
SEG_ALIGN = SUBLANES
N_TILES = N_TOK // MERGE_TM
LROWS = -(-(MERGE_TM * TOP_K + N_EXPERTS * (SEG_ALIGN - 1)) // LANES) * LANES
SEG_BITS = tuple(range(SEG_ALIGN.bit_length() - 1, (MERGE_TM * TOP_K).bit_length()))
SEG_W = 128
SEG_TOTAL = 3 * N_EXPERTS
SEG_BIG = 128
N_SLOTS = -(-(N_TOK * TOP_K + N_TILES * N_EXPERTS * (SEG_ALIGN - 1) + N_EXPERTS * (MOE_BLK - 1))
            // MOE_BLK) * MOE_BLK
N_BLK = N_SLOTS // MOE_BLK

PACK_W = D_MODEL // 2

VMEM_LIMIT = 56 * 1024 * 1024

C_KVLAT = 0
C_KSWA = C_KVLAT + MLA_KV_RANK
C_KROPE = C_KSWA + SWA_KV_HEADS * SWA_HD
C_KVEND = C_KROPE + LANES
C_QLAT = C_KVEND
C_QSWA = C_QLAT + MLA_Q_RANK
C_GA = C_QSWA + SWA_HEADS * SWA_HD
C_GB = C_GA + D_MODEL
C_END = C_GB + D_MODEL


def _mm(a, b):
    return jnp.dot(a.astype(BF16), b.astype(BF16), preferred_element_type=F32)


def _mm_nt(a, b):
    return lax.dot_general(a.astype(BF16), b.astype(BF16), (((1,), (1,)), ((), ())),
                           preferred_element_type=F32)


def _mm_tn(a, b):
    return lax.dot_general(a.astype(BF16), b.astype(BF16), (((0,), (0,)), ((), ())),
                           preferred_element_type=F32)


def _split(a):
    hi = a.astype(BF16)
    lo = (a - hi.astype(F32)).astype(BF16)
    return hi, lo


def _pack_rows(a):
    half = a.shape[1] // 2
    bits = lambda v: lax.bitcast_convert_type(v.astype(BF16).astype(F32), U32)
    return bits(a[:, half:]) | (bits(a[:, :half]) >> 16)


def _unpack_rows(p):
    lo = lax.bitcast_convert_type(p << 16, F32)
    hi = lax.bitcast_convert_type(p & jnp.uint32(0xFFFF0000), F32)
    return jnp.concatenate([lo, hi], axis=1).astype(BF16)


def _rms(x):
    return x * lax.rsqrt(jnp.mean(x * x, axis=-1, keepdims=True) + NORM_EPS)


def _ada_kernel(c_ref, w_ref, b_ref, o_ref):
    c = c_ref[...]
    s = c * jax.nn.sigmoid(c)
    shi, slo = _split(s)
    whi, wlo = _split(w_ref[...])
    acc = _mm(shi, whi) + _mm(slo, whi) + _mm(shi, wlo)
    o_ref[...] = acc + b_ref[...]


def _ada(cond, w_ada, b_ada):
    n = w_ada.shape[1]
    return pl.pallas_call(
        _ada_kernel,
        grid=(n // ADA_TN,),
        in_specs=[
            pl.BlockSpec((MOD_ROWS, D_MODEL), lambda j: (0, 0)),
            pl.BlockSpec((D_MODEL, ADA_TN), lambda j: (0, j)),
            pl.BlockSpec((1, ADA_TN), lambda j: (0, j)),
        ],
        out_specs=pl.BlockSpec((MOD_ROWS, ADA_TN), lambda j: (0, j)),
        out_shape=jax.ShapeDtypeStruct((MOD_ROWS, n), F32),
        compiler_params=pltpu.CompilerParams(
            dimension_semantics=("arbitrary",), vmem_limit_bytes=VMEM_LIMIT),
        name="ada",
    )(cond, w_ada, b_ada)


def _head_norm(xraw, ind_ref, indt_ref, inv_dim):
    w = xraw.shape[1]
    ss = _mm(xraw * xraw, ind_ref[0:w, :])
    r = lax.rsqrt(ss * inv_dim + NORM_EPS)
    scale = _mm(jnp.concatenate(_split(r), axis=1), indt_ref[:, 0:w])
    return xraw * scale


def _rope(xh, cos, sa, sb, quarter):
    return (xh * cos + pltpu.roll(xh, LANES - quarter, 1) * sa
            + pltpu.roll(xh, quarter, 1) * sb)


def _proj_kernel(*refs, with_q):
    (x_ref, sh_ref, sc_ref, g1_ref, win_ref, cm_ref, sam_ref, sbm_ref, cs_ref, sas_ref, sbs_ref,
     gkva_ref, wkn_ref, wvt_ref, wvst_ref, gk_ref, gks_ref, ind_ref, indt_ref,
     ind64_ref, indt64_ref) = refs[:21]
    if with_q:
        gqa_ref, wqup_ref, gq_ref, gqs_ref = refs[21:25]
        kmla_ref, vmla_ref, kswa_ref, vswa_ref, qmla_ref, qswa_ref, siga_ref, sigb_ref = refs[25:]
    else:
        kmla_ref, vmla_ref, kswa_ref, vswa_ref = refs[21:]

    x = x_ref[...]
    h = _rms(x) * g1_ref[...] * (1.0 + sc_ref[...]) + sh_ref[...]
    y = _mm(h, win_ref[...])

    cm, sam, sbm = cm_ref[...], sam_ref[...], sbm_ref[...]
    cs, sas, sbs = cs_ref[...], sas_ref[...], sbs_ref[...]

    kvn = _rms(y[:, C_KVLAT:C_KVLAT + MLA_KV_RANK]) * gkva_ref[...]
    kn = _mm(kvn, wkn_ref[...])
    vmla_ref[...] = _mm_nt(wvt_ref[...], kvn).astype(BF16)
    vswa_ref[...] = _mm_nt(wvst_ref[...], h).astype(BF16)
    kr = y[:, C_KROPE:C_KROPE + LANES]
    kraw = kn + jnp.concatenate([kr] * MLA_HEADS, axis=1)
    kfull = _head_norm(kraw, ind_ref, indt_ref, 1.0 / MLA_QK) * gk_ref[...]
    for hd in range(MLA_HEADS):
        sl = slice(hd * LANES, (hd + 1) * LANES)
        kmla_ref[:, sl] = _rope(kfull[:, sl], cm, sam, sbm, MLA_ROPE // 4).astype(BF16)

    ks = _head_norm(y[:, C_KSWA:C_KSWA + SWA_KV_HEADS * SWA_HD], ind64_ref, indt64_ref,
                    1.0 / SWA_HD) * gks_ref[...]
    for grp in range(SWA_KV_HEADS * SWA_HD // LANES):
        sl = slice(grp * LANES, (grp + 1) * LANES)
        kswa_ref[:, sl] = _rope(ks[:, sl], cs, sas, sbs, SWA_HD // 4).astype(BF16)

    if with_q:
        qn = _rms(y[:, C_QLAT:C_QLAT + MLA_Q_RANK]) * gqa_ref[...]
        qraw = _mm(qn, wqup_ref[...])
        qf = _head_norm(qraw, ind_ref, indt_ref, 1.0 / MLA_QK) * gq_ref[...]
        for hd in range(MLA_HEADS):
            sl = slice(hd * LANES, (hd + 1) * LANES)
            qmla_ref[:, sl] = _rope(qf[:, sl], cm, sam, sbm, MLA_ROPE // 4).astype(BF16)
        qs = _head_norm(y[:, C_QSWA:C_QSWA + SWA_HEADS * SWA_HD], ind64_ref, indt64_ref,
                        1.0 / SWA_HD) * gqs_ref[...]
        for grp in range(SWA_HEADS * SWA_HD // LANES):
            sl = slice(grp * LANES, (grp + 1) * LANES)
            qswa_ref[:, sl] = _rope(qs[:, sl], cs, sas, sbs, SWA_HD // 4).astype(BF16)
        siga_ref[...] = jax.nn.sigmoid(y[:, C_GA:C_GA + D_MODEL]).astype(BF16)
        sigb_ref[...] = jax.nn.sigmoid(y[:, C_GB:C_GB + D_MODEL]).astype(BF16)


def _proj(x2d, mod3, mod_row_fn, tab_row_fn, tabs, consts, q_consts, win, with_q, name):
    rows = x2d.shape[0]
    tm = PROJ_TM
    const = lambda shape: pl.BlockSpec(shape, lambda i: (0,) * len(shape))
    tab = pl.BlockSpec((tm, LANES), lambda i: (tab_row_fn(i), 0))
    g1, rest = consts[0], consts[1:]
    in_specs = [
        pl.BlockSpec((tm, D_MODEL), lambda i: (i, 0)),
        pl.BlockSpec((None, 1, D_MODEL), lambda i: (mod_row_fn(i), 0, 0)),
        pl.BlockSpec((None, 1, D_MODEL), lambda i: (mod_row_fn(i), 0, 1)),
        const(g1.shape),
        pl.BlockSpec(win.shape, lambda i: (0, 0), pipeline_mode=pl.Buffered(1)),
        tab, tab, tab, tab, tab, tab,
    ] + [const(c.shape) for c in rest]
    args = [x2d, mod3, mod3, g1, win] + list(tabs) + list(rest)
    outs = [(MLA_HEADS * LANES, False), (MLA_HEADS * MLA_V, True),
            (SWA_KV_HEADS * SWA_HD, False), (SWA_KV_HEADS * SWA_HD, True)]
    if with_q:
        in_specs += [const(c.shape) for c in q_consts]
        args += list(q_consts)
        outs += [(MLA_HEADS * LANES, False), (SWA_HEADS * SWA_HD, False), (D_MODEL, False), (D_MODEL, False)]
    return pl.pallas_call(
        functools.partial(_proj_kernel, with_q=with_q),
        grid=(rows // tm,),
        in_specs=in_specs,
        out_specs=[pl.BlockSpec((w, tm), lambda i: (0, i)) if t else pl.BlockSpec((tm, w), lambda i: (i, 0))
                   for w, t in outs],
        out_shape=[jax.ShapeDtypeStruct((w, rows) if t else (rows, w), BF16) for w, t in outs],
        compiler_params=pltpu.CompilerParams(
            dimension_semantics=("arbitrary",), vmem_limit_bytes=VMEM_LIMIT),
        name=name,
    )(*args)


def _mla_step(q_ref, kl_ref, vlt_ref, kc_ref, vct_ref, o_ref, cur, prev):
    s_lat_c, s_ctx_c, m_c = cur
    s_lat_p, s_ctx_p, m_p = prev
    outs = []
    for hh in range(2):
        sl = slice(hh * LANES, (hh + 1) * LANES)
        vrows = slice(hh * MLA_V, (hh + 1) * MLA_V)
        q = q_ref[:, sl]
        s1 = _mm_nt(kl_ref[:, sl], q)
        s2 = _mm_nt(kc_ref[:, sl], q)
        s_lat_c[hh] = s1
        s_ctx_c[hh] = s2
        m_c[hh] = jnp.maximum(jnp.max(s1, axis=0, keepdims=True),
                              jnp.max(s2, axis=0, keepdims=True))

        m = m_p[hh]
        p1 = jnp.exp2(s_lat_p[hh] - m)
        p2 = jnp.exp2(s_ctx_p[hh] - m)
        l = jnp.sum(p1, axis=0, keepdims=True) + jnp.sum(p2, axis=0, keepdims=True)
        o = _mm(vlt_ref[vrows, :], p1) + _mm(vct_ref[vrows, :], p2)
        outs.append(o / l)
    o_ref[...] = jnp.concatenate(outs, axis=0).astype(BF16)


def _mla_kernel(q_ref, kl_ref, vl_ref, kc_ref, vc_ref, o_ref, *scratch):
    set0, set1 = scratch[:3], scratch[3:]
    i = pl.program_id(0)

    @pl.when(i == 0)
    def _():
        for ref in set1:
            ref[...] = jnp.zeros(ref.shape, F32)

    @pl.when(lax.rem(i, 2) == 0)
    def _():
        _mla_step(q_ref, kl_ref, vl_ref, kc_ref, vc_ref, o_ref, set0, set1)

    @pl.when(lax.rem(i, 2) == 1)
    def _():
        _mla_step(q_ref, kl_ref, vl_ref, kc_ref, vc_ref, o_ref, set1, set0)


def _mla_attn(q, k_lat, vt_lat, k_ctx, vt_ctx):
    tq = MLA_TQ
    nq = SEQ // tq
    pairs = MLA_HEADS // 2
    n_items = BATCH * pairs * nq

    def item(s):
        return s // (pairs * nq), (s // nq) % pairs, s % nq

    def scored(s):
        return item(jnp.minimum(s, n_items - 1))

    def finished(s):
        return item(jnp.maximum(s - 1, 0))

    def q_map(s):
        b, hp, i = scored(s)
        return b * nq + i, hp

    def k_map(s):
        b, hp, _ = scored(s)
        return b, hp

    def v_map(s):
        b, hp, _ = finished(s)
        return hp, b

    def o_map(s):
        b, hp, i = finished(s)
        return hp, b * nq + i

    return pl.pallas_call(
        _mla_kernel,
        grid=(n_items + 1,),
        in_specs=[
            pl.BlockSpec((tq, 2 * LANES), q_map),
            pl.BlockSpec((SEQ, 2 * LANES), k_map),
            pl.BlockSpec((2 * MLA_V, SEQ), v_map),
            pl.BlockSpec((CTX_LEN, 2 * LANES), k_map),
            pl.BlockSpec((2 * MLA_V, CTX_LEN), v_map),
        ],
        out_specs=pl.BlockSpec((2 * MLA_V, tq), o_map),
        out_shape=jax.ShapeDtypeStruct((MLA_HEADS * MLA_V, N_TOK), BF16),
        scratch_shapes=2 * [pltpu.VMEM((2, SEQ, tq), F32),
                            pltpu.VMEM((2, CTX_LEN, tq), F32),
                            pltpu.VMEM((2, 1, tq), F32)],
        compiler_params=pltpu.CompilerParams(
            dimension_semantics=("arbitrary",),
            vmem_limit_bytes=VMEM_LIMIT),
        name="mla_attn",
    )(q, k_lat, vt_lat, k_ctx, vt_ctx)


def _swa_kernel(sink_ref, q_ref, k_ref, vt_ref, kc_ref, vct_ref, o_ref):
    tq = SWA_TQ
    i = pl.program_id(1)
    s0 = i * tq
    kstart = pl.multiple_of(jnp.clip(s0 - WINDOW, 0, SEQ - SWA_KW), LANES)
    kwin = k_ref[pl.ds(kstart, SWA_KW), :].astype(F32)
    vtwin = vt_ref[:, pl.ds(kstart, SWA_KW)]
    kc = kc_ref[...].astype(F32)
    vct = vct_ref[...]
    col = lax.broadcasted_iota(I32, (SWA_KW, SWA_GROUP * tq), 1)
    qpos = s0 + (col & (tq - 1))
    kpos = kstart + lax.broadcasted_iota(I32, (SWA_KW, SWA_GROUP * tq), 0)
    allowed = jnp.abs(qpos - kpos) <= WINDOW

    def half_keys(k, kh):
        lane = lax.broadcasted_iota(I32, k.shape, 1)
        own = jnp.where((lane >= kh * SWA_HD) & (lane < (kh + 1) * SWA_HD), k, 0.0)
        other = pltpu.roll(own, SWA_HD, 1)
        return (own, other) if kh == 0 else (other, own)

    order = (0, 2, 1, 3)
    for kh in range(SWA_KV_HEADS):
        vrows = slice(kh * SWA_HD, (kh + 1) * SWA_HD)
        grp0 = kh * SWA_GROUP // 2
        q2 = jnp.concatenate([q_ref[:, (grp0 + c) * LANES:(grp0 + c + 1) * LANES] for c in range(2)],
                             axis=0)
        k_lo, k_hi = half_keys(kwin, kh)
        c_lo, c_hi = half_keys(kc, kh)
        sb = jnp.concatenate([_mm_nt(k_lo, q2), _mm_nt(k_hi, q2)], axis=1)
        sb = jnp.where(allowed, sb, NEG_INF)
        sc = jnp.concatenate([_mm_nt(c_lo, q2), _mm_nt(c_hi, q2)], axis=1)
        sink = jnp.concatenate(
            [jnp.full((1, tq), sink_ref[kh * SWA_GROUP + g] * LOG2E, F32) for g in order], axis=1)
        m = jnp.maximum(jnp.maximum(jnp.max(sb, axis=0, keepdims=True),
                                    jnp.max(sc, axis=0, keepdims=True)), sink)
        pb = jnp.exp2(sb - m)
        pc = jnp.exp2(sc - m)
        l = (jnp.sum(pb, axis=0, keepdims=True) + jnp.sum(pc, axis=0, keepdims=True)
             + jnp.exp2(sink - m))
        o = (_mm(vtwin[vrows, :], pb) + _mm(vct[vrows, :], pc)) / l
        for blk, g in enumerate(order):
            hd = kh * SWA_GROUP + g
            o_ref[hd * SWA_HD:(hd + 1) * SWA_HD, :] = o[:, blk * tq:(blk + 1) * tq].astype(BF16)


def _swa_attn(sink, q, k_lat, vt_lat, k_ctx, vt_ctx):
    tq = SWA_TQ
    nq = SEQ // tq
    return pl.pallas_call(
        _swa_kernel,
        grid=(BATCH, nq),
        in_specs=[
            pl.BlockSpec(memory_space=pltpu.SMEM),
            pl.BlockSpec((tq, SWA_HEADS * SWA_HD), lambda b, i: (b * nq + i, 0)),
            pl.BlockSpec((SEQ, SWA_KV_HEADS * SWA_HD), lambda b, i: (b, 0)),
            pl.BlockSpec((SWA_KV_HEADS * SWA_HD, SEQ), lambda b, i: (0, b)),
            pl.BlockSpec((CTX_LEN, SWA_KV_HEADS * SWA_HD), lambda b, i: (b, 0)),
            pl.BlockSpec((SWA_KV_HEADS * SWA_HD, CTX_LEN), lambda b, i: (0, b)),
        ],
        out_specs=pl.BlockSpec((SWA_HEADS * SWA_HD, tq), lambda b, i: (0, b * nq + i)),
        out_shape=jax.ShapeDtypeStruct((SWA_HEADS * SWA_HD, N_TOK), BF16),
        compiler_params=pltpu.CompilerParams(
            dimension_semantics=("arbitrary", "arbitrary"), vmem_limit_bytes=VMEM_LIMIT),
        name="swa_attn",
    )(sink, q, k_lat, vt_lat, k_ctx, vt_ctx)


def _merge_kernel(oa_ref, ob_ref, sa_ref, sb_ref, x_ref, g1_ref, sh2_ref, sc2_ref, wba_ref, wbb_ref,
                  wout_ref, g2_ref, wr_ref, br_ref, utri_ref, ltri_ref,
                  xnew_ref, h2_ref, pos_ref, wts_ref, seg_ref, carry_ref):
    i = pl.program_id(0)

    @pl.when(i == 0)
    def _():
        carry_ref[...] = jnp.zeros_like(carry_ref)

    for sub in range(MERGE_SUB):
        _merge_tile(sub, oa_ref, ob_ref, sa_ref, sb_ref, x_ref, g1_ref, sh2_ref, sc2_ref, wba_ref,
                    wbb_ref, wout_ref, g2_ref, wr_ref, br_ref, utri_ref, ltri_ref,
                    xnew_ref, h2_ref, pos_ref, wts_ref, seg_ref, carry_ref)


def _merge_tile(sub, oa_ref, ob_ref, sa_ref, sb_ref, x_ref, g1_ref, sh2_ref, sc2_ref, wba_ref, wbb_ref,
                wout_ref, g2_ref, wr_ref, br_ref, utri_ref, ltri_ref,
                xnew_ref, h2_ref, pos_ref, wts_ref, seg_ref, carry_ref):
    tok = slice(sub * MERGE_TM, (sub + 1) * MERGE_TM)
    segrows = slice(sub * SUBLANES, (sub + 1) * SUBLANES)
    ya = _mm_tn(oa_ref[:, tok], wba_ref[...])
    yb = _mm_tn(ob_ref[:, tok], wbb_ref[...])
    y = sa_ref[tok, :].astype(F32) * ya + sb_ref[tok, :].astype(F32) * yb
    z = _mm(y, wout_ref[...])
    xn = x_ref[tok, :] + g1_ref[...] * z
    xnew_ref[tok, :] = xn
    h2 = _rms(xn) * g2_ref[...] * (1.0 + sc2_ref[...]) + sh2_ref[...]
    h2_ref[tok, :] = h2.astype(BF16)

    hhi, hlo = _split(h2)
    whi, wlo = _split(wr_ref[...])
    lg = _mm_nt(whi, hhi) + _mm_nt(whi, hlo) + _mm_nt(wlo, hhi) + br_ref[...]
    eiota = lax.broadcasted_iota(I32, lg.shape, 0).astype(F32)
    vals, onehots = [], []
    for k in range(TOP_K):
        m = jnp.max(lg, axis=0, keepdims=True)
        ik = jnp.min(jnp.where(lg == m, eiota, float(N_EXPERTS)), axis=0, keepdims=True)
        hit = eiota == ik
        vals.append(m)
        onehots.append(hit.astype(F32))
        lg = jnp.where(hit, -jnp.inf, lg)
    es = [jnp.exp(v - vals[0]) for v in vals]
    tot = es[0] + es[1] + es[2] + es[3]
    for k in range(TOP_K):
        wts_ref[k:k + 1, tok] = es[k] / tot

    tots = [jnp.sum(oh, axis=1, keepdims=True) for oh in onehots]
    n = tots[0] + tots[1] + tots[2] + tots[3]
    m_al = jnp.floor((n + (SEG_ALIGN - 1)) * (1.0 / SEG_ALIGN))
    m_b = jnp.broadcast_to(m_al, (N_EXPERTS, LANES))
    lstart = _mm(ltri_ref[...], m_b) * float(SEG_ALIGN)
    off = jnp.zeros_like(n)
    for k in range(TOP_K):
        prefix = _mm(onehots[k], utri_ref[...])
        lp = jnp.sum(onehots[k] * (lstart[:, 0:1] + off + prefix), axis=0, keepdims=True)
        pos_ref[k:k + 1, tok] = lp.astype(I32)
        off = off + tots[k]
    eye = (lax.broadcasted_iota(I32, (N_EXPERTS, LANES), 0)
           == lax.broadcasted_iota(I32, (N_EXPERTS, LANES), 1)).astype(F32)
    to_row = lambda v: jnp.sum(v * eye, axis=0, keepdims=True).astype(I32)
    carry = carry_ref[...]
    seg_ref[segrows, :] = jnp.concatenate(
        [to_row(lstart), to_row(m_b * float(SEG_ALIGN)), to_row(carry),
         jnp.zeros((SUBLANES - 3, LANES), I32)], axis=0)
    carry_ref[...] = carry + m_b * float(SEG_ALIGN)


def _merge(o_a, o_b, sig_a, sig_b, x2d, mod3, w_ba, w_bb, w_out, g2, w_rt, b_r):
    tm = MERGE_TM * MERGE_SUB
    tiles_per_batch = SEQ // tm
    const = lambda shape: pl.BlockSpec(shape, lambda i: (0,) * len(shape))
    modspec = lambda j: pl.BlockSpec((None, 1, D_MODEL), lambda i: (i // tiles_per_batch, 0, j))
    row = lambda w: pl.BlockSpec((tm, w), lambda i: (i, 0))
    utri = (jnp.arange(MERGE_TM)[:, None] < jnp.arange(MERGE_TM)[None, :]).astype(BF16)
    ltri = (jnp.arange(N_EXPERTS)[None, :] < jnp.arange(N_EXPERTS)[:, None]).astype(BF16)
    return pl.pallas_call(
        _merge_kernel,
        grid=(N_TILES // MERGE_SUB,),
        in_specs=[pl.BlockSpec((512, tm), lambda i: (0, i)), pl.BlockSpec((512, tm), lambda i: (0, i)),
                  row(D_MODEL), row(D_MODEL), row(D_MODEL),
                  modspec(2), modspec(3), modspec(4),
                  const(w_ba.shape), const(w_bb.shape), const(w_out.shape), const(g2.shape),
                  const(w_rt.shape), const(b_r.shape), const(utri.shape), const(ltri.shape)],
        out_specs=[row(D_MODEL), row(D_MODEL),
                   pl.BlockSpec((TOP_K, tm), lambda i: (0, i)),
                   pl.BlockSpec((TOP_K, tm), lambda i: (0, i)),
                   pl.BlockSpec((MERGE_SUB * SUBLANES, LANES), lambda i: (i, 0))],
        out_shape=[jax.ShapeDtypeStruct((N_TOK, D_MODEL), F32),
                   jax.ShapeDtypeStruct((N_TOK, D_MODEL), BF16),
                   jax.ShapeDtypeStruct((TOP_K, N_TOK), I32),
                   jax.ShapeDtypeStruct((TOP_K, N_TOK), F32),
                   jax.ShapeDtypeStruct((N_TILES * SUBLANES, LANES), I32)],
        scratch_shapes=[pltpu.VMEM((N_EXPERTS, LANES), F32)],
        compiler_params=pltpu.CompilerParams(
            dimension_semantics=("arbitrary",), vmem_limit_bytes=VMEM_LIMIT),
        name="merge",
    )(o_a, o_b, sig_a, sig_b, x2d, mod3, mod3, mod3, w_ba, w_bb, w_out, g2, w_rt, b_r, utri, ltri)


def _chunk_copy(hbm, hbm_row, buf, buf_row, size, to_hbm, sem):
    vm = buf.at[pl.ds(pl.multiple_of(buf_row, SEG_ALIGN), size)]
    hb = hbm.at[pl.ds(pl.multiple_of(hbm_row, SEG_ALIGN), size)]
    return pltpu.make_async_copy(vm, hb, sem) if to_hbm else pltpu.make_async_copy(hb, vm, sem)


def _seg_start(seg_smem, tile, hbm, buf, to_hbm, sem):
    base = tile * SEG_W

    def chunks(src, rows, dst, bits):
        for b in bits:
            size = 1 << b
            done = rows & (-2 * size)

            @pl.when((rows & size) != 0)
            def _():
                _chunk_copy(hbm, dst + done, buf, src + done, size, to_hbm, sem).start()

    def per_expert(e, carry):
        src = seg_smem[base + e]
        rows = seg_smem[base + N_EXPERTS + e]
        dst = seg_smem[base + 2 * N_EXPERTS + e]

        @pl.when(rows >= SEG_BIG)
        def _():
            chunks(src, rows, dst, [b for b in SEG_BITS if (1 << b) >= SEG_BIG])

        chunks(src, rows, dst, [b for b in SEG_BITS if (1 << b) < SEG_BIG])
        return carry

    lax.fori_loop(0, N_EXPERTS, per_expert, 0)


def _seg_wait(seg_smem, tile, hbm, buf, to_hbm, sem):
    assert LROWS < 2 << SEG_BITS[-1]
    total = seg_smem[tile * SEG_W + SEG_TOTAL]
    for b in SEG_BITS:
        size = 1 << b

        @pl.when((total & size) != 0)
        def _():
            _chunk_copy(hbm, 0, buf, 0, size, to_hbm, sem).wait()


def _dispatch_kernel(pend_ref, padded_ref, seg_ref, pos_ref, h2_ref, xs_ref, staged, sems):
    j = pl.program_id(0)
    slot = lax.rem(j, 2)

    @pl.when(j == 0)
    def _():
        staged[0, 0:MOE_BLK, :] = jnp.zeros((MOE_BLK, PACK_W), U32)

        def tail(action):
            def body(e, carry):
                @pl.when(padded_ref[e] > 0)
                def _():
                    start = pl.multiple_of(pend_ref[e] - MOE_BLK, MOE_BLK)
                    cp = pltpu.make_async_copy(staged.at[0, pl.ds(0, MOE_BLK)],
                                               xs_ref.at[pl.ds(start, MOE_BLK)], sems.at[0])
                    getattr(cp, action)()
                return carry
            lax.fori_loop(0, N_EXPERTS, body, 0)

        def unused(action):
            def body(b, carry):
                cp = pltpu.make_async_copy(
                    staged.at[0, pl.ds(0, MOE_BLK)],
                    xs_ref.at[pl.ds(pl.multiple_of(b * MOE_BLK, MOE_BLK), MOE_BLK)], sems.at[0])
                getattr(cp, action)()
                return carry
            lax.fori_loop(pend_ref[N_EXPERTS - 1] // MOE_BLK, N_BLK, body, 0)

        tail("start")
        unused("start")
        tail("wait")
        unused("wait")

    pos = pos_ref[...]
    piota = lax.broadcasted_iota(I32, (LROWS, MERGE_TM), 0)
    sel = (piota == pos[0:1, :]).astype(F32)
    for k in range(1, TOP_K):
        sel = sel + (piota == pos[k:k + 1, :]).astype(F32)
    staged[slot] = _pack_rows(_mm(sel, h2_ref[...]))

    _seg_start(seg_ref, j, xs_ref, staged.at[slot], True, sems.at[1 + slot])

    @pl.when(j > 0)
    def _():
        _seg_wait(seg_ref, j - 1, xs_ref, staged.at[1 - slot], True, sems.at[2 - slot])

    @pl.when(j == N_TILES - 1)
    def _():
        _seg_wait(seg_ref, j, xs_ref, staged.at[slot], True, sems.at[1 + slot])


def _dispatch(pends, padded, segtab, pos, h2):
    grid_spec = pltpu.PrefetchScalarGridSpec(
        num_scalar_prefetch=3,
        grid=(N_TILES,),
        in_specs=[
            pl.BlockSpec((TOP_K, MERGE_TM), lambda j, pe, pa, sg: (0, j)),
            pl.BlockSpec((MERGE_TM, D_MODEL), lambda j, pe, pa, sg: (j, 0)),
        ],
        out_specs=pl.BlockSpec(memory_space=pl.ANY),
        scratch_shapes=[pltpu.VMEM((2, LROWS, PACK_W), U32),
                        pltpu.SemaphoreType.DMA((3,))],
    )
    return pl.pallas_call(
        _dispatch_kernel,
        grid_spec=grid_spec,
        out_shape=jax.ShapeDtypeStruct((N_SLOTS, PACK_W), U32),
        compiler_params=pltpu.CompilerParams(
            dimension_semantics=("arbitrary",), vmem_limit_bytes=VMEM_LIMIT),
        name="dispatch",
    )(pends, padded, segtab, pos, h2)


SPLIT_SUB = 256


def _expert_kernel(be_ref, nused_ref, xs_ref, wgu_ref, bg_ref, bl_ref, wd_ref, bd_ref, perm_ref, o_ref,
                   wg_s, wl_s, wd_s):
    i = pl.program_id(0)
    used = i < nused_ref[0]
    fresh = jnp.logical_or(i == 0, be_ref[i] != be_ref[jnp.maximum(i - 1, 0)])

    @pl.when(jnp.logical_and(used, fresh))
    def _():
        half = SPLIT_SUB // 2
        for s in range(2 * D_EXPERT // SPLIT_SUB):
            r = _mm(wgu_ref[:, s * SPLIT_SUB:(s + 1) * SPLIT_SUB], perm_ref[...])
            wg_s[:, s * half:(s + 1) * half] = r[:, :half].astype(BF16)
            wl_s[:, s * half:(s + 1) * half] = r[:, half:].astype(BF16)
        wd_s[...] = wd_ref[...].astype(BF16)

    @pl.when(used)
    def _():
        xb = _unpack_rows(xs_ref[...])
        g = jnp.minimum(_mm(xb, wg_s[...]) + bg_ref[...], SWIGLU_LIMIT)
        l = jnp.clip(_mm(xb, wl_s[...]) + bl_ref[...], -SWIGLU_LIMIT, SWIGLU_LIMIT)
        act = g * jax.nn.sigmoid(SWIGLU_ALPHA * g) * (l + 1.0)
        o_ref[...] = _pack_rows(_mm(act, wd_s[...]) + bd_ref[...])

    @pl.when(jnp.logical_not(used))
    def _():
        o_ref[...] = jnp.zeros_like(o_ref)


def _experts(block_expert, nused, xs, w_gu, bg, bl, w_dn, bd):
    col = jnp.arange(SPLIT_SUB)
    src = jnp.where(col < SPLIT_SUB // 2, 2 * col, 2 * (col - SPLIT_SUB // 2) + 1)
    perm = (jnp.arange(SPLIT_SUB)[:, None] == src[None, :]).astype(BF16)
    blk = lambda i, be, nu: (jnp.minimum(i, nu[0] - 1), 0)
    per_expert = lambda shape: pl.BlockSpec((None,) + shape, lambda i, be, nu: (be[i], 0, 0))
    grid_spec = pltpu.PrefetchScalarGridSpec(
        num_scalar_prefetch=2,
        grid=(N_BLK,),
        in_specs=[pl.BlockSpec((MOE_BLK, PACK_W), blk),
                  per_expert((D_MODEL, 2 * D_EXPERT)), per_expert((1, D_EXPERT)), per_expert((1, D_EXPERT)),
                  per_expert((D_EXPERT, D_MODEL)), per_expert((1, D_MODEL)),
                  pl.BlockSpec((SPLIT_SUB, SPLIT_SUB), lambda i, be, nu: (0, 0))],
        out_specs=pl.BlockSpec((MOE_BLK, PACK_W), lambda i, be, nu: (i, 0)),
        scratch_shapes=[pltpu.VMEM((D_MODEL, D_EXPERT), BF16), pltpu.VMEM((D_MODEL, D_EXPERT), BF16),
                        pltpu.VMEM((D_EXPERT, D_MODEL), BF16)],
    )
    return pl.pallas_call(
        _expert_kernel,
        grid_spec=grid_spec,
        out_shape=jax.ShapeDtypeStruct((N_SLOTS, PACK_W), U32),
        compiler_params=pltpu.CompilerParams(
            dimension_semantics=("arbitrary",), vmem_limit_bytes=VMEM_LIMIT),
        name="experts",
    )(block_expert, nused, xs, w_gu, bg, bl, w_dn, bd, perm)


def _combine_kernel(seg_ref, eo_ref, xnew_ref, g2_ref, pos_ref, wt_ref, o_ref, gbuf, sems):
    j = pl.program_id(0)
    slot = lax.rem(j, 2)

    def fetch(tile, sl):
        _seg_start(seg_ref, tile, eo_ref, gbuf.at[sl], False, sems.at[sl])

    @pl.when(j == 0)
    def _():
        gbuf[...] = jnp.zeros_like(gbuf)
        fetch(0, 0)

    @pl.when(j + 1 < N_TILES)
    def _():
        fetch(j + 1, 1 - slot)

    _seg_wait(seg_ref, j, eo_ref, gbuf.at[slot], False, sems.at[slot])

    pos = pos_ref[...]
    wt = wt_ref[...]
    lane = lax.broadcasted_iota(I32, (MERGE_TM, LROWS), 1)
    wsel = jnp.where(lane == pos[:, 0:1], wt[:, 0:1], 0.0)
    for k in range(1, TOP_K):
        wsel = wsel + jnp.where(lane == pos[:, k:k + 1], wt[:, k:k + 1], 0.0)
    acc = _mm(wsel, _unpack_rows(gbuf[slot]))
    o_ref[...] = xnew_ref[...] + g2_ref[...] * acc


def _combine(segtab, eo, x_new, mod3, pos_t, wts_t):
    tiles_per_batch = SEQ // MERGE_TM
    grid_spec = pltpu.PrefetchScalarGridSpec(
        num_scalar_prefetch=1,
        grid=(N_TILES,),
        in_specs=[
            pl.BlockSpec(memory_space=pl.ANY),
            pl.BlockSpec((MERGE_TM, D_MODEL), lambda j, sg: (j, 0)),
            pl.BlockSpec((None, 1, D_MODEL), lambda j, sg: (j // tiles_per_batch, 0, 5)),
            pl.BlockSpec((MERGE_TM, TOP_K), lambda j, sg: (j, 0)),
            pl.BlockSpec((MERGE_TM, TOP_K), lambda j, sg: (j, 0)),
        ],
        out_specs=pl.BlockSpec((MERGE_TM, D_MODEL), lambda j, sg: (j, 0)),
        scratch_shapes=[pltpu.VMEM((2, LROWS, PACK_W), U32),
                        pltpu.SemaphoreType.DMA((2,))],
    )
    return pl.pallas_call(
        _combine_kernel,
        grid_spec=grid_spec,
        out_shape=jax.ShapeDtypeStruct((N_TOK, D_MODEL), F32),
        compiler_params=pltpu.CompilerParams(
            dimension_semantics=("arbitrary",), vmem_limit_bytes=VMEM_LIMIT),
        name="combine",
    )(segtab, eo, x_new, mod3, pos_t, wts_t)


def _pad_heads(w, heads, dim):
    lead = w.shape[:-1]
    w = w.reshape(lead + (heads, dim))
    w = jnp.pad(w, [(0, 0)] * len(lead) + [(0, 0), (0, LANES - dim)])
    return w.reshape(lead + (heads * LANES,))


def _rope_tables(dim, offset, repeat):
    pos = jnp.arange(SEQ, dtype=jnp.int32)
    row = (pos // GRID_W).astype(F32)
    col = (pos % GRID_W).astype(F32)
    q = dim // 4
    freqs = ROPE_THETA ** (-jnp.arange(q, dtype=F32) / q)
    ang_r = row[:, None] * freqs
    ang_c = col[:, None] * freqs
    ang = jnp.concatenate([ang_r, ang_r, ang_c, ang_c], axis=-1)
    cos, sin = jnp.cos(ang), jnp.sin(ang)
    first = (jnp.arange(dim) % (2 * q)) < q
    sa = jnp.where(first, -sin, 0.0)
    sb = jnp.where(first, 0.0, sin)
    pad = lambda t, v: jnp.pad(jnp.tile(t, (1, repeat)), ((0, 0), (offset, LANES - offset - dim * repeat)),
                               constant_values=v)
    return pad(cos, 1.0), pad(sa, 0.0), pad(sb, 0.0)


def kernel(x, c, ctx, c_ctx, w_ada, b_ada, norm1_g, norm2_g, w_in, mla_q_a_g, mla_kv_a_g, w_q_up, w_kv_up,
           mla_q_g, mla_k_g, swa_q_g, swa_k_g, swa_sink, w_branch_a, w_branch_b, w_out, w_router, b_router,
           w_gate_up, b_gate_up, w_down, b_down):
    assert x.shape == (BATCH, SEQ, D_MODEL) and ctx.shape == (BATCH, CTX_LEN, D_MODEL)
    assert w_ada.shape[0] == 1, "single layer"

    cond = jnp.concatenate([c, c_ctx[None], jnp.zeros((MOD_ROWS - BATCH - 1, D_MODEL), F32)], axis=0)
    mod = _ada(cond, w_ada[0], b_ada[0][None])
    mod3 = mod.reshape(MOD_ROWS, 1, N_MOD * D_MODEL)

    wi = w_in[0]
    o1, o2, o3 = MLA_KV_RANK, MLA_KV_RANK + MLA_ROPE, MLA_KV_RANK + MLA_ROPE + SWA_KV_HEADS * SWA_HD
    kv_cols = o3 + SWA_KV_HEADS * SWA_HD
    q1 = kv_cols + MLA_Q_RANK
    q2 = q1 + SWA_HEADS * SWA_HD
    q3 = q2 + D_MODEL
    krope_grp = jnp.pad(wi[:, o1:o2], ((0, 0), (MLA_NOPE, LANES - MLA_QK)))
    win = jnp.concatenate([
        wi[:, 0:o1], wi[:, o2:o3], krope_grp, wi[:, kv_cols:q1], wi[:, q1:q2], wi[:, q2:q3], wi[:, q3:]],
        axis=1).astype(BF16)
    assert win.shape == (D_MODEL, C_END)
    wvst = wi[:, o3:kv_cols].T.astype(BF16)
    wkv = w_kv_up[0].reshape(MLA_KV_RANK, MLA_HEADS, MLA_NOPE + MLA_V)
    wkn = _pad_heads(wkv[:, :, :MLA_NOPE].reshape(MLA_KV_RANK, -1), MLA_HEADS, MLA_NOPE).astype(BF16)
    wvt = wkv[:, :, MLA_NOPE:].reshape(MLA_KV_RANK, MLA_HEADS * MLA_V).T.astype(BF16)
    wqup = _pad_heads(w_q_up[0], MLA_HEADS, MLA_QK).astype(BF16)
    gk_t = _pad_heads(jnp.tile(mla_k_g[0], MLA_HEADS)[None], MLA_HEADS, MLA_QK)
    gq_t = _pad_heads(jnp.tile(mla_q_g[0] * (MLA_QK ** -0.5 * LOG2E), MLA_HEADS)[None], MLA_HEADS, MLA_QK)
    gks_t = jnp.tile(swa_k_g[0], SWA_KV_HEADS)[None]
    gqs_t = jnp.tile(swa_q_g[0] * (SWA_HD ** -0.5 * LOG2E), SWA_HEADS)[None]

    def lane_to_head(n_lanes, width):
        ind = ((jnp.arange(n_lanes) // width)[:, None] == jnp.arange(LANES)[None, :]).astype(BF16)
        return ind, jnp.concatenate([ind.T, ind.T], axis=0)

    ind, indt = lane_to_head(MLA_HEADS * LANES, LANES)
    ind64, indt64 = lane_to_head(SWA_HEADS * SWA_HD, SWA_HD)
    consts = [norm1_g[0][None], mla_kv_a_g[0][None], wkn, wvt, wvst, gk_t, gks_t, ind, indt, ind64, indt64]
    q_consts = [mla_q_a_g[0][None], wqup, gq_t, gqs_t]

    tabs_lat = _rope_tables(MLA_ROPE, MLA_NOPE, 1) + _rope_tables(SWA_HD, 0, LANES // SWA_HD)
    ones = jnp.ones((PROJ_TM, LANES), F32)
    zeros = jnp.zeros((PROJ_TM, LANES), F32)
    tabs_ctx = (ones, zeros, zeros, ones, zeros, zeros)

    x2d = x.reshape(N_TOK, D_MODEL)
    tiles_per_batch = SEQ // PROJ_TM
    k_mla, vt_mla, k_swa, vt_swa, q_mla, q_swa, sig_a, sig_b = _proj(
        x2d, mod3, lambda i: i // tiles_per_batch, lambda i: i % tiles_per_batch,
        tabs_lat, consts, q_consts, win, True, "proj_latent")
    kc_mla, vct_mla, kc_swa, vct_swa = _proj(
        ctx.reshape(BATCH * CTX_LEN, D_MODEL), mod3, lambda i: BATCH, lambda i: 0,
        tabs_ctx, consts, q_consts, win[:, :C_KVEND], False, "proj_ctx")

    o_a = _mla_attn(q_mla, k_mla, vt_mla, kc_mla, vct_mla)
    o_b = _swa_attn(swa_sink[0], q_swa, k_swa, vt_swa, kc_swa, vct_swa)

    x_new, h2, pos, top_w, seg = _merge(
        o_a, o_b, sig_a, sig_b, x2d, mod3,
        w_branch_a[0].astype(BF16), w_branch_b[0].astype(BF16), w_out[0].astype(BF16),
        norm2_g[0][None], w_router[0].T, b_router[0][:, None])

    seg3 = seg.reshape(N_TILES, SUBLANES, LANES)
    lstart, rows, gcarry = (seg3[:, r, :N_EXPERTS] for r in range(3))
    total = gcarry[-1] + rows[-1]
    padded = (total + MOE_BLK - 1) // MOE_BLK * MOE_BLK
    pends = jnp.cumsum(padded).astype(I32)
    pstarts = pends - padded
    nused = (pends[-1] // MOE_BLK).astype(I32)[None]
    blk_start = jnp.arange(N_BLK, dtype=I32) * MOE_BLK
    block_expert = jnp.minimum(
        jnp.sum((blk_start[:, None] >= pends[None, :]).astype(I32), axis=1), N_EXPERTS - 1)
    segtab = jnp.concatenate(
        [lstart, rows, pstarts[None, :] + gcarry,
         jnp.broadcast_to(jnp.sum(rows, axis=1, keepdims=True), rows.shape)], axis=1).reshape(-1).astype(I32)

    xs = _dispatch(pends, padded.astype(I32), segtab, pos, h2)
    bgu = b_gate_up[0]
    eo = _experts(block_expert, nused, xs, w_gate_up[0], bgu[:, None, 0::2], bgu[:, None, 1::2],
                  w_down[0], b_down[0][:, None, :])
    out = _combine(segtab, eo, x_new, mod3, pos.T, top_w.T)
    return out.reshape(BATCH, SEQ, D_MODEL)
```

```python
import functools

import jax
import jax.numpy as jnp
from jax import lax
from jax.experimental import pallas as pl
from jax.experimental.pallas import tpu as pltpu

F32 = jnp.float32
BF16 = jnp.bfloat16
I32 = jnp.int32
U32 = jnp.uint32

LANES = 128
SUBLANES = 8

D_MODEL = 1024
BATCH = 8
SEQ = 4096
CTX_LEN = 256
GRID_W = 64
ROPE_THETA = 10000.0
NORM_EPS = 1e-6
N_MOD = 6
NEG_INF = -1e30
LOG2E = 1.4426950408889634

MLA_HEADS = 8
MLA_NOPE = 64
MLA_ROPE = 32
MLA_QK = MLA_NOPE + MLA_ROPE
MLA_V = 64
MLA_Q_RANK = 384
MLA_KV_RANK = 256

SWA_HEADS = 8
SWA_KV_HEADS = 2
SWA_GROUP = SWA_HEADS // SWA_KV_HEADS
SWA_HD = 64
WINDOW = 128

N_EXPERTS = 32
TOP_K = 4
D_EXPERT = D_MODEL
SWIGLU_LIMIT = 7.0
SWIGLU_ALPHA = 1.702

N_TOK = BATCH * SEQ
MOD_ROWS = 16

ADA_TN = 1536
PROJ_TM = 512
MLA_TQ = 256
SWA_TQ = 256
SWA_KW = SWA_TQ + 2 * WINDOW
MERGE_TM = 256
MERGE_SUB = 2
MOE_BLK = 512
SEG_ALIGN = SUBLANES
N_TILES = N_TOK // MERGE_TM
LROWS = -(-(MERGE_TM * TOP_K + N_EXPERTS * (SEG_ALIGN - 1)) // LANES) * LANES
SEG_BITS = tuple(range(SEG_ALIGN.bit_length() - 1, (MERGE_TM * TOP_K).bit_length()))
SEG_W = 128
SEG_TOTAL = 3 * N_EXPERTS
SEG_BIG = 128
N_SLOTS = -(-(N_TOK * TOP_K + N_TILES * N_EXPERTS * (SEG_ALIGN - 1) + N_EXPERTS * (MOE_BLK - 1))
            // MOE_BLK) * MOE_BLK
N_BLK = N_SLOTS // MOE_BLK

PACK_W = D_MODEL // 2

VMEM_LIMIT = 56 * 1024 * 1024

C_KVLAT = 0
C_KSWA = C_KVLAT + MLA_KV_RANK
C_KROPE = C_KSWA + SWA_KV_HEADS * SWA_HD
C_KVEND = C_KROPE + LANES
C_QLAT = C_KVEND
C_QSWA = C_QLAT + MLA_Q_RANK
C_GA = C_QSWA + SWA_HEADS * SWA_HD
C_GB = C_GA + D_MODEL
C_END = C_GB + D_MODEL


def _mm(a, b):
    return jnp.dot(a.astype(BF16), b.astype(BF16), preferred_element_type=F32)


def _mm_nt(a, b):
    return lax.dot_general(a.astype(BF16), b.astype(BF16), (((1,), (1,)), ((), ())),
                           preferred_element_type=F32)


def _mm_tn(a, b):
    return lax.dot_general(a.astype(BF16), b.astype(BF16), (((0,), (0,)), ((), ())),
                           preferred_element_type=F32)


def _split(a):
    hi = a.astype(BF16)
    lo = (a - hi.astype(F32)).astype(BF16)
    return hi, lo


def _pack_rows(a):
    half = a.shape[1] // 2
    bits = lambda v: lax.bitcast_convert_type(v.astype(BF16).astype(F32), U32)
    return bits(a[:, half:]) | (bits(a[:, :half]) >> 16)


def _unpack_rows(p):
    lo = lax.bitcast_convert_type(p << 16, F32)
    hi = lax.bitcast_convert_type(p & jnp.uint32(0xFFFF0000), F32)
    return jnp.concatenate([lo, hi], axis=1).astype(BF16)


def _rms(x):
    return x * lax.rsqrt(jnp.mean(x * x, axis=-1, keepdims=True) + NORM_EPS)


def _ada_kernel(c_ref, w_ref, b_ref, o_ref):
    c = c_ref[...]
    s = c * jax.nn.sigmoid(c)
    shi, slo = _split(s)
    whi, wlo = _split(w_ref[...])
    acc = _mm(shi, whi) + _mm(slo, whi) + _mm(shi, wlo)
    o_ref[...] = acc + b_ref[...]


def _ada(cond, w_ada, b_ada):
    n = w_ada.shape[1]
    return pl.pallas_call(
        _ada_kernel,
        grid=(n // ADA_TN,),
        in_specs=[
            pl.BlockSpec((MOD_ROWS, D_MODEL), lambda j: (0, 0)),
            pl.BlockSpec((D_MODEL, ADA_TN), lambda j: (0, j)),
            pl.BlockSpec((1, ADA_TN), lambda j: (0, j)),
        ],
        out_specs=pl.BlockSpec((MOD_ROWS, ADA_TN), lambda j: (0, j)),
        out_shape=jax.ShapeDtypeStruct((MOD_ROWS, n), F32),
        compiler_params=pltpu.CompilerParams(
            dimension_semantics=("arbitrary",), vmem_limit_bytes=VMEM_LIMIT),
        name="ada",
    )(cond, w_ada, b_ada)


def _head_norm(xraw, ind_ref, indt_ref, inv_dim):
    w = xraw.shape[1]
    ss = _mm(xraw * xraw, ind_ref[0:w, :])
    r = lax.rsqrt(ss * inv_dim + NORM_EPS)
    scale = _mm(jnp.concatenate(_split(r), axis=1), indt_ref[:, 0:w])
    return xraw * scale


def _rope(xh, cos, sa, sb, quarter):
    return (xh * cos + pltpu.roll(xh, LANES - quarter, 1) * sa
            + pltpu.roll(xh, quarter, 1) * sb)


def _proj_kernel(*refs, with_q):
    (x_ref, sh_ref, sc_ref, g1_ref, win_ref, cm_ref, sam_ref, sbm_ref, cs_ref, sas_ref, sbs_ref,
     gkva_ref, wkn_ref, wvt_ref, wvst_ref, gk_ref, gks_ref, ind_ref, indt_ref,
     ind64_ref, indt64_ref) = refs[:21]
    if with_q:
        gqa_ref, wqup_ref, gq_ref, gqs_ref = refs[21:25]
        kmla_ref, vmla_ref, kswa_ref, vswa_ref, qmla_ref, qswa_ref, siga_ref, sigb_ref = refs[25:]
    else:
        kmla_ref, vmla_ref, kswa_ref, vswa_ref = refs[21:]

    x = x_ref[...]
    h = _rms(x) * g1_ref[...] * (1.0 + sc_ref[...]) + sh_ref[...]
    y = _mm(h, win_ref[...])

    cm, sam, sbm = cm_ref[...], sam_ref[...], sbm_ref[...]
    cs, sas, sbs = cs_ref[...], sas_ref[...], sbs_ref[...]

    kvn = _rms(y[:, C_KVLAT:C_KVLAT + MLA_KV_RANK]) * gkva_ref[...]
    kn = _mm(kvn, wkn_ref[...])
    vmla_ref[...] = _mm_nt(wvt_ref[...], kvn).astype(BF16)
    vswa_ref[...] = _mm_nt(wvst_ref[...], h).astype(BF16)
    kr = y[:, C_KROPE:C_KROPE + LANES]
    kraw = kn + jnp.concatenate([kr] * MLA_HEADS, axis=1)
    kfull = _head_norm(kraw, ind_ref, indt_ref, 1.0 / MLA_QK) * gk_ref[...]
    for hd in range(MLA_HEADS):
        sl = slice(hd * LANES, (hd + 1) * LANES)
        kmla_ref[:, sl] = _rope(kfull[:, sl], cm, sam, sbm, MLA_ROPE // 4).astype(BF16)

    ks = _head_norm(y[:, C_KSWA:C_KSWA + SWA_KV_HEADS * SWA_HD], ind64_ref, indt64_ref,
                    1.0 / SWA_HD) * gks_ref[...]
    for grp in range(SWA_KV_HEADS * SWA_HD // LANES):
        sl = slice(grp * LANES, (grp + 1) * LANES)
        kswa_ref[:, sl] = _rope(ks[:, sl], cs, sas, sbs, SWA_HD // 4).astype(BF16)

    if with_q:
        qn = _rms(y[:, C_QLAT:C_QLAT + MLA_Q_RANK]) * gqa_ref[...]
        qraw = _mm(qn, wqup_ref[...])
        qf = _head_norm(qraw, ind_ref, indt_ref, 1.0 / MLA_QK) * gq_ref[...]
        for hd in range(MLA_HEADS):
            sl = slice(hd * LANES, (hd + 1) * LANES)
            qmla_ref[:, sl] = _rope(qf[:, sl], cm, sam, sbm, MLA_ROPE // 4).astype(BF16)
        qs = _head_norm(y[:, C_QSWA:C_QSWA + SWA_HEADS * SWA_HD], ind64_ref, indt64_ref,
                        1.0 / SWA_HD) * gqs_ref[...]
        for grp in range(SWA_HEADS * SWA_HD // LANES):
            sl = slice(grp * LANES, (grp + 1) * LANES)
            qswa_ref[:, sl] = _rope(qs[:, sl], cs, sas, sbs, SWA_HD // 4).astype(BF16)
        siga_ref[...] = jax.nn.sigmoid(y[:, C_GA:C_GA + D_MODEL]).astype(BF16)
        sigb_ref[...] = jax.nn.sigmoid(y[:, C_GB:C_GB + D_MODEL]).astype(BF16)


def _proj(x2d, mod3, mod_row_fn, tab_row_fn, tabs, consts, q_consts, win, with_q, name):
    rows = x2d.shape[0]
    tm = PROJ_TM
    const = lambda shape: pl.BlockSpec(shape, lambda i: (0,) * len(shape))
    tab = pl.BlockSpec((tm, LANES), lambda i: (tab_row_fn(i), 0))
    g1, rest = consts[0], consts[1:]
    in_specs = [
        pl.BlockSpec((tm, D_MODEL), lambda i: (i, 0)),
        pl.BlockSpec((None, 1, D_MODEL), lambda i: (mod_row_fn(i), 0, 0)),
        pl.BlockSpec((None, 1, D_MODEL), lambda i: (mod_row_fn(i), 0, 1)),
        const(g1.shape),
        pl.BlockSpec(win.shape, lambda i: (0, 0), pipeline_mode=pl.Buffered(1)),
        tab, tab, tab, tab, tab, tab,
    ] + [const(c.shape) for c in rest]
    args = [x2d, mod3, mod3, g1, win] + list(tabs) + list(rest)
    outs = [(MLA_HEADS * LANES, False), (MLA_HEADS * MLA_V, True),
            (SWA_KV_HEADS * SWA_HD, False), (SWA_KV_HEADS * SWA_HD, True)]
    if with_q:
        in_specs += [const(c.shape) for c in q_consts]
        args += list(q_consts)
        outs += [(MLA_HEADS * LANES, False), (SWA_HEADS * SWA_HD, False), (D_MODEL, False), (D_MODEL, False)]
    return pl.pallas_call(
        functools.partial(_proj_kernel, with_q=with_q),
        grid=(rows // tm,),
        in_specs=in_specs,
        out_specs=[pl.BlockSpec((w, tm), lambda i: (0, i)) if t else pl.BlockSpec((tm, w), lambda i: (i, 0))
                   for w, t in outs],
        out_shape=[jax.ShapeDtypeStruct((w, rows) if t else (rows, w), BF16) for w, t in outs],
        compiler_params=pltpu.CompilerParams(
            dimension_semantics=("arbitrary",), vmem_limit_bytes=VMEM_LIMIT),
        name=name,
    )(*args)


def _mla_step(q_ref, kl_ref, vlt_ref, kc_ref, vct_ref, o_ref, cur, prev):
    s_lat_c, s_ctx_c, m_c = cur
    s_lat_p, s_ctx_p, m_p = prev
    outs = []
    for hh in range(2):
        sl = slice(hh * LANES, (hh + 1) * LANES)
        vrows = slice(hh * MLA_V, (hh + 1) * MLA_V)
        q = q_ref[:, sl]
        s1 = _mm_nt(kl_ref[:, sl], q)
        s2 = _mm_nt(kc_ref[:, sl], q)
        s_lat_c[hh] = s1
        s_ctx_c[hh] = s2
        m_c[hh] = jnp.maximum(jnp.max(s1, axis=0, keepdims=True),
                              jnp.max(s2, axis=0, keepdims=True))

        m = m_p[hh]
        p1 = jnp.exp2(s_lat_p[hh] - m)
        p2 = jnp.exp2(s_ctx_p[hh] - m)
        l = jnp.sum(p1, axis=0, keepdims=True) + jnp.sum(p2, axis=0, keepdims=True)
        o = _mm(vlt_ref[vrows, :], p1) + _mm(vct_ref[vrows, :], p2)
        outs.append(o / l)
    o_ref[...] = jnp.concatenate(outs, axis=0).astype(BF16)


def _mla_kernel(q_ref, kl_ref, vl_ref, kc_ref, vc_ref, o_ref, *scratch):
    set0, set1 = scratch[:3], scratch[3:]
    i = pl.program_id(0)

    @pl.when(i == 0)
    def _():
        for ref in set1:
            ref[...] = jnp.zeros(ref.shape, F32)

    @pl.when(lax.rem(i, 2) == 0)
    def _():
        _mla_step(q_ref, kl_ref, vl_ref, kc_ref, vc_ref, o_ref, set0, set1)

    @pl.when(lax.rem(i, 2) == 1)
    def _():
        _mla_step(q_ref, kl_ref, vl_ref, kc_ref, vc_ref, o_ref, set1, set0)


def _mla_attn(q, k_lat, vt_lat, k_ctx, vt_ctx):
    tq = MLA_TQ
    nq = SEQ // tq
    pairs = MLA_HEADS // 2
    n_items = BATCH * pairs * nq

    def item(s):
        return s // (pairs * nq), (s // nq) % pairs, s % nq

    def scored(s):
        return item(jnp.minimum(s, n_items - 1))

    def finished(s):
        return item(jnp.maximum(s - 1, 0))

    def q_map(s):
        b, hp, i = scored(s)
        return b * nq + i, hp

    def k_map(s):
        b, hp, _ = scored(s)
        return b, hp

    def v_map(s):
        b, hp, _ = finished(s)
        return hp, b

    def o_map(s):
        b, hp, i = finished(s)
        return hp, b * nq + i

    return pl.pallas_call(
        _mla_kernel,
        grid=(n_items + 1,),
        in_specs=[
            pl.BlockSpec((tq, 2 * LANES), q_map),
            pl.BlockSpec((SEQ, 2 * LANES), k_map),
            pl.BlockSpec((2 * MLA_V, SEQ), v_map),
            pl.BlockSpec((CTX_LEN, 2 * LANES), k_map),
            pl.BlockSpec((2 * MLA_V, CTX_LEN), v_map),
        ],
        out_specs=pl.BlockSpec((2 * MLA_V, tq), o_map),
        out_shape=jax.ShapeDtypeStruct((MLA_HEADS * MLA_V, N_TOK), BF16),
        scratch_shapes=2 * [pltpu.VMEM((2, SEQ, tq), F32),
                            pltpu.VMEM((2, CTX_LEN, tq), F32),
                            pltpu.VMEM((2, 1, tq), F32)],
        compiler_params=pltpu.CompilerParams(
            dimension_semantics=("arbitrary",),
            vmem_limit_bytes=VMEM_LIMIT),
        name="mla_attn",
    )(q, k_lat, vt_lat, k_ctx, vt_ctx)


def _swa_window_start(i):
    return jnp.clip(i * SWA_TQ - WINDOW, 0, SEQ - SWA_KW)


def _swa_kernel(sink_ref, q_ref, k_ref, vt_ref, kc_ref, vct_ref, bias_ref, o_ref):
    tq = SWA_TQ
    i = pl.program_id(1)
    kstart = pl.multiple_of(_swa_window_start(i), LANES)
    kwin = k_ref[pl.ds(kstart, SWA_KW), :].astype(F32)
    vtwin = vt_ref[:, pl.ds(kstart, SWA_KW)]
    kc = kc_ref[...].astype(F32)
    vct = vct_ref[...]
    bias = bias_ref[...]

    def half_keys(k, kh):
        lane = lax.broadcasted_iota(I32, k.shape, 1)
        own = jnp.where((lane >= kh * SWA_HD) & (lane < (kh + 1) * SWA_HD), k, 0.0)
        other = pltpu.roll(own, SWA_HD, 1)
        return (own, other) if kh == 0 else (other, own)

    order = (0, 2, 1, 3)
    for kh in range(SWA_KV_HEADS):
        vrows = slice(kh * SWA_HD, (kh + 1) * SWA_HD)
        grp0 = kh * SWA_GROUP // 2
        q2 = jnp.concatenate([q_ref[:, (grp0 + c) * LANES:(grp0 + c + 1) * LANES] for c in range(2)],
                             axis=0)
        k_lo, k_hi = half_keys(kwin, kh)
        c_lo, c_hi = half_keys(kc, kh)
        sb = jnp.concatenate([_mm_nt(k_lo, q2), _mm_nt(k_hi, q2)], axis=1) + bias
        sc = jnp.concatenate([_mm_nt(c_lo, q2), _mm_nt(c_hi, q2)], axis=1)
        sink = jnp.concatenate(
            [jnp.full((1, tq), sink_ref[kh * SWA_GROUP + g] * LOG2E, F32) for g in order], axis=1)
        m = jnp.maximum(jnp.maximum(jnp.max(sb, axis=0, keepdims=True),
                                    jnp.max(sc, axis=0, keepdims=True)), sink)
        pb = jnp.exp2(sb - m)
        pc = jnp.exp2(sc - m)
        l = (jnp.sum(pb, axis=0, keepdims=True) + jnp.sum(pc, axis=0, keepdims=True)
             + jnp.exp2(sink - m))
        o = (_mm(vtwin[vrows, :], pb) + _mm(vct[vrows, :], pc)) / l
        for blk, g in enumerate(order):
            hd = kh * SWA_GROUP + g
            o_ref[hd * SWA_HD:(hd + 1) * SWA_HD, :] = o[:, blk * tq:(blk + 1) * tq].astype(BF16)


def _swa_attn(sink, q, k_lat, vt_lat, k_ctx, vt_ctx):
    tq = SWA_TQ
    nq = SEQ // tq
    offsets = (0, WINDOW, SWA_KW - tq)
    assert all(i * tq - min(max(i * tq - WINDOW, 0), SEQ - SWA_KW) == offsets[(i > 0) + (i == nq - 1)]
               for i in range(nq))
    key = jnp.arange(SWA_KW, dtype=I32)[:, None]
    qry = (jnp.arange(SWA_GROUP * tq, dtype=I32) % tq)[None, :]
    bias = jnp.stack([jnp.where(jnp.abs(d + qry - key) <= WINDOW, 0.0, NEG_INF) for d in offsets]).astype(F32)
    case = lambda i: (i > 0).astype(I32) + (i == nq - 1).astype(I32)
    return pl.pallas_call(
        _swa_kernel,
        grid=(BATCH, nq),
        in_specs=[
            pl.BlockSpec(memory_space=pltpu.SMEM),
            pl.BlockSpec((tq, SWA_HEADS * SWA_HD), lambda b, i: (b * nq + i, 0)),
            pl.BlockSpec((SEQ, SWA_KV_HEADS * SWA_HD), lambda b, i: (b, 0)),
            pl.BlockSpec((SWA_KV_HEADS * SWA_HD, SEQ), lambda b, i: (0, b)),
            pl.BlockSpec((CTX_LEN, SWA_KV_HEADS * SWA_HD), lambda b, i: (b, 0)),
            pl.BlockSpec((SWA_KV_HEADS * SWA_HD, CTX_LEN), lambda b, i: (0, b)),
            pl.BlockSpec((None, SWA_KW, SWA_GROUP * tq), lambda b, i: (case(i), 0, 0)),
        ],
        out_specs=pl.BlockSpec((SWA_HEADS * SWA_HD, tq), lambda b, i: (0, b * nq + i)),
        out_shape=jax.ShapeDtypeStruct((SWA_HEADS * SWA_HD, N_TOK), BF16),
        compiler_params=pltpu.CompilerParams(
            dimension_semantics=("arbitrary", "arbitrary"), vmem_limit_bytes=VMEM_LIMIT),
        name="swa_attn",
    )(sink, q, k_lat, vt_lat, k_ctx, vt_ctx, bias)


def _merge_kernel(oa_ref, ob_ref, sa_ref, sb_ref, x_ref, g1_ref, sh2_ref, sc2_ref, wba_ref, wbb_ref,
                  wout_ref, g2_ref, wr_ref, br_ref, utri_ref, ltri_ref,
                  xnew_ref, h2_ref, pos_ref, wts_ref, seg_ref, carry_ref):
    i = pl.program_id(0)

    @pl.when(i == 0)
    def _():
        carry_ref[...] = jnp.zeros_like(carry_ref)

    for sub in range(MERGE_SUB):
        _merge_tile(sub, oa_ref, ob_ref, sa_ref, sb_ref, x_ref, g1_ref, sh2_ref, sc2_ref, wba_ref,
                    wbb_ref, wout_ref, g2_ref, wr_ref, br_ref, utri_ref, ltri_ref,
                    xnew_ref, h2_ref, pos_ref, wts_ref, seg_ref, carry_ref)


def _merge_tile(sub, oa_ref, ob_ref, sa_ref, sb_ref, x_ref, g1_ref, sh2_ref, sc2_ref, wba_ref, wbb_ref,
                wout_ref, g2_ref, wr_ref, br_ref, utri_ref, ltri_ref,
                xnew_ref, h2_ref, pos_ref, wts_ref, seg_ref, carry_ref):
    tok = slice(sub * MERGE_TM, (sub + 1) * MERGE_TM)
    segrows = slice(sub * SUBLANES, (sub + 1) * SUBLANES)
    ya = _mm_tn(oa_ref[:, tok], wba_ref[...])
    yb = _mm_tn(ob_ref[:, tok], wbb_ref[...])
    y = sa_ref[tok, :].astype(F32) * ya + sb_ref[tok, :].astype(F32) * yb
    z = _mm(y, wout_ref[...])
    xn = x_ref[tok, :] + g1_ref[...] * z
    xnew_ref[tok, :] = xn
    h2 = _rms(xn) * g2_ref[...] * (1.0 + sc2_ref[...]) + sh2_ref[...]
    h2_ref[tok, :] = h2.astype(BF16)

    hhi, hlo = _split(h2)
    whi, wlo = _split(wr_ref[...])
    lg = _mm_nt(whi, hhi) + _mm_nt(whi, hlo) + _mm_nt(wlo, hhi) + br_ref[...]
    eiota = lax.broadcasted_iota(I32, lg.shape, 0).astype(F32)
    vals, onehots = [], []
    for k in range(TOP_K):
        m = jnp.max(lg, axis=0, keepdims=True)
        ik = jnp.min(jnp.where(lg == m, eiota, float(N_EXPERTS)), axis=0, keepdims=True)
        hit = eiota == ik
        vals.append(m)
        onehots.append(hit.astype(F32))
        lg = jnp.where(hit, -jnp.inf, lg)
    es = [jnp.exp(v - vals[0]) for v in vals]
    tot = es[0] + es[1] + es[2] + es[3]
    for k in range(TOP_K):
        wts_ref[k:k + 1, tok] = es[k] / tot

    tots = [jnp.sum(oh, axis=1, keepdims=True) for oh in onehots]
    n = tots[0] + tots[1] + tots[2] + tots[3]
    m_al = jnp.floor((n + (SEG_ALIGN - 1)) * (1.0 / SEG_ALIGN))
    m_b = jnp.broadcast_to(m_al, (N_EXPERTS, LANES))
    lstart = _mm(ltri_ref[...], m_b) * float(SEG_ALIGN)
    off = jnp.zeros_like(n)
    for k in range(TOP_K):
        prefix = _mm(onehots[k], utri_ref[...])
        lp = jnp.sum(onehots[k] * (lstart[:, 0:1] + off + prefix), axis=0, keepdims=True)
        pos_ref[k:k + 1, tok] = lp.astype(I32)
        off = off + tots[k]
    eye = (lax.broadcasted_iota(I32, (N_EXPERTS, LANES), 0)
           == lax.broadcasted_iota(I32, (N_EXPERTS, LANES), 1)).astype(F32)
    to_row = lambda v: jnp.sum(v * eye, axis=0, keepdims=True).astype(I32)
    carry = carry_ref[...]
    seg_ref[segrows, :] = jnp.concatenate(
        [to_row(lstart), to_row(m_b * float(SEG_ALIGN)), to_row(carry),
         jnp.zeros((SUBLANES - 3, LANES), I32)], axis=0)
    carry_ref[...] = carry + m_b * float(SEG_ALIGN)


def _merge(o_a, o_b, sig_a, sig_b, x2d, mod3, w_ba, w_bb, w_out, g2, w_rt, b_r):
    tm = MERGE_TM * MERGE_SUB
    tiles_per_batch = SEQ // tm
    const = lambda shape: pl.BlockSpec(shape, lambda i: (0,) * len(shape))
    modspec = lambda j: pl.BlockSpec((None, 1, D_MODEL), lambda i: (i // tiles_per_batch, 0, j))
    row = lambda w: pl.BlockSpec((tm, w), lambda i: (i, 0))
    utri = (jnp.arange(MERGE_TM)[:, None] < jnp.arange(MERGE_TM)[None, :]).astype(BF16)
    ltri = (jnp.arange(N_EXPERTS)[None, :] < jnp.arange(N_EXPERTS)[:, None]).astype(BF16)
    return pl.pallas_call(
        _merge_kernel,
        grid=(N_TILES // MERGE_SUB,),
        in_specs=[pl.BlockSpec((512, tm), lambda i: (0, i)), pl.BlockSpec((512, tm), lambda i: (0, i)),
                  row(D_MODEL), row(D_MODEL), row(D_MODEL),
                  modspec(2), modspec(3), modspec(4),
                  const(w_ba.shape), const(w_bb.shape), const(w_out.shape), const(g2.shape),
                  const(w_rt.shape), const(b_r.shape), const(utri.shape), const(ltri.shape)],
        out_specs=[row(D_MODEL), row(D_MODEL),
                   pl.BlockSpec((TOP_K, tm), lambda i: (0, i)),
                   pl.BlockSpec((TOP_K, tm), lambda i: (0, i)),
                   pl.BlockSpec((MERGE_SUB * SUBLANES, LANES), lambda i: (i, 0))],
        out_shape=[jax.ShapeDtypeStruct((N_TOK, D_MODEL), F32),
                   jax.ShapeDtypeStruct((N_TOK, D_MODEL), BF16),
                   jax.ShapeDtypeStruct((TOP_K, N_TOK), I32),
                   jax.ShapeDtypeStruct((TOP_K, N_TOK), F32),
                   jax.ShapeDtypeStruct((N_TILES * SUBLANES, LANES), I32)],
        scratch_shapes=[pltpu.VMEM((N_EXPERTS, LANES), F32)],
        compiler_params=pltpu.CompilerParams(
            dimension_semantics=("arbitrary",), vmem_limit_bytes=VMEM_LIMIT),
        name="merge",
    )(o_a, o_b, sig_a, sig_b, x2d, mod3, mod3, mod3, w_ba, w_bb, w_out, g2, w_rt, b_r, utri, ltri)


def _chunk_copy(hbm, hbm_row, buf, buf_row, size, to_hbm, sem):
    vm = buf.at[pl.ds(pl.multiple_of(buf_row, SEG_ALIGN), size)]
    hb = hbm.at[pl.ds(pl.multiple_of(hbm_row, SEG_ALIGN), size)]
    return pltpu.make_async_copy(vm, hb, sem) if to_hbm else pltpu.make_async_copy(hb, vm, sem)


def _seg_start(seg_smem, tile, hbm, buf, to_hbm, sem):
    base = tile * SEG_W

    def chunks(src, rows, dst, bits):
        for b in bits:
            size = 1 << b
            done = rows & (-2 * size)

            @pl.when((rows & size) != 0)
            def _():
                _chunk_copy(hbm, dst + done, buf, src + done, size, to_hbm, sem).start()

    def per_expert(e, carry):
        src = seg_smem[base + e]
        rows = seg_smem[base + N_EXPERTS + e]
        dst = seg_smem[base + 2 * N_EXPERTS + e]

        @pl.when(rows >= SEG_BIG)
        def _():
            chunks(src, rows, dst, [b for b in SEG_BITS if (1 << b) >= SEG_BIG])

        chunks(src, rows, dst, [b for b in SEG_BITS if (1 << b) < SEG_BIG])
        return carry

    lax.fori_loop(0, N_EXPERTS, per_expert, 0)


def _seg_wait(seg_smem, tile, hbm, buf, to_hbm, sem):
    assert LROWS < 2 << SEG_BITS[-1]
    total = seg_smem[tile * SEG_W + SEG_TOTAL]
    for b in SEG_BITS:
        size = 1 << b

        @pl.when((total & size) != 0)
        def _():
            _chunk_copy(hbm, 0, buf, 0, size, to_hbm, sem).wait()


def _dispatch_kernel(pend_ref, padded_ref, seg_ref, pos_ref, h2_ref, xs_ref, staged, sems):
    j = pl.program_id(0)
    slot = lax.rem(j, 2)

    @pl.when(j == 0)
    def _():
        staged[0, 0:MOE_BLK, :] = jnp.zeros((MOE_BLK, PACK_W), U32)

        def tail(action):
            def body(e, carry):
                @pl.when(padded_ref[e] > 0)
                def _():
                    start = pl.multiple_of(pend_ref[e] - MOE_BLK, MOE_BLK)
                    cp = pltpu.make_async_copy(staged.at[0, pl.ds(0, MOE_BLK)],
                                               xs_ref.at[pl.ds(start, MOE_BLK)], sems.at[0])
                    getattr(cp, action)()
                return carry
            lax.fori_loop(0, N_EXPERTS, body, 0)

        def unused(action):
            def body(b, carry):
                cp = pltpu.make_async_copy(
                    staged.at[0, pl.ds(0, MOE_BLK)],
                    xs_ref.at[pl.ds(pl.multiple_of(b * MOE_BLK, MOE_BLK), MOE_BLK)], sems.at[0])
                getattr(cp, action)()
                return carry
            lax.fori_loop(pend_ref[N_EXPERTS - 1] // MOE_BLK, N_BLK, body, 0)

        tail("start")
        unused("start")
        tail("wait")
        unused("wait")

    pos = pos_ref[...]
    piota = lax.broadcasted_iota(I32, (LROWS, MERGE_TM), 0)
    hit = piota == pos[0:1, :]
    for k in range(1, TOP_K):
        hit = hit | (piota == pos[k:k + 1, :])
    staged[slot] = _pack_rows(_mm(jnp.where(hit, 1.0, 0.0), h2_ref[...]))

    _seg_start(seg_ref, j, xs_ref, staged.at[slot], True, sems.at[1 + slot])

    @pl.when(j > 0)
    def _():
        _seg_wait(seg_ref, j - 1, xs_ref, staged.at[1 - slot], True, sems.at[2 - slot])

    @pl.when(j == N_TILES - 1)
    def _():
        _seg_wait(seg_ref, j, xs_ref, staged.at[slot], True, sems.at[1 + slot])


def _dispatch(pends, padded, segtab, pos, h2):
    grid_spec = pltpu.PrefetchScalarGridSpec(
        num_scalar_prefetch=3,
        grid=(N_TILES,),
        in_specs=[
            pl.BlockSpec((TOP_K, MERGE_TM), lambda j, pe, pa, sg: (0, j)),
            pl.BlockSpec((MERGE_TM, D_MODEL), lambda j, pe, pa, sg: (j, 0)),
        ],
        out_specs=pl.BlockSpec(memory_space=pl.ANY),
        scratch_shapes=[pltpu.VMEM((2, LROWS, PACK_W), U32),
                        pltpu.SemaphoreType.DMA((3,))],
    )
    return pl.pallas_call(
        _dispatch_kernel,
        grid_spec=grid_spec,
        out_shape=jax.ShapeDtypeStruct((N_SLOTS, PACK_W), U32),
        compiler_params=pltpu.CompilerParams(
            dimension_semantics=("arbitrary",), vmem_limit_bytes=VMEM_LIMIT),
        name="dispatch",
    )(pends, padded, segtab, pos, h2)


SPLIT_SUB = 256


def _expert_kernel(be_ref, nused_ref, xs_ref, wgu_ref, bg_ref, bl_ref, wd_ref, bd_ref, perm_ref, o_ref,
                   wg_s, wl_s, wd_s):
    i = pl.program_id(0)
    used = i < nused_ref[0]
    fresh = jnp.logical_or(i == 0, be_ref[i] != be_ref[jnp.maximum(i - 1, 0)])

    @pl.when(jnp.logical_and(used, fresh))
    def _():
        half = SPLIT_SUB // 2
        for s in range(2 * D_EXPERT // SPLIT_SUB):
            r = _mm(wgu_ref[:, s * SPLIT_SUB:(s + 1) * SPLIT_SUB], perm_ref[...])
            wg_s[:, s * half:(s + 1) * half] = r[:, :half].astype(BF16)
            wl_s[:, s * half:(s + 1) * half] = r[:, half:].astype(BF16)
        wd_s[...] = wd_ref[...].astype(BF16)

    @pl.when(used)
    def _():
        xb = _unpack_rows(xs_ref[...])
        g = jnp.minimum(_mm(xb, wg_s[...]) + bg_ref[...], SWIGLU_LIMIT)
        l = jnp.clip(_mm(xb, wl_s[...]) + bl_ref[...], -SWIGLU_LIMIT, SWIGLU_LIMIT)
        act = g * jax.nn.sigmoid(SWIGLU_ALPHA * g) * (l + 1.0)
        o_ref[...] = _pack_rows(_mm(act, wd_s[...]) + bd_ref[...])

    @pl.when(jnp.logical_not(used))
    def _():
        o_ref[...] = jnp.zeros_like(o_ref)


def _experts(block_expert, nused, xs, w_gu, bg, bl, w_dn, bd):
    col = jnp.arange(SPLIT_SUB)
    src = jnp.where(col < SPLIT_SUB // 2, 2 * col, 2 * (col - SPLIT_SUB // 2) + 1)
    perm = (jnp.arange(SPLIT_SUB)[:, None] == src[None, :]).astype(BF16)
    blk = lambda i, be, nu: (jnp.minimum(i, nu[0] - 1), 0)
    per_expert = lambda shape: pl.BlockSpec((None,) + shape, lambda i, be, nu: (be[i], 0, 0))
    grid_spec = pltpu.PrefetchScalarGridSpec(
        num_scalar_prefetch=2,
        grid=(N_BLK,),
        in_specs=[pl.BlockSpec((MOE_BLK, PACK_W), blk),
                  per_expert((D_MODEL, 2 * D_EXPERT)), per_expert((1, D_EXPERT)), per_expert((1, D_EXPERT)),
                  per_expert((D_EXPERT, D_MODEL)), per_expert((1, D_MODEL)),
                  pl.BlockSpec((SPLIT_SUB, SPLIT_SUB), lambda i, be, nu: (0, 0))],
        out_specs=pl.BlockSpec((MOE_BLK, PACK_W), lambda i, be, nu: (i, 0)),
        scratch_shapes=[pltpu.VMEM((D_MODEL, D_EXPERT), BF16), pltpu.VMEM((D_MODEL, D_EXPERT), BF16),
                        pltpu.VMEM((D_EXPERT, D_MODEL), BF16)],
    )
    return pl.pallas_call(
        _expert_kernel,
        grid_spec=grid_spec,
        out_shape=jax.ShapeDtypeStruct((N_SLOTS, PACK_W), U32),
        compiler_params=pltpu.CompilerParams(
            dimension_semantics=("arbitrary",), vmem_limit_bytes=VMEM_LIMIT),
        name="experts",
    )(block_expert, nused, xs, w_gu, bg, bl, w_dn, bd, perm)


def _combine_kernel(seg_ref, eo_ref, xnew_ref, g2_ref, pos_ref, wt_ref, o_ref, gbuf, sems):
    j = pl.program_id(0)
    slot = lax.rem(j, 2)

    def fetch(tile, sl):
        _seg_start(seg_ref, tile, eo_ref, gbuf.at[sl], False, sems.at[sl])

    @pl.when(j == 0)
    def _():
        gbuf[...] = jnp.zeros_like(gbuf)
        fetch(0, 0)

    @pl.when(j + 1 < N_TILES)
    def _():
        fetch(j + 1, 1 - slot)

    _seg_wait(seg_ref, j, eo_ref, gbuf.at[slot], False, sems.at[slot])

    pos = pos_ref[...]
    wt = wt_ref[...]
    lane = lax.broadcasted_iota(I32, (MERGE_TM, LROWS), 1)
    wsel = jnp.zeros((MERGE_TM, LROWS), F32)
    for k in range(TOP_K):
        wsel = jnp.where(lane == pos[:, k:k + 1], wt[:, k:k + 1], wsel)
    acc = _mm(wsel, _unpack_rows(gbuf[slot]))
    o_ref[...] = xnew_ref[...] + g2_ref[...] * acc


def _combine(segtab, eo, x_new, mod3, pos_t, wts_t):
    tiles_per_batch = SEQ // MERGE_TM
    grid_spec = pltpu.PrefetchScalarGridSpec(
        num_scalar_prefetch=1,
        grid=(N_TILES,),
        in_specs=[
            pl.BlockSpec(memory_space=pl.ANY),
            pl.BlockSpec((MERGE_TM, D_MODEL), lambda j, sg: (j, 0)),
            pl.BlockSpec((None, 1, D_MODEL), lambda j, sg: (j // tiles_per_batch, 0, 5)),
            pl.BlockSpec((MERGE_TM, TOP_K), lambda j, sg: (j, 0)),
            pl.BlockSpec((MERGE_TM, TOP_K), lambda j, sg: (j, 0)),
        ],
        out_specs=pl.BlockSpec((MERGE_TM, D_MODEL), lambda j, sg: (j, 0)),
        scratch_shapes=[pltpu.VMEM((2, LROWS, PACK_W), U32),
                        pltpu.SemaphoreType.DMA((2,))],
    )
    return pl.pallas_call(
        _combine_kernel,
        grid_spec=grid_spec,
        out_shape=jax.ShapeDtypeStruct((N_TOK, D_MODEL), F32),
        compiler_params=pltpu.CompilerParams(
            dimension_semantics=("arbitrary",), vmem_limit_bytes=VMEM_LIMIT),
        name="combine",
    )(segtab, eo, x_new, mod3, pos_t, wts_t)


def _pad_heads(w, heads, dim):
    lead = w.shape[:-1]
    w = w.reshape(lead + (heads, dim))
    w = jnp.pad(w, [(0, 0)] * len(lead) + [(0, 0), (0, LANES - dim)])
    return w.reshape(lead + (heads * LANES,))


def _rope_tables(dim, offset, repeat):
    pos = jnp.arange(SEQ, dtype=jnp.int32)
    row = (pos // GRID_W).astype(F32)
    col = (pos % GRID_W).astype(F32)
    q = dim // 4
    freqs = ROPE_THETA ** (-jnp.arange(q, dtype=F32) / q)
    ang_r = row[:, None] * freqs
    ang_c = col[:, None] * freqs
    ang = jnp.concatenate([ang_r, ang_r, ang_c, ang_c], axis=-1)
    cos, sin = jnp.cos(ang), jnp.sin(ang)
    first = (jnp.arange(dim) % (2 * q)) < q
    sa = jnp.where(first, -sin, 0.0)
    sb = jnp.where(first, 0.0, sin)
    pad = lambda t, v: jnp.pad(jnp.tile(t, (1, repeat)), ((0, 0), (offset, LANES - offset - dim * repeat)),
                               constant_values=v)
    return pad(cos, 1.0), pad(sa, 0.0), pad(sb, 0.0)


def kernel(x, c, ctx, c_ctx, w_ada, b_ada, norm1_g, norm2_g, w_in, mla_q_a_g, mla_kv_a_g, w_q_up, w_kv_up,
           mla_q_g, mla_k_g, swa_q_g, swa_k_g, swa_sink, w_branch_a, w_branch_b, w_out, w_router, b_router,
           w_gate_up, b_gate_up, w_down, b_down):
    assert x.shape == (BATCH, SEQ, D_MODEL) and ctx.shape == (BATCH, CTX_LEN, D_MODEL)
    assert w_ada.shape[0] == 1, "single layer"

    cond = jnp.concatenate([c, c_ctx[None], jnp.zeros((MOD_ROWS - BATCH - 1, D_MODEL), F32)], axis=0)
    mod = _ada(cond, w_ada[0], b_ada[0][None])
    mod3 = mod.reshape(MOD_ROWS, 1, N_MOD * D_MODEL)

    wi = w_in[0]
    o1, o2, o3 = MLA_KV_RANK, MLA_KV_RANK + MLA_ROPE, MLA_KV_RANK + MLA_ROPE + SWA_KV_HEADS * SWA_HD
    kv_cols = o3 + SWA_KV_HEADS * SWA_HD
    q1 = kv_cols + MLA_Q_RANK
    q2 = q1 + SWA_HEADS * SWA_HD
    q3 = q2 + D_MODEL
    krope_grp = jnp.pad(wi[:, o1:o2], ((0, 0), (MLA_NOPE, LANES - MLA_QK)))
    win = jnp.concatenate([
        wi[:, 0:o1], wi[:, o2:o3], krope_grp, wi[:, kv_cols:q1], wi[:, q1:q2], wi[:, q2:q3], wi[:, q3:]],
        axis=1).astype(BF16)
    assert win.shape == (D_MODEL, C_END)
    wvst = wi[:, o3:kv_cols].T.astype(BF16)
    wkv = w_kv_up[0].reshape(MLA_KV_RANK, MLA_HEADS, MLA_NOPE + MLA_V)
    wkn = _pad_heads(wkv[:, :, :MLA_NOPE].reshape(MLA_KV_RANK, -1), MLA_HEADS, MLA_NOPE).astype(BF16)
    wvt = wkv[:, :, MLA_NOPE:].reshape(MLA_KV_RANK, MLA_HEADS * MLA_V).T.astype(BF16)
    wqup = _pad_heads(w_q_up[0], MLA_HEADS, MLA_QK).astype(BF16)
    gk_t = _pad_heads(jnp.tile(mla_k_g[0], MLA_HEADS)[None], MLA_HEADS, MLA_QK)
    gq_t = _pad_heads(jnp.tile(mla_q_g[0] * (MLA_QK ** -0.5 * LOG2E), MLA_HEADS)[None], MLA_HEADS, MLA_QK)
    gks_t = jnp.tile(swa_k_g[0], SWA_KV_HEADS)[None]
    gqs_t = jnp.tile(swa_q_g[0] * (SWA_HD ** -0.5 * LOG2E), SWA_HEADS)[None]

    def lane_to_head(n_lanes, width):
        ind = ((jnp.arange(n_lanes) // width)[:, None] == jnp.arange(LANES)[None, :]).astype(BF16)
        return ind, jnp.concatenate([ind.T, ind.T], axis=0)

    ind, indt = lane_to_head(MLA_HEADS * LANES, LANES)
    ind64, indt64 = lane_to_head(SWA_HEADS * SWA_HD, SWA_HD)
    consts = [norm1_g[0][None], mla_kv_a_g[0][None], wkn, wvt, wvst, gk_t, gks_t, ind, indt, ind64, indt64]
    q_consts = [mla_q_a_g[0][None], wqup, gq_t, gqs_t]

    tabs_lat = _rope_tables(MLA_ROPE, MLA_NOPE, 1) + _rope_tables(SWA_HD, 0, LANES // SWA_HD)
    ones = jnp.ones((PROJ_TM, LANES), F32)
    zeros = jnp.zeros((PROJ_TM, LANES), F32)
    tabs_ctx = (ones, zeros, zeros, ones, zeros, zeros)

    x2d = x.reshape(N_TOK, D_MODEL)
    tiles_per_batch = SEQ // PROJ_TM
    k_mla, vt_mla, k_swa, vt_swa, q_mla, q_swa, sig_a, sig_b = _proj(
        x2d, mod3, lambda i: i // tiles_per_batch, lambda i: i % tiles_per_batch,
        tabs_lat, consts, q_consts, win, True, "proj_latent")
    kc_mla, vct_mla, kc_swa, vct_swa = _proj(
        ctx.reshape(BATCH * CTX_LEN, D_MODEL), mod3, lambda i: BATCH, lambda i: 0,
        tabs_ctx, consts, q_consts, win[:, :C_KVEND], False, "proj_ctx")

    o_a = _mla_attn(q_mla, k_mla, vt_mla, kc_mla, vct_mla)
    o_b = _swa_attn(swa_sink[0], q_swa, k_swa, vt_swa, kc_swa, vct_swa)

    x_new, h2, pos, top_w, seg = _merge(
        o_a, o_b, sig_a, sig_b, x2d, mod3,
        w_branch_a[0].astype(BF16), w_branch_b[0].astype(BF16), w_out[0].astype(BF16),
        norm2_g[0][None], w_router[0].T, b_router[0][:, None])

    seg3 = seg.reshape(N_TILES, SUBLANES, LANES)
    lstart, rows, gcarry = (seg3[:, r, :N_EXPERTS] for r in range(3))
    total = gcarry[-1] + rows[-1]
    padded = (total + MOE_BLK - 1) // MOE_BLK * MOE_BLK
    pends = jnp.cumsum(padded).astype(I32)
    pstarts = pends - padded
    nused = (pends[-1] // MOE_BLK).astype(I32)[None]
    blk_start = jnp.arange(N_BLK, dtype=I32) * MOE_BLK
    block_expert = jnp.minimum(
        jnp.sum((blk_start[:, None] >= pends[None, :]).astype(I32), axis=1), N_EXPERTS - 1)
    segtab = jnp.concatenate(
        [lstart, rows, pstarts[None, :] + gcarry,
         jnp.broadcast_to(jnp.sum(rows, axis=1, keepdims=True), rows.shape)], axis=1).reshape(-1).astype(I32)

    xs = _dispatch(pends, padded.astype(I32), segtab, pos, h2)
    bgu = b_gate_up[0]
    eo = _experts(block_expert, nused, xs, w_gate_up[0], bgu[:, None, 0::2], bgu[:, None, 1::2],
                  w_down[0], b_down[0][:, None, :])
    out = _combine(segtab, eo, x_new, mod3, pos.T, top_w.T)
    return out.reshape(BATCH, SEQ, D_MODEL)
```

```python
import functools

import jax
import jax.numpy as jnp
from jax import lax
from jax.experimental import pallas as pl
from jax.experimental.pallas import tpu as pltpu

F32 = jnp.float32
BF16 = jnp.bfloat16
I32 = jnp.int32
U32 = jnp.uint32

LANES = 128
SUBLANES = 8

D_MODEL = 1024
BATCH = 8
SEQ = 4096
CTX_LEN = 256
GRID_W = 64
ROPE_THETA = 10000.0
NORM_EPS = 1e-6
N_MOD = 6
NEG_INF = -1e30
LOG2E = 1.4426950408889634

MLA_HEADS = 8
MLA_NOPE = 64
MLA_ROPE = 32
MLA_QK = MLA_NOPE + MLA_ROPE
MLA_V = 64
MLA_Q_RANK = 384
MLA_KV_RANK = 256

SWA_HEADS = 8
SWA_KV_HEADS = 2
SWA_GROUP = SWA_HEADS // SWA_KV_HEADS
SWA_HD = 64
WINDOW = 128

N_EXPERTS = 32
TOP_K = 4
D_EXPERT = D_MODEL
SWIGLU_LIMIT = 7.0
SWIGLU_ALPHA = 1.702

N_TOK = BATCH * SEQ
MOD_ROWS = 16

ADA_TN = 1536
PROJ_TM = 512
MLA_TQ = 256
SWA_TQ = 256
SWA_KW = SWA_TQ + 2 * WINDOW
MERGE_TM = 256
MERGE_SUB = 2
MOE_BLK = 512
SEG_ALIGN = SUBLANES
N_TILES = N_TOK // MERGE_TM
LROWS = -(-(MERGE_TM * TOP_K + N_EXPERTS * (SEG_ALIGN - 1)) // LANES) * LANES
SEG_BITS = tuple(range(SEG_ALIGN.bit_length() - 1, (MERGE_TM * TOP_K).bit_length()))
SEG_W = 128
SEG_TOTAL = 3 * N_EXPERTS
SEG_BIG = 128
MOVE_SLOTS = 3
N_SLOTS = -(-(N_TOK * TOP_K + N_TILES * N_EXPERTS * (SEG_ALIGN - 1) + N_EXPERTS * (MOE_BLK - 1))
            // MOE_BLK) * MOE_BLK
N_BLK = N_SLOTS // MOE_BLK

PACK_W = D_MODEL // 2

VMEM_LIMIT = 56 * 1024 * 1024

C_KVLAT = 0
C_KSWA = C_KVLAT + MLA_KV_RANK
C_KROPE = C_KSWA + SWA_KV_HEADS * SWA_HD
C_KVEND = C_KROPE + LANES
C_QLAT = C_KVEND
C_QSWA = C_QLAT + MLA_Q_RANK
C_GA = C_QSWA + SWA_HEADS * SWA_HD
C_GB = C_GA + D_MODEL
C_END = C_GB + D_MODEL


def _mm(a, b):
    return jnp.dot(a.astype(BF16), b.astype(BF16), preferred_element_type=F32)


def _mm_nt(a, b):
    return lax.dot_general(a.astype(BF16), b.astype(BF16), (((1,), (1,)), ((), ())),
                           preferred_element_type=F32)


def _mm_tn(a, b):
    return lax.dot_general(a.astype(BF16), b.astype(BF16), (((0,), (0,)), ((), ())),
                           preferred_element_type=F32)


def _split(a):
    hi = a.astype(BF16)
    lo = (a - hi.astype(F32)).astype(BF16)
    return hi, lo


def _pack_rows(a):
    half = a.shape[1] // 2
    bits = lambda v: lax.bitcast_convert_type(v.astype(BF16).astype(F32), U32)
    return bits(a[:, half:]) | (bits(a[:, :half]) >> 16)


def _unpack_rows(p):
    lo = lax.bitcast_convert_type(p << 16, F32)
    hi = lax.bitcast_convert_type(p & jnp.uint32(0xFFFF0000), F32)
    return jnp.concatenate([lo, hi], axis=1).astype(BF16)


def _rms(x):
    return x * lax.rsqrt(jnp.mean(x * x, axis=-1, keepdims=True) + NORM_EPS)


def _ada_kernel(c_ref, w_ref, b_ref, o_ref):
    c = c_ref[...]
    s = c * jax.nn.sigmoid(c)
    shi, slo = _split(s)
    whi, wlo = _split(w_ref[...])
    acc = _mm(shi, whi) + _mm(slo, whi) + _mm(shi, wlo)
    o_ref[...] = acc + b_ref[...]


def _ada(cond, w_ada, b_ada):
    n = w_ada.shape[1]
    return pl.pallas_call(
        _ada_kernel,
        grid=(n // ADA_TN,),
        in_specs=[
            pl.BlockSpec((MOD_ROWS, D_MODEL), lambda j: (0, 0)),
            pl.BlockSpec((D_MODEL, ADA_TN), lambda j: (0, j)),
            pl.BlockSpec((1, ADA_TN), lambda j: (0, j)),
        ],
        out_specs=pl.BlockSpec((MOD_ROWS, ADA_TN), lambda j: (0, j)),
        out_shape=jax.ShapeDtypeStruct((MOD_ROWS, n), F32),
        compiler_params=pltpu.CompilerParams(
            dimension_semantics=("arbitrary",), vmem_limit_bytes=VMEM_LIMIT),
        name="ada",
    )(cond, w_ada, b_ada)


def _head_norm(xraw, ind_ref, indt_ref, inv_dim):
    w = xraw.shape[1]
    ss = _mm(xraw * xraw, ind_ref[0:w, :])
    r = lax.rsqrt(ss * inv_dim + NORM_EPS)
    scale = _mm(jnp.concatenate(_split(r), axis=1), indt_ref[:, 0:w])
    return xraw * scale


def _rope(xh, cos, sa, sb, quarter):
    return (xh * cos + pltpu.roll(xh, LANES - quarter, 1) * sa
            + pltpu.roll(xh, quarter, 1) * sb)


def _proj_kernel(*refs, with_q):
    (x_ref, sh_ref, sc_ref, g1_ref, win_ref, cm_ref, sam_ref, sbm_ref, cs_ref, sas_ref, sbs_ref,
     gkva_ref, wkn_ref, wvt_ref, wvst_ref, gk_ref, gks_ref, ind_ref, indt_ref,
     ind64_ref, indt64_ref) = refs[:21]
    if with_q:
        gqa_ref, wqup_ref, gq_ref, gqs_ref = refs[21:25]
        kmla_ref, vmla_ref, kswa_ref, vswa_ref, qmla_ref, qswa_ref, siga_ref, sigb_ref = refs[25:]
    else:
        kmla_ref, vmla_ref, kswa_ref, vswa_ref = refs[21:]

    x = x_ref[...]
    h = _rms(x) * g1_ref[...] * (1.0 + sc_ref[...]) + sh_ref[...]
    y = _mm(h, win_ref[...])

    cm, sam, sbm = cm_ref[...], sam_ref[...], sbm_ref[...]
    cs, sas, sbs = cs_ref[...], sas_ref[...], sbs_ref[...]

    kvn = _rms(y[:, C_KVLAT:C_KVLAT + MLA_KV_RANK]) * gkva_ref[...]
    kn = _mm(kvn, wkn_ref[...])
    vmla_ref[...] = _mm_nt(wvt_ref[...], kvn).astype(BF16)
    vswa_ref[...] = _mm_nt(wvst_ref[...], h).astype(BF16)
    kr = y[:, C_KROPE:C_KROPE + LANES]
    kraw = kn + jnp.concatenate([kr] * MLA_HEADS, axis=1)
    kfull = _head_norm(kraw, ind_ref, indt_ref, 1.0 / MLA_QK) * gk_ref[...]
    for hd in range(MLA_HEADS):
        sl = slice(hd * LANES, (hd + 1) * LANES)
        kmla_ref[:, sl] = _rope(kfull[:, sl], cm, sam, sbm, MLA_ROPE // 4).astype(BF16)

    ks = _head_norm(y[:, C_KSWA:C_KSWA + SWA_KV_HEADS * SWA_HD], ind64_ref, indt64_ref,
                    1.0 / SWA_HD) * gks_ref[...]
    for grp in range(SWA_KV_HEADS * SWA_HD // LANES):
        sl = slice(grp * LANES, (grp + 1) * LANES)
        kswa_ref[:, sl] = _rope(ks[:, sl], cs, sas, sbs, SWA_HD // 4).astype(BF16)

    if with_q:
        qn = _rms(y[:, C_QLAT:C_QLAT + MLA_Q_RANK]) * gqa_ref[...]
        qraw = _mm(qn, wqup_ref[...])
        qf = _head_norm(qraw, ind_ref, indt_ref, 1.0 / MLA_QK) * gq_ref[...]
        for hd in range(MLA_HEADS):
            sl = slice(hd * LANES, (hd + 1) * LANES)
            qmla_ref[:, sl] = _rope(qf[:, sl], cm, sam, sbm, MLA_ROPE // 4).astype(BF16)
        qs = _head_norm(y[:, C_QSWA:C_QSWA + SWA_HEADS * SWA_HD], ind64_ref, indt64_ref,
                        1.0 / SWA_HD) * gqs_ref[...]
        for grp in range(SWA_HEADS * SWA_HD // LANES):
            sl = slice(grp * LANES, (grp + 1) * LANES)
            qswa_ref[:, sl] = _rope(qs[:, sl], cs, sas, sbs, SWA_HD // 4).astype(BF16)
        siga_ref[...] = jax.nn.sigmoid(y[:, C_GA:C_GA + D_MODEL]).astype(BF16)
        sigb_ref[...] = jax.nn.sigmoid(y[:, C_GB:C_GB + D_MODEL]).astype(BF16)


def _proj(x2d, mod3, mod_row_fn, tab_row_fn, tabs, consts, q_consts, win, with_q, name):
    rows = x2d.shape[0]
    tm = PROJ_TM
    const = lambda shape: pl.BlockSpec(shape, lambda i: (0,) * len(shape))
    tab = pl.BlockSpec((tm, LANES), lambda i: (tab_row_fn(i), 0))
    g1, rest = consts[0], consts[1:]
    in_specs = [
        pl.BlockSpec((tm, D_MODEL), lambda i: (i, 0)),
        pl.BlockSpec((None, 1, D_MODEL), lambda i: (mod_row_fn(i), 0, 0)),
        pl.BlockSpec((None, 1, D_MODEL), lambda i: (mod_row_fn(i), 0, 1)),
        const(g1.shape),
        pl.BlockSpec(win.shape, lambda i: (0, 0), pipeline_mode=pl.Buffered(1)),
        tab, tab, tab, tab, tab, tab,
    ] + [const(c.shape) for c in rest]
    args = [x2d, mod3, mod3, g1, win] + list(tabs) + list(rest)
    outs = [(MLA_HEADS * LANES, False), (MLA_HEADS * MLA_V, True),
            (SWA_KV_HEADS * SWA_HD, False), (SWA_KV_HEADS * SWA_HD, True)]
    if with_q:
        in_specs += [const(c.shape) for c in q_consts]
        args += list(q_consts)
        outs += [(MLA_HEADS * LANES, False), (SWA_HEADS * SWA_HD, False), (D_MODEL, False), (D_MODEL, False)]
    return pl.pallas_call(
        functools.partial(_proj_kernel, with_q=with_q),
        grid=(rows // tm,),
        in_specs=in_specs,
        out_specs=[pl.BlockSpec((w, tm), lambda i: (0, i)) if t else pl.BlockSpec((tm, w), lambda i: (i, 0))
                   for w, t in outs],
        out_shape=[jax.ShapeDtypeStruct((w, rows) if t else (rows, w), BF16) for w, t in outs],
        compiler_params=pltpu.CompilerParams(
            dimension_semantics=("arbitrary",), vmem_limit_bytes=VMEM_LIMIT),
        name=name,
    )(*args)


def _mla_step(q_ref, kl_ref, vlt_ref, kc_ref, vct_ref, o_ref, cur, prev):
    s_lat_c, s_ctx_c, m_c = cur
    s_lat_p, s_ctx_p, m_p = prev
    outs = []
    for hh in range(2):
        sl = slice(hh * LANES, (hh + 1) * LANES)
        vrows = slice(hh * MLA_V, (hh + 1) * MLA_V)
        q = q_ref[:, sl]
        s1 = _mm_nt(kl_ref[:, sl], q)
        s2 = _mm_nt(kc_ref[:, sl], q)
        s_lat_c[hh] = s1
        s_ctx_c[hh] = s2
        m_c[hh] = jnp.maximum(jnp.max(s1, axis=0, keepdims=True),
                              jnp.max(s2, axis=0, keepdims=True))

        m = m_p[hh]
        p1 = jnp.exp2(s_lat_p[hh] - m)
        p2 = jnp.exp2(s_ctx_p[hh] - m)
        l = jnp.sum(p1, axis=0, keepdims=True) + jnp.sum(p2, axis=0, keepdims=True)
        o = _mm(vlt_ref[vrows, :], p1) + _mm(vct_ref[vrows, :], p2)
        outs.append(o / l)
    o_ref[...] = jnp.concatenate(outs, axis=0).astype(BF16)


def _mla_kernel(q_ref, kl_ref, vl_ref, kc_ref, vc_ref, o_ref, *scratch):
    set0, set1 = scratch[:3], scratch[3:]
    i = pl.program_id(0)

    @pl.when(i == 0)
    def _():
        for ref in set1:
            ref[...] = jnp.zeros(ref.shape, F32)

    @pl.when(lax.rem(i, 2) == 0)
    def _():
        _mla_step(q_ref, kl_ref, vl_ref, kc_ref, vc_ref, o_ref, set0, set1)

    @pl.when(lax.rem(i, 2) == 1)
    def _():
        _mla_step(q_ref, kl_ref, vl_ref, kc_ref, vc_ref, o_ref, set1, set0)


def _mla_attn(q, k_lat, vt_lat, k_ctx, vt_ctx):
    tq = MLA_TQ
    nq = SEQ // tq
    pairs = MLA_HEADS // 2
    n_items = BATCH * pairs * nq

    def item(s):
        return s // (pairs * nq), (s // nq) % pairs, s % nq

    def scored(s):
        return item(jnp.minimum(s, n_items - 1))

    def finished(s):
        return item(jnp.maximum(s - 1, 0))

    def q_map(s):
        b, hp, i = scored(s)
        return b * nq + i, hp

    def k_map(s):
        b, hp, _ = scored(s)
        return b, hp

    def v_map(s):
        b, hp, _ = finished(s)
        return hp, b

    def o_map(s):
        b, hp, i = finished(s)
        return hp, b * nq + i

    return pl.pallas_call(
        _mla_kernel,
        grid=(n_items + 1,),
        in_specs=[
            pl.BlockSpec((tq, 2 * LANES), q_map),
            pl.BlockSpec((SEQ, 2 * LANES), k_map),
            pl.BlockSpec((2 * MLA_V, SEQ), v_map),
            pl.BlockSpec((CTX_LEN, 2 * LANES), k_map),
            pl.BlockSpec((2 * MLA_V, CTX_LEN), v_map),
        ],
        out_specs=pl.BlockSpec((2 * MLA_V, tq), o_map),
        out_shape=jax.ShapeDtypeStruct((MLA_HEADS * MLA_V, N_TOK), BF16),
        scratch_shapes=2 * [pltpu.VMEM((2, SEQ, tq), F32),
                            pltpu.VMEM((2, CTX_LEN, tq), F32),
                            pltpu.VMEM((2, 1, tq), F32)],
        compiler_params=pltpu.CompilerParams(
            dimension_semantics=("arbitrary",),
            vmem_limit_bytes=VMEM_LIMIT),
        name="mla_attn",
    )(q, k_lat, vt_lat, k_ctx, vt_ctx)


def _swa_window_start(i):
    return jnp.clip(i * SWA_TQ - WINDOW, 0, SEQ - SWA_KW)


def _swa_kernel(sink_ref, q_ref, k_ref, vt_ref, kc_ref, vct_ref, bias_ref, o_ref):
    tq = SWA_TQ
    i = pl.program_id(1)
    kstart = pl.multiple_of(_swa_window_start(i), LANES)
    kwin = k_ref[pl.ds(kstart, SWA_KW), :].astype(F32)
    vtwin = vt_ref[:, pl.ds(kstart, SWA_KW)]
    kc = kc_ref[...].astype(F32)
    vct = vct_ref[...]
    bias = bias_ref[...]

    def half_keys(k, kh):
        lane = lax.broadcasted_iota(I32, k.shape, 1)
        own = jnp.where((lane >= kh * SWA_HD) & (lane < (kh + 1) * SWA_HD), k, 0.0)
        other = pltpu.roll(own, SWA_HD, 1)
        return (own, other) if kh == 0 else (other, own)

    order = (0, 2, 1, 3)
    for kh in range(SWA_KV_HEADS):
        vrows = slice(kh * SWA_HD, (kh + 1) * SWA_HD)
        grp0 = kh * SWA_GROUP // 2
        q2 = jnp.concatenate([q_ref[:, (grp0 + c) * LANES:(grp0 + c + 1) * LANES] for c in range(2)],
                             axis=0)
        k_lo, k_hi = half_keys(kwin, kh)
        c_lo, c_hi = half_keys(kc, kh)
        sb = jnp.concatenate([_mm_nt(k_lo, q2), _mm_nt(k_hi, q2)], axis=1) + bias
        sc = jnp.concatenate([_mm_nt(c_lo, q2), _mm_nt(c_hi, q2)], axis=1)
        sink = jnp.concatenate(
            [jnp.full((1, tq), sink_ref[kh * SWA_GROUP + g] * LOG2E, F32) for g in order], axis=1)
        m = jnp.maximum(jnp.maximum(jnp.max(sb, axis=0, keepdims=True),
                                    jnp.max(sc, axis=0, keepdims=True)), sink)
        pb = jnp.exp2(sb - m)
        pc = jnp.exp2(sc - m)
        l = (jnp.sum(pb, axis=0, keepdims=True) + jnp.sum(pc, axis=0, keepdims=True)
             + jnp.exp2(sink - m))
        o = (_mm(vtwin[vrows, :], pb) + _mm(vct[vrows, :], pc)) / l
        for blk, g in enumerate(order):
            hd = kh * SWA_GROUP + g
            o_ref[hd * SWA_HD:(hd + 1) * SWA_HD, :] = o[:, blk * tq:(blk + 1) * tq].astype(BF16)


def _swa_attn(sink, q, k_lat, vt_lat, k_ctx, vt_ctx):
    tq = SWA_TQ
    nq = SEQ // tq
    offsets = (0, WINDOW, SWA_KW - tq)
    assert all(i * tq - min(max(i * tq - WINDOW, 0), SEQ - SWA_KW) == offsets[(i > 0) + (i == nq - 1)]
               for i in range(nq))
    key = jnp.arange(SWA_KW, dtype=I32)[:, None]
    qry = (jnp.arange(SWA_GROUP * tq, dtype=I32) % tq)[None, :]
    bias = jnp.stack([jnp.where(jnp.abs(d + qry - key) <= WINDOW, 0.0, NEG_INF) for d in offsets]).astype(F32)
    case = lambda i: (i > 0).astype(I32) + (i == nq - 1).astype(I32)
    return pl.pallas_call(
        _swa_kernel,
        grid=(BATCH, nq),
        in_specs=[
            pl.BlockSpec(memory_space=pltpu.SMEM),
            pl.BlockSpec((tq, SWA_HEADS * SWA_HD), lambda b, i: (b * nq + i, 0)),
            pl.BlockSpec((SEQ, SWA_KV_HEADS * SWA_HD), lambda b, i: (b, 0)),
            pl.BlockSpec((SWA_KV_HEADS * SWA_HD, SEQ), lambda b, i: (0, b)),
            pl.BlockSpec((CTX_LEN, SWA_KV_HEADS * SWA_HD), lambda b, i: (b, 0)),
            pl.BlockSpec((SWA_KV_HEADS * SWA_HD, CTX_LEN), lambda b, i: (0, b)),
            pl.BlockSpec((None, SWA_KW, SWA_GROUP * tq), lambda b, i: (case(i), 0, 0)),
        ],
        out_specs=pl.BlockSpec((SWA_HEADS * SWA_HD, tq), lambda b, i: (0, b * nq + i)),
        out_shape=jax.ShapeDtypeStruct((SWA_HEADS * SWA_HD, N_TOK), BF16),
        compiler_params=pltpu.CompilerParams(
            dimension_semantics=("arbitrary", "arbitrary"), vmem_limit_bytes=VMEM_LIMIT),
        name="swa_attn",
    )(sink, q, k_lat, vt_lat, k_ctx, vt_ctx, bias)


def _merge_kernel(oa_ref, ob_ref, sa_ref, sb_ref, x_ref, g1_ref, sh2_ref, sc2_ref, wba_ref, wbb_ref,
                  wout_ref, g2_ref, wr_ref, br_ref, utri_ref, ltri_ref,
                  xnew_ref, h2_ref, pos_ref, wts_ref, seg_ref, carry_ref):
    i = pl.program_id(0)

    @pl.when(i == 0)
    def _():
        carry_ref[...] = jnp.zeros_like(carry_ref)

    for sub in range(MERGE_SUB):
        _merge_tile(sub, oa_ref, ob_ref, sa_ref, sb_ref, x_ref, g1_ref, sh2_ref, sc2_ref, wba_ref,
                    wbb_ref, wout_ref, g2_ref, wr_ref, br_ref, utri_ref, ltri_ref,
                    xnew_ref, h2_ref, pos_ref, wts_ref, seg_ref, carry_ref)


def _merge_tile(sub, oa_ref, ob_ref, sa_ref, sb_ref, x_ref, g1_ref, sh2_ref, sc2_ref, wba_ref, wbb_ref,
                wout_ref, g2_ref, wr_ref, br_ref, utri_ref, ltri_ref,
                xnew_ref, h2_ref, pos_ref, wts_ref, seg_ref, carry_ref):
    tok = slice(sub * MERGE_TM, (sub + 1) * MERGE_TM)
    segrows = slice(sub * SUBLANES, (sub + 1) * SUBLANES)
    ya = _mm_tn(oa_ref[:, tok], wba_ref[...])
    yb = _mm_tn(ob_ref[:, tok], wbb_ref[...])
    y = sa_ref[tok, :].astype(F32) * ya + sb_ref[tok, :].astype(F32) * yb
    z = _mm(y, wout_ref[...])
    xn = x_ref[tok, :] + g1_ref[...] * z
    xnew_ref[tok, :] = xn
    h2 = _rms(xn) * g2_ref[...] * (1.0 + sc2_ref[...]) + sh2_ref[...]
    h2_ref[tok, :] = h2.astype(BF16)

    hhi, hlo = _split(h2)
    whi, wlo = _split(wr_ref[...])
    lg = _mm_nt(whi, hhi) + _mm_nt(whi, hlo) + _mm_nt(wlo, hhi) + br_ref[...]
    eiota = lax.broadcasted_iota(I32, lg.shape, 0).astype(F32)
    vals, onehots = [], []
    for k in range(TOP_K):
        m = jnp.max(lg, axis=0, keepdims=True)
        ik = jnp.min(jnp.where(lg == m, eiota, float(N_EXPERTS)), axis=0, keepdims=True)
        hit = eiota == ik
        vals.append(m)
        onehots.append(hit.astype(F32))
        lg = jnp.where(hit, -jnp.inf, lg)
    es = [jnp.exp(v - vals[0]) for v in vals]
    tot = es[0] + es[1] + es[2] + es[3]
    for k in range(TOP_K):
        wts_ref[k:k + 1, tok] = es[k] / tot

    tots = [jnp.sum(oh, axis=1, keepdims=True) for oh in onehots]
    n = tots[0] + tots[1] + tots[2] + tots[3]
    m_al = jnp.floor((n + (SEG_ALIGN - 1)) * (1.0 / SEG_ALIGN))
    m_b = jnp.broadcast_to(m_al, (N_EXPERTS, LANES))
    lstart = _mm(ltri_ref[...], m_b) * float(SEG_ALIGN)
    off = jnp.zeros_like(n)
    for k in range(TOP_K):
        prefix = _mm(onehots[k], utri_ref[...])
        lp = jnp.sum(onehots[k] * (lstart[:, 0:1] + off + prefix), axis=0, keepdims=True)
        pos_ref[k:k + 1, tok] = lp.astype(I32)
        off = off + tots[k]
    eye = (lax.broadcasted_iota(I32, (N_EXPERTS, LANES), 0)
           == lax.broadcasted_iota(I32, (N_EXPERTS, LANES), 1)).astype(F32)
    to_row = lambda v: jnp.sum(v * eye, axis=0, keepdims=True).astype(I32)
    carry = carry_ref[...]
    seg_ref[segrows, :] = jnp.concatenate(
        [to_row(lstart), to_row(m_b * float(SEG_ALIGN)), to_row(carry),
         jnp.zeros((SUBLANES - 3, LANES), I32)], axis=0)
    carry_ref[...] = carry + m_b * float(SEG_ALIGN)


def _merge(o_a, o_b, sig_a, sig_b, x2d, mod3, w_ba, w_bb, w_out, g2, w_rt, b_r):
    tm = MERGE_TM * MERGE_SUB
    tiles_per_batch = SEQ // tm
    const = lambda shape: pl.BlockSpec(shape, lambda i: (0,) * len(shape))
    modspec = lambda j: pl.BlockSpec((None, 1, D_MODEL), lambda i: (i // tiles_per_batch, 0, j))
    row = lambda w: pl.BlockSpec((tm, w), lambda i: (i, 0))
    utri = (jnp.arange(MERGE_TM)[:, None] < jnp.arange(MERGE_TM)[None, :]).astype(BF16)
    ltri = (jnp.arange(N_EXPERTS)[None, :] < jnp.arange(N_EXPERTS)[:, None]).astype(BF16)
    return pl.pallas_call(
        _merge_kernel,
        grid=(N_TILES // MERGE_SUB,),
        in_specs=[pl.BlockSpec((512, tm), lambda i: (0, i)), pl.BlockSpec((512, tm), lambda i: (0, i)),
                  row(D_MODEL), row(D_MODEL), row(D_MODEL),
                  modspec(2), modspec(3), modspec(4),
                  const(w_ba.shape), const(w_bb.shape), const(w_out.shape), const(g2.shape),
                  const(w_rt.shape), const(b_r.shape), const(utri.shape), const(ltri.shape)],
        out_specs=[row(D_MODEL), row(D_MODEL),
                   pl.BlockSpec((TOP_K, tm), lambda i: (0, i)),
                   pl.BlockSpec((TOP_K, tm), lambda i: (0, i)),
                   pl.BlockSpec((MERGE_SUB * SUBLANES, LANES), lambda i: (i, 0))],
        out_shape=[jax.ShapeDtypeStruct((N_TOK, D_MODEL), F32),
                   jax.ShapeDtypeStruct((N_TOK, D_MODEL), BF16),
                   jax.ShapeDtypeStruct((TOP_K, N_TOK), I32),
                   jax.ShapeDtypeStruct((TOP_K, N_TOK), F32),
                   jax.ShapeDtypeStruct((N_TILES * SUBLANES, LANES), I32)],
        scratch_shapes=[pltpu.VMEM((N_EXPERTS, LANES), F32)],
        compiler_params=pltpu.CompilerParams(
            dimension_semantics=("arbitrary",), vmem_limit_bytes=VMEM_LIMIT),
        name="merge",
    )(o_a, o_b, sig_a, sig_b, x2d, mod3, mod3, mod3, w_ba, w_bb, w_out, g2, w_rt, b_r, utri, ltri)


def _chunk_copy(hbm, hbm_row, buf, buf_row, size, to_hbm, sem):
    vm = buf.at[pl.ds(pl.multiple_of(buf_row, SEG_ALIGN), size)]
    hb = hbm.at[pl.ds(pl.multiple_of(hbm_row, SEG_ALIGN), size)]
    return pltpu.make_async_copy(vm, hb, sem) if to_hbm else pltpu.make_async_copy(hb, vm, sem)


def _seg_start(seg_smem, tile, hbm, buf, to_hbm, sem):
    base = tile * SEG_W

    def chunks(src, rows, dst, bits):
        for b in bits:
            size = 1 << b
            done = rows & (-2 * size)

            @pl.when((rows & size) != 0)
            def _():
                _chunk_copy(hbm, dst + done, buf, src + done, size, to_hbm, sem).start()

    def per_expert(e, carry):
        src = seg_smem[base + e]
        rows = seg_smem[base + N_EXPERTS + e]
        dst = seg_smem[base + 2 * N_EXPERTS + e]

        @pl.when(rows >= SEG_BIG)
        def _():
            chunks(src, rows, dst, [b for b in SEG_BITS if (1 << b) >= SEG_BIG])

        chunks(src, rows, dst, [b for b in SEG_BITS if (1 << b) < SEG_BIG])
        return carry

    lax.fori_loop(0, N_EXPERTS, per_expert, 0)


def _seg_wait(seg_smem, tile, hbm, buf, to_hbm, sem):
    assert LROWS < 2 << SEG_BITS[-1]
    total = seg_smem[tile * SEG_W + SEG_TOTAL]
    for b in SEG_BITS:
        size = 1 << b

        @pl.when((total & size) != 0)
        def _():
            _chunk_copy(hbm, 0, buf, 0, size, to_hbm, sem).wait()


def _dispatch_kernel(pend_ref, padded_ref, seg_ref, pos_ref, h2_ref, xs_ref, staged, sems):
    j = pl.program_id(0)
    slot = lax.rem(j, MOVE_SLOTS)

    @pl.when(j == 0)
    def _():
        staged[0, 0:MOE_BLK, :] = jnp.zeros((MOE_BLK, PACK_W), U32)

        def tail(action):
            def body(e, carry):
                @pl.when(padded_ref[e] > 0)
                def _():
                    start = pl.multiple_of(pend_ref[e] - MOE_BLK, MOE_BLK)
                    cp = pltpu.make_async_copy(staged.at[0, pl.ds(0, MOE_BLK)],
                                               xs_ref.at[pl.ds(start, MOE_BLK)], sems.at[0])
                    getattr(cp, action)()
                return carry
            lax.fori_loop(0, N_EXPERTS, body, 0)

        def unused(action):
            def body(b, carry):
                cp = pltpu.make_async_copy(
                    staged.at[0, pl.ds(0, MOE_BLK)],
                    xs_ref.at[pl.ds(pl.multiple_of(b * MOE_BLK, MOE_BLK), MOE_BLK)], sems.at[0])
                getattr(cp, action)()
                return carry
            lax.fori_loop(pend_ref[N_EXPERTS - 1] // MOE_BLK, N_BLK, body, 0)

        tail("start")
        unused("start")
        tail("wait")
        unused("wait")

    pos = pos_ref[...]
    piota = lax.broadcasted_iota(I32, (LROWS, MERGE_TM), 0)
    hit = piota == pos[0:1, :]
    for k in range(1, TOP_K):
        hit = hit | (piota == pos[k:k + 1, :])
    staged[slot] = _pack_rows(_mm(jnp.where(hit, 1.0, 0.0), h2_ref[...]))

    _seg_start(seg_ref, j, xs_ref, staged.at[slot], True, sems.at[1 + slot])

    lag = MOVE_SLOTS - 1

    def wait_tile(t):
        s = lax.rem(t, MOVE_SLOTS)
        _seg_wait(seg_ref, t, xs_ref, staged.at[s], True, sems.at[1 + s])

    @pl.when(j >= lag)
    def _():
        wait_tile(j - lag)

    @pl.when(j == N_TILES - 1)
    def _():
        for back in reversed(range(lag)):
            wait_tile(j - back)


def _dispatch(pends, padded, segtab, pos, h2):
    grid_spec = pltpu.PrefetchScalarGridSpec(
        num_scalar_prefetch=3,
        grid=(N_TILES,),
        in_specs=[
            pl.BlockSpec((TOP_K, MERGE_TM), lambda j, pe, pa, sg: (0, j)),
            pl.BlockSpec((MERGE_TM, D_MODEL), lambda j, pe, pa, sg: (j, 0)),
        ],
        out_specs=pl.BlockSpec(memory_space=pl.ANY),
        scratch_shapes=[pltpu.VMEM((MOVE_SLOTS, LROWS, PACK_W), U32),
                        pltpu.SemaphoreType.DMA((1 + MOVE_SLOTS,))],
    )
    return pl.pallas_call(
        _dispatch_kernel,
        grid_spec=grid_spec,
        out_shape=jax.ShapeDtypeStruct((N_SLOTS, PACK_W), U32),
        compiler_params=pltpu.CompilerParams(
            dimension_semantics=("arbitrary",), vmem_limit_bytes=VMEM_LIMIT),
        name="dispatch",
    )(pends, padded, segtab, pos, h2)


SPLIT_SUB = 256


def _expert_kernel(be_ref, nused_ref, xs_ref, wgu_ref, bg_ref, bl_ref, wd_ref, bd_ref, perm_ref, o_ref,
                   wg_s, wl_s, wd_s):
    i = pl.program_id(0)
    used = i < nused_ref[0]
    fresh = jnp.logical_or(i == 0, be_ref[i] != be_ref[jnp.maximum(i - 1, 0)])

    @pl.when(jnp.logical_and(used, fresh))
    def _():
        half = SPLIT_SUB // 2
        for s in range(2 * D_EXPERT // SPLIT_SUB):
            r = _mm(wgu_ref[:, s * SPLIT_SUB:(s + 1) * SPLIT_SUB], perm_ref[...])
            wg_s[:, s * half:(s + 1) * half] = r[:, :half].astype(BF16)
            wl_s[:, s * half:(s + 1) * half] = r[:, half:].astype(BF16)
        wd_s[...] = wd_ref[...].astype(BF16)

    @pl.when(used)
    def _():
        xb = _unpack_rows(xs_ref[...])
        g = jnp.minimum(_mm(xb, wg_s[...]) + bg_ref[...], SWIGLU_LIMIT)
        l = jnp.clip(_mm(xb, wl_s[...]) + bl_ref[...], -SWIGLU_LIMIT, SWIGLU_LIMIT)
        act = g * jax.nn.sigmoid(SWIGLU_ALPHA * g) * (l + 1.0)
        o_ref[...] = _pack_rows(_mm(act, wd_s[...]) + bd_ref[...])

    @pl.when(jnp.logical_not(used))
    def _():
        o_ref[...] = jnp.zeros_like(o_ref)


def _experts(block_expert, nused, xs, w_gu, bg, bl, w_dn, bd):
    col = jnp.arange(SPLIT_SUB)
    src = jnp.where(col < SPLIT_SUB // 2, 2 * col, 2 * (col - SPLIT_SUB // 2) + 1)
    perm = (jnp.arange(SPLIT_SUB)[:, None] == src[None, :]).astype(BF16)
    blk = lambda i, be, nu: (jnp.minimum(i, nu[0] - 1), 0)
    per_expert = lambda shape: pl.BlockSpec((None,) + shape, lambda i, be, nu: (be[i], 0, 0))
    grid_spec = pltpu.PrefetchScalarGridSpec(
        num_scalar_prefetch=2,
        grid=(N_BLK,),
        in_specs=[pl.BlockSpec((MOE_BLK, PACK_W), blk),
                  per_expert((D_MODEL, 2 * D_EXPERT)), per_expert((1, D_EXPERT)), per_expert((1, D_EXPERT)),
                  per_expert((D_EXPERT, D_MODEL)), per_expert((1, D_MODEL)),
                  pl.BlockSpec((SPLIT_SUB, SPLIT_SUB), lambda i, be, nu: (0, 0))],
        out_specs=pl.BlockSpec((MOE_BLK, PACK_W), lambda i, be, nu: (i, 0)),
        scratch_shapes=[pltpu.VMEM((D_MODEL, D_EXPERT), BF16), pltpu.VMEM((D_MODEL, D_EXPERT), BF16),
                        pltpu.VMEM((D_EXPERT, D_MODEL), BF16)],
    )
    return pl.pallas_call(
        _expert_kernel,
        grid_spec=grid_spec,
        out_shape=jax.ShapeDtypeStruct((N_SLOTS, PACK_W), U32),
        compiler_params=pltpu.CompilerParams(
            dimension_semantics=("arbitrary",), vmem_limit_bytes=VMEM_LIMIT),
        name="experts",
    )(block_expert, nused, xs, w_gu, bg, bl, w_dn, bd, perm)


def _combine_kernel(seg_ref, eo_ref, xnew_ref, g2_ref, pos_ref, wt_ref, o_ref, gbuf, sems):
    j = pl.program_id(0)
    slot = lax.rem(j, MOVE_SLOTS)
    ahead = MOVE_SLOTS - 1

    def fetch(tile):
        sl = lax.rem(tile, MOVE_SLOTS)
        _seg_start(seg_ref, tile, eo_ref, gbuf.at[sl], False, sems.at[sl])

    @pl.when(j == 0)
    def _():
        gbuf[...] = jnp.zeros_like(gbuf)
        for t in range(ahead):
            fetch(jnp.int32(t))

    @pl.when(j + ahead < N_TILES)
    def _():
        fetch(j + ahead)

    _seg_wait(seg_ref, j, eo_ref, gbuf.at[slot], False, sems.at[slot])

    pos = pos_ref[...]
    wt = wt_ref[...]
    lane = lax.broadcasted_iota(I32, (MERGE_TM, LROWS), 1)
    wsel = jnp.zeros((MERGE_TM, LROWS), F32)
    for k in range(TOP_K):
        wsel = jnp.where(lane == pos[:, k:k + 1], wt[:, k:k + 1], wsel)
    acc = _mm(wsel, _unpack_rows(gbuf[slot]))
    o_ref[...] = xnew_ref[...] + g2_ref[...] * acc


def _combine(segtab, eo, x_new, mod3, pos_t, wts_t):
    tiles_per_batch = SEQ // MERGE_TM
    grid_spec = pltpu.PrefetchScalarGridSpec(
        num_scalar_prefetch=1,
        grid=(N_TILES,),
        in_specs=[
            pl.BlockSpec(memory_space=pl.ANY),
            pl.BlockSpec((MERGE_TM, D_MODEL), lambda j, sg: (j, 0)),
            pl.BlockSpec((None, 1, D_MODEL), lambda j, sg: (j // tiles_per_batch, 0, 5)),
            pl.BlockSpec((MERGE_TM, TOP_K), lambda j, sg: (j, 0)),
            pl.BlockSpec((MERGE_TM, TOP_K), lambda j, sg: (j, 0)),
        ],
        out_specs=pl.BlockSpec((MERGE_TM, D_MODEL), lambda j, sg: (j, 0)),
        scratch_shapes=[pltpu.VMEM((MOVE_SLOTS, LROWS, PACK_W), U32),
                        pltpu.SemaphoreType.DMA((MOVE_SLOTS,))],
    )
    return pl.pallas_call(
        _combine_kernel,
        grid_spec=grid_spec,
        out_shape=jax.ShapeDtypeStruct((N_TOK, D_MODEL), F32),
        compiler_params=pltpu.CompilerParams(
            dimension_semantics=("arbitrary",), vmem_limit_bytes=VMEM_LIMIT),
        name="combine",
    )(segtab, eo, x_new, mod3, pos_t, wts_t)


def _pad_heads(w, heads, dim):
    lead = w.shape[:-1]
    w = w.reshape(lead + (heads, dim))
    w = jnp.pad(w, [(0, 0)] * len(lead) + [(0, 0), (0, LANES - dim)])
    return w.reshape(lead + (heads * LANES,))


def _rope_tables(dim, offset, repeat):
    pos = jnp.arange(SEQ, dtype=jnp.int32)
    row = (pos // GRID_W).astype(F32)
    col = (pos % GRID_W).astype(F32)
    q = dim // 4
    freqs = ROPE_THETA ** (-jnp.arange(q, dtype=F32) / q)
    ang_r = row[:, None] * freqs
    ang_c = col[:, None] * freqs
    ang = jnp.concatenate([ang_r, ang_r, ang_c, ang_c], axis=-1)
    cos, sin = jnp.cos(ang), jnp.sin(ang)
    first = (jnp.arange(dim) % (2 * q)) < q
    sa = jnp.where(first, -sin, 0.0)
    sb = jnp.where(first, 0.0, sin)
    pad = lambda t, v: jnp.pad(jnp.tile(t, (1, repeat)), ((0, 0), (offset, LANES - offset - dim * repeat)),
                               constant_values=v)
    return pad(cos, 1.0), pad(sa, 0.0), pad(sb, 0.0)


def kernel(x, c, ctx, c_ctx, w_ada, b_ada, norm1_g, norm2_g, w_in, mla_q_a_g, mla_kv_a_g, w_q_up, w_kv_up,
           mla_q_g, mla_k_g, swa_q_g, swa_k_g, swa_sink, w_branch_a, w_branch_b, w_out, w_router, b_router,
           w_gate_up, b_gate_up, w_down, b_down):
    assert x.shape == (BATCH, SEQ, D_MODEL) and ctx.shape == (BATCH, CTX_LEN, D_MODEL)
    assert w_ada.shape[0] == 1, "single layer"

    cond = jnp.concatenate([c, c_ctx[None], jnp.zeros((MOD_ROWS - BATCH - 1, D_MODEL), F32)], axis=0)
    mod = _ada(cond, w_ada[0], b_ada[0][None])
    mod3 = mod.reshape(MOD_ROWS, 1, N_MOD * D_MODEL)

    wi = w_in[0]
    o1, o2, o3 = MLA_KV_RANK, MLA_KV_RANK + MLA_ROPE, MLA_KV_RANK + MLA_ROPE + SWA_KV_HEADS * SWA_HD
    kv_cols = o3 + SWA_KV_HEADS * SWA_HD
    q1 = kv_cols + MLA_Q_RANK
    q2 = q1 + SWA_HEADS * SWA_HD
    q3 = q2 + D_MODEL
    krope_grp = jnp.pad(wi[:, o1:o2], ((0, 0), (MLA_NOPE, LANES - MLA_QK)))
    win = jnp.concatenate([
        wi[:, 0:o1], wi[:, o2:o3], krope_grp, wi[:, kv_cols:q1], wi[:, q1:q2], wi[:, q2:q3], wi[:, q3:]],
        axis=1).astype(BF16)
    assert win.shape == (D_MODEL, C_END)
    wvst = wi[:, o3:kv_cols].T.astype(BF16)
    wkv = w_kv_up[0].reshape(MLA_KV_RANK, MLA_HEADS, MLA_NOPE + MLA_V)
    wkn = _pad_heads(wkv[:, :, :MLA_NOPE].reshape(MLA_KV_RANK, -1), MLA_HEADS, MLA_NOPE).astype(BF16)
    wvt = wkv[:, :, MLA_NOPE:].reshape(MLA_KV_RANK, MLA_HEADS * MLA_V).T.astype(BF16)
    wqup = _pad_heads(w_q_up[0], MLA_HEADS, MLA_QK).astype(BF16)
    gk_t = _pad_heads(jnp.tile(mla_k_g[0], MLA_HEADS)[None], MLA_HEADS, MLA_QK)
    gq_t = _pad_heads(jnp.tile(mla_q_g[0] * (MLA_QK ** -0.5 * LOG2E), MLA_HEADS)[None], MLA_HEADS, MLA_QK)
    gks_t = jnp.tile(swa_k_g[0], SWA_KV_HEADS)[None]
    gqs_t = jnp.tile(swa_q_g[0] * (SWA_HD ** -0.5 * LOG2E), SWA_HEADS)[None]

    def lane_to_head(n_lanes, width):
        ind = ((jnp.arange(n_lanes) // width)[:, None] == jnp.arange(LANES)[None, :]).astype(BF16)
        return ind, jnp.concatenate([ind.T, ind.T], axis=0)

    ind, indt = lane_to_head(MLA_HEADS * LANES, LANES)
    ind64, indt64 = lane_to_head(SWA_HEADS * SWA_HD, SWA_HD)
    consts = [norm1_g[0][None], mla_kv_a_g[0][None], wkn, wvt, wvst, gk_t, gks_t, ind, indt, ind64, indt64]
    q_consts = [mla_q_a_g[0][None], wqup, gq_t, gqs_t]

    tabs_lat = _rope_tables(MLA_ROPE, MLA_NOPE, 1) + _rope_tables(SWA_HD, 0, LANES // SWA_HD)
    ones = jnp.ones((PROJ_TM, LANES), F32)
    zeros = jnp.zeros((PROJ_TM, LANES), F32)
    tabs_ctx = (ones, zeros, zeros, ones, zeros, zeros)

    x2d = x.reshape(N_TOK, D_MODEL)
    tiles_per_batch = SEQ // PROJ_TM
    k_mla, vt_mla, k_swa, vt_swa, q_mla, q_swa, sig_a, sig_b = _proj(
        x2d, mod3, lambda i: i // tiles_per_batch, lambda i: i % tiles_per_batch,
        tabs_lat, consts, q_consts, win, True, "proj_latent")
    kc_mla, vct_mla, kc_swa, vct_swa = _proj(
        ctx.reshape(BATCH * CTX_LEN, D_MODEL), mod3, lambda i: BATCH, lambda i: 0,
        tabs_ctx, consts, q_consts, win[:, :C_KVEND], False, "proj_ctx")

    o_a = _mla_attn(q_mla, k_mla, vt_mla, kc_mla, vct_mla)
    o_b = _swa_attn(swa_sink[0], q_swa, k_swa, vt_swa, kc_swa, vct_swa)

    x_new, h2, pos, top_w, seg = _merge(
        o_a, o_b, sig_a, sig_b, x2d, mod3,
        w_branch_a[0].astype(BF16), w_branch_b[0].astype(BF16), w_out[0].astype(BF16),
        norm2_g[0][None], w_router[0].T, b_router[0][:, None])

    seg3 = seg.reshape(N_TILES, SUBLANES, LANES)
    lstart, rows, gcarry = (seg3[:, r, :N_EXPERTS] for r in range(3))
    total = gcarry[-1] + rows[-1]
    padded = (total + MOE_BLK - 1) // MOE_BLK * MOE_BLK
    pends = jnp.cumsum(padded).astype(I32)
    pstarts = pends - padded
    nused = (pends[-1] // MOE_BLK).astype(I32)[None]
    blk_start = jnp.arange(N_BLK, dtype=I32) * MOE_BLK
    block_expert = jnp.minimum(
        jnp.sum((blk_start[:, None] >= pends[None, :]).astype(I32), axis=1), N_EXPERTS - 1)
    segtab = jnp.concatenate(
        [lstart, rows, pstarts[None, :] + gcarry,
         jnp.broadcast_to(jnp.sum(rows, axis=1, keepdims=True), rows.shape)], axis=1).reshape(-1).astype(I32)

    xs = _dispatch(pends, padded.astype(I32), segtab, pos, h2)
    bgu = b_gate_up[0]
    eo = _experts(block_expert, nused, xs, w_gate_up[0], bgu[:, None, 0::2], bgu[:, None, 1::2],
                  w_down[0], b_down[0][:, None, :])
    out = _combine(segtab, eo, x_new, mod3, pos.T, top_w.T)
    return out.reshape(BATCH, SEQ, D_MODEL)
```

```python
import functools

import jax
import jax.numpy as jnp
from jax import lax
from jax.experimental import pallas as pl
from jax.experimental.pallas import tpu as pltpu

F32 = jnp.float32
BF16 = jnp.bfloat16
I32 = jnp.int32
U32 = jnp.uint32

LANES = 128
SUBLANES = 8

D_MODEL = 1024
BATCH = 8
SEQ = 4096
CTX_LEN = 256
GRID_W = 64
ROPE_THETA = 10000.0
NORM_EPS = 1e-6
N_MOD = 6
NEG_INF = -1e30
LOG2E = 1.4426950408889634

MLA_HEADS = 8
MLA_NOPE = 64
MLA_ROPE = 32
MLA_QK = MLA_NOPE + MLA_ROPE
MLA_V = 64
MLA_Q_RANK = 384
MLA_KV_RANK = 256

SWA_HEADS = 8
SWA_KV_HEADS = 2
SWA_GROUP = SWA_HEADS // SWA_KV_HEADS
SWA_HD = 64
WINDOW = 128

N_EXPERTS = 32
TOP_K = 4
D_EXPERT = D_MODEL
SWIGLU_LIMIT = 7.0
SWIGLU_ALPHA = 1.702

N_TOK = BATCH * SEQ
MOD_ROWS = 16

ADA_TN = 1536
PROJ_TM = 512
MLA_TQ = 256
SWA_TQ = 256
SWA_KW = SWA_TQ + 2 * WINDOW
MERGE_TM = 256
MERGE_SUB = 2
MOE_BLK = 512
SEG_ALIGN = SUBLANES
N_TILES = N_TOK // MERGE_TM
LROWS = -(-(MERGE_TM * TOP_K + N_EXPERTS * (SEG_ALIGN - 1)) // LANES) * LANES
SEG_BITS = tuple(range(SEG_ALIGN.bit_length() - 1, (MERGE_TM * TOP_K).bit_length()))
SEG_COUNT = len(SEG_BITS) * N_EXPERTS
SEG_TOTAL = SEG_COUNT + len(SEG_BITS)
SEG_W = 512
CH_SRC_BITS = 11
MOVE_SLOTS = 2
N_SLOTS = -(-(N_TOK * TOP_K + N_TILES * N_EXPERTS * (SEG_ALIGN - 1) + N_EXPERTS * (MOE_BLK - 1))
            // MOE_BLK) * MOE_BLK
N_BLK = N_SLOTS // MOE_BLK

PACK_W = D_MODEL // 2

VMEM_LIMIT = 56 * 1024 * 1024

C_KVLAT = 0
C_KSWA = C_KVLAT + MLA_KV_RANK
C_KROPE = C_KSWA + SWA_KV_HEADS * SWA_HD
C_KVEND = C_KROPE + LANES
C_QLAT = C_KVEND
C_QSWA = C_QLAT + MLA_Q_RANK
C_GA = C_QSWA + SWA_HEADS * SWA_HD
C_GB = C_GA + D_MODEL
C_END = C_GB + D_MODEL


def _mm(a, b):
    return jnp.dot(a.astype(BF16), b.astype(BF16), preferred_element_type=F32)


def _mm_nt(a, b):
    return lax.dot_general(a.astype(BF16), b.astype(BF16), (((1,), (1,)), ((), ())),
                           preferred_element_type=F32)


def _mm_tn(a, b):
    return lax.dot_general(a.astype(BF16), b.astype(BF16), (((0,), (0,)), ((), ())),
                           preferred_element_type=F32)


def _split(a):
    hi = a.astype(BF16)
    lo = (a - hi.astype(F32)).astype(BF16)
    return hi, lo


def _pack_rows(a):
    half = a.shape[1] // 2
    bits = lambda v: lax.bitcast_convert_type(v.astype(BF16).astype(F32), U32)
    return bits(a[:, half:]) | (bits(a[:, :half]) >> 16)


def _unpack_rows(p):
    lo = lax.bitcast_convert_type(p << 16, F32)
    hi = lax.bitcast_convert_type(p & jnp.uint32(0xFFFF0000), F32)
    return jnp.concatenate([lo, hi], axis=1).astype(BF16)


def _rms(x):
    return x * lax.rsqrt(jnp.mean(x * x, axis=-1, keepdims=True) + NORM_EPS)


def _ada_kernel(c_ref, w_ref, b_ref, o_ref):
    c = c_ref[...]
    s = c * jax.nn.sigmoid(c)
    shi, slo = _split(s)
    whi, wlo = _split(w_ref[...])
    acc = _mm(shi, whi) + _mm(slo, whi) + _mm(shi, wlo)
    o_ref[...] = acc + b_ref[...]


def _ada(cond, w_ada, b_ada):
    n = w_ada.shape[1]
    return pl.pallas_call(
        _ada_kernel,
        grid=(n // ADA_TN,),
        in_specs=[
            pl.BlockSpec((MOD_ROWS, D_MODEL), lambda j: (0, 0)),
            pl.BlockSpec((D_MODEL, ADA_TN), lambda j: (0, j)),
            pl.BlockSpec((1, ADA_TN), lambda j: (0, j)),
        ],
        out_specs=pl.BlockSpec((MOD_ROWS, ADA_TN), lambda j: (0, j)),
        out_shape=jax.ShapeDtypeStruct((MOD_ROWS, n), F32),
        compiler_params=pltpu.CompilerParams(
            dimension_semantics=("arbitrary",), vmem_limit_bytes=VMEM_LIMIT),
        name="ada",
    )(cond, w_ada, b_ada)


def _head_norm(xraw, ind_ref, indt_ref, inv_dim):
    w = xraw.shape[1]
    ss = _mm(xraw * xraw, ind_ref[0:w, :])
    r = lax.rsqrt(ss * inv_dim + NORM_EPS)
    scale = _mm(jnp.concatenate(_split(r), axis=1), indt_ref[:, 0:w])
    return xraw * scale


def _rope(xh, cos, sa, sb, quarter):
    return (xh * cos + pltpu.roll(xh, LANES - quarter, 1) * sa
            + pltpu.roll(xh, quarter, 1) * sb)


def _proj_kernel(*refs, with_q):
    (x_ref, sh_ref, sc_ref, g1_ref, win_ref, cm_ref, sam_ref, sbm_ref, cs_ref, sas_ref, sbs_ref,
     gkva_ref, wkn_ref, wvt_ref, wvst_ref, gk_ref, gks_ref, ind_ref, indt_ref,
     ind64_ref, indt64_ref) = refs[:21]
    if with_q:
        gqa_ref, wqup_ref, gq_ref, gqs_ref = refs[21:25]
        kmla_ref, vmla_ref, kswa_ref, vswa_ref, qmla_ref, qswa_ref, siga_ref, sigb_ref = refs[25:]
    else:
        kmla_ref, vmla_ref, kswa_ref, vswa_ref = refs[21:]

    x = x_ref[...]
    h = _rms(x) * g1_ref[...] * (1.0 + sc_ref[...]) + sh_ref[...]
    y = _mm(h, win_ref[...])

    cm, sam, sbm = cm_ref[...], sam_ref[...], sbm_ref[...]
    cs, sas, sbs = cs_ref[...], sas_ref[...], sbs_ref[...]

    kvn = _rms(y[:, C_KVLAT:C_KVLAT + MLA_KV_RANK]) * gkva_ref[...]
    kn = _mm(kvn, wkn_ref[...])
    vmla_ref[...] = _mm_nt(wvt_ref[...], kvn).astype(BF16)
    vswa_ref[...] = _mm_nt(wvst_ref[...], h).astype(BF16)
    kr = y[:, C_KROPE:C_KROPE + LANES]
    kraw = kn + jnp.concatenate([kr] * MLA_HEADS, axis=1)
    kfull = _head_norm(kraw, ind_ref, indt_ref, 1.0 / MLA_QK) * gk_ref[...]
    for hd in range(MLA_HEADS):
        sl = slice(hd * LANES, (hd + 1) * LANES)
        kmla_ref[:, sl] = _rope(kfull[:, sl], cm, sam, sbm, MLA_ROPE // 4).astype(BF16)

    ks = _head_norm(y[:, C_KSWA:C_KSWA + SWA_KV_HEADS * SWA_HD], ind64_ref, indt64_ref,
                    1.0 / SWA_HD) * gks_ref[...]
    for grp in range(SWA_KV_HEADS * SWA_HD // LANES):
        sl = slice(grp * LANES, (grp + 1) * LANES)
        kswa_ref[:, sl] = _rope(ks[:, sl], cs, sas, sbs, SWA_HD // 4).astype(BF16)

    if with_q:
        qn = _rms(y[:, C_QLAT:C_QLAT + MLA_Q_RANK]) * gqa_ref[...]
        qraw = _mm(qn, wqup_ref[...])
        qf = _head_norm(qraw, ind_ref, indt_ref, 1.0 / MLA_QK) * gq_ref[...]
        for hd in range(MLA_HEADS):
            sl = slice(hd * LANES, (hd + 1) * LANES)
            qmla_ref[:, sl] = _rope(qf[:, sl], cm, sam, sbm, MLA_ROPE // 4).astype(BF16)
        qs = _head_norm(y[:, C_QSWA:C_QSWA + SWA_HEADS * SWA_HD], ind64_ref, indt64_ref,
                        1.0 / SWA_HD) * gqs_ref[...]
        for grp in range(SWA_HEADS * SWA_HD // LANES):
            sl = slice(grp * LANES, (grp + 1) * LANES)
            qswa_ref[:, sl] = _rope(qs[:, sl], cs, sas, sbs, SWA_HD // 4).astype(BF16)
        siga_ref[...] = jax.nn.sigmoid(y[:, C_GA:C_GA + D_MODEL]).astype(BF16)
        sigb_ref[...] = jax.nn.sigmoid(y[:, C_GB:C_GB + D_MODEL]).astype(BF16)


def _proj(x2d, mod3, mod_row_fn, tab_row_fn, tabs, consts, q_consts, win, with_q, name):
    rows = x2d.shape[0]
    tm = PROJ_TM
    const = lambda shape: pl.BlockSpec(shape, lambda i: (0,) * len(shape))
    tab = pl.BlockSpec((tm, LANES), lambda i: (tab_row_fn(i), 0))
    g1, rest = consts[0], consts[1:]
    in_specs = [
        pl.BlockSpec((tm, D_MODEL), lambda i: (i, 0)),
        pl.BlockSpec((None, 1, D_MODEL), lambda i: (mod_row_fn(i), 0, 0)),
        pl.BlockSpec((None, 1, D_MODEL), lambda i: (mod_row_fn(i), 0, 1)),
        const(g1.shape),
        pl.BlockSpec(win.shape, lambda i: (0, 0), pipeline_mode=pl.Buffered(1)),
        tab, tab, tab, tab, tab, tab,
    ] + [const(c.shape) for c in rest]
    args = [x2d, mod3, mod3, g1, win] + list(tabs) + list(rest)
    outs = [(MLA_HEADS * LANES, False), (MLA_HEADS * MLA_V, True),
            (SWA_KV_HEADS * SWA_HD, False), (SWA_KV_HEADS * SWA_HD, True)]
    if with_q:
        in_specs += [const(c.shape) for c in q_consts]
        args += list(q_consts)
        outs += [(MLA_HEADS * LANES, False), (SWA_HEADS * SWA_HD, False), (D_MODEL, False), (D_MODEL, False)]
    return pl.pallas_call(
        functools.partial(_proj_kernel, with_q=with_q),
        grid=(rows // tm,),
        in_specs=in_specs,
        out_specs=[pl.BlockSpec((w, tm), lambda i: (0, i)) if t else pl.BlockSpec((tm, w), lambda i: (i, 0))
                   for w, t in outs],
        out_shape=[jax.ShapeDtypeStruct((w, rows) if t else (rows, w), BF16) for w, t in outs],
        compiler_params=pltpu.CompilerParams(
            dimension_semantics=("arbitrary",), vmem_limit_bytes=VMEM_LIMIT),
        name=name,
    )(*args)


def _mla_step(q_ref, kl_ref, vlt_ref, kc_ref, vct_ref, o_ref, cur, prev):
    s_lat_c, s_ctx_c, m_c = cur
    s_lat_p, s_ctx_p, m_p = prev
    outs = []
    for hh in range(2):
        sl = slice(hh * LANES, (hh + 1) * LANES)
        vrows = slice(hh * MLA_V, (hh + 1) * MLA_V)
        q = q_ref[:, sl]
        s1 = _mm_nt(kl_ref[:, sl], q)
        s2 = _mm_nt(kc_ref[:, sl], q)
        s_lat_c[hh] = s1
        s_ctx_c[hh] = s2
        m_c[hh] = jnp.maximum(jnp.max(s1, axis=0, keepdims=True),
                              jnp.max(s2, axis=0, keepdims=True))

        m = m_p[hh]
        p1 = jnp.exp2(s_lat_p[hh] - m)
        p2 = jnp.exp2(s_ctx_p[hh] - m)
        l = jnp.sum(p1, axis=0, keepdims=True) + jnp.sum(p2, axis=0, keepdims=True)
        o = _mm(vlt_ref[vrows, :], p1) + _mm(vct_ref[vrows, :], p2)
        outs.append(o / l)
    o_ref[...] = jnp.concatenate(outs, axis=0).astype(BF16)


def _mla_kernel(q_ref, kl_ref, vl_ref, kc_ref, vc_ref, o_ref, *scratch):
    set0, set1 = scratch[:3], scratch[3:]
    i = pl.program_id(0)

    @pl.when(i == 0)
    def _():
        for ref in set1:
            ref[...] = jnp.zeros(ref.shape, F32)

    @pl.when(lax.rem(i, 2) == 0)
    def _():
        _mla_step(q_ref, kl_ref, vl_ref, kc_ref, vc_ref, o_ref, set0, set1)

    @pl.when(lax.rem(i, 2) == 1)
    def _():
        _mla_step(q_ref, kl_ref, vl_ref, kc_ref, vc_ref, o_ref, set1, set0)


def _mla_attn(q, k_lat, vt_lat, k_ctx, vt_ctx):
    tq = MLA_TQ
    nq = SEQ // tq
    pairs = MLA_HEADS // 2
    n_items = BATCH * pairs * nq

    def item(s):
        return s // (pairs * nq), (s // nq) % pairs, s % nq

    def scored(s):
        return item(jnp.minimum(s, n_items - 1))

    def finished(s):
        return item(jnp.maximum(s - 1, 0))

    def q_map(s):
        b, hp, i = scored(s)
        return b * nq + i, hp

    def k_map(s):
        b, hp, _ = scored(s)
        return b, hp

    def v_map(s):
        b, hp, _ = finished(s)
        return hp, b

    def o_map(s):
        b, hp, i = finished(s)
        return hp, b * nq + i

    return pl.pallas_call(
        _mla_kernel,
        grid=(n_items + 1,),
        in_specs=[
            pl.BlockSpec((tq, 2 * LANES), q_map),
            pl.BlockSpec((SEQ, 2 * LANES), k_map),
            pl.BlockSpec((2 * MLA_V, SEQ), v_map),
            pl.BlockSpec((CTX_LEN, 2 * LANES), k_map),
            pl.BlockSpec((2 * MLA_V, CTX_LEN), v_map),
        ],
        out_specs=pl.BlockSpec((2 * MLA_V, tq), o_map),
        out_shape=jax.ShapeDtypeStruct((MLA_HEADS * MLA_V, N_TOK), BF16),
        scratch_shapes=2 * [pltpu.VMEM((2, SEQ, tq), F32),
                            pltpu.VMEM((2, CTX_LEN, tq), F32),
                            pltpu.VMEM((2, 1, tq), F32)],
        compiler_params=pltpu.CompilerParams(
            dimension_semantics=("arbitrary",),
            vmem_limit_bytes=VMEM_LIMIT),
        name="mla_attn",
    )(q, k_lat, vt_lat, k_ctx, vt_ctx)


def _swa_window_start(i):
    return jnp.clip(i * SWA_TQ - WINDOW, 0, SEQ - SWA_KW)


def _swa_kernel(sink_ref, q_ref, k_ref, vt_ref, kc_ref, vct_ref, bias_ref, o_ref):
    tq = SWA_TQ
    i = pl.program_id(1)
    kstart = pl.multiple_of(_swa_window_start(i), LANES)
    kwin = k_ref[pl.ds(kstart, SWA_KW), :].astype(F32)
    vtwin = vt_ref[:, pl.ds(kstart, SWA_KW)]
    kc = kc_ref[...].astype(F32)
    vct = vct_ref[...]
    bias = bias_ref[...]

    def half_keys(k, kh):
        lane = lax.broadcasted_iota(I32, k.shape, 1)
        own = jnp.where((lane >= kh * SWA_HD) & (lane < (kh + 1) * SWA_HD), k, 0.0)
        other = pltpu.roll(own, SWA_HD, 1)
        return (own, other) if kh == 0 else (other, own)

    order = (0, 2, 1, 3)
    for kh in range(SWA_KV_HEADS):
        vrows = slice(kh * SWA_HD, (kh + 1) * SWA_HD)
        grp0 = kh * SWA_GROUP // 2
        q2 = jnp.concatenate([q_ref[:, (grp0 + c) * LANES:(grp0 + c + 1) * LANES] for c in range(2)],
                             axis=0)
        k_lo, k_hi = half_keys(kwin, kh)
        c_lo, c_hi = half_keys(kc, kh)
        sb = jnp.concatenate([_mm_nt(k_lo, q2), _mm_nt(k_hi, q2)], axis=1) + bias
        sc = jnp.concatenate([_mm_nt(c_lo, q2), _mm_nt(c_hi, q2)], axis=1)
        sink = jnp.concatenate(
            [jnp.full((1, tq), sink_ref[kh * SWA_GROUP + g] * LOG2E, F32) for g in order], axis=1)
        m = jnp.maximum(jnp.maximum(jnp.max(sb, axis=0, keepdims=True),
                                    jnp.max(sc, axis=0, keepdims=True)), sink)
        pb = jnp.exp2(sb - m)
        pc = jnp.exp2(sc - m)
        l = (jnp.sum(pb, axis=0, keepdims=True) + jnp.sum(pc, axis=0, keepdims=True)
             + jnp.exp2(sink - m))
        o = (_mm(vtwin[vrows, :], pb) + _mm(vct[vrows, :], pc)) / l
        for blk, g in enumerate(order):
            hd = kh * SWA_GROUP + g
            o_ref[hd * SWA_HD:(hd + 1) * SWA_HD, :] = o[:, blk * tq:(blk + 1) * tq].astype(BF16)


def _swa_attn(sink, q, k_lat, vt_lat, k_ctx, vt_ctx):
    tq = SWA_TQ
    nq = SEQ // tq
    offsets = (0, WINDOW, SWA_KW - tq)
    assert all(i * tq - min(max(i * tq - WINDOW, 0), SEQ - SWA_KW) == offsets[(i > 0) + (i == nq - 1)]
               for i in range(nq))
    key = jnp.arange(SWA_KW, dtype=I32)[:, None]
    qry = (jnp.arange(SWA_GROUP * tq, dtype=I32) % tq)[None, :]
    bias = jnp.stack([jnp.where(jnp.abs(d + qry - key) <= WINDOW, 0.0, NEG_INF) for d in offsets]).astype(F32)
    case = lambda i: (i > 0).astype(I32) + (i == nq - 1).astype(I32)
    return pl.pallas_call(
        _swa_kernel,
        grid=(BATCH, nq),
        in_specs=[
            pl.BlockSpec(memory_space=pltpu.SMEM),
            pl.BlockSpec((tq, SWA_HEADS * SWA_HD), lambda b, i: (b * nq + i, 0)),
            pl.BlockSpec((SEQ, SWA_KV_HEADS * SWA_HD), lambda b, i: (b, 0)),
            pl.BlockSpec((SWA_KV_HEADS * SWA_HD, SEQ), lambda b, i: (0, b)),
            pl.BlockSpec((CTX_LEN, SWA_KV_HEADS * SWA_HD), lambda b, i: (b, 0)),
            pl.BlockSpec((SWA_KV_HEADS * SWA_HD, CTX_LEN), lambda b, i: (0, b)),
            pl.BlockSpec((None, SWA_KW, SWA_GROUP * tq), lambda b, i: (case(i), 0, 0)),
        ],
        out_specs=pl.BlockSpec((SWA_HEADS * SWA_HD, tq), lambda b, i: (0, b * nq + i)),
        out_shape=jax.ShapeDtypeStruct((SWA_HEADS * SWA_HD, N_TOK), BF16),
        compiler_params=pltpu.CompilerParams(
            dimension_semantics=("arbitrary", "arbitrary"), vmem_limit_bytes=VMEM_LIMIT),
        name="swa_attn",
    )(sink, q, k_lat, vt_lat, k_ctx, vt_ctx, bias)


def _merge_kernel(oa_ref, ob_ref, sa_ref, sb_ref, x_ref, g1_ref, sh2_ref, sc2_ref, wba_ref, wbb_ref,
                  wout_ref, g2_ref, wr_ref, br_ref, utri_ref, ltri_ref,
                  xnew_ref, h2_ref, pos_ref, wts_ref, seg_ref, carry_ref):
    i = pl.program_id(0)

    @pl.when(i == 0)
    def _():
        carry_ref[...] = jnp.zeros_like(carry_ref)

    for sub in range(MERGE_SUB):
        _merge_tile(sub, oa_ref, ob_ref, sa_ref, sb_ref, x_ref, g1_ref, sh2_ref, sc2_ref, wba_ref,
                    wbb_ref, wout_ref, g2_ref, wr_ref, br_ref, utri_ref, ltri_ref,
                    xnew_ref, h2_ref, pos_ref, wts_ref, seg_ref, carry_ref)


def _merge_tile(sub, oa_ref, ob_ref, sa_ref, sb_ref, x_ref, g1_ref, sh2_ref, sc2_ref, wba_ref, wbb_ref,
                wout_ref, g2_ref, wr_ref, br_ref, utri_ref, ltri_ref,
                xnew_ref, h2_ref, pos_ref, wts_ref, seg_ref, carry_ref):
    tok = slice(sub * MERGE_TM, (sub + 1) * MERGE_TM)
    segrows = slice(sub * SUBLANES, (sub + 1) * SUBLANES)
    ya = _mm_tn(oa_ref[:, tok], wba_ref[...])
    yb = _mm_tn(ob_ref[:, tok], wbb_ref[...])
    y = sa_ref[tok, :].astype(F32) * ya + sb_ref[tok, :].astype(F32) * yb
    z = _mm(y, wout_ref[...])
    xn = x_ref[tok, :] + g1_ref[...] * z
    xnew_ref[tok, :] = xn
    h2 = _rms(xn) * g2_ref[...] * (1.0 + sc2_ref[...]) + sh2_ref[...]
    h2_ref[tok, :] = h2.astype(BF16)

    hhi, hlo = _split(h2)
    whi, wlo = _split(wr_ref[...])
    lg = _mm_nt(whi, hhi) + _mm_nt(whi, hlo) + _mm_nt(wlo, hhi) + br_ref[...]
    eiota = lax.broadcasted_iota(I32, lg.shape, 0).astype(F32)
    vals, onehots = [], []
    for k in range(TOP_K):
        m = jnp.max(lg, axis=0, keepdims=True)
        ik = jnp.min(jnp.where(lg == m, eiota, float(N_EXPERTS)), axis=0, keepdims=True)
        hit = eiota == ik
        vals.append(m)
        onehots.append(hit.astype(F32))
        lg = jnp.where(hit, -jnp.inf, lg)
    es = [jnp.exp(v - vals[0]) for v in vals]
    tot = es[0] + es[1] + es[2] + es[3]
    for k in range(TOP_K):
        wts_ref[k:k + 1, tok] = es[k] / tot

    tots = [jnp.sum(oh, axis=1, keepdims=True) for oh in onehots]
    n = tots[0] + tots[1] + tots[2] + tots[3]
    m_al = jnp.floor((n + (SEG_ALIGN - 1)) * (1.0 / SEG_ALIGN))
    m_b = jnp.broadcast_to(m_al, (N_EXPERTS, LANES))
    lstart = _mm(ltri_ref[...], m_b) * float(SEG_ALIGN)
    off = jnp.zeros_like(n)
    for k in range(TOP_K):
        prefix = _mm(onehots[k], utri_ref[...])
        lp = jnp.sum(onehots[k] * (lstart[:, 0:1] + off + prefix), axis=0, keepdims=True)
        pos_ref[k:k + 1, tok] = lp.astype(I32)
        off = off + tots[k]
    eye = (lax.broadcasted_iota(I32, (N_EXPERTS, LANES), 0)
           == lax.broadcasted_iota(I32, (N_EXPERTS, LANES), 1)).astype(F32)
    to_row = lambda v: jnp.sum(v * eye, axis=0, keepdims=True).astype(I32)
    carry = carry_ref[...]
    seg_ref[segrows, :] = jnp.concatenate(
        [to_row(lstart), to_row(m_b * float(SEG_ALIGN)), to_row(carry),
         jnp.zeros((SUBLANES - 3, LANES), I32)], axis=0)
    carry_ref[...] = carry + m_b * float(SEG_ALIGN)


def _merge(o_a, o_b, sig_a, sig_b, x2d, mod3, w_ba, w_bb, w_out, g2, w_rt, b_r):
    tm = MERGE_TM * MERGE_SUB
    tiles_per_batch = SEQ // tm
    const = lambda shape: pl.BlockSpec(shape, lambda i: (0,) * len(shape))
    modspec = lambda j: pl.BlockSpec((None, 1, D_MODEL), lambda i: (i // tiles_per_batch, 0, j))
    row = lambda w: pl.BlockSpec((tm, w), lambda i: (i, 0))
    utri = (jnp.arange(MERGE_TM)[:, None] < jnp.arange(MERGE_TM)[None, :]).astype(BF16)
    ltri = (jnp.arange(N_EXPERTS)[None, :] < jnp.arange(N_EXPERTS)[:, None]).astype(BF16)
    return pl.pallas_call(
        _merge_kernel,
        grid=(N_TILES // MERGE_SUB,),
        in_specs=[pl.BlockSpec((512, tm), lambda i: (0, i)), pl.BlockSpec((512, tm), lambda i: (0, i)),
                  row(D_MODEL), row(D_MODEL), row(D_MODEL),
                  modspec(2), modspec(3), modspec(4),
                  const(w_ba.shape), const(w_bb.shape), const(w_out.shape), const(g2.shape),
                  const(w_rt.shape), const(b_r.shape), const(utri.shape), const(ltri.shape)],
        out_specs=[row(D_MODEL), row(D_MODEL),
                   pl.BlockSpec((TOP_K, tm), lambda i: (0, i)),
                   pl.BlockSpec((TOP_K, tm), lambda i: (0, i)),
                   pl.BlockSpec((MERGE_SUB * SUBLANES, LANES), lambda i: (i, 0))],
        out_shape=[jax.ShapeDtypeStruct((N_TOK, D_MODEL), F32),
                   jax.ShapeDtypeStruct((N_TOK, D_MODEL), BF16),
                   jax.ShapeDtypeStruct((TOP_K, N_TOK), I32),
                   jax.ShapeDtypeStruct((TOP_K, N_TOK), F32),
                   jax.ShapeDtypeStruct((N_TILES * SUBLANES, LANES), I32)],
        scratch_shapes=[pltpu.VMEM((N_EXPERTS, LANES), F32)],
        compiler_params=pltpu.CompilerParams(
            dimension_semantics=("arbitrary",), vmem_limit_bytes=VMEM_LIMIT),
        name="merge",
    )(o_a, o_b, sig_a, sig_b, x2d, mod3, mod3, mod3, w_ba, w_bb, w_out, g2, w_rt, b_r, utri, ltri)


def _chunk_copy(hbm, hbm_row, buf, buf_row, size, to_hbm, sem):
    vm = buf.at[pl.ds(pl.multiple_of(buf_row, SEG_ALIGN), size)]
    hb = hbm.at[pl.ds(pl.multiple_of(hbm_row, SEG_ALIGN), size)]
    return pltpu.make_async_copy(vm, hb, sem) if to_hbm else pltpu.make_async_copy(hb, vm, sem)


def _seg_start(seg_smem, tile, hbm, buf, to_hbm, sem):
    base = tile * SEG_W
    for c, b in enumerate(SEG_BITS):
        def issue(t, carry, c=c, size=1 << b):
            word = seg_smem[base + c * N_EXPERTS + t]
            _chunk_copy(hbm, word >> CH_SRC_BITS, buf, word & ((1 << CH_SRC_BITS) - 1),
                        size, to_hbm, sem).start()
            return carry

        lax.fori_loop(0, seg_smem[base + SEG_COUNT + c], issue, 0)


def _seg_wait(seg_smem, tile, hbm, buf, to_hbm, sem):
    assert LROWS < 2 << SEG_BITS[-1]
    total = seg_smem[tile * SEG_W + SEG_TOTAL]
    for b in SEG_BITS:
        size = 1 << b

        @pl.when((total & size) != 0)
        def _():
            _chunk_copy(hbm, 0, buf, 0, size, to_hbm, sem).wait()


def _dispatch_kernel(pend_ref, padded_ref, seg_ref, pos_ref, h2_ref, xs_ref, staged, sems):
    j = pl.program_id(0)
    slot = lax.rem(j, MOVE_SLOTS)

    @pl.when(j == 0)
    def _():
        staged[0, 0:MOE_BLK, :] = jnp.zeros((MOE_BLK, PACK_W), U32)

        def tail(action):
            def body(e, carry):
                @pl.when(padded_ref[e] > 0)
                def _():
                    start = pl.multiple_of(pend_ref[e] - MOE_BLK, MOE_BLK)
                    cp = pltpu.make_async_copy(staged.at[0, pl.ds(0, MOE_BLK)],
                                               xs_ref.at[pl.ds(start, MOE_BLK)], sems.at[0])
                    getattr(cp, action)()
                return carry
            lax.fori_loop(0, N_EXPERTS, body, 0)

        def unused(action):
            def body(b, carry):
                cp = pltpu.make_async_copy(
                    staged.at[0, pl.ds(0, MOE_BLK)],
                    xs_ref.at[pl.ds(pl.multiple_of(b * MOE_BLK, MOE_BLK), MOE_BLK)], sems.at[0])
                getattr(cp, action)()
                return carry
            lax.fori_loop(pend_ref[N_EXPERTS - 1] // MOE_BLK, N_BLK, body, 0)

        tail("start")
        unused("start")
        tail("wait")
        unused("wait")

    pos = pos_ref[...]
    piota = lax.broadcasted_iota(I32, (LROWS, MERGE_TM), 0)
    hit = piota == pos[0:1, :]
    for k in range(1, TOP_K):
        hit = hit | (piota == pos[k:k + 1, :])
    staged[slot] = _pack_rows(_mm(jnp.where(hit, 1.0, 0.0), h2_ref[...]))

    _seg_start(seg_ref, j, xs_ref, staged.at[slot], True, sems.at[1 + slot])

    lag = MOVE_SLOTS - 1

    def wait_tile(t):
        s = lax.rem(t, MOVE_SLOTS)
        _seg_wait(seg_ref, t, xs_ref, staged.at[s], True, sems.at[1 + s])

    @pl.when(j >= lag)
    def _():
        wait_tile(j - lag)

    @pl.when(j == N_TILES - 1)
    def _():
        for back in reversed(range(lag)):
            wait_tile(j - back)


def _dispatch(pends, padded, segtab, pos, h2):
    grid_spec = pltpu.PrefetchScalarGridSpec(
        num_scalar_prefetch=3,
        grid=(N_TILES,),
        in_specs=[
            pl.BlockSpec((TOP_K, MERGE_TM), lambda j, pe, pa, sg: (0, j)),
            pl.BlockSpec((MERGE_TM, D_MODEL), lambda j, pe, pa, sg: (j, 0)),
        ],
        out_specs=pl.BlockSpec(memory_space=pl.ANY),
        scratch_shapes=[pltpu.VMEM((MOVE_SLOTS, LROWS, PACK_W), U32),
                        pltpu.SemaphoreType.DMA((1 + MOVE_SLOTS,))],
    )
    return pl.pallas_call(
        _dispatch_kernel,
        grid_spec=grid_spec,
        out_shape=jax.ShapeDtypeStruct((N_SLOTS, PACK_W), U32),
        compiler_params=pltpu.CompilerParams(
            dimension_semantics=("arbitrary",), vmem_limit_bytes=VMEM_LIMIT),
        name="dispatch",
    )(pends, padded, segtab, pos, h2)


SPLIT_SUB = 256


def _expert_kernel(be_ref, nused_ref, xs_ref, wgu_ref, bg_ref, bl_ref, wd_ref, bd_ref, perm_ref, o_ref,
                   wg_s, wl_s, wd_s):
    i = pl.program_id(0)
    used = i < nused_ref[0]
    fresh = jnp.logical_or(i == 0, be_ref[i] != be_ref[jnp.maximum(i - 1, 0)])

    @pl.when(jnp.logical_and(used, fresh))
    def _():
        half = SPLIT_SUB // 2
        for s in range(2 * D_EXPERT // SPLIT_SUB):
            r = _mm(wgu_ref[:, s * SPLIT_SUB:(s + 1) * SPLIT_SUB], perm_ref[...])
            wg_s[:, s * half:(s + 1) * half] = r[:, :half].astype(BF16)
            wl_s[:, s * half:(s + 1) * half] = r[:, half:].astype(BF16)
        wd_s[...] = wd_ref[...].astype(BF16)

    @pl.when(used)
    def _():
        xb = _unpack_rows(xs_ref[...])
        g = jnp.minimum(_mm(xb, wg_s[...]) + bg_ref[...], SWIGLU_LIMIT)
        l = jnp.clip(_mm(xb, wl_s[...]) + bl_ref[...], -SWIGLU_LIMIT, SWIGLU_LIMIT)
        act = g * jax.nn.sigmoid(SWIGLU_ALPHA * g) * (l + 1.0)
        o_ref[...] = _pack_rows(_mm(act, wd_s[...]) + bd_ref[...])

    @pl.when(jnp.logical_not(used))
    def _():
        o_ref[...] = jnp.zeros_like(o_ref)


def _experts(block_expert, nused, xs, w_gu, bg, bl, w_dn, bd):
    col = jnp.arange(SPLIT_SUB)
    src = jnp.where(col < SPLIT_SUB // 2, 2 * col, 2 * (col - SPLIT_SUB // 2) + 1)
    perm = (jnp.arange(SPLIT_SUB)[:, None] == src[None, :]).astype(BF16)
    blk = lambda i, be, nu: (jnp.minimum(i, nu[0] - 1), 0)
    per_expert = lambda shape: pl.BlockSpec((None,) + shape, lambda i, be, nu: (be[i], 0, 0))
    grid_spec = pltpu.PrefetchScalarGridSpec(
        num_scalar_prefetch=2,
        grid=(N_BLK,),
        in_specs=[pl.BlockSpec((MOE_BLK, PACK_W), blk),
                  per_expert((D_MODEL, 2 * D_EXPERT)), per_expert((1, D_EXPERT)), per_expert((1, D_EXPERT)),
                  per_expert((D_EXPERT, D_MODEL)), per_expert((1, D_MODEL)),
                  pl.BlockSpec((SPLIT_SUB, SPLIT_SUB), lambda i, be, nu: (0, 0))],
        out_specs=pl.BlockSpec((MOE_BLK, PACK_W), lambda i, be, nu: (i, 0)),
        scratch_shapes=[pltpu.VMEM((D_MODEL, D_EXPERT), BF16), pltpu.VMEM((D_MODEL, D_EXPERT), BF16),
                        pltpu.VMEM((D_EXPERT, D_MODEL), BF16)],
    )
    return pl.pallas_call(
        _expert_kernel,
        grid_spec=grid_spec,
        out_shape=jax.ShapeDtypeStruct((N_SLOTS, PACK_W), U32),
        compiler_params=pltpu.CompilerParams(
            dimension_semantics=("arbitrary",), vmem_limit_bytes=VMEM_LIMIT),
        name="experts",
    )(block_expert, nused, xs, w_gu, bg, bl, w_dn, bd, perm)


def _combine_kernel(seg_ref, eo_ref, xnew_ref, g2_ref, pos_ref, wt_ref, o_ref, gbuf, sems):
    j = pl.program_id(0)
    slot = lax.rem(j, MOVE_SLOTS)
    ahead = MOVE_SLOTS - 1

    def fetch(tile):
        sl = lax.rem(tile, MOVE_SLOTS)
        _seg_start(seg_ref, tile, eo_ref, gbuf.at[sl], False, sems.at[sl])

    @pl.when(j == 0)
    def _():
        gbuf[...] = jnp.zeros_like(gbuf)
        for t in range(ahead):
            fetch(jnp.int32(t))

    @pl.when(j + ahead < N_TILES)
    def _():
        fetch(j + ahead)

    _seg_wait(seg_ref, j, eo_ref, gbuf.at[slot], False, sems.at[slot])

    pos = pos_ref[...]
    wt = wt_ref[...]
    lane = lax.broadcasted_iota(I32, (MERGE_TM, LROWS), 1)
    wsel = jnp.zeros((MERGE_TM, LROWS), F32)
    for k in range(TOP_K):
        wsel = jnp.where(lane == pos[:, k:k + 1], wt[:, k:k + 1], wsel)
    acc = _mm(wsel, _unpack_rows(gbuf[slot]))
    o_ref[...] = xnew_ref[...] + g2_ref[...] * acc


def _combine(segtab, eo, x_new, mod3, pos_t, wts_t):
    tiles_per_batch = SEQ // MERGE_TM
    grid_spec = pltpu.PrefetchScalarGridSpec(
        num_scalar_prefetch=1,
        grid=(N_TILES,),
        in_specs=[
            pl.BlockSpec(memory_space=pl.ANY),
            pl.BlockSpec((MERGE_TM, D_MODEL), lambda j, sg: (j, 0)),
            pl.BlockSpec((None, 1, D_MODEL), lambda j, sg: (j // tiles_per_batch, 0, 5)),
            pl.BlockSpec((MERGE_TM, TOP_K), lambda j, sg: (j, 0)),
            pl.BlockSpec((MERGE_TM, TOP_K), lambda j, sg: (j, 0)),
        ],
        out_specs=pl.BlockSpec((MERGE_TM, D_MODEL), lambda j, sg: (j, 0)),
        scratch_shapes=[pltpu.VMEM((MOVE_SLOTS, LROWS, PACK_W), U32),
                        pltpu.SemaphoreType.DMA((MOVE_SLOTS,))],
    )
    return pl.pallas_call(
        _combine_kernel,
        grid_spec=grid_spec,
        out_shape=jax.ShapeDtypeStruct((N_TOK, D_MODEL), F32),
        compiler_params=pltpu.CompilerParams(
            dimension_semantics=("arbitrary",), vmem_limit_bytes=VMEM_LIMIT),
        name="combine",
    )(segtab, eo, x_new, mod3, pos_t, wts_t)


def _pad_heads(w, heads, dim):
    lead = w.shape[:-1]
    w = w.reshape(lead + (heads, dim))
    w = jnp.pad(w, [(0, 0)] * len(lead) + [(0, 0), (0, LANES - dim)])
    return w.reshape(lead + (heads * LANES,))


def _rope_tables(dim, offset, repeat):
    pos = jnp.arange(SEQ, dtype=jnp.int32)
    row = (pos // GRID_W).astype(F32)
    col = (pos % GRID_W).astype(F32)
    q = dim // 4
    freqs = ROPE_THETA ** (-jnp.arange(q, dtype=F32) / q)
    ang_r = row[:, None] * freqs
    ang_c = col[:, None] * freqs
    ang = jnp.concatenate([ang_r, ang_r, ang_c, ang_c], axis=-1)
    cos, sin = jnp.cos(ang), jnp.sin(ang)
    first = (jnp.arange(dim) % (2 * q)) < q
    sa = jnp.where(first, -sin, 0.0)
    sb = jnp.where(first, 0.0, sin)
    pad = lambda t, v: jnp.pad(jnp.tile(t, (1, repeat)), ((0, 0), (offset, LANES - offset - dim * repeat)),
                               constant_values=v)
    return pad(cos, 1.0), pad(sa, 0.0), pad(sb, 0.0)


def kernel(x, c, ctx, c_ctx, w_ada, b_ada, norm1_g, norm2_g, w_in, mla_q_a_g, mla_kv_a_g, w_q_up, w_kv_up,
           mla_q_g, mla_k_g, swa_q_g, swa_k_g, swa_sink, w_branch_a, w_branch_b, w_out, w_router, b_router,
           w_gate_up, b_gate_up, w_down, b_down):
    assert x.shape == (BATCH, SEQ, D_MODEL) and ctx.shape == (BATCH, CTX_LEN, D_MODEL)
    assert w_ada.shape[0] == 1, "single layer"

    cond = jnp.concatenate([c, c_ctx[None], jnp.zeros((MOD_ROWS - BATCH - 1, D_MODEL), F32)], axis=0)
    mod = _ada(cond, w_ada[0], b_ada[0][None])
    mod3 = mod.reshape(MOD_ROWS, 1, N_MOD * D_MODEL)

    wi = w_in[0]
    o1, o2, o3 = MLA_KV_RANK, MLA_KV_RANK + MLA_ROPE, MLA_KV_RANK + MLA_ROPE + SWA_KV_HEADS * SWA_HD
    kv_cols = o3 + SWA_KV_HEADS * SWA_HD
    q1 = kv_cols + MLA_Q_RANK
    q2 = q1 + SWA_HEADS * SWA_HD
    q3 = q2 + D_MODEL
    krope_grp = jnp.pad(wi[:, o1:o2], ((0, 0), (MLA_NOPE, LANES - MLA_QK)))
    win = jnp.concatenate([
        wi[:, 0:o1], wi[:, o2:o3], krope_grp, wi[:, kv_cols:q1], wi[:, q1:q2], wi[:, q2:q3], wi[:, q3:]],
        axis=1).astype(BF16)
    assert win.shape == (D_MODEL, C_END)
    wvst = wi[:, o3:kv_cols].T.astype(BF16)
    wkv = w_kv_up[0].reshape(MLA_KV_RANK, MLA_HEADS, MLA_NOPE + MLA_V)
    wkn = _pad_heads(wkv[:, :, :MLA_NOPE].reshape(MLA_KV_RANK, -1), MLA_HEADS, MLA_NOPE).astype(BF16)
    wvt = wkv[:, :, MLA_NOPE:].reshape(MLA_KV_RANK, MLA_HEADS * MLA_V).T.astype(BF16)
    wqup = _pad_heads(w_q_up[0], MLA_HEADS, MLA_QK).astype(BF16)
    gk_t = _pad_heads(jnp.tile(mla_k_g[0], MLA_HEADS)[None], MLA_HEADS, MLA_QK)
    gq_t = _pad_heads(jnp.tile(mla_q_g[0] * (MLA_QK ** -0.5 * LOG2E), MLA_HEADS)[None], MLA_HEADS, MLA_QK)
    gks_t = jnp.tile(swa_k_g[0], SWA_KV_HEADS)[None]
    gqs_t = jnp.tile(swa_q_g[0] * (SWA_HD ** -0.5 * LOG2E), SWA_HEADS)[None]

    def lane_to_head(n_lanes, width):
        ind = ((jnp.arange(n_lanes) // width)[:, None] == jnp.arange(LANES)[None, :]).astype(BF16)
        return ind, jnp.concatenate([ind.T, ind.T], axis=0)

    ind, indt = lane_to_head(MLA_HEADS * LANES, LANES)
    ind64, indt64 = lane_to_head(SWA_HEADS * SWA_HD, SWA_HD)
    consts = [norm1_g[0][None], mla_kv_a_g[0][None], wkn, wvt, wvst, gk_t, gks_t, ind, indt, ind64, indt64]
    q_consts = [mla_q_a_g[0][None], wqup, gq_t, gqs_t]

    tabs_lat = _rope_tables(MLA_ROPE, MLA_NOPE, 1) + _rope_tables(SWA_HD, 0, LANES // SWA_HD)
    ones = jnp.ones((PROJ_TM, LANES), F32)
    zeros = jnp.zeros((PROJ_TM, LANES), F32)
    tabs_ctx = (ones, zeros, zeros, ones, zeros, zeros)

    x2d = x.reshape(N_TOK, D_MODEL)
    tiles_per_batch = SEQ // PROJ_TM
    k_mla, vt_mla, k_swa, vt_swa, q_mla, q_swa, sig_a, sig_b = _proj(
        x2d, mod3, lambda i: i // tiles_per_batch, lambda i: i % tiles_per_batch,
        tabs_lat, consts, q_consts, win, True, "proj_latent")
    kc_mla, vct_mla, kc_swa, vct_swa = _proj(
        ctx.reshape(BATCH * CTX_LEN, D_MODEL), mod3, lambda i: BATCH, lambda i: 0,
        tabs_ctx, consts, q_consts, win[:, :C_KVEND], False, "proj_ctx")

    o_a = _mla_attn(q_mla, k_mla, vt_mla, kc_mla, vct_mla)
    o_b = _swa_attn(swa_sink[0], q_swa, k_swa, vt_swa, kc_swa, vct_swa)

    x_new, h2, pos, top_w, seg = _merge(
        o_a, o_b, sig_a, sig_b, x2d, mod3,
        w_branch_a[0].astype(BF16), w_branch_b[0].astype(BF16), w_out[0].astype(BF16),
        norm2_g[0][None], w_router[0].T, b_router[0][:, None])

    seg3 = seg.reshape(N_TILES, SUBLANES, LANES)
    lstart, rows, gcarry = (seg3[:, r, :N_EXPERTS] for r in range(3))
    total = gcarry[-1] + rows[-1]
    padded = (total + MOE_BLK - 1) // MOE_BLK * MOE_BLK
    pends = jnp.cumsum(padded).astype(I32)
    pstarts = pends - padded
    nused = (pends[-1] // MOE_BLK).astype(I32)[None]
    blk_start = jnp.arange(N_BLK, dtype=I32) * MOE_BLK
    block_expert = jnp.minimum(
        jnp.sum((blk_start[:, None] >= pends[None, :]).astype(I32), axis=1), N_EXPERTS - 1)
    assert LROWS <= 1 << CH_SRC_BITS and N_SLOTS < 1 << (31 - CH_SRC_BITS) and SEG_TOTAL < SEG_W
    units = (rows // SEG_ALIGN)[:, None, :]
    cls = jnp.arange(len(SEG_BITS), dtype=I32)[None, :, None]
    has = (units >> cls) & 1
    done = ((units >> (cls + 1)) << (cls + 1)) * SEG_ALIGN
    word = ((pstarts[None, None, :] + gcarry[:, None, :] + done) << CH_SRC_BITS) + lstart[:, None, :] + done
    place = jnp.cumsum(has, axis=2) - has
    lists = jnp.sum(jnp.where((place[..., None] == jnp.arange(N_EXPERTS, dtype=I32)) & (has[..., None] == 1),
                              word[..., None], 0), axis=2)
    segtab = jnp.concatenate(
        [lists.reshape(N_TILES, SEG_COUNT), jnp.sum(has, axis=2), jnp.sum(rows, axis=1, keepdims=True),
         jnp.zeros((N_TILES, SEG_W - SEG_TOTAL - 1), I32)], axis=1).reshape(-1).astype(I32)

    xs = _dispatch(pends, padded.astype(I32), segtab, pos, h2)
    bgu = b_gate_up[0]
    eo = _experts(block_expert, nused, xs, w_gate_up[0], bgu[:, None, 0::2], bgu[:, None, 1::2],
                  w_down[0], b_down[0][:, None, :])
    out = _combine(segtab, eo, x_new, mod3, pos.T, top_w.T)
    return out.reshape(BATCH, SEQ, D_MODEL)
```

```python
import functools

import jax
import jax.numpy as jnp
from jax import lax
from jax.experimental import pallas as pl
from jax.experimental.pallas import tpu as pltpu

F32 = jnp.float32
BF16 = jnp.bfloat16
I32 = jnp.int32
U32 = jnp.uint32

LANES = 128
SUBLANES = 8

D_MODEL = 1024
BATCH = 8
SEQ = 4096
CTX_LEN = 256
GRID_W = 64
ROPE_THETA = 10000.0
NORM_EPS = 1e-6
N_MOD = 6
NEG_INF = -1e30
LOG2E = 1.4426950408889634

MLA_HEADS = 8
MLA_NOPE = 64
MLA_ROPE = 32
MLA_QK = MLA_NOPE + MLA_ROPE
MLA_V = 64
MLA_Q_RANK = 384
MLA_KV_RANK = 256

SWA_HEADS = 8
SWA_KV_HEADS = 2
SWA_GROUP = SWA_HEADS // SWA_KV_HEADS
SWA_HD = 64
WINDOW = 128

N_EXPERTS = 32
TOP_K = 4
D_EXPERT = D_MODEL
SWIGLU_LIMIT = 7.0
SWIGLU_ALPHA = 1.702

N_TOK = BATCH * SEQ
MOD_ROWS = 16

ADA_TN = 1536
PROJ_TM = 512
MLA_TQ = 256
SWA_TQ = 256
SWA_KW = SWA_TQ + 2 * WINDOW
MERGE_TM = 256
MERGE_SUB = 2
MOE_BLK = 512
SEG_ALIGN = SUBLANES
N_TILES = N_TOK // MERGE_TM
LROWS = -(-(MERGE_TM * TOP_K + N_EXPERTS * (SEG_ALIGN - 1)) // LANES) * LANES
SEG_BITS = tuple(range(SEG_ALIGN.bit_length() - 1, (MERGE_TM * TOP_K).bit_length()))
SEG_COUNT = len(SEG_BITS) * N_EXPERTS
SEG_TOTAL = SEG_COUNT + len(SEG_BITS)
SEG_W = 512
CH_SRC_BITS = 11
MOVE_SLOTS = 2
N_SLOTS = -(-(N_TOK * TOP_K + N_TILES * N_EXPERTS * (SEG_ALIGN - 1) + N_EXPERTS * (MOE_BLK - 1))
            // MOE_BLK) * MOE_BLK
N_BLK = N_SLOTS // MOE_BLK

PACK_W = D_MODEL // 2

VMEM_LIMIT = 56 * 1024 * 1024

C_KVLAT = 0
C_KSWA = C_KVLAT + MLA_KV_RANK
C_KROPE = C_KSWA + SWA_KV_HEADS * SWA_HD
C_KVEND = C_KROPE + LANES
C_QLAT = C_KVEND
C_QSWA = C_QLAT + MLA_Q_RANK
C_GA = C_QSWA + SWA_HEADS * SWA_HD
C_GB = C_GA + D_MODEL
C_END = C_GB + D_MODEL


def _mm(a, b):
    return jnp.dot(a.astype(BF16), b.astype(BF16), preferred_element_type=F32)


def _mm_nt(a, b):
    return lax.dot_general(a.astype(BF16), b.astype(BF16), (((1,), (1,)), ((), ())),
                           preferred_element_type=F32)


def _mm_tn(a, b):
    return lax.dot_general(a.astype(BF16), b.astype(BF16), (((0,), (0,)), ((), ())),
                           preferred_element_type=F32)


def _split(a):
    hi = a.astype(BF16)
    lo = (a - hi.astype(F32)).astype(BF16)
    return hi, lo


def _pack_rows(a, already_bf16=False):
    half = a.shape[1] // 2
    rnd = (lambda v: v) if already_bf16 else (lambda v: v.astype(BF16).astype(F32))
    bits = lambda v: lax.bitcast_convert_type(rnd(v), U32)
    return bits(a[:, half:]) | (bits(a[:, :half]) >> 16)


def _unpack_rows(p):
    lo = lax.bitcast_convert_type(p << 16, F32)
    hi = lax.bitcast_convert_type(p & jnp.uint32(0xFFFF0000), F32)
    return jnp.concatenate([lo, hi], axis=1).astype(BF16)


def _rms(x):
    return x * lax.rsqrt(jnp.mean(x * x, axis=-1, keepdims=True) + NORM_EPS)


def _ada_kernel(c_ref, w_ref, b_ref, o_ref):
    c = c_ref[...]
    s = c * jax.nn.sigmoid(c)
    shi, slo = _split(s)
    whi, wlo = _split(w_ref[...])
    acc = _mm(shi, whi) + _mm(slo, whi) + _mm(shi, wlo)
    o_ref[...] = acc + b_ref[...]


def _ada(cond, w_ada, b_ada):
    n = w_ada.shape[1]
    return pl.pallas_call(
        _ada_kernel,
        grid=(n // ADA_TN,),
        in_specs=[
            pl.BlockSpec((MOD_ROWS, D_MODEL), lambda j: (0, 0)),
            pl.BlockSpec((D_MODEL, ADA_TN), lambda j: (0, j)),
            pl.BlockSpec((1, ADA_TN), lambda j: (0, j)),
        ],
        out_specs=pl.BlockSpec((MOD_ROWS, ADA_TN), lambda j: (0, j)),
        out_shape=jax.ShapeDtypeStruct((MOD_ROWS, n), F32),
        compiler_params=pltpu.CompilerParams(
            dimension_semantics=("arbitrary",), vmem_limit_bytes=VMEM_LIMIT),
        name="ada",
    )(cond, w_ada, b_ada)


def _head_norm(xraw, ind_ref, indt_ref, inv_dim):
    w = xraw.shape[1]
    ss = _mm(xraw * xraw, ind_ref[0:w, :])
    r = lax.rsqrt(ss * inv_dim + NORM_EPS)
    scale = _mm(jnp.concatenate(_split(r), axis=1), indt_ref[:, 0:w])
    return xraw * scale


def _rope(xh, cos, sa, sb, quarter):
    return (xh * cos + pltpu.roll(xh, LANES - quarter, 1) * sa
            + pltpu.roll(xh, quarter, 1) * sb)


def _proj_kernel(*refs, with_q):
    (x_ref, sh_ref, sc_ref, g1_ref, win_ref, cm_ref, sam_ref, sbm_ref, cs_ref, sas_ref, sbs_ref,
     gkva_ref, wkn_ref, wvt_ref, wvst_ref, gk_ref, gks_ref, ind_ref, indt_ref,
     ind64_ref, indt64_ref) = refs[:21]
    if with_q:
        gqa_ref, wqup_ref, gq_ref, gqs_ref = refs[21:25]
        kmla_ref, vmla_ref, kswa_ref, vswa_ref, qmla_ref, qswa_ref, siga_ref, sigb_ref = refs[25:]
    else:
        kmla_ref, vmla_ref, kswa_ref, vswa_ref = refs[21:]

    x = x_ref[...]
    h = _rms(x) * g1_ref[...] * (1.0 + sc_ref[...]) + sh_ref[...]
    y = _mm(h, win_ref[...])

    cm, sam, sbm = cm_ref[...], sam_ref[...], sbm_ref[...]
    cs, sas, sbs = cs_ref[...], sas_ref[...], sbs_ref[...]

    kvn = _rms(y[:, C_KVLAT:C_KVLAT + MLA_KV_RANK]) * gkva_ref[...]
    kn = _mm(kvn, wkn_ref[...])
    vmla_ref[...] = _mm_nt(wvt_ref[...], kvn).astype(BF16)
    vswa_ref[...] = _mm_nt(wvst_ref[...], h).astype(BF16)
    kr = y[:, C_KROPE:C_KROPE + LANES]
    kraw = kn + jnp.concatenate([kr] * MLA_HEADS, axis=1)
    kfull = _head_norm(kraw, ind_ref, indt_ref, 1.0 / MLA_QK) * gk_ref[...]
    for hd in range(MLA_HEADS):
        sl = slice(hd * LANES, (hd + 1) * LANES)
        kmla_ref[:, sl] = _rope(kfull[:, sl], cm, sam, sbm, MLA_ROPE // 4).astype(BF16)

    ks = _head_norm(y[:, C_KSWA:C_KSWA + SWA_KV_HEADS * SWA_HD], ind64_ref, indt64_ref,
                    1.0 / SWA_HD) * gks_ref[...]
    for grp in range(SWA_KV_HEADS * SWA_HD // LANES):
        sl = slice(grp * LANES, (grp + 1) * LANES)
        kswa_ref[:, sl] = _rope(ks[:, sl], cs, sas, sbs, SWA_HD // 4).astype(BF16)

    if with_q:
        qn = _rms(y[:, C_QLAT:C_QLAT + MLA_Q_RANK]) * gqa_ref[...]
        qraw = _mm(qn, wqup_ref[...])
        qf = _head_norm(qraw, ind_ref, indt_ref, 1.0 / MLA_QK) * gq_ref[...]
        for hd in range(MLA_HEADS):
            sl = slice(hd * LANES, (hd + 1) * LANES)
            qmla_ref[:, sl] = _rope(qf[:, sl], cm, sam, sbm, MLA_ROPE // 4).astype(BF16)
        qs = _head_norm(y[:, C_QSWA:C_QSWA + SWA_HEADS * SWA_HD], ind64_ref, indt64_ref,
                        1.0 / SWA_HD) * gqs_ref[...]
        for grp in range(SWA_HEADS * SWA_HD // LANES):
            sl = slice(grp * LANES, (grp + 1) * LANES)
            qswa_ref[:, sl] = _rope(qs[:, sl], cs, sas, sbs, SWA_HD // 4).astype(BF16)
        siga_ref[...] = jax.nn.sigmoid(y[:, C_GA:C_GA + D_MODEL]).astype(BF16)
        sigb_ref[...] = jax.nn.sigmoid(y[:, C_GB:C_GB + D_MODEL]).astype(BF16)


def _proj(x2d, mod3, mod_row_fn, tab_row_fn, tabs, consts, q_consts, win, with_q, name):
    rows = x2d.shape[0]
    tm = PROJ_TM
    const = lambda shape: pl.BlockSpec(shape, lambda i: (0,) * len(shape))
    tab = pl.BlockSpec((tm, LANES), lambda i: (tab_row_fn(i), 0))
    g1, rest = consts[0], consts[1:]
    in_specs = [
        pl.BlockSpec((tm, D_MODEL), lambda i: (i, 0)),
        pl.BlockSpec((None, 1, D_MODEL), lambda i: (mod_row_fn(i), 0, 0)),
        pl.BlockSpec((None, 1, D_MODEL), lambda i: (mod_row_fn(i), 0, 1)),
        const(g1.shape),
        pl.BlockSpec(win.shape, lambda i: (0, 0), pipeline_mode=pl.Buffered(1)),
        tab, tab, tab, tab, tab, tab,
    ] + [const(c.shape) for c in rest]
    args = [x2d, mod3, mod3, g1, win] + list(tabs) + list(rest)
    outs = [(MLA_HEADS * LANES, False), (MLA_HEADS * MLA_V, True),
            (SWA_KV_HEADS * SWA_HD, False), (SWA_KV_HEADS * SWA_HD, True)]
    if with_q:
        in_specs += [const(c.shape) for c in q_consts]
        args += list(q_consts)
        outs += [(MLA_HEADS * LANES, False), (SWA_HEADS * SWA_HD, False), (D_MODEL, False), (D_MODEL, False)]
    return pl.pallas_call(
        functools.partial(_proj_kernel, with_q=with_q),
        grid=(rows // tm,),
        in_specs=in_specs,
        out_specs=[pl.BlockSpec((w, tm), lambda i: (0, i)) if t else pl.BlockSpec((tm, w), lambda i: (i, 0))
                   for w, t in outs],
        out_shape=[jax.ShapeDtypeStruct((w, rows) if t else (rows, w), BF16) for w, t in outs],
        compiler_params=pltpu.CompilerParams(
            dimension_semantics=("arbitrary",), vmem_limit_bytes=VMEM_LIMIT),
        name=name,
    )(*args)


def _mla_step(q_ref, kl_ref, vlt_ref, kc_ref, vct_ref, o_ref, cur, prev):
    s_lat_c, s_ctx_c, m_c = cur
    s_lat_p, s_ctx_p, m_p = prev
    outs = []
    for hh in range(2):
        sl = slice(hh * LANES, (hh + 1) * LANES)
        vrows = slice(hh * MLA_V, (hh + 1) * MLA_V)
        q = q_ref[:, sl]
        s1 = _mm_nt(kl_ref[:, sl], q)
        s2 = _mm_nt(kc_ref[:, sl], q)
        s_lat_c[hh] = s1
        s_ctx_c[hh] = s2
        m_c[hh] = jnp.maximum(jnp.max(s1, axis=0, keepdims=True),
                              jnp.max(s2, axis=0, keepdims=True))

        m = m_p[hh]
        p1 = jnp.exp2(s_lat_p[hh] - m)
        p2 = jnp.exp2(s_ctx_p[hh] - m)
        l = jnp.sum(p1, axis=0, keepdims=True) + jnp.sum(p2, axis=0, keepdims=True)
        o = _mm(vlt_ref[vrows, :], p1) + _mm(vct_ref[vrows, :], p2)
        outs.append(o / l)
    o_ref[...] = jnp.concatenate(outs, axis=0).astype(BF16)


def _mla_kernel(q_ref, kl_ref, vl_ref, kc_ref, vc_ref, o_ref, *scratch):
    set0, set1 = scratch[:3], scratch[3:]
    i = pl.program_id(0)

    @pl.when(i == 0)
    def _():
        for ref in set1:
            ref[...] = jnp.zeros(ref.shape, F32)

    @pl.when(lax.rem(i, 2) == 0)
    def _():
        _mla_step(q_ref, kl_ref, vl_ref, kc_ref, vc_ref, o_ref, set0, set1)

    @pl.when(lax.rem(i, 2) == 1)
    def _():
        _mla_step(q_ref, kl_ref, vl_ref, kc_ref, vc_ref, o_ref, set1, set0)


def _mla_attn(q, k_lat, vt_lat, k_ctx, vt_ctx):
    tq = MLA_TQ
    nq = SEQ // tq
    pairs = MLA_HEADS // 2
    n_items = BATCH * pairs * nq

    def item(s):
        return s // (pairs * nq), (s // nq) % pairs, s % nq

    def scored(s):
        return item(jnp.minimum(s, n_items - 1))

    def finished(s):
        return item(jnp.maximum(s - 1, 0))

    def q_map(s):
        b, hp, i = scored(s)
        return b * nq + i, hp

    def k_map(s):
        b, hp, _ = scored(s)
        return b, hp

    def v_map(s):
        b, hp, _ = finished(s)
        return hp, b

    def o_map(s):
        b, hp, i = finished(s)
        return hp, b * nq + i

    return pl.pallas_call(
        _mla_kernel,
        grid=(n_items + 1,),
        in_specs=[
            pl.BlockSpec((tq, 2 * LANES), q_map),
            pl.BlockSpec((SEQ, 2 * LANES), k_map),
            pl.BlockSpec((2 * MLA_V, SEQ), v_map),
            pl.BlockSpec((CTX_LEN, 2 * LANES), k_map),
            pl.BlockSpec((2 * MLA_V, CTX_LEN), v_map),
        ],
        out_specs=pl.BlockSpec((2 * MLA_V, tq), o_map),
        out_shape=jax.ShapeDtypeStruct((MLA_HEADS * MLA_V, N_TOK), BF16),
        scratch_shapes=2 * [pltpu.VMEM((2, SEQ, tq), F32),
                            pltpu.VMEM((2, CTX_LEN, tq), F32),
                            pltpu.VMEM((2, 1, tq), F32)],
        compiler_params=pltpu.CompilerParams(
            dimension_semantics=("arbitrary",),
            vmem_limit_bytes=VMEM_LIMIT),
        name="mla_attn",
    )(q, k_lat, vt_lat, k_ctx, vt_ctx)


def _swa_window_start(i):
    return jnp.clip(i * SWA_TQ - WINDOW, 0, SEQ - SWA_KW)


def _swa_step(i_cur, i_prev, sink_ref, q_ref, k_ref, vt_ref, kc_ref, vct_ref, bias_ref, o_ref, cur, prev):
    tq = SWA_TQ
    sb_c, sc_c, m_c = cur
    sb_p, sc_p, m_p = prev
    kstart = pl.multiple_of(_swa_window_start(i_cur), LANES)
    kwin = k_ref[pl.ds(kstart, SWA_KW), :].astype(F32)
    vstart = pl.multiple_of(_swa_window_start(i_prev), LANES)
    vtwin = vt_ref[:, pl.ds(vstart, SWA_KW)]
    kc = kc_ref[...].astype(F32)
    vct = vct_ref[...]
    bias = bias_ref[...]

    def half_keys(k, kh):
        lane = lax.broadcasted_iota(I32, k.shape, 1)
        own = jnp.where((lane >= kh * SWA_HD) & (lane < (kh + 1) * SWA_HD), k, 0.0)
        other = pltpu.roll(own, SWA_HD, 1)
        return (own, other) if kh == 0 else (other, own)

    order = (0, 2, 1, 3)
    for kh in range(SWA_KV_HEADS):
        vrows = slice(kh * SWA_HD, (kh + 1) * SWA_HD)
        grp0 = kh * SWA_GROUP // 2
        q2 = jnp.concatenate([q_ref[:, (grp0 + c) * LANES:(grp0 + c + 1) * LANES] for c in range(2)],
                             axis=0)
        k_lo, k_hi = half_keys(kwin, kh)
        c_lo, c_hi = half_keys(kc, kh)
        sb = jnp.concatenate([_mm_nt(k_lo, q2), _mm_nt(k_hi, q2)], axis=1) + bias
        sc = jnp.concatenate([_mm_nt(c_lo, q2), _mm_nt(c_hi, q2)], axis=1)
        sink = jnp.concatenate(
            [jnp.full((1, tq), sink_ref[kh * SWA_GROUP + g] * LOG2E, F32) for g in order], axis=1)
        sb_c[kh] = sb
        sc_c[kh] = sc
        m_c[kh] = jnp.maximum(jnp.maximum(jnp.max(sb, axis=0, keepdims=True),
                                          jnp.max(sc, axis=0, keepdims=True)), sink)

        m = m_p[kh]
        pb = jnp.exp2(sb_p[kh] - m)
        pc = jnp.exp2(sc_p[kh] - m)
        l = (jnp.sum(pb, axis=0, keepdims=True) + jnp.sum(pc, axis=0, keepdims=True)
             + jnp.exp2(sink - m))
        o = (_mm(vtwin[vrows, :], pb) + _mm(vct[vrows, :], pc)) / l
        for blk, g in enumerate(order):
            hd = kh * SWA_GROUP + g
            o_ref[hd * SWA_HD:(hd + 1) * SWA_HD, :] = o[:, blk * tq:(blk + 1) * tq].astype(BF16)


def _swa_kernel(sink_ref, q_ref, k_ref, vt_ref, kc_ref, vct_ref, bias_ref, o_ref, *scratch):
    set0, set1 = scratch[:3], scratch[3:]
    nq = SEQ // SWA_TQ
    s = pl.program_id(0)
    i_cur = lax.rem(jnp.minimum(s, BATCH * nq - 1), nq)
    i_prev = lax.rem(jnp.maximum(s - 1, 0), nq)

    @pl.when(s == 0)
    def _():
        for ref in set1:
            ref[...] = jnp.zeros(ref.shape, F32)

    @pl.when(lax.rem(s, 2) == 0)
    def _():
        _swa_step(i_cur, i_prev, sink_ref, q_ref, k_ref, vt_ref, kc_ref, vct_ref, bias_ref, o_ref, set0, set1)

    @pl.when(lax.rem(s, 2) == 1)
    def _():
        _swa_step(i_cur, i_prev, sink_ref, q_ref, k_ref, vt_ref, kc_ref, vct_ref, bias_ref, o_ref, set1, set0)


def _swa_attn(sink, q, k_lat, vt_lat, k_ctx, vt_ctx):
    tq = SWA_TQ
    nq = SEQ // tq
    offsets = (0, WINDOW, SWA_KW - tq)
    assert all(i * tq - min(max(i * tq - WINDOW, 0), SEQ - SWA_KW) == offsets[(i > 0) + (i == nq - 1)]
               for i in range(nq))
    key = jnp.arange(SWA_KW, dtype=I32)[:, None]
    qry = (jnp.arange(SWA_GROUP * tq, dtype=I32) % tq)[None, :]
    bias = jnp.stack([jnp.where(jnp.abs(d + qry - key) <= WINDOW, 0.0, NEG_INF) for d in offsets]).astype(F32)
    case = lambda i: (i > 0).astype(I32) + (i == nq - 1).astype(I32)
    n_items = BATCH * nq
    scored = lambda s: jnp.minimum(s, n_items - 1)
    finished = lambda s: jnp.maximum(s - 1, 0)
    cols = SWA_GROUP * tq
    return pl.pallas_call(
        _swa_kernel,
        grid=(n_items + 1,),
        in_specs=[
            pl.BlockSpec(memory_space=pltpu.SMEM),
            pl.BlockSpec((tq, SWA_HEADS * SWA_HD), lambda s: (scored(s), 0)),
            pl.BlockSpec((SEQ, SWA_KV_HEADS * SWA_HD), lambda s: (scored(s) // nq, 0)),
            pl.BlockSpec((SWA_KV_HEADS * SWA_HD, SEQ), lambda s: (0, finished(s) // nq)),
            pl.BlockSpec((CTX_LEN, SWA_KV_HEADS * SWA_HD), lambda s: (scored(s) // nq, 0)),
            pl.BlockSpec((SWA_KV_HEADS * SWA_HD, CTX_LEN), lambda s: (0, finished(s) // nq)),
            pl.BlockSpec((None, SWA_KW, cols), lambda s: (case(scored(s) % nq), 0, 0)),
        ],
        out_specs=pl.BlockSpec((SWA_HEADS * SWA_HD, tq), lambda s: (0, finished(s))),
        out_shape=jax.ShapeDtypeStruct((SWA_HEADS * SWA_HD, N_TOK), BF16),
        scratch_shapes=2 * [pltpu.VMEM((SWA_KV_HEADS, SWA_KW, cols), F32),
                            pltpu.VMEM((SWA_KV_HEADS, CTX_LEN, cols), F32),
                            pltpu.VMEM((SWA_KV_HEADS, 1, cols), F32)],
        compiler_params=pltpu.CompilerParams(
            dimension_semantics=("arbitrary",), vmem_limit_bytes=VMEM_LIMIT),
        name="swa_attn",
    )(sink, q, k_lat, vt_lat, k_ctx, vt_ctx, bias)


def _merge_kernel(oa_ref, ob_ref, sa_ref, sb_ref, x_ref, g1_ref, sh2_ref, sc2_ref, wba_ref, wbb_ref,
                  wout_ref, g2_ref, wr_ref, br_ref, utri_ref, ltri_ref,
                  xnew_ref, h2_ref, pos_ref, wts_ref, seg_ref, carry_ref):
    i = pl.program_id(0)

    @pl.when(i == 0)
    def _():
        carry_ref[...] = jnp.zeros_like(carry_ref)

    for sub in range(MERGE_SUB):
        _merge_tile(sub, oa_ref, ob_ref, sa_ref, sb_ref, x_ref, g1_ref, sh2_ref, sc2_ref, wba_ref,
                    wbb_ref, wout_ref, g2_ref, wr_ref, br_ref, utri_ref, ltri_ref,
                    xnew_ref, h2_ref, pos_ref, wts_ref, seg_ref, carry_ref)


def _merge_tile(sub, oa_ref, ob_ref, sa_ref, sb_ref, x_ref, g1_ref, sh2_ref, sc2_ref, wba_ref, wbb_ref,
                wout_ref, g2_ref, wr_ref, br_ref, utri_ref, ltri_ref,
                xnew_ref, h2_ref, pos_ref, wts_ref, seg_ref, carry_ref):
    tok = slice(sub * MERGE_TM, (sub + 1) * MERGE_TM)
    segrows = slice(sub * SUBLANES, (sub + 1) * SUBLANES)
    ya = _mm_tn(oa_ref[:, tok], wba_ref[...])
    yb = _mm_tn(ob_ref[:, tok], wbb_ref[...])
    y = sa_ref[tok, :].astype(F32) * ya + sb_ref[tok, :].astype(F32) * yb
    z = _mm(y, wout_ref[...])
    xn = x_ref[tok, :] + g1_ref[...] * z
    xnew_ref[tok, :] = xn
    h2 = _rms(xn) * g2_ref[...] * (1.0 + sc2_ref[...]) + sh2_ref[...]
    h2_ref[tok, :] = h2.astype(BF16)

    hhi, hlo = _split(h2)
    whi, wlo = _split(wr_ref[...])
    lg = _mm_nt(whi, hhi) + _mm_nt(whi, hlo) + _mm_nt(wlo, hhi) + br_ref[...]
    eiota = lax.broadcasted_iota(I32, lg.shape, 0).astype(F32)
    vals, onehots = [], []
    for k in range(TOP_K):
        m = jnp.max(lg, axis=0, keepdims=True)
        ik = jnp.min(jnp.where(lg == m, eiota, float(N_EXPERTS)), axis=0, keepdims=True)
        hit = eiota == ik
        vals.append(m)
        onehots.append(hit.astype(F32))
        lg = jnp.where(hit, -jnp.inf, lg)
    es = [jnp.exp(v - vals[0]) for v in vals]
    tot = es[0] + es[1] + es[2] + es[3]
    for k in range(TOP_K):
        wts_ref[k:k + 1, tok] = es[k] / tot

    tots = [jnp.sum(oh, axis=1, keepdims=True) for oh in onehots]
    n = tots[0] + tots[1] + tots[2] + tots[3]
    m_al = jnp.floor((n + (SEG_ALIGN - 1)) * (1.0 / SEG_ALIGN))
    m_b = jnp.broadcast_to(m_al, (N_EXPERTS, LANES))
    lstart = _mm(ltri_ref[...], m_b) * float(SEG_ALIGN)
    off = jnp.zeros_like(n)
    for k in range(TOP_K):
        prefix = _mm(onehots[k], utri_ref[...])
        lp = jnp.sum(onehots[k] * (lstart[:, 0:1] + off + prefix), axis=0, keepdims=True)
        pos_ref[k:k + 1, tok] = lp.astype(I32)
        off = off + tots[k]
    eye = (lax.broadcasted_iota(I32, (N_EXPERTS, LANES), 0)
           == lax.broadcasted_iota(I32, (N_EXPERTS, LANES), 1)).astype(F32)
    to_row = lambda v: jnp.sum(v * eye, axis=0, keepdims=True).astype(I32)
    carry = carry_ref[...]
    seg_ref[segrows, :] = jnp.concatenate(
        [to_row(lstart), to_row(m_b * float(SEG_ALIGN)), to_row(carry),
         jnp.zeros((SUBLANES - 3, LANES), I32)], axis=0)
    carry_ref[...] = carry + m_b * float(SEG_ALIGN)


def _merge(o_a, o_b, sig_a, sig_b, x2d, mod3, w_ba, w_bb, w_out, g2, w_rt, b_r):
    tm = MERGE_TM * MERGE_SUB
    tiles_per_batch = SEQ // tm
    const = lambda shape: pl.BlockSpec(shape, lambda i: (0,) * len(shape))
    modspec = lambda j: pl.BlockSpec((None, 1, D_MODEL), lambda i: (i // tiles_per_batch, 0, j))
    row = lambda w: pl.BlockSpec((tm, w), lambda i: (i, 0))
    utri = (jnp.arange(MERGE_TM)[:, None] < jnp.arange(MERGE_TM)[None, :]).astype(BF16)
    ltri = (jnp.arange(N_EXPERTS)[None, :] < jnp.arange(N_EXPERTS)[:, None]).astype(BF16)
    return pl.pallas_call(
        _merge_kernel,
        grid=(N_TILES // MERGE_SUB,),
        in_specs=[pl.BlockSpec((512, tm), lambda i: (0, i)), pl.BlockSpec((512, tm), lambda i: (0, i)),
                  row(D_MODEL), row(D_MODEL), row(D_MODEL),
                  modspec(2), modspec(3), modspec(4),
                  const(w_ba.shape), const(w_bb.shape), const(w_out.shape), const(g2.shape),
                  const(w_rt.shape), const(b_r.shape), const(utri.shape), const(ltri.shape)],
        out_specs=[row(D_MODEL), row(D_MODEL),
                   pl.BlockSpec((TOP_K, tm), lambda i: (0, i)),
                   pl.BlockSpec((TOP_K, tm), lambda i: (0, i)),
                   pl.BlockSpec((MERGE_SUB * SUBLANES, LANES), lambda i: (i, 0))],
        out_shape=[jax.ShapeDtypeStruct((N_TOK, D_MODEL), F32),
                   jax.ShapeDtypeStruct((N_TOK, D_MODEL), BF16),
                   jax.ShapeDtypeStruct((TOP_K, N_TOK), I32),
                   jax.ShapeDtypeStruct((TOP_K, N_TOK), F32),
                   jax.ShapeDtypeStruct((N_TILES * SUBLANES, LANES), I32)],
        scratch_shapes=[pltpu.VMEM((N_EXPERTS, LANES), F32)],
        compiler_params=pltpu.CompilerParams(
            dimension_semantics=("arbitrary",), vmem_limit_bytes=VMEM_LIMIT),
        name="merge",
    )(o_a, o_b, sig_a, sig_b, x2d, mod3, mod3, mod3, w_ba, w_bb, w_out, g2, w_rt, b_r, utri, ltri)


def _chunk_copy(hbm, hbm_row, buf, buf_row, size, to_hbm, sem):
    vm = buf.at[pl.ds(pl.multiple_of(buf_row, SEG_ALIGN), size)]
    hb = hbm.at[pl.ds(pl.multiple_of(hbm_row, SEG_ALIGN), size)]
    return pltpu.make_async_copy(vm, hb, sem) if to_hbm else pltpu.make_async_copy(hb, vm, sem)


def _seg_start(seg_smem, tile, hbm, buf, to_hbm, sem):
    base = tile * SEG_W
    for c, b in enumerate(SEG_BITS):
        def issue(t, carry, c=c, size=1 << b):
            word = seg_smem[base + c * N_EXPERTS + t]
            _chunk_copy(hbm, word >> CH_SRC_BITS, buf, word & ((1 << CH_SRC_BITS) - 1),
                        size, to_hbm, sem).start()
            return carry

        lax.fori_loop(0, seg_smem[base + SEG_COUNT + c], issue, 0)


def _seg_wait(seg_smem, tile, hbm, buf, to_hbm, sem):
    assert LROWS < 2 << SEG_BITS[-1]
    total = seg_smem[tile * SEG_W + SEG_TOTAL]
    for b in SEG_BITS:
        size = 1 << b

        @pl.when((total & size) != 0)
        def _():
            _chunk_copy(hbm, 0, buf, 0, size, to_hbm, sem).wait()


def _dispatch_kernel(pend_ref, padded_ref, seg_ref, pos_ref, h2_ref, xs_ref, staged, sems):
    j = pl.program_id(0)
    slot = lax.rem(j, MOVE_SLOTS)

    @pl.when(j == 0)
    def _():
        staged[0, 0:MOE_BLK, :] = jnp.zeros((MOE_BLK, PACK_W), U32)

        def tail(action):
            def body(e, carry):
                @pl.when(padded_ref[e] > 0)
                def _():
                    start = pl.multiple_of(pend_ref[e] - MOE_BLK, MOE_BLK)
                    cp = pltpu.make_async_copy(staged.at[0, pl.ds(0, MOE_BLK)],
                                               xs_ref.at[pl.ds(start, MOE_BLK)], sems.at[0])
                    getattr(cp, action)()
                return carry
            lax.fori_loop(0, N_EXPERTS, body, 0)

        def unused(action):
            def body(b, carry):
                cp = pltpu.make_async_copy(
                    staged.at[0, pl.ds(0, MOE_BLK)],
                    xs_ref.at[pl.ds(pl.multiple_of(b * MOE_BLK, MOE_BLK), MOE_BLK)], sems.at[0])
                getattr(cp, action)()
                return carry
            lax.fori_loop(pend_ref[N_EXPERTS - 1] // MOE_BLK, N_BLK, body, 0)

        tail("start")
        unused("start")
        tail("wait")
        unused("wait")

    pos = pos_ref[...]
    piota = lax.broadcasted_iota(I32, (LROWS, MERGE_TM), 0)
    hit = piota == pos[0:1, :]
    for k in range(1, TOP_K):
        hit = hit | (piota == pos[k:k + 1, :])
    staged[slot] = _pack_rows(_mm(jnp.where(hit, 1.0, 0.0), h2_ref[...]), already_bf16=True)

    _seg_start(seg_ref, j, xs_ref, staged.at[slot], True, sems.at[1 + slot])

    lag = MOVE_SLOTS - 1

    def wait_tile(t):
        s = lax.rem(t, MOVE_SLOTS)
        _seg_wait(seg_ref, t, xs_ref, staged.at[s], True, sems.at[1 + s])

    @pl.when(j >= lag)
    def _():
        wait_tile(j - lag)

    @pl.when(j == N_TILES - 1)
    def _():
        for back in reversed(range(lag)):
            wait_tile(j - back)


def _dispatch(pends, padded, segtab, pos, h2):
    grid_spec = pltpu.PrefetchScalarGridSpec(
        num_scalar_prefetch=3,
        grid=(N_TILES,),
        in_specs=[
            pl.BlockSpec((TOP_K, MERGE_TM), lambda j, pe, pa, sg: (0, j)),
            pl.BlockSpec((MERGE_TM, D_MODEL), lambda j, pe, pa, sg: (j, 0)),
        ],
        out_specs=pl.BlockSpec(memory_space=pl.ANY),
        scratch_shapes=[pltpu.VMEM((MOVE_SLOTS, LROWS, PACK_W), U32),
                        pltpu.SemaphoreType.DMA((1 + MOVE_SLOTS,))],
    )
    return pl.pallas_call(
        _dispatch_kernel,
        grid_spec=grid_spec,
        out_shape=jax.ShapeDtypeStruct((N_SLOTS, PACK_W), U32),
        compiler_params=pltpu.CompilerParams(
            dimension_semantics=("arbitrary",), vmem_limit_bytes=VMEM_LIMIT),
        name="dispatch",
    )(pends, padded, segtab, pos, h2)


SPLIT_SUB = 256


def _expert_kernel(be_ref, nused_ref, xs_ref, wgu_ref, bg_ref, bl_ref, wd_ref, bd_ref, perm_ref, o_ref,
                   wg_s, wl_s, wd_s):
    i = pl.program_id(0)
    used = i < nused_ref[0]
    fresh = jnp.logical_or(i == 0, be_ref[i] != be_ref[jnp.maximum(i - 1, 0)])

    @pl.when(jnp.logical_and(used, fresh))
    def _():
        half = SPLIT_SUB // 2
        for s in range(2 * D_EXPERT // SPLIT_SUB):
            r = _mm(wgu_ref[:, s * SPLIT_SUB:(s + 1) * SPLIT_SUB], perm_ref[...])
            wg_s[:, s * half:(s + 1) * half] = r[:, :half].astype(BF16)
            wl_s[:, s * half:(s + 1) * half] = r[:, half:].astype(BF16)
        wd_s[...] = wd_ref[...].astype(BF16)

    @pl.when(used)
    def _():
        xb = _unpack_rows(xs_ref[...])
        g = jnp.minimum(_mm(xb, wg_s[...]) + bg_ref[...], SWIGLU_LIMIT)
        l = jnp.clip(_mm(xb, wl_s[...]) + bl_ref[...], -SWIGLU_LIMIT, SWIGLU_LIMIT)
        act = g * jax.nn.sigmoid(SWIGLU_ALPHA * g) * (l + 1.0)
        o_ref[...] = _pack_rows(_mm(act, wd_s[...]) + bd_ref[...])

    @pl.when(jnp.logical_not(used))
    def _():
        o_ref[...] = jnp.zeros_like(o_ref)


def _experts(block_expert, nused, xs, w_gu, bg, bl, w_dn, bd):
    col = jnp.arange(SPLIT_SUB)
    src = jnp.where(col < SPLIT_SUB // 2, 2 * col, 2 * (col - SPLIT_SUB // 2) + 1)
    perm = (jnp.arange(SPLIT_SUB)[:, None] == src[None, :]).astype(BF16)
    blk = lambda i, be, nu: (jnp.minimum(i, nu[0] - 1), 0)
    per_expert = lambda shape: pl.BlockSpec((None,) + shape, lambda i, be, nu: (be[i], 0, 0))
    grid_spec = pltpu.PrefetchScalarGridSpec(
        num_scalar_prefetch=2,
        grid=(N_BLK,),
        in_specs=[pl.BlockSpec((MOE_BLK, PACK_W), blk),
                  per_expert((D_MODEL, 2 * D_EXPERT)), per_expert((1, D_EXPERT)), per_expert((1, D_EXPERT)),
                  per_expert((D_EXPERT, D_MODEL)), per_expert((1, D_MODEL)),
                  pl.BlockSpec((SPLIT_SUB, SPLIT_SUB), lambda i, be, nu: (0, 0))],
        out_specs=pl.BlockSpec((MOE_BLK, PACK_W), lambda i, be, nu: (i, 0)),
        scratch_shapes=[pltpu.VMEM((D_MODEL, D_EXPERT), BF16), pltpu.VMEM((D_MODEL, D_EXPERT), BF16),
                        pltpu.VMEM((D_EXPERT, D_MODEL), BF16)],
    )
    return pl.pallas_call(
        _expert_kernel,
        grid_spec=grid_spec,
        out_shape=jax.ShapeDtypeStruct((N_SLOTS, PACK_W), U32),
        compiler_params=pltpu.CompilerParams(
            dimension_semantics=("arbitrary",), vmem_limit_bytes=VMEM_LIMIT),
        name="experts",
    )(block_expert, nused, xs, w_gu, bg, bl, w_dn, bd, perm)


def _combine_kernel(seg_ref, eo_ref, xnew_ref, g2_ref, pos_ref, wt_ref, o_ref, gbuf, sems):
    j = pl.program_id(0)
    slot = lax.rem(j, MOVE_SLOTS)
    ahead = MOVE_SLOTS - 1

    def fetch(tile):
        sl = lax.rem(tile, MOVE_SLOTS)
        _seg_start(seg_ref, tile, eo_ref, gbuf.at[sl], False, sems.at[sl])

    @pl.when(j == 0)
    def _():
        gbuf[...] = jnp.zeros_like(gbuf)
        for t in range(ahead):
            fetch(jnp.int32(t))

    @pl.when(j + ahead < N_TILES)
    def _():
        fetch(j + ahead)

    _seg_wait(seg_ref, j, eo_ref, gbuf.at[slot], False, sems.at[slot])

    pos = pos_ref[...]
    wt = wt_ref[...]
    lane = lax.broadcasted_iota(I32, (MERGE_TM, LROWS), 1)
    wsel = jnp.zeros((MERGE_TM, LROWS), F32)
    for k in range(TOP_K):
        wsel = jnp.where(lane == pos[:, k:k + 1], wt[:, k:k + 1], wsel)
    acc = _mm(wsel, _unpack_rows(gbuf[slot]))
    o_ref[...] = xnew_ref[...] + g2_ref[...] * acc


def _combine(segtab, eo, x_new, mod3, pos_t, wts_t):
    tiles_per_batch = SEQ // MERGE_TM
    grid_spec = pltpu.PrefetchScalarGridSpec(
        num_scalar_prefetch=1,
        grid=(N_TILES,),
        in_specs=[
            pl.BlockSpec(memory_space=pl.ANY),
            pl.BlockSpec((MERGE_TM, D_MODEL), lambda j, sg: (j, 0)),
            pl.BlockSpec((None, 1, D_MODEL), lambda j, sg: (j // tiles_per_batch, 0, 5)),
            pl.BlockSpec((MERGE_TM, TOP_K), lambda j, sg: (j, 0)),
            pl.BlockSpec((MERGE_TM, TOP_K), lambda j, sg: (j, 0)),
        ],
        out_specs=pl.BlockSpec((MERGE_TM, D_MODEL), lambda j, sg: (j, 0)),
        scratch_shapes=[pltpu.VMEM((MOVE_SLOTS, LROWS, PACK_W), U32),
                        pltpu.SemaphoreType.DMA((MOVE_SLOTS,))],
    )
    return pl.pallas_call(
        _combine_kernel,
        grid_spec=grid_spec,
        out_shape=jax.ShapeDtypeStruct((N_TOK, D_MODEL), F32),
        compiler_params=pltpu.CompilerParams(
            dimension_semantics=("arbitrary",), vmem_limit_bytes=VMEM_LIMIT),
        name="combine",
    )(segtab, eo, x_new, mod3, pos_t, wts_t)


def _pad_heads(w, heads, dim):
    lead = w.shape[:-1]
    w = w.reshape(lead + (heads, dim))
    w = jnp.pad(w, [(0, 0)] * len(lead) + [(0, 0), (0, LANES - dim)])
    return w.reshape(lead + (heads * LANES,))


def _rope_tables(dim, offset, repeat):
    pos = jnp.arange(SEQ, dtype=jnp.int32)
    row = (pos // GRID_W).astype(F32)
    col = (pos % GRID_W).astype(F32)
    q = dim // 4
    freqs = ROPE_THETA ** (-jnp.arange(q, dtype=F32) / q)
    ang_r = row[:, None] * freqs
    ang_c = col[:, None] * freqs
    ang = jnp.concatenate([ang_r, ang_r, ang_c, ang_c], axis=-1)
    cos, sin = jnp.cos(ang), jnp.sin(ang)
    first = (jnp.arange(dim) % (2 * q)) < q
    sa = jnp.where(first, -sin, 0.0)
    sb = jnp.where(first, 0.0, sin)
    pad = lambda t, v: jnp.pad(jnp.tile(t, (1, repeat)), ((0, 0), (offset, LANES - offset - dim * repeat)),
                               constant_values=v)
    return pad(cos, 1.0), pad(sa, 0.0), pad(sb, 0.0)


def kernel(x, c, ctx, c_ctx, w_ada, b_ada, norm1_g, norm2_g, w_in, mla_q_a_g, mla_kv_a_g, w_q_up, w_kv_up,
           mla_q_g, mla_k_g, swa_q_g, swa_k_g, swa_sink, w_branch_a, w_branch_b, w_out, w_router, b_router,
           w_gate_up, b_gate_up, w_down, b_down):
    assert x.shape == (BATCH, SEQ, D_MODEL) and ctx.shape == (BATCH, CTX_LEN, D_MODEL)
    assert w_ada.shape[0] == 1, "single layer"

    cond = jnp.concatenate([c, c_ctx[None], jnp.zeros((MOD_ROWS - BATCH - 1, D_MODEL), F32)], axis=0)
    mod = _ada(cond, w_ada[0], b_ada[0][None])
    mod3 = mod.reshape(MOD_ROWS, 1, N_MOD * D_MODEL)

    wi = w_in[0]
    o1, o2, o3 = MLA_KV_RANK, MLA_KV_RANK + MLA_ROPE, MLA_KV_RANK + MLA_ROPE + SWA_KV_HEADS * SWA_HD
    kv_cols = o3 + SWA_KV_HEADS * SWA_HD
    q1 = kv_cols + MLA_Q_RANK
    q2 = q1 + SWA_HEADS * SWA_HD
    q3 = q2 + D_MODEL
    krope_grp = jnp.pad(wi[:, o1:o2], ((0, 0), (MLA_NOPE, LANES - MLA_QK)))
    win = jnp.concatenate([
        wi[:, 0:o1], wi[:, o2:o3], krope_grp, wi[:, kv_cols:q1], wi[:, q1:q2], wi[:, q2:q3], wi[:, q3:]],
        axis=1).astype(BF16)
    assert win.shape == (D_MODEL, C_END)
    wvst = wi[:, o3:kv_cols].T.astype(BF16)
    wkv = w_kv_up[0].reshape(MLA_KV_RANK, MLA_HEADS, MLA_NOPE + MLA_V)
    wkn = _pad_heads(wkv[:, :, :MLA_NOPE].reshape(MLA_KV_RANK, -1), MLA_HEADS, MLA_NOPE).astype(BF16)
    wvt = wkv[:, :, MLA_NOPE:].reshape(MLA_KV_RANK, MLA_HEADS * MLA_V).T.astype(BF16)
    wqup = _pad_heads(w_q_up[0], MLA_HEADS, MLA_QK).astype(BF16)
    gk_t = _pad_heads(jnp.tile(mla_k_g[0], MLA_HEADS)[None], MLA_HEADS, MLA_QK)
    gq_t = _pad_heads(jnp.tile(mla_q_g[0] * (MLA_QK ** -0.5 * LOG2E), MLA_HEADS)[None], MLA_HEADS, MLA_QK)
    gks_t = jnp.tile(swa_k_g[0], SWA_KV_HEADS)[None]
    gqs_t = jnp.tile(swa_q_g[0] * (SWA_HD ** -0.5 * LOG2E), SWA_HEADS)[None]

    def lane_to_head(n_lanes, width):
        ind = ((jnp.arange(n_lanes) // width)[:, None] == jnp.arange(LANES)[None, :]).astype(BF16)
        return ind, jnp.concatenate([ind.T, ind.T], axis=0)

    ind, indt = lane_to_head(MLA_HEADS * LANES, LANES)
    ind64, indt64 = lane_to_head(SWA_HEADS * SWA_HD, SWA_HD)
    consts = [norm1_g[0][None], mla_kv_a_g[0][None], wkn, wvt, wvst, gk_t, gks_t, ind, indt, ind64, indt64]
    q_consts = [mla_q_a_g[0][None], wqup, gq_t, gqs_t]

    tabs_lat = _rope_tables(MLA_ROPE, MLA_NOPE, 1) + _rope_tables(SWA_HD, 0, LANES // SWA_HD)
    ones = jnp.ones((PROJ_TM, LANES), F32)
    zeros = jnp.zeros((PROJ_TM, LANES), F32)
    tabs_ctx = (ones, zeros, zeros, ones, zeros, zeros)

    x2d = x.reshape(N_TOK, D_MODEL)
    tiles_per_batch = SEQ // PROJ_TM
    k_mla, vt_mla, k_swa, vt_swa, q_mla, q_swa, sig_a, sig_b = _proj(
        x2d, mod3, lambda i: i // tiles_per_batch, lambda i: i % tiles_per_batch,
        tabs_lat, consts, q_consts, win, True, "proj_latent")
    kc_mla, vct_mla, kc_swa, vct_swa = _proj(
        ctx.reshape(BATCH * CTX_LEN, D_MODEL), mod3, lambda i: BATCH, lambda i: 0,
        tabs_ctx, consts, q_consts, win[:, :C_KVEND], False, "proj_ctx")

    o_a = _mla_attn(q_mla, k_mla, vt_mla, kc_mla, vct_mla)
    o_b = _swa_attn(swa_sink[0], q_swa, k_swa, vt_swa, kc_swa, vct_swa)

    x_new, h2, pos, top_w, seg = _merge(
        o_a, o_b, sig_a, sig_b, x2d, mod3,
        w_branch_a[0].astype(BF16), w_branch_b[0].astype(BF16), w_out[0].astype(BF16),
        norm2_g[0][None], w_router[0].T, b_router[0][:, None])

    seg3 = seg.reshape(N_TILES, SUBLANES, LANES)
    lstart, rows, gcarry = (seg3[:, r, :N_EXPERTS] for r in range(3))
    total = gcarry[-1] + rows[-1]
    padded = (total + MOE_BLK - 1) // MOE_BLK * MOE_BLK
    pends = jnp.cumsum(padded).astype(I32)
    pstarts = pends - padded
    nused = (pends[-1] // MOE_BLK).astype(I32)[None]
    blk_start = jnp.arange(N_BLK, dtype=I32) * MOE_BLK
    block_expert = jnp.minimum(
        jnp.sum((blk_start[:, None] >= pends[None, :]).astype(I32), axis=1), N_EXPERTS - 1)
    assert LROWS <= 1 << CH_SRC_BITS and N_SLOTS < 1 << (31 - CH_SRC_BITS) and SEG_TOTAL < SEG_W
    units = (rows // SEG_ALIGN)[:, None, :]
    cls = jnp.arange(len(SEG_BITS), dtype=I32)[None, :, None]
    has = (units >> cls) & 1
    done = ((units >> (cls + 1)) << (cls + 1)) * SEG_ALIGN
    word = ((pstarts[None, None, :] + gcarry[:, None, :] + done) << CH_SRC_BITS) + lstart[:, None, :] + done
    place = jnp.cumsum(has, axis=2) - has
    lists = jnp.sum(jnp.where((place[..., None] == jnp.arange(N_EXPERTS, dtype=I32)) & (has[..., None] == 1),
                              word[..., None], 0), axis=2)
    segtab = jnp.concatenate(
        [lists.reshape(N_TILES, SEG_COUNT), jnp.sum(has, axis=2), jnp.sum(rows, axis=1, keepdims=True),
         jnp.zeros((N_TILES, SEG_W - SEG_TOTAL - 1), I32)], axis=1).reshape(-1).astype(I32)

    xs = _dispatch(pends, padded.astype(I32), segtab, pos, h2)
    bgu = b_gate_up[0]
    eo = _experts(block_expert, nused, xs, w_gate_up[0], bgu[:, None, 0::2], bgu[:, None, 1::2],
                  w_down[0], b_down[0][:, None, :])
    out = _combine(segtab, eo, x_new, mod3, pos.T, top_w.T)
    return out.reshape(BATCH, SEQ, D_MODEL)
```

```python
import functools

import jax
import jax.numpy as jnp
from jax import lax
from jax.experimental import pallas as pl
from jax.experimental.pallas import tpu as pltpu

F32 = jnp.float32
BF16 = jnp.bfloat16
I32 = jnp.int32
U32 = jnp.uint32

LANES = 128
SUBLANES = 8

D_MODEL = 1024
BATCH = 8
SEQ = 4096
CTX_LEN = 256
GRID_W = 64
ROPE_THETA = 10000.0
NORM_EPS = 1e-6
N_MOD = 6
NEG_INF = -1e30
LOG2E = 1.4426950408889634

MLA_HEADS = 8
MLA_NOPE = 64
MLA_ROPE = 32
MLA_QK = MLA_NOPE + MLA_ROPE
MLA_V = 64
MLA_Q_RANK = 384
MLA_KV_RANK = 256

SWA_HEADS = 8
SWA_KV_HEADS = 2
SWA_GROUP = SWA_HEADS // SWA_KV_HEADS
SWA_HD = 64
WINDOW = 128

N_EXPERTS = 32
TOP_K = 4
D_EXPERT = D_MODEL
SWIGLU_LIMIT = 7.0
SWIGLU_ALPHA = 1.702

N_TOK = BATCH * SEQ
MOD_ROWS = 16

ADA_TN = 1536
PROJ_TM = 512
MLA_TQ = 256
SWA_TQ = 256
SWA_KW = SWA_TQ + 2 * WINDOW
MERGE_TM = 256
MOE_BLK = 512
SEG_ALIGN = SUBLANES
N_TILES = N_TOK // MERGE_TM
LROWS = -(-(MERGE_TM * TOP_K + N_EXPERTS * (SEG_ALIGN - 1)) // LANES) * LANES
SEG_BITS = tuple(range(SEG_ALIGN.bit_length() - 1, (MERGE_TM * TOP_K).bit_length()))
SEG_COUNT = len(SEG_BITS) * N_EXPERTS
SEG_TOTAL = SEG_COUNT + len(SEG_BITS)
SEG_W = 512
CH_SRC_BITS = 11
MOVE_SLOTS = 2
N_SLOTS = -(-(N_TOK * TOP_K + N_TILES * N_EXPERTS * (SEG_ALIGN - 1) + N_EXPERTS * (MOE_BLK - 1))
            // MOE_BLK) * MOE_BLK
N_BLK = N_SLOTS // MOE_BLK

PACK_W = D_MODEL // 2

VMEM_LIMIT = 56 * 1024 * 1024

C_KVLAT = 0
C_KSWA = C_KVLAT + MLA_KV_RANK
C_KROPE = C_KSWA + SWA_KV_HEADS * SWA_HD
C_KVEND = C_KROPE + LANES
C_QLAT = C_KVEND
C_QSWA = C_QLAT + MLA_Q_RANK
C_GA = C_QSWA + SWA_HEADS * SWA_HD
C_GB = C_GA + D_MODEL
C_END = C_GB + D_MODEL


def _mm(a, b):
    return jnp.dot(a.astype(BF16), b.astype(BF16), preferred_element_type=F32)


def _mm_nt(a, b):
    return lax.dot_general(a.astype(BF16), b.astype(BF16), (((1,), (1,)), ((), ())),
                           preferred_element_type=F32)


def _mm_tn(a, b):
    return lax.dot_general(a.astype(BF16), b.astype(BF16), (((0,), (0,)), ((), ())),
                           preferred_element_type=F32)


def _split(a):
    hi = a.astype(BF16)
    lo = (a - hi.astype(F32)).astype(BF16)
    return hi, lo


def _pack_rows(a, already_bf16=False):
    half = a.shape[1] // 2
    rnd = (lambda v: v) if already_bf16 else (lambda v: v.astype(BF16).astype(F32))
    bits = lambda v: lax.bitcast_convert_type(rnd(v), U32)
    return bits(a[:, half:]) | (bits(a[:, :half]) >> 16)


def _unpack_rows(p):
    lo = lax.bitcast_convert_type(p << 16, F32)
    hi = lax.bitcast_convert_type(p & jnp.uint32(0xFFFF0000), F32)
    return jnp.concatenate([lo, hi], axis=1).astype(BF16)


def _rms(x):
    return x * lax.rsqrt(jnp.mean(x * x, axis=-1, keepdims=True) + NORM_EPS)


def _ada_kernel(c_ref, w_ref, b_ref, o_ref):
    c = c_ref[...]
    s = c * jax.nn.sigmoid(c)
    shi, slo = _split(s)
    whi, wlo = _split(w_ref[...])
    acc = _mm(shi, whi) + _mm(slo, whi) + _mm(shi, wlo)
    o_ref[...] = acc + b_ref[...]


def _ada(cond, w_ada, b_ada):
    n = w_ada.shape[1]
    return pl.pallas_call(
        _ada_kernel,
        grid=(n // ADA_TN,),
        in_specs=[
            pl.BlockSpec((MOD_ROWS, D_MODEL), lambda j: (0, 0)),
            pl.BlockSpec((D_MODEL, ADA_TN), lambda j: (0, j)),
            pl.BlockSpec((1, ADA_TN), lambda j: (0, j)),
        ],
        out_specs=pl.BlockSpec((MOD_ROWS, ADA_TN), lambda j: (0, j)),
        out_shape=jax.ShapeDtypeStruct((MOD_ROWS, n), F32),
        compiler_params=pltpu.CompilerParams(
            dimension_semantics=("arbitrary",), vmem_limit_bytes=VMEM_LIMIT),
        name="ada",
    )(cond, w_ada, b_ada)


def _head_norm(xraw, ind_ref, indt_ref, inv_dim):
    w = xraw.shape[1]
    ss = _mm(xraw * xraw, ind_ref[0:w, :])
    r = lax.rsqrt(ss * inv_dim + NORM_EPS)
    scale = _mm(jnp.concatenate(_split(r), axis=1), indt_ref[:, 0:w])
    return xraw * scale


def _rope(xh, cos, sa, sb, quarter):
    return (xh * cos + pltpu.roll(xh, LANES - quarter, 1) * sa
            + pltpu.roll(xh, quarter, 1) * sb)


def _proj_kernel(*refs, with_q):
    (x_ref, sh_ref, sc_ref, g1_ref, win_ref, cm_ref, sam_ref, sbm_ref, cs_ref, sas_ref, sbs_ref,
     gkva_ref, wkn_ref, wvt_ref, wvst_ref, gk_ref, gks_ref, ind_ref, indt_ref,
     ind64_ref, indt64_ref) = refs[:21]
    if with_q:
        gqa_ref, wqup_ref, gq_ref, gqs_ref = refs[21:25]
        kmla_ref, vmla_ref, kswa_ref, vswa_ref, qmla_ref, qswa_ref, siga_ref, sigb_ref = refs[25:]
    else:
        kmla_ref, vmla_ref, kswa_ref, vswa_ref = refs[21:]

    x = x_ref[...]
    h = _rms(x) * g1_ref[...] * (1.0 + sc_ref[...]) + sh_ref[...]
    y = _mm(h, win_ref[...])

    cm, sam, sbm = cm_ref[...], sam_ref[...], sbm_ref[...]
    cs, sas, sbs = cs_ref[...], sas_ref[...], sbs_ref[...]

    kvn = _rms(y[:, C_KVLAT:C_KVLAT + MLA_KV_RANK]) * gkva_ref[...]
    kn = _mm(kvn, wkn_ref[...])
    vmla_ref[...] = _mm_nt(wvt_ref[...], kvn).astype(BF16)
    vswa_ref[...] = _mm_nt(wvst_ref[...], h).astype(BF16)
    kr = y[:, C_KROPE:C_KROPE + LANES]
    kraw = kn + jnp.concatenate([kr] * MLA_HEADS, axis=1)
    kfull = _head_norm(kraw, ind_ref, indt_ref, 1.0 / MLA_QK) * gk_ref[...]
    for hd in range(MLA_HEADS):
        sl = slice(hd * LANES, (hd + 1) * LANES)
        kmla_ref[:, sl] = _rope(kfull[:, sl], cm, sam, sbm, MLA_ROPE // 4).astype(BF16)

    ks = _head_norm(y[:, C_KSWA:C_KSWA + SWA_KV_HEADS * SWA_HD], ind64_ref, indt64_ref,
                    1.0 / SWA_HD) * gks_ref[...]
    for grp in range(SWA_KV_HEADS * SWA_HD // LANES):
        sl = slice(grp * LANES, (grp + 1) * LANES)
        kswa_ref[:, sl] = _rope(ks[:, sl], cs, sas, sbs, SWA_HD // 4).astype(BF16)

    if with_q:
        qn = _rms(y[:, C_QLAT:C_QLAT + MLA_Q_RANK]) * gqa_ref[...]
        qraw = _mm(qn, wqup_ref[...])
        qf = _head_norm(qraw, ind_ref, indt_ref, 1.0 / MLA_QK) * gq_ref[...]
        for hd in range(MLA_HEADS):
            sl = slice(hd * LANES, (hd + 1) * LANES)
            qmla_ref[:, sl] = _rope(qf[:, sl], cm, sam, sbm, MLA_ROPE // 4).astype(BF16)
        qs = _head_norm(y[:, C_QSWA:C_QSWA + SWA_HEADS * SWA_HD], ind64_ref, indt64_ref,
                        1.0 / SWA_HD) * gqs_ref[...]
        for grp in range(SWA_HEADS * SWA_HD // LANES):
            sl = slice(grp * LANES, (grp + 1) * LANES)
            qswa_ref[:, sl] = _rope(qs[:, sl], cs, sas, sbs, SWA_HD // 4).astype(BF16)
        siga_ref[...] = jax.nn.sigmoid(y[:, C_GA:C_GA + D_MODEL]).astype(BF16)
        sigb_ref[...] = jax.nn.sigmoid(y[:, C_GB:C_GB + D_MODEL]).astype(BF16)


def _proj(x2d, mod3, mod_row_fn, tab_row_fn, tabs, consts, q_consts, win, with_q, name):
    rows = x2d.shape[0]
    tm = PROJ_TM
    const = lambda shape: pl.BlockSpec(shape, lambda i: (0,) * len(shape))
    tab = pl.BlockSpec((tm, LANES), lambda i: (tab_row_fn(i), 0))
    g1, rest = consts[0], consts[1:]
    in_specs = [
        pl.BlockSpec((tm, D_MODEL), lambda i: (i, 0)),
        pl.BlockSpec((None, 1, D_MODEL), lambda i: (mod_row_fn(i), 0, 0)),
        pl.BlockSpec((None, 1, D_MODEL), lambda i: (mod_row_fn(i), 0, 1)),
        const(g1.shape),
        pl.BlockSpec(win.shape, lambda i: (0, 0), pipeline_mode=pl.Buffered(1)),
        tab, tab, tab, tab, tab, tab,
    ] + [const(c.shape) for c in rest]
    args = [x2d, mod3, mod3, g1, win] + list(tabs) + list(rest)
    outs = [(MLA_HEADS * LANES, False), (MLA_HEADS * MLA_V, True),
            (SWA_KV_HEADS * SWA_HD, False), (SWA_KV_HEADS * SWA_HD, True)]
    if with_q:
        in_specs += [const(c.shape) for c in q_consts]
        args += list(q_consts)
        outs += [(MLA_HEADS * LANES, False), (SWA_HEADS * SWA_HD, False), (D_MODEL, False), (D_MODEL, False)]
    return pl.pallas_call(
        functools.partial(_proj_kernel, with_q=with_q),
        grid=(rows // tm,),
        in_specs=in_specs,
        out_specs=[pl.BlockSpec((w, tm), lambda i: (0, i)) if t else pl.BlockSpec((tm, w), lambda i: (i, 0))
                   for w, t in outs],
        out_shape=[jax.ShapeDtypeStruct((w, rows) if t else (rows, w), BF16) for w, t in outs],
        compiler_params=pltpu.CompilerParams(
            dimension_semantics=("arbitrary",), vmem_limit_bytes=VMEM_LIMIT),
        name=name,
    )(*args)


def _mla_step(q_ref, kl_ref, vlt_ref, kc_ref, vct_ref, o_ref, cur, prev):
    s_lat_c, s_ctx_c, m_c = cur
    s_lat_p, s_ctx_p, m_p = prev
    outs = []
    for hh in range(2):
        sl = slice(hh * LANES, (hh + 1) * LANES)
        vrows = slice(hh * MLA_V, (hh + 1) * MLA_V)
        q = q_ref[:, sl]
        s1 = _mm_nt(kl_ref[:, sl], q)
        s2 = _mm_nt(kc_ref[:, sl], q)
        s_lat_c[hh] = s1
        s_ctx_c[hh] = s2
        m_c[hh] = jnp.maximum(jnp.max(s1, axis=0, keepdims=True),
                              jnp.max(s2, axis=0, keepdims=True))

        m = m_p[hh]
        p1 = jnp.exp2(s_lat_p[hh] - m)
        p2 = jnp.exp2(s_ctx_p[hh] - m)
        l = jnp.sum(p1, axis=0, keepdims=True) + jnp.sum(p2, axis=0, keepdims=True)
        o = _mm(vlt_ref[vrows, :], p1) + _mm(vct_ref[vrows, :], p2)
        outs.append(o / l)
    o_ref[...] = jnp.concatenate(outs, axis=0).astype(BF16)


def _mla_kernel(q_ref, kl_ref, vl_ref, kc_ref, vc_ref, o_ref, *scratch):
    set0, set1 = scratch[:3], scratch[3:]
    i = pl.program_id(0)

    @pl.when(i == 0)
    def _():
        for ref in set1:
            ref[...] = jnp.zeros(ref.shape, F32)

    @pl.when(lax.rem(i, 2) == 0)
    def _():
        _mla_step(q_ref, kl_ref, vl_ref, kc_ref, vc_ref, o_ref, set0, set1)

    @pl.when(lax.rem(i, 2) == 1)
    def _():
        _mla_step(q_ref, kl_ref, vl_ref, kc_ref, vc_ref, o_ref, set1, set0)


def _mla_attn(q, k_lat, vt_lat, k_ctx, vt_ctx):
    tq = MLA_TQ
    nq = SEQ // tq
    pairs = MLA_HEADS // 2
    n_items = BATCH * pairs * nq

    def item(s):
        return s // (pairs * nq), (s // nq) % pairs, s % nq

    def scored(s):
        return item(jnp.minimum(s, n_items - 1))

    def finished(s):
        return item(jnp.maximum(s - 1, 0))

    def q_map(s):
        b, hp, i = scored(s)
        return b * nq + i, hp

    def k_map(s):
        b, hp, _ = scored(s)
        return b, hp

    def v_map(s):
        b, hp, _ = finished(s)
        return hp, b

    def o_map(s):
        b, hp, i = finished(s)
        return hp, b * nq + i

    return pl.pallas_call(
        _mla_kernel,
        grid=(n_items + 1,),
        in_specs=[
            pl.BlockSpec((tq, 2 * LANES), q_map),
            pl.BlockSpec((SEQ, 2 * LANES), k_map),
            pl.BlockSpec((2 * MLA_V, SEQ), v_map),
            pl.BlockSpec((CTX_LEN, 2 * LANES), k_map),
            pl.BlockSpec((2 * MLA_V, CTX_LEN), v_map),
        ],
        out_specs=pl.BlockSpec((2 * MLA_V, tq), o_map),
        out_shape=jax.ShapeDtypeStruct((MLA_HEADS * MLA_V, N_TOK), BF16),
        scratch_shapes=2 * [pltpu.VMEM((2, SEQ, tq), F32),
                            pltpu.VMEM((2, CTX_LEN, tq), F32),
                            pltpu.VMEM((2, 1, tq), F32)],
        compiler_params=pltpu.CompilerParams(
            dimension_semantics=("arbitrary",),
            vmem_limit_bytes=VMEM_LIMIT),
        name="mla_attn",
    )(q, k_lat, vt_lat, k_ctx, vt_ctx)


def _swa_window_start(i):
    return jnp.clip(i * SWA_TQ - WINDOW, 0, SEQ - SWA_KW)


def _swa_step(i_cur, i_prev, sink_ref, q_ref, k_ref, vt_ref, kc_ref, vct_ref, bias_ref, o_ref, cur, prev):
    tq = SWA_TQ
    sb_c, sc_c, m_c = cur
    sb_p, sc_p, m_p = prev
    kstart = pl.multiple_of(_swa_window_start(i_cur), LANES)
    kwin = k_ref[pl.ds(kstart, SWA_KW), :].astype(F32)
    vstart = pl.multiple_of(_swa_window_start(i_prev), LANES)
    vtwin = vt_ref[:, pl.ds(vstart, SWA_KW)]
    kc = kc_ref[...].astype(F32)
    vct = vct_ref[...]
    bias = bias_ref[...]

    def half_keys(k, kh):
        lane = lax.broadcasted_iota(I32, k.shape, 1)
        own = jnp.where((lane >= kh * SWA_HD) & (lane < (kh + 1) * SWA_HD), k, 0.0)
        other = pltpu.roll(own, SWA_HD, 1)
        return (own, other) if kh == 0 else (other, own)

    order = (0, 2, 1, 3)
    for kh in range(SWA_KV_HEADS):
        vrows = slice(kh * SWA_HD, (kh + 1) * SWA_HD)
        grp0 = kh * SWA_GROUP // 2
        q2 = jnp.concatenate([q_ref[:, (grp0 + c) * LANES:(grp0 + c + 1) * LANES] for c in range(2)],
                             axis=0)
        k_lo, k_hi = half_keys(kwin, kh)
        c_lo, c_hi = half_keys(kc, kh)
        sb = jnp.concatenate([_mm_nt(k_lo, q2), _mm_nt(k_hi, q2)], axis=1) + bias
        sc = jnp.concatenate([_mm_nt(c_lo, q2), _mm_nt(c_hi, q2)], axis=1)
        sink = jnp.concatenate(
            [jnp.full((1, tq), sink_ref[kh * SWA_GROUP + g] * LOG2E, F32) for g in order], axis=1)
        sb_c[kh] = sb
        sc_c[kh] = sc
        m_c[kh] = jnp.maximum(jnp.maximum(jnp.max(sb, axis=0, keepdims=True),
                                          jnp.max(sc, axis=0, keepdims=True)), sink)

        m = m_p[kh]
        pb = jnp.exp2(sb_p[kh] - m)
        pc = jnp.exp2(sc_p[kh] - m)
        l = (jnp.sum(pb, axis=0, keepdims=True) + jnp.sum(pc, axis=0, keepdims=True)
             + jnp.exp2(sink - m))
        o = (_mm(vtwin[vrows, :], pb) + _mm(vct[vrows, :], pc)) / l
        for blk, g in enumerate(order):
            hd = kh * SWA_GROUP + g
            o_ref[hd * SWA_HD:(hd + 1) * SWA_HD, :] = o[:, blk * tq:(blk + 1) * tq].astype(BF16)


def _swa_kernel(sink_ref, q_ref, k_ref, vt_ref, kc_ref, vct_ref, bias_ref, o_ref, *scratch):
    set0, set1 = scratch[:3], scratch[3:]
    nq = SEQ // SWA_TQ
    s = pl.program_id(0)
    i_cur = lax.rem(jnp.minimum(s, BATCH * nq - 1), nq)
    i_prev = lax.rem(jnp.maximum(s - 1, 0), nq)

    @pl.when(s == 0)
    def _():
        for ref in set1:
            ref[...] = jnp.zeros(ref.shape, F32)

    @pl.when(lax.rem(s, 2) == 0)
    def _():
        _swa_step(i_cur, i_prev, sink_ref, q_ref, k_ref, vt_ref, kc_ref, vct_ref, bias_ref, o_ref, set0, set1)

    @pl.when(lax.rem(s, 2) == 1)
    def _():
        _swa_step(i_cur, i_prev, sink_ref, q_ref, k_ref, vt_ref, kc_ref, vct_ref, bias_ref, o_ref, set1, set0)


def _swa_attn(sink, q, k_lat, vt_lat, k_ctx, vt_ctx):
    tq = SWA_TQ
    nq = SEQ // tq
    offsets = (0, WINDOW, SWA_KW - tq)
    assert all(i * tq - min(max(i * tq - WINDOW, 0), SEQ - SWA_KW) == offsets[(i > 0) + (i == nq - 1)]
               for i in range(nq))
    key = jnp.arange(SWA_KW, dtype=I32)[:, None]
    qry = (jnp.arange(SWA_GROUP * tq, dtype=I32) % tq)[None, :]
    bias = jnp.stack([jnp.where(jnp.abs(d + qry - key) <= WINDOW, 0.0, NEG_INF) for d in offsets]).astype(F32)
    case = lambda i: (i > 0).astype(I32) + (i == nq - 1).astype(I32)
    n_items = BATCH * nq
    scored = lambda s: jnp.minimum(s, n_items - 1)
    finished = lambda s: jnp.maximum(s - 1, 0)
    cols = SWA_GROUP * tq
    return pl.pallas_call(
        _swa_kernel,
        grid=(n_items + 1,),
        in_specs=[
            pl.BlockSpec(memory_space=pltpu.SMEM),
            pl.BlockSpec((tq, SWA_HEADS * SWA_HD), lambda s: (scored(s), 0)),
            pl.BlockSpec((SEQ, SWA_KV_HEADS * SWA_HD), lambda s: (scored(s) // nq, 0)),
            pl.BlockSpec((SWA_KV_HEADS * SWA_HD, SEQ), lambda s: (0, finished(s) // nq)),
            pl.BlockSpec((CTX_LEN, SWA_KV_HEADS * SWA_HD), lambda s: (scored(s) // nq, 0)),
            pl.BlockSpec((SWA_KV_HEADS * SWA_HD, CTX_LEN), lambda s: (0, finished(s) // nq)),
            pl.BlockSpec((None, SWA_KW, cols), lambda s: (case(scored(s) % nq), 0, 0)),
        ],
        out_specs=pl.BlockSpec((SWA_HEADS * SWA_HD, tq), lambda s: (0, finished(s))),
        out_shape=jax.ShapeDtypeStruct((SWA_HEADS * SWA_HD, N_TOK), BF16),
        scratch_shapes=2 * [pltpu.VMEM((SWA_KV_HEADS, SWA_KW, cols), F32),
                            pltpu.VMEM((SWA_KV_HEADS, CTX_LEN, cols), F32),
                            pltpu.VMEM((SWA_KV_HEADS, 1, cols), F32)],
        compiler_params=pltpu.CompilerParams(
            dimension_semantics=("arbitrary",), vmem_limit_bytes=VMEM_LIMIT),
        name="swa_attn",
    )(sink, q, k_lat, vt_lat, k_ctx, vt_ctx, bias)


def _merge_kernel(oa_ref, ob_ref, sa_ref, sb_ref, x_ref, g1_ref, sh2_ref, sc2_ref, wba_ref, wbb_ref,
                  wout_ref, g2_ref, wr_ref, br_ref, utri_ref, ltri_ref,
                  xnew_ref, h2_ref, pos_ref, wts_ref, seg_ref, carry_ref, lg0_ref, lg1_ref):
    s = pl.program_id(0)

    @pl.when(s == 0)
    def _():
        carry_ref[...] = jnp.zeros_like(carry_ref)
        lg1_ref[...] = jnp.zeros_like(lg1_ref)

    def step(lg_cur, lg_prev):
        _merge_project(oa_ref, ob_ref, sa_ref, sb_ref, x_ref, g1_ref, sh2_ref, sc2_ref, wba_ref, wbb_ref,
                       wout_ref, g2_ref, wr_ref, br_ref, xnew_ref, h2_ref, lg_cur)
        _merge_route(s > 0, lg_prev, utri_ref, ltri_ref, pos_ref, wts_ref, seg_ref, carry_ref)

    @pl.when(lax.rem(s, 2) == 0)
    def _():
        step(lg0_ref, lg1_ref)

    @pl.when(lax.rem(s, 2) == 1)
    def _():
        step(lg1_ref, lg0_ref)


def _merge_project(oa_ref, ob_ref, sa_ref, sb_ref, x_ref, g1_ref, sh2_ref, sc2_ref, wba_ref, wbb_ref,
                   wout_ref, g2_ref, wr_ref, br_ref, xnew_ref, h2_ref, lg_ref):
    ya = _mm_tn(oa_ref[...], wba_ref[...])
    yb = _mm_tn(ob_ref[...], wbb_ref[...])
    y = sa_ref[...].astype(F32) * ya + sb_ref[...].astype(F32) * yb
    z = _mm(y, wout_ref[...])
    xn = x_ref[...] + g1_ref[...] * z
    xnew_ref[...] = xn
    h2 = _rms(xn) * g2_ref[...] * (1.0 + sc2_ref[...]) + sh2_ref[...]
    h2_ref[...] = h2.astype(BF16)

    hhi, hlo = _split(h2)
    whi, wlo = _split(wr_ref[...])
    lg_ref[...] = _mm_nt(whi, hhi) + _mm_nt(whi, hlo) + _mm_nt(wlo, hhi) + br_ref[...]


def _merge_route(real, lg_ref, utri_ref, ltri_ref, pos_ref, wts_ref, seg_ref, carry_ref):
    lg = lg_ref[...]
    eiota = lax.broadcasted_iota(I32, lg.shape, 0).astype(F32)
    vals, onehots = [], []
    for k in range(TOP_K):
        m = jnp.max(lg, axis=0, keepdims=True)
        ik = jnp.min(jnp.where(lg == m, eiota, float(N_EXPERTS)), axis=0, keepdims=True)
        hit = eiota == ik
        vals.append(m)
        onehots.append(hit.astype(F32))
        lg = jnp.where(hit, -jnp.inf, lg)
    es = [jnp.exp(v - vals[0]) for v in vals]
    tot = es[0] + es[1] + es[2] + es[3]
    for k in range(TOP_K):
        wts_ref[k:k + 1, :] = es[k] / tot

    tots = [jnp.sum(oh, axis=1, keepdims=True) for oh in onehots]
    n = tots[0] + tots[1] + tots[2] + tots[3]
    m_al = jnp.floor((n + (SEG_ALIGN - 1)) * (1.0 / SEG_ALIGN))
    m_b = jnp.broadcast_to(m_al, (N_EXPERTS, LANES))
    lstart = _mm(ltri_ref[...], m_b) * float(SEG_ALIGN)
    off = jnp.zeros_like(n)
    for k in range(TOP_K):
        prefix = _mm(onehots[k], utri_ref[...])
        lp = jnp.sum(onehots[k] * (lstart[:, 0:1] + off + prefix), axis=0, keepdims=True)
        pos_ref[k:k + 1, :] = lp.astype(I32)
        off = off + tots[k]
    eye = (lax.broadcasted_iota(I32, (N_EXPERTS, LANES), 0)
           == lax.broadcasted_iota(I32, (N_EXPERTS, LANES), 1)).astype(F32)
    to_row = lambda v: jnp.sum(v * eye, axis=0, keepdims=True).astype(I32)
    carry = carry_ref[...]
    seg_ref[...] = jnp.concatenate(
        [to_row(lstart), to_row(m_b * float(SEG_ALIGN)), to_row(carry),
         jnp.zeros((SUBLANES - 3, LANES), I32)], axis=0)
    carry_ref[...] = carry + jnp.where(real, m_b * float(SEG_ALIGN), 0.0)


def _merge(o_a, o_b, sig_a, sig_b, x2d, mod3, w_ba, w_bb, w_out, g2, w_rt, b_r):
    tm = MERGE_TM
    tiles_per_batch = SEQ // tm
    proj = lambda s: jnp.minimum(s, N_TILES - 1)
    routed = lambda s: jnp.maximum(s - 1, 0)
    const = lambda shape: pl.BlockSpec(shape, lambda s: (0,) * len(shape))
    modspec = lambda j: pl.BlockSpec((None, 1, D_MODEL), lambda s: (proj(s) // tiles_per_batch, 0, j))
    row = lambda w: pl.BlockSpec((tm, w), lambda s: (proj(s), 0))
    utri = (jnp.arange(tm)[:, None] < jnp.arange(tm)[None, :]).astype(BF16)
    ltri = (jnp.arange(N_EXPERTS)[None, :] < jnp.arange(N_EXPERTS)[:, None]).astype(BF16)
    return pl.pallas_call(
        _merge_kernel,
        grid=(N_TILES + 1,),
        in_specs=[pl.BlockSpec((512, tm), lambda s: (0, proj(s))), pl.BlockSpec((512, tm), lambda s: (0, proj(s))),
                  row(D_MODEL), row(D_MODEL), row(D_MODEL),
                  modspec(2), modspec(3), modspec(4),
                  const(w_ba.shape), const(w_bb.shape), const(w_out.shape), const(g2.shape),
                  const(w_rt.shape), const(b_r.shape), const(utri.shape), const(ltri.shape)],
        out_specs=[row(D_MODEL), row(D_MODEL),
                   pl.BlockSpec((TOP_K, tm), lambda s: (0, routed(s))),
                   pl.BlockSpec((TOP_K, tm), lambda s: (0, routed(s))),
                   pl.BlockSpec((SUBLANES, LANES), lambda s: (routed(s), 0))],
        out_shape=[jax.ShapeDtypeStruct((N_TOK, D_MODEL), F32),
                   jax.ShapeDtypeStruct((N_TOK, D_MODEL), BF16),
                   jax.ShapeDtypeStruct((TOP_K, N_TOK), I32),
                   jax.ShapeDtypeStruct((TOP_K, N_TOK), F32),
                   jax.ShapeDtypeStruct((N_TILES * SUBLANES, LANES), I32)],
        scratch_shapes=[pltpu.VMEM((N_EXPERTS, LANES), F32),
                        pltpu.VMEM((N_EXPERTS, tm), F32), pltpu.VMEM((N_EXPERTS, tm), F32)],
        compiler_params=pltpu.CompilerParams(
            dimension_semantics=("arbitrary",), vmem_limit_bytes=VMEM_LIMIT),
        name="merge",
    )(o_a, o_b, sig_a, sig_b, x2d, mod3, mod3, mod3, w_ba, w_bb, w_out, g2, w_rt, b_r, utri, ltri)


def _chunk_copy(hbm, hbm_row, buf, buf_row, size, to_hbm, sem):
    vm = buf.at[pl.ds(pl.multiple_of(buf_row, SEG_ALIGN), size)]
    hb = hbm.at[pl.ds(pl.multiple_of(hbm_row, SEG_ALIGN), size)]
    return pltpu.make_async_copy(vm, hb, sem) if to_hbm else pltpu.make_async_copy(hb, vm, sem)


def _seg_start(seg_smem, tile, hbm, buf, to_hbm, sem):
    base = tile * SEG_W
    for c, b in enumerate(SEG_BITS):
        def issue(t, carry, c=c, size=1 << b):
            word = seg_smem[base + c * N_EXPERTS + t]
            _chunk_copy(hbm, word >> CH_SRC_BITS, buf, word & ((1 << CH_SRC_BITS) - 1),
                        size, to_hbm, sem).start()
            return carry

        lax.fori_loop(0, seg_smem[base + SEG_COUNT + c], issue, 0)


def _seg_wait(seg_smem, tile, hbm, buf, to_hbm, sem):
    assert LROWS < 2 << SEG_BITS[-1]
    total = seg_smem[tile * SEG_W + SEG_TOTAL]
    for b in SEG_BITS:
        size = 1 << b

        @pl.when((total & size) != 0)
        def _():
            _chunk_copy(hbm, 0, buf, 0, size, to_hbm, sem).wait()


def _dispatch_kernel(pend_ref, padded_ref, seg_ref, pos_ref, h2_ref, xs_ref, staged, sems):
    j = pl.program_id(0)
    slot = lax.rem(j, MOVE_SLOTS)

    @pl.when(j == 0)
    def _():
        staged[0, 0:MOE_BLK, :] = jnp.zeros((MOE_BLK, PACK_W), U32)

        def tail(action):
            def body(e, carry):
                @pl.when(padded_ref[e] > 0)
                def _():
                    start = pl.multiple_of(pend_ref[e] - MOE_BLK, MOE_BLK)
                    cp = pltpu.make_async_copy(staged.at[0, pl.ds(0, MOE_BLK)],
                                               xs_ref.at[pl.ds(start, MOE_BLK)], sems.at[0])
                    getattr(cp, action)()
                return carry
            lax.fori_loop(0, N_EXPERTS, body, 0)

        def unused(action):
            def body(b, carry):
                cp = pltpu.make_async_copy(
                    staged.at[0, pl.ds(0, MOE_BLK)],
                    xs_ref.at[pl.ds(pl.multiple_of(b * MOE_BLK, MOE_BLK), MOE_BLK)], sems.at[0])
                getattr(cp, action)()
                return carry
            lax.fori_loop(pend_ref[N_EXPERTS - 1] // MOE_BLK, N_BLK, body, 0)

        tail("start")
        unused("start")
        tail("wait")
        unused("wait")

    pos = pos_ref[...]
    piota = lax.broadcasted_iota(I32, (LROWS, MERGE_TM), 0)
    hit = piota == pos[0:1, :]
    for k in range(1, TOP_K):
        hit = hit | (piota == pos[k:k + 1, :])
    staged[slot] = _pack_rows(_mm(jnp.where(hit, 1.0, 0.0), h2_ref[...]), already_bf16=True)

    _seg_start(seg_ref, j, xs_ref, staged.at[slot], True, sems.at[1 + slot])

    lag = MOVE_SLOTS - 1

    def wait_tile(t):
        s = lax.rem(t, MOVE_SLOTS)
        _seg_wait(seg_ref, t, xs_ref, staged.at[s], True, sems.at[1 + s])

    @pl.when(j >= lag)
    def _():
        wait_tile(j - lag)

    @pl.when(j == N_TILES - 1)
    def _():
        for back in reversed(range(lag)):
            wait_tile(j - back)


def _dispatch(pends, padded, segtab, pos, h2):
    grid_spec = pltpu.PrefetchScalarGridSpec(
        num_scalar_prefetch=3,
        grid=(N_TILES,),
        in_specs=[
            pl.BlockSpec((TOP_K, MERGE_TM), lambda j, pe, pa, sg: (0, j)),
            pl.BlockSpec((MERGE_TM, D_MODEL), lambda j, pe, pa, sg: (j, 0)),
        ],
        out_specs=pl.BlockSpec(memory_space=pl.ANY),
        scratch_shapes=[pltpu.VMEM((MOVE_SLOTS, LROWS, PACK_W), U32),
                        pltpu.SemaphoreType.DMA((1 + MOVE_SLOTS,))],
    )
    return pl.pallas_call(
        _dispatch_kernel,
        grid_spec=grid_spec,
        out_shape=jax.ShapeDtypeStruct((N_SLOTS, PACK_W), U32),
        compiler_params=pltpu.CompilerParams(
            dimension_semantics=("arbitrary",), vmem_limit_bytes=VMEM_LIMIT),
        name="dispatch",
    )(pends, padded, segtab, pos, h2)


SPLIT_SUB = 256


def _expert_kernel(be_ref, nused_ref, xs_ref, wgu_ref, bg_ref, bl_ref, wd_ref, bd_ref, perm_ref, o_ref,
                   wg_s, wl_s, wd_s):
    i = pl.program_id(0)
    used = i < nused_ref[0]
    fresh = jnp.logical_or(i == 0, be_ref[i] != be_ref[jnp.maximum(i - 1, 0)])

    @pl.when(jnp.logical_and(used, fresh))
    def _():
        half = SPLIT_SUB // 2
        for s in range(2 * D_EXPERT // SPLIT_SUB):
            r = _mm(wgu_ref[:, s * SPLIT_SUB:(s + 1) * SPLIT_SUB], perm_ref[...])
            wg_s[:, s * half:(s + 1) * half] = r[:, :half].astype(BF16)
            wl_s[:, s * half:(s + 1) * half] = r[:, half:].astype(BF16)
        wd_s[...] = wd_ref[...].astype(BF16)

    @pl.when(used)
    def _():
        xb = _unpack_rows(xs_ref[...])
        g = jnp.minimum(_mm(xb, wg_s[...]) + bg_ref[...], SWIGLU_LIMIT)
        l = jnp.clip(_mm(xb, wl_s[...]) + bl_ref[...], -SWIGLU_LIMIT, SWIGLU_LIMIT)
        act = g * jax.nn.sigmoid(SWIGLU_ALPHA * g) * (l + 1.0)
        o_ref[...] = _pack_rows(_mm(act, wd_s[...]) + bd_ref[...])

    @pl.when(jnp.logical_not(used))
    def _():
        o_ref[...] = jnp.zeros_like(o_ref)


def _experts(block_expert, nused, xs, w_gu, bg, bl, w_dn, bd):
    col = jnp.arange(SPLIT_SUB)
    src = jnp.where(col < SPLIT_SUB // 2, 2 * col, 2 * (col - SPLIT_SUB // 2) + 1)
    perm = (jnp.arange(SPLIT_SUB)[:, None] == src[None, :]).astype(BF16)
    blk = lambda i, be, nu: (jnp.minimum(i, nu[0] - 1), 0)
    per_expert = lambda shape: pl.BlockSpec((None,) + shape, lambda i, be, nu: (be[i], 0, 0))
    grid_spec = pltpu.PrefetchScalarGridSpec(
        num_scalar_prefetch=2,
        grid=(N_BLK,),
        in_specs=[pl.BlockSpec((MOE_BLK, PACK_W), blk),
                  per_expert((D_MODEL, 2 * D_EXPERT)), per_expert((1, D_EXPERT)), per_expert((1, D_EXPERT)),
                  per_expert((D_EXPERT, D_MODEL)), per_expert((1, D_MODEL)),
                  pl.BlockSpec((SPLIT_SUB, SPLIT_SUB), lambda i, be, nu: (0, 0))],
        out_specs=pl.BlockSpec((MOE_BLK, PACK_W), lambda i, be, nu: (i, 0)),
        scratch_shapes=[pltpu.VMEM((D_MODEL, D_EXPERT), BF16), pltpu.VMEM((D_MODEL, D_EXPERT), BF16),
                        pltpu.VMEM((D_EXPERT, D_MODEL), BF16)],
    )
    return pl.pallas_call(
        _expert_kernel,
        grid_spec=grid_spec,
        out_shape=jax.ShapeDtypeStruct((N_SLOTS, PACK_W), U32),
        compiler_params=pltpu.CompilerParams(
            dimension_semantics=("arbitrary",), vmem_limit_bytes=VMEM_LIMIT),
        name="experts",
    )(block_expert, nused, xs, w_gu, bg, bl, w_dn, bd, perm)


def _combine_kernel(seg_ref, eo_ref, xnew_ref, g2_ref, pos_ref, wt_ref, o_ref, gbuf, sems):
    j = pl.program_id(0)
    slot = lax.rem(j, MOVE_SLOTS)
    ahead = MOVE_SLOTS - 1

    def fetch(tile):
        sl = lax.rem(tile, MOVE_SLOTS)
        _seg_start(seg_ref, tile, eo_ref, gbuf.at[sl], False, sems.at[sl])

    @pl.when(j == 0)
    def _():
        gbuf[...] = jnp.zeros_like(gbuf)
        for t in range(ahead):
            fetch(jnp.int32(t))

    @pl.when(j + ahead < N_TILES)
    def _():
        fetch(j + ahead)

    _seg_wait(seg_ref, j, eo_ref, gbuf.at[slot], False, sems.at[slot])

    pos = pos_ref[...]
    wt = wt_ref[...]
    lane = lax.broadcasted_iota(I32, (MERGE_TM, LROWS), 1)
    wsel = jnp.zeros((MERGE_TM, LROWS), F32)
    for k in range(TOP_K):
        wsel = jnp.where(lane == pos[:, k:k + 1], wt[:, k:k + 1], wsel)
    acc = _mm(wsel, _unpack_rows(gbuf[slot]))
    o_ref[...] = xnew_ref[...] + g2_ref[...] * acc


def _combine(segtab, eo, x_new, mod3, pos_t, wts_t):
    tiles_per_batch = SEQ // MERGE_TM
    grid_spec = pltpu.PrefetchScalarGridSpec(
        num_scalar_prefetch=1,
        grid=(N_TILES,),
        in_specs=[
            pl.BlockSpec(memory_space=pl.ANY),
            pl.BlockSpec((MERGE_TM, D_MODEL), lambda j, sg: (j, 0)),
            pl.BlockSpec((None, 1, D_MODEL), lambda j, sg: (j // tiles_per_batch, 0, 5)),
            pl.BlockSpec((MERGE_TM, TOP_K), lambda j, sg: (j, 0)),
            pl.BlockSpec((MERGE_TM, TOP_K), lambda j, sg: (j, 0)),
        ],
        out_specs=pl.BlockSpec((MERGE_TM, D_MODEL), lambda j, sg: (j, 0)),
        scratch_shapes=[pltpu.VMEM((MOVE_SLOTS, LROWS, PACK_W), U32),
                        pltpu.SemaphoreType.DMA((MOVE_SLOTS,))],
    )
    return pl.pallas_call(
        _combine_kernel,
        grid_spec=grid_spec,
        out_shape=jax.ShapeDtypeStruct((N_TOK, D_MODEL), F32),
        compiler_params=pltpu.CompilerParams(
            dimension_semantics=("arbitrary",), vmem_limit_bytes=VMEM_LIMIT),
        name="combine",
    )(segtab, eo, x_new, mod3, pos_t, wts_t)


def _pad_heads(w, heads, dim):
    lead = w.shape[:-1]
    w = w.reshape(lead + (heads, dim))
    w = jnp.pad(w, [(0, 0)] * len(lead) + [(0, 0), (0, LANES - dim)])
    return w.reshape(lead + (heads * LANES,))


def _rope_tables(dim, offset, repeat):
    pos = jnp.arange(SEQ, dtype=jnp.int32)
    row = (pos // GRID_W).astype(F32)
    col = (pos % GRID_W).astype(F32)
    q = dim // 4
    freqs = ROPE_THETA ** (-jnp.arange(q, dtype=F32) / q)
    ang_r = row[:, None] * freqs
    ang_c = col[:, None] * freqs
    ang = jnp.concatenate([ang_r, ang_r, ang_c, ang_c], axis=-1)
    cos, sin = jnp.cos(ang), jnp.sin(ang)
    first = (jnp.arange(dim) % (2 * q)) < q
    sa = jnp.where(first, -sin, 0.0)
    sb = jnp.where(first, 0.0, sin)
    pad = lambda t, v: jnp.pad(jnp.tile(t, (1, repeat)), ((0, 0), (offset, LANES - offset - dim * repeat)),
                               constant_values=v)
    return pad(cos, 1.0), pad(sa, 0.0), pad(sb, 0.0)


def kernel(x, c, ctx, c_ctx, w_ada, b_ada, norm1_g, norm2_g, w_in, mla_q_a_g, mla_kv_a_g, w_q_up, w_kv_up,
           mla_q_g, mla_k_g, swa_q_g, swa_k_g, swa_sink, w_branch_a, w_branch_b, w_out, w_router, b_router,
           w_gate_up, b_gate_up, w_down, b_down):
    assert x.shape == (BATCH, SEQ, D_MODEL) and ctx.shape == (BATCH, CTX_LEN, D_MODEL)
    assert w_ada.shape[0] == 1, "single layer"

    cond = jnp.concatenate([c, c_ctx[None], jnp.zeros((MOD_ROWS - BATCH - 1, D_MODEL), F32)], axis=0)
    mod = _ada(cond, w_ada[0], b_ada[0][None])
    mod3 = mod.reshape(MOD_ROWS, 1, N_MOD * D_MODEL)

    wi = w_in[0]
    o1, o2, o3 = MLA_KV_RANK, MLA_KV_RANK + MLA_ROPE, MLA_KV_RANK + MLA_ROPE + SWA_KV_HEADS * SWA_HD
    kv_cols = o3 + SWA_KV_HEADS * SWA_HD
    q1 = kv_cols + MLA_Q_RANK
    q2 = q1 + SWA_HEADS * SWA_HD
    q3 = q2 + D_MODEL
    krope_grp = jnp.pad(wi[:, o1:o2], ((0, 0), (MLA_NOPE, LANES - MLA_QK)))
    win = jnp.concatenate([
        wi[:, 0:o1], wi[:, o2:o3], krope_grp, wi[:, kv_cols:q1], wi[:, q1:q2], wi[:, q2:q3], wi[:, q3:]],
        axis=1).astype(BF16)
    assert win.shape == (D_MODEL, C_END)
    wvst = wi[:, o3:kv_cols].T.astype(BF16)
    wkv = w_kv_up[0].reshape(MLA_KV_RANK, MLA_HEADS, MLA_NOPE + MLA_V)
    wkn = _pad_heads(wkv[:, :, :MLA_NOPE].reshape(MLA_KV_RANK, -1), MLA_HEADS, MLA_NOPE).astype(BF16)
    wvt = wkv[:, :, MLA_NOPE:].reshape(MLA_KV_RANK, MLA_HEADS * MLA_V).T.astype(BF16)
    wqup = _pad_heads(w_q_up[0], MLA_HEADS, MLA_QK).astype(BF16)
    gk_t = _pad_heads(jnp.tile(mla_k_g[0], MLA_HEADS)[None], MLA_HEADS, MLA_QK)
    gq_t = _pad_heads(jnp.tile(mla_q_g[0] * (MLA_QK ** -0.5 * LOG2E), MLA_HEADS)[None], MLA_HEADS, MLA_QK)
    gks_t = jnp.tile(swa_k_g[0], SWA_KV_HEADS)[None]
    gqs_t = jnp.tile(swa_q_g[0] * (SWA_HD ** -0.5 * LOG2E), SWA_HEADS)[None]

    def lane_to_head(n_lanes, width):
        ind = ((jnp.arange(n_lanes) // width)[:, None] == jnp.arange(LANES)[None, :]).astype(BF16)
        return ind, jnp.concatenate([ind.T, ind.T], axis=0)

    ind, indt = lane_to_head(MLA_HEADS * LANES, LANES)
    ind64, indt64 = lane_to_head(SWA_HEADS * SWA_HD, SWA_HD)
    consts = [norm1_g[0][None], mla_kv_a_g[0][None], wkn, wvt, wvst, gk_t, gks_t, ind, indt, ind64, indt64]
    q_consts = [mla_q_a_g[0][None], wqup, gq_t, gqs_t]

    tabs_lat = _rope_tables(MLA_ROPE, MLA_NOPE, 1) + _rope_tables(SWA_HD, 0, LANES // SWA_HD)
    ones = jnp.ones((PROJ_TM, LANES), F32)
    zeros = jnp.zeros((PROJ_TM, LANES), F32)
    tabs_ctx = (ones, zeros, zeros, ones, zeros, zeros)

    x2d = x.reshape(N_TOK, D_MODEL)
    tiles_per_batch = SEQ // PROJ_TM
    k_mla, vt_mla, k_swa, vt_swa, q_mla, q_swa, sig_a, sig_b = _proj(
        x2d, mod3, lambda i: i // tiles_per_batch, lambda i: i % tiles_per_batch,
        tabs_lat, consts, q_consts, win, True, "proj_latent")
    kc_mla, vct_mla, kc_swa, vct_swa = _proj(
        ctx.reshape(BATCH * CTX_LEN, D_MODEL), mod3, lambda i: BATCH, lambda i: 0,
        tabs_ctx, consts, q_consts, win[:, :C_KVEND], False, "proj_ctx")

    o_a = _mla_attn(q_mla, k_mla, vt_mla, kc_mla, vct_mla)
    o_b = _swa_attn(swa_sink[0], q_swa, k_swa, vt_swa, kc_swa, vct_swa)

    x_new, h2, pos, top_w, seg = _merge(
        o_a, o_b, sig_a, sig_b, x2d, mod3,
        w_branch_a[0].astype(BF16), w_branch_b[0].astype(BF16), w_out[0].astype(BF16),
        norm2_g[0][None], w_router[0].T, b_router[0][:, None])

    seg3 = seg.reshape(N_TILES, SUBLANES, LANES)
    lstart, rows, gcarry = (seg3[:, r, :N_EXPERTS] for r in range(3))
    total = gcarry[-1] + rows[-1]
    padded = (total + MOE_BLK - 1) // MOE_BLK * MOE_BLK
    pends = jnp.cumsum(padded).astype(I32)
    pstarts = pends - padded
    nused = (pends[-1] // MOE_BLK).astype(I32)[None]
    blk_start = jnp.arange(N_BLK, dtype=I32) * MOE_BLK
    block_expert = jnp.minimum(
        jnp.sum((blk_start[:, None] >= pends[None, :]).astype(I32), axis=1), N_EXPERTS - 1)
    assert LROWS <= 1 << CH_SRC_BITS and N_SLOTS < 1 << (31 - CH_SRC_BITS) and SEG_TOTAL < SEG_W
    units = (rows // SEG_ALIGN)[:, None, :]
    cls = jnp.arange(len(SEG_BITS), dtype=I32)[None, :, None]
    has = (units >> cls) & 1
    done = ((units >> (cls + 1)) << (cls + 1)) * SEG_ALIGN
    word = ((pstarts[None, None, :] + gcarry[:, None, :] + done) << CH_SRC_BITS) + lstart[:, None, :] + done
    place = jnp.cumsum(has, axis=2) - has
    lists = jnp.sum(jnp.where((place[..., None] == jnp.arange(N_EXPERTS, dtype=I32)) & (has[..., None] == 1),
                              word[..., None], 0), axis=2)
    segtab = jnp.concatenate(
        [lists.reshape(N_TILES, SEG_COUNT), jnp.sum(has, axis=2), jnp.sum(rows, axis=1, keepdims=True),
         jnp.zeros((N_TILES, SEG_W - SEG_TOTAL - 1), I32)], axis=1).reshape(-1).astype(I32)

    xs = _dispatch(pends, padded.astype(I32), segtab, pos, h2)
    bgu = b_gate_up[0]
    eo = _experts(block_expert, nused, xs, w_gate_up[0], bgu[:, None, 0::2], bgu[:, None, 1::2],
                  w_down[0], b_down[0][:, None, :])
    out = _combine(segtab, eo, x_new, mod3, pos.T, top_w.T)
    return out.reshape(BATCH, SEQ, D_MODEL)
```

```python
import functools

import jax
import jax.numpy as jnp
from jax import lax
from jax.experimental import pallas as pl
from jax.experimental.pallas import tpu as pltpu

F32 = jnp.float32
BF16 = jnp.bfloat16
I32 = jnp.int32
U32 = jnp.uint32

LANES = 128
SUBLANES = 8

D_MODEL = 1024
BATCH = 8
SEQ = 4096
CTX_LEN = 256
GRID_W = 64
ROPE_THETA = 10000.0
NORM_EPS = 1e-6
N_MOD = 6
NEG_INF = -1e30
LOG2E = 1.4426950408889634

MLA_HEADS = 8
MLA_NOPE = 64
MLA_ROPE = 32
MLA_QK = MLA_NOPE + MLA_ROPE
MLA_V = 64
MLA_Q_RANK = 384
MLA_KV_RANK = 256

SWA_HEADS = 8
SWA_KV_HEADS = 2
SWA_GROUP = SWA_HEADS // SWA_KV_HEADS
SWA_HD = 64
WINDOW = 128

N_EXPERTS = 32
TOP_K = 4
D_EXPERT = D_MODEL
SWIGLU_LIMIT = 7.0
SWIGLU_ALPHA = 1.702

N_TOK = BATCH * SEQ
MOD_ROWS = 16

ADA_TN = 1536
PROJ_TM = 512
MLA_TQ = 256
SWA_TQ = 256
SWA_KW = SWA_TQ + 2 * WINDOW
MERGE_TM = 256
MERGE_PARTS = 2
MOE_BLK = 512
SEG_ALIGN = SUBLANES
N_TILES = N_TOK // MERGE_TM
LROWS = -(-(MERGE_TM * TOP_K + N_EXPERTS * (SEG_ALIGN - 1)) // LANES) * LANES
SEG_BITS = tuple(range(SEG_ALIGN.bit_length() - 1, (MERGE_TM * TOP_K).bit_length()))
SEG_COUNT = len(SEG_BITS) * N_EXPERTS
SEG_TOTAL = SEG_COUNT + len(SEG_BITS)
SEG_W = 512
CH_SRC_BITS = 11
MOVE_SLOTS = 2
N_SLOTS = -(-(N_TOK * TOP_K + N_TILES * N_EXPERTS * (SEG_ALIGN - 1) + N_EXPERTS * (MOE_BLK - 1))
            // MOE_BLK) * MOE_BLK
N_BLK = N_SLOTS // MOE_BLK

PACK_W = D_MODEL // 2

VMEM_LIMIT = 56 * 1024 * 1024

C_KVLAT = 0
C_KSWA = C_KVLAT + MLA_KV_RANK
C_KROPE = C_KSWA + SWA_KV_HEADS * SWA_HD
C_KVEND = C_KROPE + LANES
C_QLAT = C_KVEND
C_QSWA = C_QLAT + MLA_Q_RANK
C_GA = C_QSWA + SWA_HEADS * SWA_HD
C_GB = C_GA + D_MODEL
C_END = C_GB + D_MODEL


def _mm(a, b):
    return jnp.dot(a.astype(BF16), b.astype(BF16), preferred_element_type=F32)


def _mm_nt(a, b):
    return lax.dot_general(a.astype(BF16), b.astype(BF16), (((1,), (1,)), ((), ())),
                           preferred_element_type=F32)


def _mm_tn(a, b):
    return lax.dot_general(a.astype(BF16), b.astype(BF16), (((0,), (0,)), ((), ())),
                           preferred_element_type=F32)


def _split(a):
    hi = a.astype(BF16)
    lo = (a - hi.astype(F32)).astype(BF16)
    return hi, lo


def _pack_rows(a, already_bf16=False):
    half = a.shape[1] // 2
    rnd = (lambda v: v) if already_bf16 else (lambda v: v.astype(BF16).astype(F32))
    bits = lambda v: lax.bitcast_convert_type(rnd(v), U32)
    return bits(a[:, half:]) | (bits(a[:, :half]) >> 16)


def _unpack_rows(p):
    lo = lax.bitcast_convert_type(p << 16, F32)
    hi = lax.bitcast_convert_type(p & jnp.uint32(0xFFFF0000), F32)
    return jnp.concatenate([lo, hi], axis=1).astype(BF16)


def _rms(x):
    return x * lax.rsqrt(jnp.mean(x * x, axis=-1, keepdims=True) + NORM_EPS)


def _ada_kernel(c_ref, w_ref, b_ref, o_ref):
    c = c_ref[...]
    s = c * jax.nn.sigmoid(c)
    shi, slo = _split(s)
    whi, wlo = _split(w_ref[...])
    acc = _mm(shi, whi) + _mm(slo, whi) + _mm(shi, wlo)
    o_ref[...] = acc + b_ref[...]


def _ada(cond, w_ada, b_ada):
    n = w_ada.shape[1]
    return pl.pallas_call(
        _ada_kernel,
        grid=(n // ADA_TN,),
        in_specs=[
            pl.BlockSpec((MOD_ROWS, D_MODEL), lambda j: (0, 0)),
            pl.BlockSpec((D_MODEL, ADA_TN), lambda j: (0, j)),
            pl.BlockSpec((1, ADA_TN), lambda j: (0, j)),
        ],
        out_specs=pl.BlockSpec((MOD_ROWS, ADA_TN), lambda j: (0, j)),
        out_shape=jax.ShapeDtypeStruct((MOD_ROWS, n), F32),
        compiler_params=pltpu.CompilerParams(
            dimension_semantics=("arbitrary",), vmem_limit_bytes=VMEM_LIMIT),
        name="ada",
    )(cond, w_ada, b_ada)


def _head_norm(xraw, ind_ref, indt_ref, inv_dim):
    w = xraw.shape[1]
    ss = _mm(xraw * xraw, ind_ref[0:w, :])
    r = lax.rsqrt(ss * inv_dim + NORM_EPS)
    scale = _mm(jnp.concatenate(_split(r), axis=1), indt_ref[:, 0:w])
    return xraw * scale


def _rope(xh, cos, sa, sb, quarter):
    return (xh * cos + pltpu.roll(xh, LANES - quarter, 1) * sa
            + pltpu.roll(xh, quarter, 1) * sb)


def _proj_kernel(*refs, with_q):
    (x_ref, sh_ref, sc_ref, g1_ref, win_ref, cm_ref, sam_ref, sbm_ref, cs_ref, sas_ref, sbs_ref,
     gkva_ref, wkn_ref, wvt_ref, wvst_ref, gk_ref, gks_ref, ind_ref, indt_ref,
     ind64_ref, indt64_ref) = refs[:21]
    if with_q:
        gqa_ref, wqup_ref, gq_ref, gqs_ref = refs[21:25]
        kmla_ref, vmla_ref, kswa_ref, vswa_ref, qmla_ref, qswa_ref, siga_ref, sigb_ref = refs[25:]
    else:
        kmla_ref, vmla_ref, kswa_ref, vswa_ref = refs[21:]

    x = x_ref[...]
    h = _rms(x) * g1_ref[...] * (1.0 + sc_ref[...]) + sh_ref[...]
    y = _mm(h, win_ref[...])

    cm, sam, sbm = cm_ref[...], sam_ref[...], sbm_ref[...]
    cs, sas, sbs = cs_ref[...], sas_ref[...], sbs_ref[...]

    kvn = _rms(y[:, C_KVLAT:C_KVLAT + MLA_KV_RANK]) * gkva_ref[...]
    kn = _mm(kvn, wkn_ref[...])
    vmla_ref[...] = _mm_nt(wvt_ref[...], kvn).astype(BF16)
    vswa_ref[...] = _mm_nt(wvst_ref[...], h).astype(BF16)
    kr = y[:, C_KROPE:C_KROPE + LANES]
    kraw = kn + jnp.concatenate([kr] * MLA_HEADS, axis=1)
    kfull = _head_norm(kraw, ind_ref, indt_ref, 1.0 / MLA_QK) * gk_ref[...]
    for hd in range(MLA_HEADS):
        sl = slice(hd * LANES, (hd + 1) * LANES)
        kmla_ref[:, sl] = _rope(kfull[:, sl], cm, sam, sbm, MLA_ROPE // 4).astype(BF16)

    ks = _head_norm(y[:, C_KSWA:C_KSWA + SWA_KV_HEADS * SWA_HD], ind64_ref, indt64_ref,
                    1.0 / SWA_HD) * gks_ref[...]
    for grp in range(SWA_KV_HEADS * SWA_HD // LANES):
        sl = slice(grp * LANES, (grp + 1) * LANES)
        kswa_ref[:, sl] = _rope(ks[:, sl], cs, sas, sbs, SWA_HD // 4).astype(BF16)

    if with_q:
        qn = _rms(y[:, C_QLAT:C_QLAT + MLA_Q_RANK]) * gqa_ref[...]
        qraw = _mm(qn, wqup_ref[...])
        qf = _head_norm(qraw, ind_ref, indt_ref, 1.0 / MLA_QK) * gq_ref[...]
        for hd in range(MLA_HEADS):
            sl = slice(hd * LANES, (hd + 1) * LANES)
            qmla_ref[:, sl] = _rope(qf[:, sl], cm, sam, sbm, MLA_ROPE // 4).astype(BF16)
        qs = _head_norm(y[:, C_QSWA:C_QSWA + SWA_HEADS * SWA_HD], ind64_ref, indt64_ref,
                        1.0 / SWA_HD) * gqs_ref[...]
        for grp in range(SWA_HEADS * SWA_HD // LANES):
            sl = slice(grp * LANES, (grp + 1) * LANES)
            qswa_ref[:, sl] = _rope(qs[:, sl], cs, sas, sbs, SWA_HD // 4).astype(BF16)
        siga_ref[...] = jax.nn.sigmoid(y[:, C_GA:C_GA + D_MODEL]).astype(BF16)
        sigb_ref[...] = jax.nn.sigmoid(y[:, C_GB:C_GB + D_MODEL]).astype(BF16)


def _proj(x2d, mod3, mod_row_fn, tab_row_fn, tabs, consts, q_consts, win, with_q, name):
    rows = x2d.shape[0]
    tm = PROJ_TM
    const = lambda shape: pl.BlockSpec(shape, lambda i: (0,) * len(shape))
    tab = pl.BlockSpec((tm, LANES), lambda i: (tab_row_fn(i), 0))
    g1, rest = consts[0], consts[1:]
    in_specs = [
        pl.BlockSpec((tm, D_MODEL), lambda i: (i, 0)),
        pl.BlockSpec((None, 1, D_MODEL), lambda i: (mod_row_fn(i), 0, 0)),
        pl.BlockSpec((None, 1, D_MODEL), lambda i: (mod_row_fn(i), 0, 1)),
        const(g1.shape),
        pl.BlockSpec(win.shape, lambda i: (0, 0), pipeline_mode=pl.Buffered(1)),
        tab, tab, tab, tab, tab, tab,
    ] + [const(c.shape) for c in rest]
    args = [x2d, mod3, mod3, g1, win] + list(tabs) + list(rest)
    outs = [(MLA_HEADS * LANES, False), (MLA_HEADS * MLA_V, True),
            (SWA_KV_HEADS * SWA_HD, False), (SWA_KV_HEADS * SWA_HD, True)]
    if with_q:
        in_specs += [const(c.shape) for c in q_consts]
        args += list(q_consts)
        outs += [(MLA_HEADS * LANES, False), (SWA_HEADS * SWA_HD, False), (D_MODEL, False), (D_MODEL, False)]
    return pl.pallas_call(
        functools.partial(_proj_kernel, with_q=with_q),
        grid=(rows // tm,),
        in_specs=in_specs,
        out_specs=[pl.BlockSpec((w, tm), lambda i: (0, i)) if t else pl.BlockSpec((tm, w), lambda i: (i, 0))
                   for w, t in outs],
        out_shape=[jax.ShapeDtypeStruct((w, rows) if t else (rows, w), BF16) for w, t in outs],
        compiler_params=pltpu.CompilerParams(
            dimension_semantics=("arbitrary",), vmem_limit_bytes=VMEM_LIMIT),
        name=name,
    )(*args)


def _mla_step(q_ref, kl_ref, vlt_ref, kc_ref, vct_ref, o_ref, cur, prev):
    s_lat_c, s_ctx_c, m_c = cur
    s_lat_p, s_ctx_p, m_p = prev
    outs = []
    for hh in range(2):
        sl = slice(hh * LANES, (hh + 1) * LANES)
        vrows = slice(hh * MLA_V, (hh + 1) * MLA_V)
        q = q_ref[:, sl]
        s1 = _mm_nt(kl_ref[:, sl], q)
        s2 = _mm_nt(kc_ref[:, sl], q)
        s_lat_c[hh] = s1
        s_ctx_c[hh] = s2
        m_c[hh] = jnp.maximum(jnp.max(s1, axis=0, keepdims=True),
                              jnp.max(s2, axis=0, keepdims=True))

        m = m_p[hh]
        p1 = jnp.exp2(s_lat_p[hh] - m)
        p2 = jnp.exp2(s_ctx_p[hh] - m)
        l = jnp.sum(p1, axis=0, keepdims=True) + jnp.sum(p2, axis=0, keepdims=True)
        o = _mm(vlt_ref[vrows, :], p1) + _mm(vct_ref[vrows, :], p2)
        outs.append(o / l)
    o_ref[...] = jnp.concatenate(outs, axis=0).astype(BF16)


def _mla_kernel(q_ref, kl_ref, vl_ref, kc_ref, vc_ref, o_ref, *scratch):
    set0, set1 = scratch[:3], scratch[3:]
    i = pl.program_id(0)

    @pl.when(i == 0)
    def _():
        for ref in set1:
            ref[...] = jnp.zeros(ref.shape, F32)

    @pl.when(lax.rem(i, 2) == 0)
    def _():
        _mla_step(q_ref, kl_ref, vl_ref, kc_ref, vc_ref, o_ref, set0, set1)

    @pl.when(lax.rem(i, 2) == 1)
    def _():
        _mla_step(q_ref, kl_ref, vl_ref, kc_ref, vc_ref, o_ref, set1, set0)


def _mla_attn(q, k_lat, vt_lat, k_ctx, vt_ctx):
    tq = MLA_TQ
    nq = SEQ // tq
    pairs = MLA_HEADS // 2
    n_items = BATCH * pairs * nq

    def item(s):
        return s // (pairs * nq), (s // nq) % pairs, s % nq

    def scored(s):
        return item(jnp.minimum(s, n_items - 1))

    def finished(s):
        return item(jnp.maximum(s - 1, 0))

    def q_map(s):
        b, hp, i = scored(s)
        return b * nq + i, hp

    def k_map(s):
        b, hp, _ = scored(s)
        return b, hp

    def v_map(s):
        b, hp, _ = finished(s)
        return hp, b

    def o_map(s):
        b, hp, i = finished(s)
        return hp, b * nq + i

    return pl.pallas_call(
        _mla_kernel,
        grid=(n_items + 1,),
        in_specs=[
            pl.BlockSpec((tq, 2 * LANES), q_map),
            pl.BlockSpec((SEQ, 2 * LANES), k_map),
            pl.BlockSpec((2 * MLA_V, SEQ), v_map),
            pl.BlockSpec((CTX_LEN, 2 * LANES), k_map),
            pl.BlockSpec((2 * MLA_V, CTX_LEN), v_map),
        ],
        out_specs=pl.BlockSpec((2 * MLA_V, tq), o_map),
        out_shape=jax.ShapeDtypeStruct((MLA_HEADS * MLA_V, N_TOK), BF16),
        scratch_shapes=2 * [pltpu.VMEM((2, SEQ, tq), F32),
                            pltpu.VMEM((2, CTX_LEN, tq), F32),
                            pltpu.VMEM((2, 1, tq), F32)],
        compiler_params=pltpu.CompilerParams(
            dimension_semantics=("arbitrary",),
            vmem_limit_bytes=VMEM_LIMIT),
        name="mla_attn",
    )(q, k_lat, vt_lat, k_ctx, vt_ctx)


def _swa_window_start(i):
    return jnp.clip(i * SWA_TQ - WINDOW, 0, SEQ - SWA_KW)


def _swa_step(i_cur, i_prev, sink_ref, q_ref, k_ref, vt_ref, kc_ref, vct_ref, bias_ref, o_ref, cur, prev):
    tq = SWA_TQ
    sb_c, sc_c, m_c = cur
    sb_p, sc_p, m_p = prev
    kstart = pl.multiple_of(_swa_window_start(i_cur), LANES)
    kwin = k_ref[pl.ds(kstart, SWA_KW), :].astype(F32)
    vstart = pl.multiple_of(_swa_window_start(i_prev), LANES)
    vtwin = vt_ref[:, pl.ds(vstart, SWA_KW)]
    kc = kc_ref[...].astype(F32)
    vct = vct_ref[...]
    bias = bias_ref[...]

    def half_keys(k, kh):
        lane = lax.broadcasted_iota(I32, k.shape, 1)
        own = jnp.where((lane >= kh * SWA_HD) & (lane < (kh + 1) * SWA_HD), k, 0.0)
        other = pltpu.roll(own, SWA_HD, 1)
        return (own, other) if kh == 0 else (other, own)

    order = (0, 2, 1, 3)
    for kh in range(SWA_KV_HEADS):
        vrows = slice(kh * SWA_HD, (kh + 1) * SWA_HD)
        grp0 = kh * SWA_GROUP // 2
        q2 = jnp.concatenate([q_ref[:, (grp0 + c) * LANES:(grp0 + c + 1) * LANES] for c in range(2)],
                             axis=0)
        k_lo, k_hi = half_keys(kwin, kh)
        c_lo, c_hi = half_keys(kc, kh)
        sb = jnp.concatenate([_mm_nt(k_lo, q2), _mm_nt(k_hi, q2)], axis=1) + bias
        sc = jnp.concatenate([_mm_nt(c_lo, q2), _mm_nt(c_hi, q2)], axis=1)
        sink = jnp.concatenate(
            [jnp.full((1, tq), sink_ref[kh * SWA_GROUP + g] * LOG2E, F32) for g in order], axis=1)
        sb_c[kh] = sb
        sc_c[kh] = sc
        m_c[kh] = jnp.maximum(jnp.maximum(jnp.max(sb, axis=0, keepdims=True),
                                          jnp.max(sc, axis=0, keepdims=True)), sink)

        m = m_p[kh]
        pb = jnp.exp2(sb_p[kh] - m)
        pc = jnp.exp2(sc_p[kh] - m)
        l = (jnp.sum(pb, axis=0, keepdims=True) + jnp.sum(pc, axis=0, keepdims=True)
             + jnp.exp2(sink - m))
        o = (_mm(vtwin[vrows, :], pb) + _mm(vct[vrows, :], pc)) / l
        for blk, g in enumerate(order):
            hd = kh * SWA_GROUP + g
            o_ref[hd * SWA_HD:(hd + 1) * SWA_HD, :] = o[:, blk * tq:(blk + 1) * tq].astype(BF16)


def _swa_kernel(sink_ref, q_ref, k_ref, vt_ref, kc_ref, vct_ref, bias_ref, o_ref, *scratch):
    set0, set1 = scratch[:3], scratch[3:]
    nq = SEQ // SWA_TQ
    s = pl.program_id(0)
    i_cur = lax.rem(jnp.minimum(s, BATCH * nq - 1), nq)
    i_prev = lax.rem(jnp.maximum(s - 1, 0), nq)

    @pl.when(s == 0)
    def _():
        for ref in set1:
            ref[...] = jnp.zeros(ref.shape, F32)

    @pl.when(lax.rem(s, 2) == 0)
    def _():
        _swa_step(i_cur, i_prev, sink_ref, q_ref, k_ref, vt_ref, kc_ref, vct_ref, bias_ref, o_ref, set0, set1)

    @pl.when(lax.rem(s, 2) == 1)
    def _():
        _swa_step(i_cur, i_prev, sink_ref, q_ref, k_ref, vt_ref, kc_ref, vct_ref, bias_ref, o_ref, set1, set0)


def _swa_attn(sink, q, k_lat, vt_lat, k_ctx, vt_ctx):
    tq = SWA_TQ
    nq = SEQ // tq
    offsets = (0, WINDOW, SWA_KW - tq)
    assert all(i * tq - min(max(i * tq - WINDOW, 0), SEQ - SWA_KW) == offsets[(i > 0) + (i == nq - 1)]
               for i in range(nq))
    key = jnp.arange(SWA_KW, dtype=I32)[:, None]
    qry = (jnp.arange(SWA_GROUP * tq, dtype=I32) % tq)[None, :]
    bias = jnp.stack([jnp.where(jnp.abs(d + qry - key) <= WINDOW, 0.0, NEG_INF) for d in offsets]).astype(F32)
    case = lambda i: (i > 0).astype(I32) + (i == nq - 1).astype(I32)
    n_items = BATCH * nq
    scored = lambda s: jnp.minimum(s, n_items - 1)
    finished = lambda s: jnp.maximum(s - 1, 0)
    cols = SWA_GROUP * tq
    return pl.pallas_call(
        _swa_kernel,
        grid=(n_items + 1,),
        in_specs=[
            pl.BlockSpec(memory_space=pltpu.SMEM),
            pl.BlockSpec((tq, SWA_HEADS * SWA_HD), lambda s: (scored(s), 0)),
            pl.BlockSpec((SEQ, SWA_KV_HEADS * SWA_HD), lambda s: (scored(s) // nq, 0)),
            pl.BlockSpec((SWA_KV_HEADS * SWA_HD, SEQ), lambda s: (0, finished(s) // nq)),
            pl.BlockSpec((CTX_LEN, SWA_KV_HEADS * SWA_HD), lambda s: (scored(s) // nq, 0)),
            pl.BlockSpec((SWA_KV_HEADS * SWA_HD, CTX_LEN), lambda s: (0, finished(s) // nq)),
            pl.BlockSpec((None, SWA_KW, cols), lambda s: (case(scored(s) % nq), 0, 0)),
        ],
        out_specs=pl.BlockSpec((SWA_HEADS * SWA_HD, tq), lambda s: (0, finished(s))),
        out_shape=jax.ShapeDtypeStruct((SWA_HEADS * SWA_HD, N_TOK), BF16),
        scratch_shapes=2 * [pltpu.VMEM((SWA_KV_HEADS, SWA_KW, cols), F32),
                            pltpu.VMEM((SWA_KV_HEADS, CTX_LEN, cols), F32),
                            pltpu.VMEM((SWA_KV_HEADS, 1, cols), F32)],
        compiler_params=pltpu.CompilerParams(
            dimension_semantics=("arbitrary",), vmem_limit_bytes=VMEM_LIMIT),
        name="swa_attn",
    )(sink, q, k_lat, vt_lat, k_ctx, vt_ctx, bias)


def _merge_kernel(oa_ref, ob_ref, sa_ref, sb_ref, x_ref, g1_ref, sh2_ref, sc2_ref, wba_ref, wbb_ref,
                  wout_ref, g2_ref, wr_ref, br_ref, utri_ref, ltri_ref,
                  xnew_ref, h2_ref, pos_ref, wts_ref, seg_ref, carry_ref, lg0_ref, lg1_ref):
    s = pl.program_id(0)

    @pl.when(s == 0)
    def _():
        carry_ref[...] = jnp.zeros_like(carry_ref)
        lg1_ref[...] = jnp.zeros_like(lg1_ref)

    def step(lg_cur, lg_prev):
        _merge_project(oa_ref, ob_ref, sa_ref, sb_ref, x_ref, g1_ref, sh2_ref, sc2_ref, wba_ref, wbb_ref,
                       wout_ref, g2_ref, wr_ref, br_ref, xnew_ref, h2_ref, lg_cur)
        _merge_route(s > 0, lg_prev, utri_ref, ltri_ref, pos_ref, wts_ref, seg_ref, carry_ref)

    @pl.when(lax.rem(s, 2) == 0)
    def _():
        step(lg0_ref, lg1_ref)

    @pl.when(lax.rem(s, 2) == 1)
    def _():
        step(lg1_ref, lg0_ref)


def _merge_project(oa_ref, ob_ref, sa_ref, sb_ref, x_ref, g1_ref, sh2_ref, sc2_ref, wba_ref, wbb_ref,
                   wout_ref, g2_ref, wr_ref, br_ref, xnew_ref, h2_ref, lg_ref):
    whi, wlo = _split(wr_ref[...])
    half = MERGE_TM // MERGE_PARTS
    for part in range(MERGE_PARTS):
        tok = slice(part * half, (part + 1) * half)
        ya = _mm_tn(oa_ref[:, tok], wba_ref[...])
        yb = _mm_tn(ob_ref[:, tok], wbb_ref[...])
        y = sa_ref[tok, :].astype(F32) * ya + sb_ref[tok, :].astype(F32) * yb
        z = _mm(y, wout_ref[...])
        xn = x_ref[tok, :] + g1_ref[...] * z
        xnew_ref[tok, :] = xn
        h2 = _rms(xn) * g2_ref[...] * (1.0 + sc2_ref[...]) + sh2_ref[...]
        h2_ref[tok, :] = h2.astype(BF16)

        hhi, hlo = _split(h2)
        lg_ref[:, tok] = _mm_nt(whi, hhi) + _mm_nt(whi, hlo) + _mm_nt(wlo, hhi) + br_ref[...]


def _merge_route(real, lg_ref, utri_ref, ltri_ref, pos_ref, wts_ref, seg_ref, carry_ref):
    lg = lg_ref[...]
    eiota = lax.broadcasted_iota(I32, lg.shape, 0).astype(F32)
    vals, onehots = [], []
    for k in range(TOP_K):
        m = jnp.max(lg, axis=0, keepdims=True)
        ik = jnp.min(jnp.where(lg == m, eiota, float(N_EXPERTS)), axis=0, keepdims=True)
        hit = eiota == ik
        vals.append(m)
        onehots.append(hit.astype(F32))
        lg = jnp.where(hit, -jnp.inf, lg)
    es = [jnp.exp(v - vals[0]) for v in vals]
    tot = es[0] + es[1] + es[2] + es[3]
    for k in range(TOP_K):
        wts_ref[k:k + 1, :] = es[k] / tot

    tots = [jnp.sum(oh, axis=1, keepdims=True) for oh in onehots]
    n = tots[0] + tots[1] + tots[2] + tots[3]
    m_al = jnp.floor((n + (SEG_ALIGN - 1)) * (1.0 / SEG_ALIGN))
    m_b = jnp.broadcast_to(m_al, (N_EXPERTS, LANES))
    lstart = _mm(ltri_ref[...], m_b) * float(SEG_ALIGN)
    off = jnp.zeros_like(n)
    for k in range(TOP_K):
        prefix = _mm(onehots[k], utri_ref[...])
        lp = jnp.sum(onehots[k] * (lstart[:, 0:1] + off + prefix), axis=0, keepdims=True)
        pos_ref[k:k + 1, :] = lp.astype(I32)
        off = off + tots[k]
    eye = (lax.broadcasted_iota(I32, (N_EXPERTS, LANES), 0)
           == lax.broadcasted_iota(I32, (N_EXPERTS, LANES), 1)).astype(F32)
    to_row = lambda v: jnp.sum(v * eye, axis=0, keepdims=True).astype(I32)
    carry = carry_ref[...]
    seg_ref[...] = jnp.concatenate(
        [to_row(lstart), to_row(m_b * float(SEG_ALIGN)), to_row(carry),
         jnp.zeros((SUBLANES - 3, LANES), I32)], axis=0)
    carry_ref[...] = carry + jnp.where(real, m_b * float(SEG_ALIGN), 0.0)


def _merge(o_a, o_b, sig_a, sig_b, x2d, mod3, w_ba, w_bb, w_out, g2, w_rt, b_r):
    tm = MERGE_TM
    tiles_per_batch = SEQ // tm
    proj = lambda s: jnp.minimum(s, N_TILES - 1)
    routed = lambda s: jnp.maximum(s - 1, 0)
    const = lambda shape: pl.BlockSpec(shape, lambda s: (0,) * len(shape))
    modspec = lambda j: pl.BlockSpec((None, 1, D_MODEL), lambda s: (proj(s) // tiles_per_batch, 0, j))
    row = lambda w: pl.BlockSpec((tm, w), lambda s: (proj(s), 0))
    utri = (jnp.arange(tm)[:, None] < jnp.arange(tm)[None, :]).astype(BF16)
    ltri = (jnp.arange(N_EXPERTS)[None, :] < jnp.arange(N_EXPERTS)[:, None]).astype(BF16)
    return pl.pallas_call(
        _merge_kernel,
        grid=(N_TILES + 1,),
        in_specs=[pl.BlockSpec((512, tm), lambda s: (0, proj(s))), pl.BlockSpec((512, tm), lambda s: (0, proj(s))),
                  row(D_MODEL), row(D_MODEL), row(D_MODEL),
                  modspec(2), modspec(3), modspec(4),
                  const(w_ba.shape), const(w_bb.shape), const(w_out.shape), const(g2.shape),
                  const(w_rt.shape), const(b_r.shape), const(utri.shape), const(ltri.shape)],
        out_specs=[row(D_MODEL), row(D_MODEL),
                   pl.BlockSpec((TOP_K, tm), lambda s: (0, routed(s))),
                   pl.BlockSpec((TOP_K, tm), lambda s: (0, routed(s))),
                   pl.BlockSpec((SUBLANES, LANES), lambda s: (routed(s), 0))],
        out_shape=[jax.ShapeDtypeStruct((N_TOK, D_MODEL), F32),
                   jax.ShapeDtypeStruct((N_TOK, D_MODEL), BF16),
                   jax.ShapeDtypeStruct((TOP_K, N_TOK), I32),
                   jax.ShapeDtypeStruct((TOP_K, N_TOK), F32),
                   jax.ShapeDtypeStruct((N_TILES * SUBLANES, LANES), I32)],
        scratch_shapes=[pltpu.VMEM((N_EXPERTS, LANES), F32),
                        pltpu.VMEM((N_EXPERTS, tm), F32), pltpu.VMEM((N_EXPERTS, tm), F32)],
        compiler_params=pltpu.CompilerParams(
            dimension_semantics=("arbitrary",), vmem_limit_bytes=VMEM_LIMIT),
        name="merge",
    )(o_a, o_b, sig_a, sig_b, x2d, mod3, mod3, mod3, w_ba, w_bb, w_out, g2, w_rt, b_r, utri, ltri)


def _chunk_copy(hbm, hbm_row, buf, buf_row, size, to_hbm, sem):
    vm = buf.at[pl.ds(pl.multiple_of(buf_row, SEG_ALIGN), size)]
    hb = hbm.at[pl.ds(pl.multiple_of(hbm_row, SEG_ALIGN), size)]
    return pltpu.make_async_copy(vm, hb, sem) if to_hbm else pltpu.make_async_copy(hb, vm, sem)


def _seg_start(seg_smem, tile, hbm, buf, to_hbm, sem):
    base = tile * SEG_W
    for c, b in enumerate(SEG_BITS):
        def issue(t, carry, c=c, size=1 << b):
            word = seg_smem[base + c * N_EXPERTS + t]
            _chunk_copy(hbm, word >> CH_SRC_BITS, buf, word & ((1 << CH_SRC_BITS) - 1),
                        size, to_hbm, sem).start()
            return carry

        lax.fori_loop(0, seg_smem[base + SEG_COUNT + c], issue, 0)


def _seg_wait(seg_smem, tile, hbm, buf, to_hbm, sem):
    assert LROWS < 2 << SEG_BITS[-1]
    total = seg_smem[tile * SEG_W + SEG_TOTAL]
    for b in SEG_BITS:
        size = 1 << b

        @pl.when((total & size) != 0)
        def _():
            _chunk_copy(hbm, 0, buf, 0, size, to_hbm, sem).wait()


def _dispatch_kernel(pend_ref, padded_ref, seg_ref, pos_ref, h2_ref, xs_ref, staged, sems):
    j = pl.program_id(0)
    slot = lax.rem(j, MOVE_SLOTS)

    @pl.when(j == 0)
    def _():
        staged[0, 0:MOE_BLK, :] = jnp.zeros((MOE_BLK, PACK_W), U32)

        def tail(action):
            def body(e, carry):
                @pl.when(padded_ref[e] > 0)
                def _():
                    start = pl.multiple_of(pend_ref[e] - MOE_BLK, MOE_BLK)
                    cp = pltpu.make_async_copy(staged.at[0, pl.ds(0, MOE_BLK)],
                                               xs_ref.at[pl.ds(start, MOE_BLK)], sems.at[0])
                    getattr(cp, action)()
                return carry
            lax.fori_loop(0, N_EXPERTS, body, 0)

        def unused(action):
            def body(b, carry):
                cp = pltpu.make_async_copy(
                    staged.at[0, pl.ds(0, MOE_BLK)],
                    xs_ref.at[pl.ds(pl.multiple_of(b * MOE_BLK, MOE_BLK), MOE_BLK)], sems.at[0])
                getattr(cp, action)()
                return carry
            lax.fori_loop(pend_ref[N_EXPERTS - 1] // MOE_BLK, N_BLK, body, 0)

        tail("start")
        unused("start")
        tail("wait")
        unused("wait")

    pos = pos_ref[...]
    piota = lax.broadcasted_iota(I32, (LROWS, MERGE_TM), 0)
    hit = piota == pos[0:1, :]
    for k in range(1, TOP_K):
        hit = hit | (piota == pos[k:k + 1, :])
    staged[slot] = _pack_rows(_mm(jnp.where(hit, 1.0, 0.0), h2_ref[...]), already_bf16=True)

    _seg_start(seg_ref, j, xs_ref, staged.at[slot], True, sems.at[1 + slot])

    lag = MOVE_SLOTS - 1

    def wait_tile(t):
        s = lax.rem(t, MOVE_SLOTS)
        _seg_wait(seg_ref, t, xs_ref, staged.at[s], True, sems.at[1 + s])

    @pl.when(j >= lag)
    def _():
        wait_tile(j - lag)

    @pl.when(j == N_TILES - 1)
    def _():
        for back in reversed(range(lag)):
            wait_tile(j - back)


def _dispatch(pends, padded, segtab, pos, h2):
    grid_spec = pltpu.PrefetchScalarGridSpec(
        num_scalar_prefetch=3,
        grid=(N_TILES,),
        in_specs=[
            pl.BlockSpec((TOP_K, MERGE_TM), lambda j, pe, pa, sg: (0, j)),
            pl.BlockSpec((MERGE_TM, D_MODEL), lambda j, pe, pa, sg: (j, 0)),
        ],
        out_specs=pl.BlockSpec(memory_space=pl.ANY),
        scratch_shapes=[pltpu.VMEM((MOVE_SLOTS, LROWS, PACK_W), U32),
                        pltpu.SemaphoreType.DMA((1 + MOVE_SLOTS,))],
    )
    return pl.pallas_call(
        _dispatch_kernel,
        grid_spec=grid_spec,
        out_shape=jax.ShapeDtypeStruct((N_SLOTS, PACK_W), U32),
        compiler_params=pltpu.CompilerParams(
            dimension_semantics=("arbitrary",), vmem_limit_bytes=VMEM_LIMIT),
        name="dispatch",
    )(pends, padded, segtab, pos, h2)


SPLIT_SUB = 256


def _expert_kernel(be_ref, nused_ref, xs_ref, wgu_ref, bg_ref, bl_ref, wd_ref, bd_ref, perm_ref, o_ref,
                   wg_s, wl_s, wd_s):
    i = pl.program_id(0)
    used = i < nused_ref[0]
    fresh = jnp.logical_or(i == 0, be_ref[i] != be_ref[jnp.maximum(i - 1, 0)])

    @pl.when(jnp.logical_and(used, fresh))
    def _():
        half = SPLIT_SUB // 2
        for s in range(2 * D_EXPERT // SPLIT_SUB):
            r = _mm(wgu_ref[:, s * SPLIT_SUB:(s + 1) * SPLIT_SUB], perm_ref[...])
            wg_s[:, s * half:(s + 1) * half] = r[:, :half].astype(BF16)
            wl_s[:, s * half:(s + 1) * half] = r[:, half:].astype(BF16)
        wd_s[...] = wd_ref[...].astype(BF16)

    @pl.when(used)
    def _():
        xb = _unpack_rows(xs_ref[...])
        g = jnp.minimum(_mm(xb, wg_s[...]) + bg_ref[...], SWIGLU_LIMIT)
        l = jnp.clip(_mm(xb, wl_s[...]) + bl_ref[...], -SWIGLU_LIMIT, SWIGLU_LIMIT)
        act = g * jax.nn.sigmoid(SWIGLU_ALPHA * g) * (l + 1.0)
        o_ref[...] = _pack_rows(_mm(act, wd_s[...]) + bd_ref[...])

    @pl.when(jnp.logical_not(used))
    def _():
        o_ref[...] = jnp.zeros_like(o_ref)


def _experts(block_expert, nused, xs, w_gu, bg, bl, w_dn, bd):
    col = jnp.arange(SPLIT_SUB)
    src = jnp.where(col < SPLIT_SUB // 2, 2 * col, 2 * (col - SPLIT_SUB // 2) + 1)
    perm = (jnp.arange(SPLIT_SUB)[:, None] == src[None, :]).astype(BF16)
    blk = lambda i, be, nu: (jnp.minimum(i, nu[0] - 1), 0)
    per_expert = lambda shape: pl.BlockSpec((None,) + shape, lambda i, be, nu: (be[i], 0, 0))
    grid_spec = pltpu.PrefetchScalarGridSpec(
        num_scalar_prefetch=2,
        grid=(N_BLK,),
        in_specs=[pl.BlockSpec((MOE_BLK, PACK_W), blk),
                  per_expert((D_MODEL, 2 * D_EXPERT)), per_expert((1, D_EXPERT)), per_expert((1, D_EXPERT)),
                  per_expert((D_EXPERT, D_MODEL)), per_expert((1, D_MODEL)),
                  pl.BlockSpec((SPLIT_SUB, SPLIT_SUB), lambda i, be, nu: (0, 0))],
        out_specs=pl.BlockSpec((MOE_BLK, PACK_W), lambda i, be, nu: (i, 0)),
        scratch_shapes=[pltpu.VMEM((D_MODEL, D_EXPERT), BF16), pltpu.VMEM((D_MODEL, D_EXPERT), BF16),
                        pltpu.VMEM((D_EXPERT, D_MODEL), BF16)],
    )
    return pl.pallas_call(
        _expert_kernel,
        grid_spec=grid_spec,
        out_shape=jax.ShapeDtypeStruct((N_SLOTS, PACK_W), U32),
        compiler_params=pltpu.CompilerParams(
            dimension_semantics=("arbitrary",), vmem_limit_bytes=VMEM_LIMIT),
        name="experts",
    )(block_expert, nused, xs, w_gu, bg, bl, w_dn, bd, perm)


def _combine_kernel(seg_ref, eo_ref, xnew_ref, g2_ref, pos_ref, wt_ref, o_ref, gbuf, sems):
    j = pl.program_id(0)
    slot = lax.rem(j, MOVE_SLOTS)
    ahead = MOVE_SLOTS - 1

    def fetch(tile):
        sl = lax.rem(tile, MOVE_SLOTS)
        _seg_start(seg_ref, tile, eo_ref, gbuf.at[sl], False, sems.at[sl])

    @pl.when(j == 0)
    def _():
        gbuf[...] = jnp.zeros_like(gbuf)
        for t in range(ahead):
            fetch(jnp.int32(t))

    @pl.when(j + ahead < N_TILES)
    def _():
        fetch(j + ahead)

    _seg_wait(seg_ref, j, eo_ref, gbuf.at[slot], False, sems.at[slot])

    pos = pos_ref[...]
    wt = wt_ref[...]
    lane = lax.broadcasted_iota(I32, (MERGE_TM, LROWS), 1)
    wsel = jnp.zeros((MERGE_TM, LROWS), F32)
    for k in range(TOP_K):
        wsel = jnp.where(lane == pos[:, k:k + 1], wt[:, k:k + 1], wsel)
    acc = _mm(wsel, _unpack_rows(gbuf[slot]))
    o_ref[...] = xnew_ref[...] + g2_ref[...] * acc


def _combine(segtab, eo, x_new, mod3, pos_t, wts_t):
    tiles_per_batch = SEQ // MERGE_TM
    grid_spec = pltpu.PrefetchScalarGridSpec(
        num_scalar_prefetch=1,
        grid=(N_TILES,),
        in_specs=[
            pl.BlockSpec(memory_space=pl.ANY),
            pl.BlockSpec((MERGE_TM, D_MODEL), lambda j, sg: (j, 0)),
            pl.BlockSpec((None, 1, D_MODEL), lambda j, sg: (j // tiles_per_batch, 0, 5)),
            pl.BlockSpec((MERGE_TM, TOP_K), lambda j, sg: (j, 0)),
            pl.BlockSpec((MERGE_TM, TOP_K), lambda j, sg: (j, 0)),
        ],
        out_specs=pl.BlockSpec((MERGE_TM, D_MODEL), lambda j, sg: (j, 0)),
        scratch_shapes=[pltpu.VMEM((MOVE_SLOTS, LROWS, PACK_W), U32),
                        pltpu.SemaphoreType.DMA((MOVE_SLOTS,))],
    )
    return pl.pallas_call(
        _combine_kernel,
        grid_spec=grid_spec,
        out_shape=jax.ShapeDtypeStruct((N_TOK, D_MODEL), F32),
        compiler_params=pltpu.CompilerParams(
            dimension_semantics=("arbitrary",), vmem_limit_bytes=VMEM_LIMIT),
        name="combine",
    )(segtab, eo, x_new, mod3, pos_t, wts_t)


def _pad_heads(w, heads, dim):
    lead = w.shape[:-1]
    w = w.reshape(lead + (heads, dim))
    w = jnp.pad(w, [(0, 0)] * len(lead) + [(0, 0), (0, LANES - dim)])
    return w.reshape(lead + (heads * LANES,))


def _rope_tables(dim, offset, repeat):
    pos = jnp.arange(SEQ, dtype=jnp.int32)
    row = (pos // GRID_W).astype(F32)
    col = (pos % GRID_W).astype(F32)
    q = dim // 4
    freqs = ROPE_THETA ** (-jnp.arange(q, dtype=F32) / q)
    ang_r = row[:, None] * freqs
    ang_c = col[:, None] * freqs
    ang = jnp.concatenate([ang_r, ang_r, ang_c, ang_c], axis=-1)
    cos, sin = jnp.cos(ang), jnp.sin(ang)
    first = (jnp.arange(dim) % (2 * q)) < q
    sa = jnp.where(first, -sin, 0.0)
    sb = jnp.where(first, 0.0, sin)
    pad = lambda t, v: jnp.pad(jnp.tile(t, (1, repeat)), ((0, 0), (offset, LANES - offset - dim * repeat)),
                               constant_values=v)
    return pad(cos, 1.0), pad(sa, 0.0), pad(sb, 0.0)


def kernel(x, c, ctx, c_ctx, w_ada, b_ada, norm1_g, norm2_g, w_in, mla_q_a_g, mla_kv_a_g, w_q_up, w_kv_up,
           mla_q_g, mla_k_g, swa_q_g, swa_k_g, swa_sink, w_branch_a, w_branch_b, w_out, w_router, b_router,
           w_gate_up, b_gate_up, w_down, b_down):
    assert x.shape == (BATCH, SEQ, D_MODEL) and ctx.shape == (BATCH, CTX_LEN, D_MODEL)
    assert w_ada.shape[0] == 1, "single layer"

    cond = jnp.concatenate([c, c_ctx[None], jnp.zeros((MOD_ROWS - BATCH - 1, D_MODEL), F32)], axis=0)
    mod = _ada(cond, w_ada[0], b_ada[0][None])
    mod3 = mod.reshape(MOD_ROWS, 1, N_MOD * D_MODEL)

    wi = w_in[0]
    o1, o2, o3 = MLA_KV_RANK, MLA_KV_RANK + MLA_ROPE, MLA_KV_RANK + MLA_ROPE + SWA_KV_HEADS * SWA_HD
    kv_cols = o3 + SWA_KV_HEADS * SWA_HD
    q1 = kv_cols + MLA_Q_RANK
    q2 = q1 + SWA_HEADS * SWA_HD
    q3 = q2 + D_MODEL
    krope_grp = jnp.pad(wi[:, o1:o2], ((0, 0), (MLA_NOPE, LANES - MLA_QK)))
    win = jnp.concatenate([
        wi[:, 0:o1], wi[:, o2:o3], krope_grp, wi[:, kv_cols:q1], wi[:, q1:q2], wi[:, q2:q3], wi[:, q3:]],
        axis=1).astype(BF16)
    assert win.shape == (D_MODEL, C_END)
    wvst = wi[:, o3:kv_cols].T.astype(BF16)
    wkv = w_kv_up[0].reshape(MLA_KV_RANK, MLA_HEADS, MLA_NOPE + MLA_V)
    wkn = _pad_heads(wkv[:, :, :MLA_NOPE].reshape(MLA_KV_RANK, -1), MLA_HEADS, MLA_NOPE).astype(BF16)
    wvt = wkv[:, :, MLA_NOPE:].reshape(MLA_KV_RANK, MLA_HEADS * MLA_V).T.astype(BF16)
    wqup = _pad_heads(w_q_up[0], MLA_HEADS, MLA_QK).astype(BF16)
    gk_t = _pad_heads(jnp.tile(mla_k_g[0], MLA_HEADS)[None], MLA_HEADS, MLA_QK)
    gq_t = _pad_heads(jnp.tile(mla_q_g[0] * (MLA_QK ** -0.5 * LOG2E), MLA_HEADS)[None], MLA_HEADS, MLA_QK)
    gks_t = jnp.tile(swa_k_g[0], SWA_KV_HEADS)[None]
    gqs_t = jnp.tile(swa_q_g[0] * (SWA_HD ** -0.5 * LOG2E), SWA_HEADS)[None]

    def lane_to_head(n_lanes, width):
        ind = ((jnp.arange(n_lanes) // width)[:, None] == jnp.arange(LANES)[None, :]).astype(BF16)
        return ind, jnp.concatenate([ind.T, ind.T], axis=0)

    ind, indt = lane_to_head(MLA_HEADS * LANES, LANES)
    ind64, indt64 = lane_to_head(SWA_HEADS * SWA_HD, SWA_HD)
    consts = [norm1_g[0][None], mla_kv_a_g[0][None], wkn, wvt, wvst, gk_t, gks_t, ind, indt, ind64, indt64]
    q_consts = [mla_q_a_g[0][None], wqup, gq_t, gqs_t]

    tabs_lat = _rope_tables(MLA_ROPE, MLA_NOPE, 1) + _rope_tables(SWA_HD, 0, LANES // SWA_HD)
    ones = jnp.ones((PROJ_TM, LANES), F32)
    zeros = jnp.zeros((PROJ_TM, LANES), F32)
    tabs_ctx = (ones, zeros, zeros, ones, zeros, zeros)

    x2d = x.reshape(N_TOK, D_MODEL)
    tiles_per_batch = SEQ // PROJ_TM
    k_mla, vt_mla, k_swa, vt_swa, q_mla, q_swa, sig_a, sig_b = _proj(
        x2d, mod3, lambda i: i // tiles_per_batch, lambda i: i % tiles_per_batch,
        tabs_lat, consts, q_consts, win, True, "proj_latent")
    kc_mla, vct_mla, kc_swa, vct_swa = _proj(
        ctx.reshape(BATCH * CTX_LEN, D_MODEL), mod3, lambda i: BATCH, lambda i: 0,
        tabs_ctx, consts, q_consts, win[:, :C_KVEND], False, "proj_ctx")

    o_a = _mla_attn(q_mla, k_mla, vt_mla, kc_mla, vct_mla)
    o_b = _swa_attn(swa_sink[0], q_swa, k_swa, vt_swa, kc_swa, vct_swa)

    x_new, h2, pos, top_w, seg = _merge(
        o_a, o_b, sig_a, sig_b, x2d, mod3,
        w_branch_a[0].astype(BF16), w_branch_b[0].astype(BF16), w_out[0].astype(BF16),
        norm2_g[0][None], w_router[0].T, b_router[0][:, None])

    seg3 = seg.reshape(N_TILES, SUBLANES, LANES)
    lstart, rows, gcarry = (seg3[:, r, :N_EXPERTS] for r in range(3))
    total = gcarry[-1] + rows[-1]
    padded = (total + MOE_BLK - 1) // MOE_BLK * MOE_BLK
    pends = jnp.cumsum(padded).astype(I32)
    pstarts = pends - padded
    nused = (pends[-1] // MOE_BLK).astype(I32)[None]
    blk_start = jnp.arange(N_BLK, dtype=I32) * MOE_BLK
    block_expert = jnp.minimum(
        jnp.sum((blk_start[:, None] >= pends[None, :]).astype(I32), axis=1), N_EXPERTS - 1)
    assert LROWS <= 1 << CH_SRC_BITS and N_SLOTS < 1 << (31 - CH_SRC_BITS) and SEG_TOTAL < SEG_W
    units = (rows // SEG_ALIGN)[:, None, :]
    cls = jnp.arange(len(SEG_BITS), dtype=I32)[None, :, None]
    has = (units >> cls) & 1
    done = ((units >> (cls + 1)) << (cls + 1)) * SEG_ALIGN
    word = ((pstarts[None, None, :] + gcarry[:, None, :] + done) << CH_SRC_BITS) + lstart[:, None, :] + done
    place = jnp.cumsum(has, axis=2) - has
    lists = jnp.sum(jnp.where((place[..., None] == jnp.arange(N_EXPERTS, dtype=I32)) & (has[..., None] == 1),
                              word[..., None], 0), axis=2)
    segtab = jnp.concatenate(
        [lists.reshape(N_TILES, SEG_COUNT), jnp.sum(has, axis=2), jnp.sum(rows, axis=1, keepdims=True),
         jnp.zeros((N_TILES, SEG_W - SEG_TOTAL - 1), I32)], axis=1).reshape(-1).astype(I32)

    xs = _dispatch(pends, padded.astype(I32), segtab, pos, h2)
    bgu = b_gate_up[0]
    eo = _experts(block_expert, nused, xs, w_gate_up[0], bgu[:, None, 0::2], bgu[:, None, 1::2],
                  w_down[0], b_down[0][:, None, :])
    out = _combine(segtab, eo, x_new, mod3, pos.T, top_w.T)
    return out.reshape(BATCH, SEQ, D_MODEL)
```

```python
import functools

import jax
import jax.numpy as jnp
from jax import lax
from jax.experimental import pallas as pl
from jax.experimental.pallas import tpu as pltpu

F32 = jnp.float32
BF16 = jnp.bfloat16
I32 = jnp.int32
U32 = jnp.uint32

LANES = 128
SUBLANES = 8

D_MODEL = 1024
BATCH = 8
SEQ = 4096
CTX_LEN = 256
GRID_W = 64
ROPE_THETA = 10000.0
NORM_EPS = 1e-6
N_MOD = 6
NEG_INF = -1e30
LOG2E = 1.4426950408889634

MLA_HEADS = 8
MLA_NOPE = 64
MLA_ROPE = 32
MLA_QK = MLA_NOPE + MLA_ROPE
MLA_V = 64
MLA_Q_RANK = 384
MLA_KV_RANK = 256

SWA_HEADS = 8
SWA_KV_HEADS = 2
SWA_GROUP = SWA_HEADS // SWA_KV_HEADS
SWA_HD = 64
WINDOW = 128

N_EXPERTS = 32
TOP_K = 4
D_EXPERT = D_MODEL
SWIGLU_LIMIT = 7.0
SWIGLU_ALPHA = 1.702

N_TOK = BATCH * SEQ
MOD_ROWS = 16

ADA_TN = 1536
PROJ_TM = 512
MLA_TQ = 256
SWA_TQ = 256
SWA_KW = SWA_TQ + 2 * WINDOW
MERGE_TM = 256
MOE_BLK = 512
SEG_ALIGN = SUBLANES
N_TILES = N_TOK // MERGE_TM
LROWS = -(-(MERGE_TM * TOP_K + N_EXPERTS * (SEG_ALIGN - 1)) // LANES) * LANES
SEG_BITS = tuple(range(SEG_ALIGN.bit_length() - 1, (MERGE_TM * TOP_K).bit_length()))
SEG_COUNT = len(SEG_BITS) * N_EXPERTS
SEG_TOTAL = SEG_COUNT + len(SEG_BITS)
SEG_W = 512
CH_SRC_BITS = 11
MOVE_SLOTS = 2
N_SLOTS = -(-(N_TOK * TOP_K + N_TILES * N_EXPERTS * (SEG_ALIGN - 1) + N_EXPERTS * (MOE_BLK - 1))
            // MOE_BLK) * MOE_BLK
N_BLK = N_SLOTS // MOE_BLK

PACK_W = D_MODEL // 2

VMEM_LIMIT = 56 * 1024 * 1024

C_KVLAT = 0
C_KSWA = C_KVLAT + MLA_KV_RANK
C_KROPE = C_KSWA + SWA_KV_HEADS * SWA_HD
C_KVEND = C_KROPE + LANES
C_QLAT = C_KVEND
C_QSWA = C_QLAT + MLA_Q_RANK
C_GA = C_QSWA + SWA_HEADS * SWA_HD
C_GB = C_GA + D_MODEL
C_END = C_GB + D_MODEL


def _mm(a, b):
    return jnp.dot(a.astype(BF16), b.astype(BF16), preferred_element_type=F32)


def _mm_nt(a, b):
    return lax.dot_general(a.astype(BF16), b.astype(BF16), (((1,), (1,)), ((), ())),
                           preferred_element_type=F32)


def _mm_tn(a, b):
    return lax.dot_general(a.astype(BF16), b.astype(BF16), (((0,), (0,)), ((), ())),
                           preferred_element_type=F32)


def _split(a):
    hi = a.astype(BF16)
    lo = (a - hi.astype(F32)).astype(BF16)
    return hi, lo


def _pack_rows(a, already_bf16=False):
    half = a.shape[1] // 2
    rnd = (lambda v: v) if already_bf16 else (lambda v: v.astype(BF16).astype(F32))
    bits = lambda v: lax.bitcast_convert_type(rnd(v), U32)
    return bits(a[:, half:]) | (bits(a[:, :half]) >> 16)


def _unpack_rows(p):
    lo = lax.bitcast_convert_type(p << 16, F32)
    hi = lax.bitcast_convert_type(p & jnp.uint32(0xFFFF0000), F32)
    return jnp.concatenate([lo, hi], axis=1).astype(BF16)


def _rms(x):
    return x * lax.rsqrt(jnp.mean(x * x, axis=-1, keepdims=True) + NORM_EPS)


def _ada_kernel(c_ref, w_ref, b_ref, o_ref):
    c = c_ref[...]
    s = c * jax.nn.sigmoid(c)
    shi, slo = _split(s)
    whi, wlo = _split(w_ref[...])
    acc = _mm(shi, whi) + _mm(slo, whi) + _mm(shi, wlo)
    o_ref[...] = acc + b_ref[...]


def _ada(cond, w_ada, b_ada):
    n = w_ada.shape[1]
    return pl.pallas_call(
        _ada_kernel,
        grid=(n // ADA_TN,),
        in_specs=[
            pl.BlockSpec((MOD_ROWS, D_MODEL), lambda j: (0, 0)),
            pl.BlockSpec((D_MODEL, ADA_TN), lambda j: (0, j)),
            pl.BlockSpec((1, ADA_TN), lambda j: (0, j)),
        ],
        out_specs=pl.BlockSpec((MOD_ROWS, ADA_TN), lambda j: (0, j)),
        out_shape=jax.ShapeDtypeStruct((MOD_ROWS, n), F32),
        compiler_params=pltpu.CompilerParams(
            dimension_semantics=("arbitrary",), vmem_limit_bytes=VMEM_LIMIT),
        name="ada",
    )(cond, w_ada, b_ada)


def _head_norm(xraw, ind_ref, indt_ref, inv_dim):
    w = xraw.shape[1]
    ss = _mm(xraw * xraw, ind_ref[0:w, :])
    r = lax.rsqrt(ss * inv_dim + NORM_EPS)
    scale = _mm(jnp.concatenate(_split(r), axis=1), indt_ref[:, 0:w])
    return xraw * scale


def _rope(xh, cos, sa, sb, quarter):
    return (xh * cos + pltpu.roll(xh, LANES - quarter, 1) * sa
            + pltpu.roll(xh, quarter, 1) * sb)


def _proj_kernel(*refs, with_q):
    (x_ref, sh_ref, sc_ref, g1_ref, win_ref, cm_ref, sam_ref, sbm_ref, cs_ref, sas_ref, sbs_ref,
     gkva_ref, wkn_ref, wvt_ref, wvst_ref, gk_ref, gks_ref, ind_ref, indt_ref,
     ind64_ref, indt64_ref) = refs[:21]
    if with_q:
        gqa_ref, wqup_ref, gq_ref, gqs_ref = refs[21:25]
        kmla_ref, vmla_ref, kswa_ref, vswa_ref, qmla_ref, qswa_ref, siga_ref, sigb_ref = refs[25:]
    else:
        kmla_ref, vmla_ref, kswa_ref, vswa_ref = refs[21:]

    x = x_ref[...]
    h = _rms(x) * g1_ref[...] * (1.0 + sc_ref[...]) + sh_ref[...]
    y = _mm(h, win_ref[...])

    cm, sam, sbm = cm_ref[...], sam_ref[...], sbm_ref[...]
    cs, sas, sbs = cs_ref[...], sas_ref[...], sbs_ref[...]

    kvn = _rms(y[:, C_KVLAT:C_KVLAT + MLA_KV_RANK]) * gkva_ref[...]
    kn = _mm(kvn, wkn_ref[...])
    vmla_ref[...] = _mm_nt(wvt_ref[...], kvn).astype(BF16)
    vswa_ref[...] = _mm_nt(wvst_ref[...], h).astype(BF16)
    kr = y[:, C_KROPE:C_KROPE + LANES]
    kraw = kn + jnp.concatenate([kr] * MLA_HEADS, axis=1)
    kfull = _head_norm(kraw, ind_ref, indt_ref, 1.0 / MLA_QK) * gk_ref[...]
    for hd in range(MLA_HEADS):
        sl = slice(hd * LANES, (hd + 1) * LANES)
        kmla_ref[:, sl] = _rope(kfull[:, sl], cm, sam, sbm, MLA_ROPE // 4).astype(BF16)

    ks = _head_norm(y[:, C_KSWA:C_KSWA + SWA_KV_HEADS * SWA_HD], ind64_ref, indt64_ref,
                    1.0 / SWA_HD) * gks_ref[...]
    for grp in range(SWA_KV_HEADS * SWA_HD // LANES):
        sl = slice(grp * LANES, (grp + 1) * LANES)
        kswa_ref[:, sl] = _rope(ks[:, sl], cs, sas, sbs, SWA_HD // 4).astype(BF16)

    if with_q:
        qn = _rms(y[:, C_QLAT:C_QLAT + MLA_Q_RANK]) * gqa_ref[...]
        qraw = _mm(qn, wqup_ref[...])
        qf = _head_norm(qraw, ind_ref, indt_ref, 1.0 / MLA_QK) * gq_ref[...]
        for hd in range(MLA_HEADS):
            sl = slice(hd * LANES, (hd + 1) * LANES)
            qmla_ref[:, sl] = _rope(qf[:, sl], cm, sam, sbm, MLA_ROPE // 4).astype(BF16)
        qs = _head_norm(y[:, C_QSWA:C_QSWA + SWA_HEADS * SWA_HD], ind64_ref, indt64_ref,
                        1.0 / SWA_HD) * gqs_ref[...]
        for grp in range(SWA_HEADS * SWA_HD // LANES):
            sl = slice(grp * LANES, (grp + 1) * LANES)
            qswa_ref[:, sl] = _rope(qs[:, sl], cs, sas, sbs, SWA_HD // 4).astype(BF16)
        siga_ref[...] = jax.nn.sigmoid(y[:, C_GA:C_GA + D_MODEL]).astype(BF16)
        sigb_ref[...] = jax.nn.sigmoid(y[:, C_GB:C_GB + D_MODEL]).astype(BF16)


def _proj(x2d, mod3, mod_row_fn, tab_row_fn, tabs, consts, q_consts, win, with_q, name):
    rows = x2d.shape[0]
    tm = PROJ_TM
    const = lambda shape: pl.BlockSpec(shape, lambda i: (0,) * len(shape))
    tab = pl.BlockSpec((tm, LANES), lambda i: (tab_row_fn(i), 0))
    g1, rest = consts[0], consts[1:]
    in_specs = [
        pl.BlockSpec((tm, D_MODEL), lambda i: (i, 0)),
        pl.BlockSpec((None, 1, D_MODEL), lambda i: (mod_row_fn(i), 0, 0)),
        pl.BlockSpec((None, 1, D_MODEL), lambda i: (mod_row_fn(i), 0, 1)),
        const(g1.shape),
        pl.BlockSpec(win.shape, lambda i: (0, 0), pipeline_mode=pl.Buffered(1)),
        tab, tab, tab, tab, tab, tab,
    ] + [const(c.shape) for c in rest]
    args = [x2d, mod3, mod3, g1, win] + list(tabs) + list(rest)
    outs = [(MLA_HEADS * LANES, False), (MLA_HEADS * MLA_V, True),
            (SWA_KV_HEADS * SWA_HD, False), (SWA_KV_HEADS * SWA_HD, True)]
    if with_q:
        in_specs += [const(c.shape) for c in q_consts]
        args += list(q_consts)
        outs += [(MLA_HEADS * LANES, False), (SWA_HEADS * SWA_HD, False), (D_MODEL, False), (D_MODEL, False)]
    return pl.pallas_call(
        functools.partial(_proj_kernel, with_q=with_q),
        grid=(rows // tm,),
        in_specs=in_specs,
        out_specs=[pl.BlockSpec((w, tm), lambda i: (0, i)) if t else pl.BlockSpec((tm, w), lambda i: (i, 0))
                   for w, t in outs],
        out_shape=[jax.ShapeDtypeStruct((w, rows) if t else (rows, w), BF16) for w, t in outs],
        compiler_params=pltpu.CompilerParams(
            dimension_semantics=("arbitrary",), vmem_limit_bytes=VMEM_LIMIT),
        name=name,
    )(*args)


def _mla_step(q_ref, kl_ref, vlt_ref, kc_ref, vct_ref, o_ref, cur, prev):
    s_lat_c, s_ctx_c, m_c = cur
    s_lat_p, s_ctx_p, m_p = prev
    outs = []
    for hh in range(2):
        sl = slice(hh * LANES, (hh + 1) * LANES)
        vrows = slice(hh * MLA_V, (hh + 1) * MLA_V)
        q = q_ref[:, sl]
        s1 = _mm_nt(kl_ref[:, sl], q)
        s2 = _mm_nt(kc_ref[:, sl], q)
        s_lat_c[hh] = s1
        s_ctx_c[hh] = s2
        m_c[hh] = jnp.maximum(jnp.max(s1, axis=0, keepdims=True),
                              jnp.max(s2, axis=0, keepdims=True))

        m = m_p[hh]
        p1 = jnp.exp2(s_lat_p[hh] - m)
        p2 = jnp.exp2(s_ctx_p[hh] - m)
        l = jnp.sum(p1, axis=0, keepdims=True) + jnp.sum(p2, axis=0, keepdims=True)
        o = _mm(vlt_ref[vrows, :], p1) + _mm(vct_ref[vrows, :], p2)
        outs.append(o / l)
    o_ref[...] = jnp.concatenate(outs, axis=0).astype(BF16)


def _mla_kernel(q_ref, kl_ref, vl_ref, kc_ref, vc_ref, o_ref, *scratch):
    set0, set1 = scratch[:3], scratch[3:]
    i = pl.program_id(0)

    @pl.when(i == 0)
    def _():
        for ref in set1:
            ref[...] = jnp.zeros(ref.shape, F32)

    @pl.when(lax.rem(i, 2) == 0)
    def _():
        _mla_step(q_ref, kl_ref, vl_ref, kc_ref, vc_ref, o_ref, set0, set1)

    @pl.when(lax.rem(i, 2) == 1)
    def _():
        _mla_step(q_ref, kl_ref, vl_ref, kc_ref, vc_ref, o_ref, set1, set0)


def _mla_attn(q, k_lat, vt_lat, k_ctx, vt_ctx):
    tq = MLA_TQ
    nq = SEQ // tq
    pairs = MLA_HEADS // 2
    n_items = BATCH * pairs * nq

    def item(s):
        return s // (pairs * nq), (s // nq) % pairs, s % nq

    def scored(s):
        return item(jnp.minimum(s, n_items - 1))

    def finished(s):
        return item(jnp.maximum(s - 1, 0))

    def q_map(s):
        b, hp, i = scored(s)
        return b * nq + i, hp

    def k_map(s):
        b, hp, _ = scored(s)
        return b, hp

    def v_map(s):
        b, hp, _ = finished(s)
        return hp, b

    def o_map(s):
        b, hp, i = finished(s)
        return hp, b * nq + i

    return pl.pallas_call(
        _mla_kernel,
        grid=(n_items + 1,),
        in_specs=[
            pl.BlockSpec((tq, 2 * LANES), q_map),
            pl.BlockSpec((SEQ, 2 * LANES), k_map),
            pl.BlockSpec((2 * MLA_V, SEQ), v_map),
            pl.BlockSpec((CTX_LEN, 2 * LANES), k_map),
            pl.BlockSpec((2 * MLA_V, CTX_LEN), v_map),
        ],
        out_specs=pl.BlockSpec((2 * MLA_V, tq), o_map),
        out_shape=jax.ShapeDtypeStruct((MLA_HEADS * MLA_V, N_TOK), BF16),
        scratch_shapes=2 * [pltpu.VMEM((2, SEQ, tq), F32),
                            pltpu.VMEM((2, CTX_LEN, tq), F32),
                            pltpu.VMEM((2, 1, tq), F32)],
        compiler_params=pltpu.CompilerParams(
            dimension_semantics=("arbitrary",),
            vmem_limit_bytes=VMEM_LIMIT),
        name="mla_attn",
    )(q, k_lat, vt_lat, k_ctx, vt_ctx)


def _swa_window_start(i):
    return jnp.clip(i * SWA_TQ - WINDOW, 0, SEQ - SWA_KW)


def _swa_step(i_cur, i_prev, sink_ref, q_ref, k_ref, vt_ref, kc_ref, vct_ref, bias_ref, o_ref, cur, prev):
    tq = SWA_TQ
    sb_c, sc_c, m_c = cur
    sb_p, sc_p, m_p = prev
    kstart = pl.multiple_of(_swa_window_start(i_cur), LANES)
    kwin = k_ref[pl.ds(kstart, SWA_KW), :].astype(F32)
    vstart = pl.multiple_of(_swa_window_start(i_prev), LANES)
    vtwin = vt_ref[:, pl.ds(vstart, SWA_KW)]
    kc = kc_ref[...].astype(F32)
    vct = vct_ref[...]
    bias = bias_ref[...]

    def half_keys(k, kh):
        lane = lax.broadcasted_iota(I32, k.shape, 1)
        own = jnp.where((lane >= kh * SWA_HD) & (lane < (kh + 1) * SWA_HD), k, 0.0)
        other = pltpu.roll(own, SWA_HD, 1)
        return (own, other) if kh == 0 else (other, own)

    order = (0, 2, 1, 3)
    for kh in range(SWA_KV_HEADS):
        vrows = slice(kh * SWA_HD, (kh + 1) * SWA_HD)
        grp0 = kh * SWA_GROUP // 2
        q2 = jnp.concatenate([q_ref[:, (grp0 + c) * LANES:(grp0 + c + 1) * LANES] for c in range(2)],
                             axis=0)
        k_lo, k_hi = half_keys(kwin, kh)
        c_lo, c_hi = half_keys(kc, kh)
        sb = jnp.concatenate([_mm_nt(k_lo, q2), _mm_nt(k_hi, q2)], axis=1) + bias
        sc = jnp.concatenate([_mm_nt(c_lo, q2), _mm_nt(c_hi, q2)], axis=1)
        sink = jnp.concatenate(
            [jnp.full((1, tq), sink_ref[kh * SWA_GROUP + g] * LOG2E, F32) for g in order], axis=1)
        sb_c[kh] = sb
        sc_c[kh] = sc
        m_c[kh] = jnp.maximum(jnp.maximum(jnp.max(sb, axis=0, keepdims=True),
                                          jnp.max(sc, axis=0, keepdims=True)), sink)

        m = m_p[kh]
        pb = jnp.exp2(sb_p[kh] - m)
        pc = jnp.exp2(sc_p[kh] - m)
        l = (jnp.sum(pb, axis=0, keepdims=True) + jnp.sum(pc, axis=0, keepdims=True)
             + jnp.exp2(sink - m))
        o = (_mm(vtwin[vrows, :], pb) + _mm(vct[vrows, :], pc)) / l
        for blk, g in enumerate(order):
            hd = kh * SWA_GROUP + g
            o_ref[hd * SWA_HD:(hd + 1) * SWA_HD, :] = o[:, blk * tq:(blk + 1) * tq].astype(BF16)


def _swa_kernel(sink_ref, q_ref, k_ref, vt_ref, kc_ref, vct_ref, bias_ref, o_ref, *scratch):
    set0, set1 = scratch[:3], scratch[3:]
    nq = SEQ // SWA_TQ
    s = pl.program_id(0)
    i_cur = lax.rem(jnp.minimum(s, BATCH * nq - 1), nq)
    i_prev = lax.rem(jnp.maximum(s - 1, 0), nq)

    @pl.when(s == 0)
    def _():
        for ref in set1:
            ref[...] = jnp.zeros(ref.shape, F32)

    @pl.when(lax.rem(s, 2) == 0)
    def _():
        _swa_step(i_cur, i_prev, sink_ref, q_ref, k_ref, vt_ref, kc_ref, vct_ref, bias_ref, o_ref, set0, set1)

    @pl.when(lax.rem(s, 2) == 1)
    def _():
        _swa_step(i_cur, i_prev, sink_ref, q_ref, k_ref, vt_ref, kc_ref, vct_ref, bias_ref, o_ref, set1, set0)


def _swa_attn(sink, q, k_lat, vt_lat, k_ctx, vt_ctx):
    tq = SWA_TQ
    nq = SEQ // tq
    offsets = (0, WINDOW, SWA_KW - tq)
    assert all(i * tq - min(max(i * tq - WINDOW, 0), SEQ - SWA_KW) == offsets[(i > 0) + (i == nq - 1)]
               for i in range(nq))
    key = jnp.arange(SWA_KW, dtype=I32)[:, None]
    qry = (jnp.arange(SWA_GROUP * tq, dtype=I32) % tq)[None, :]
    bias = jnp.stack([jnp.where(jnp.abs(d + qry - key) <= WINDOW, 0.0, NEG_INF) for d in offsets]).astype(F32)
    case = lambda i: (i > 0).astype(I32) + (i == nq - 1).astype(I32)
    n_items = BATCH * nq
    scored = lambda s: jnp.minimum(s, n_items - 1)
    finished = lambda s: jnp.maximum(s - 1, 0)
    cols = SWA_GROUP * tq
    return pl.pallas_call(
        _swa_kernel,
        grid=(n_items + 1,),
        in_specs=[
            pl.BlockSpec(memory_space=pltpu.SMEM),
            pl.BlockSpec((tq, SWA_HEADS * SWA_HD), lambda s: (scored(s), 0)),
            pl.BlockSpec((SEQ, SWA_KV_HEADS * SWA_HD), lambda s: (scored(s) // nq, 0)),
            pl.BlockSpec((SWA_KV_HEADS * SWA_HD, SEQ), lambda s: (0, finished(s) // nq)),
            pl.BlockSpec((CTX_LEN, SWA_KV_HEADS * SWA_HD), lambda s: (scored(s) // nq, 0)),
            pl.BlockSpec((SWA_KV_HEADS * SWA_HD, CTX_LEN), lambda s: (0, finished(s) // nq)),
            pl.BlockSpec((None, SWA_KW, cols), lambda s: (case(scored(s) % nq), 0, 0)),
        ],
        out_specs=pl.BlockSpec((SWA_HEADS * SWA_HD, tq), lambda s: (0, finished(s))),
        out_shape=jax.ShapeDtypeStruct((SWA_HEADS * SWA_HD, N_TOK), BF16),
        scratch_shapes=2 * [pltpu.VMEM((SWA_KV_HEADS, SWA_KW, cols), F32),
                            pltpu.VMEM((SWA_KV_HEADS, CTX_LEN, cols), F32),
                            pltpu.VMEM((SWA_KV_HEADS, 1, cols), F32)],
        compiler_params=pltpu.CompilerParams(
            dimension_semantics=("arbitrary",), vmem_limit_bytes=VMEM_LIMIT),
        name="swa_attn",
    )(sink, q, k_lat, vt_lat, k_ctx, vt_ctx, bias)


def _merge_kernel(oa_ref, ob_ref, sa_ref, sb_ref, x_ref, g1_ref, sh2_ref, sc2_ref, wba_ref, wbb_ref,
                  wout_ref, g2_ref, wr_ref, br_ref, utri_ref, ltri_ref,
                  xnew_ref, h2_ref, pos_ref, wts_ref, seg_ref, carry_ref, lg0_ref, lg1_ref):
    s = pl.program_id(0)

    @pl.when(s == 0)
    def _():
        carry_ref[...] = jnp.zeros_like(carry_ref)
        lg1_ref[...] = jnp.zeros_like(lg1_ref)

    def step(lg_cur, lg_prev):
        _merge_project(oa_ref, ob_ref, sa_ref, sb_ref, x_ref, g1_ref, sh2_ref, sc2_ref, wba_ref, wbb_ref,
                       wout_ref, g2_ref, wr_ref, br_ref, xnew_ref, h2_ref, lg_cur)
        _merge_route(s > 0, lg_prev, utri_ref, ltri_ref, pos_ref, wts_ref, seg_ref, carry_ref)

    @pl.when(lax.rem(s, 2) == 0)
    def _():
        step(lg0_ref, lg1_ref)

    @pl.when(lax.rem(s, 2) == 1)
    def _():
        step(lg1_ref, lg0_ref)


def _merge_project(oa_ref, ob_ref, sa_ref, sb_ref, x_ref, g1_ref, sh2_ref, sc2_ref, wba_ref, wbb_ref,
                   wout_ref, g2_ref, wr_ref, br_ref, xnew_ref, h2_ref, lg_ref):
    ya = _mm_tn(oa_ref[...], wba_ref[...])
    yb = _mm_tn(ob_ref[...], wbb_ref[...])
    y = sa_ref[...].astype(F32) * ya + sb_ref[...].astype(F32) * yb
    z = _mm(y, wout_ref[...])
    xn = x_ref[...] + g1_ref[...] * z
    xnew_ref[...] = xn
    h2 = _rms(xn) * g2_ref[...] * (1.0 + sc2_ref[...]) + sh2_ref[...]
    h2_ref[...] = h2.astype(BF16)

    hhi, hlo = _split(h2)
    whi, wlo = _split(wr_ref[...])
    lg_ref[...] = _mm_nt(whi, hhi) + _mm_nt(whi, hlo) + _mm_nt(wlo, hhi) + br_ref[...]


def _merge_route(real, lg_ref, utri_ref, ltri_ref, pos_ref, wts_ref, seg_ref, carry_ref):
    lg = lg_ref[...]
    eiota = lax.broadcasted_iota(I32, lg.shape, 0).astype(F32)
    vals, onehots = [], []
    for k in range(TOP_K):
        m = jnp.max(lg, axis=0, keepdims=True)
        ik = jnp.min(jnp.where(lg == m, eiota, float(N_EXPERTS)), axis=0, keepdims=True)
        hit = eiota == ik
        vals.append(m)
        onehots.append(hit.astype(F32))
        lg = jnp.where(hit, -jnp.inf, lg)
    es = [jnp.exp(v - vals[0]) for v in vals]
    tot = es[0] + es[1] + es[2] + es[3]
    for k in range(TOP_K):
        wts_ref[k:k + 1, :] = es[k] / tot

    tots = [jnp.sum(oh, axis=1, keepdims=True) for oh in onehots]
    n = tots[0] + tots[1] + tots[2] + tots[3]
    m_al = jnp.floor((n + (SEG_ALIGN - 1)) * (1.0 / SEG_ALIGN))
    m_b = jnp.broadcast_to(m_al, (N_EXPERTS, LANES))
    lstart = _mm(ltri_ref[...], m_b) * float(SEG_ALIGN)
    off = jnp.zeros_like(n)
    for k in range(TOP_K):
        prefix = _mm(onehots[k], utri_ref[...])
        lp = jnp.sum(onehots[k] * (lstart[:, 0:1] + off + prefix), axis=0, keepdims=True)
        pos_ref[k:k + 1, :] = lp.astype(I32)
        off = off + tots[k]
    eye = (lax.broadcasted_iota(I32, (N_EXPERTS, LANES), 0)
           == lax.broadcasted_iota(I32, (N_EXPERTS, LANES), 1)).astype(F32)
    to_row = lambda v: jnp.sum(v * eye, axis=0, keepdims=True).astype(I32)
    carry = carry_ref[...]
    seg_ref[...] = jnp.concatenate(
        [to_row(lstart), to_row(m_b * float(SEG_ALIGN)), to_row(carry),
         jnp.zeros((SUBLANES - 3, LANES), I32)], axis=0)
    carry_ref[...] = carry + jnp.where(real, m_b * float(SEG_ALIGN), 0.0)


def _merge(o_a, o_b, sig_a, sig_b, x2d, mod3, w_ba, w_bb, w_out, g2, w_rt, b_r):
    tm = MERGE_TM
    tiles_per_batch = SEQ // tm
    proj = lambda s: jnp.minimum(s, N_TILES - 1)
    routed = lambda s: jnp.maximum(s - 1, 0)
    const = lambda shape: pl.BlockSpec(shape, lambda s: (0,) * len(shape))
    modspec = lambda j: pl.BlockSpec((None, 1, D_MODEL), lambda s: (proj(s) // tiles_per_batch, 0, j))
    row = lambda w: pl.BlockSpec((tm, w), lambda s: (proj(s), 0))
    utri = (jnp.arange(tm)[:, None] < jnp.arange(tm)[None, :]).astype(BF16)
    ltri = (jnp.arange(N_EXPERTS)[None, :] < jnp.arange(N_EXPERTS)[:, None]).astype(BF16)
    return pl.pallas_call(
        _merge_kernel,
        grid=(N_TILES + 1,),
        in_specs=[pl.BlockSpec((512, tm), lambda s: (0, proj(s))), pl.BlockSpec((512, tm), lambda s: (0, proj(s))),
                  row(D_MODEL), row(D_MODEL), row(D_MODEL),
                  modspec(2), modspec(3), modspec(4),
                  const(w_ba.shape), const(w_bb.shape), const(w_out.shape), const(g2.shape),
                  const(w_rt.shape), const(b_r.shape), const(utri.shape), const(ltri.shape)],
        out_specs=[row(D_MODEL), row(D_MODEL),
                   pl.BlockSpec((TOP_K, tm), lambda s: (0, routed(s))),
                   pl.BlockSpec((TOP_K, tm), lambda s: (0, routed(s))),
                   pl.BlockSpec((SUBLANES, LANES), lambda s: (routed(s), 0))],
        out_shape=[jax.ShapeDtypeStruct((N_TOK, D_MODEL), F32),
                   jax.ShapeDtypeStruct((N_TOK, D_MODEL), BF16),
                   jax.ShapeDtypeStruct((TOP_K, N_TOK), I32),
                   jax.ShapeDtypeStruct((TOP_K, N_TOK), F32),
                   jax.ShapeDtypeStruct((N_TILES * SUBLANES, LANES), I32)],
        scratch_shapes=[pltpu.VMEM((N_EXPERTS, LANES), F32),
                        pltpu.VMEM((N_EXPERTS, tm), F32), pltpu.VMEM((N_EXPERTS, tm), F32)],
        compiler_params=pltpu.CompilerParams(
            dimension_semantics=("arbitrary",), vmem_limit_bytes=VMEM_LIMIT),
        name="merge",
    )(o_a, o_b, sig_a, sig_b, x2d, mod3, mod3, mod3, w_ba, w_bb, w_out, g2, w_rt, b_r, utri, ltri)


def _chunk_copy(hbm, hbm_row, buf, buf_row, size, to_hbm, sem):
    vm = buf.at[pl.ds(pl.multiple_of(buf_row, SEG_ALIGN), size)]
    hb = hbm.at[pl.ds(pl.multiple_of(hbm_row, SEG_ALIGN), size)]
    return pltpu.make_async_copy(vm, hb, sem) if to_hbm else pltpu.make_async_copy(hb, vm, sem)


def _seg_start(seg_smem, tile, hbm, buf, to_hbm, sem):
    base = tile * SEG_W
    for c, b in enumerate(SEG_BITS):
        def issue(t, carry, c=c, size=1 << b, prio=(c // 2) % 2):
            word = seg_smem[base + c * N_EXPERTS + t]
            _chunk_copy(hbm, word >> CH_SRC_BITS, buf, word & ((1 << CH_SRC_BITS) - 1),
                        size, to_hbm, sem).start(priority=prio)
            return carry

        lax.fori_loop(0, seg_smem[base + SEG_COUNT + c], issue, 0)


def _seg_wait(seg_smem, tile, hbm, buf, to_hbm, sem):
    assert LROWS < 2 << SEG_BITS[-1]
    total = seg_smem[tile * SEG_W + SEG_TOTAL]
    for b in SEG_BITS:
        size = 1 << b

        @pl.when((total & size) != 0)
        def _():
            _chunk_copy(hbm, 0, buf, 0, size, to_hbm, sem).wait()


def _dispatch_kernel(pend_ref, padded_ref, seg_ref, pos_ref, h2_ref, xs_ref, staged, sems):
    j = pl.program_id(0)
    slot = lax.rem(j, MOVE_SLOTS)

    @pl.when(j == 0)
    def _():
        staged[0, 0:MOE_BLK, :] = jnp.zeros((MOE_BLK, PACK_W), U32)

        def tail(action):
            def body(e, carry):
                @pl.when(padded_ref[e] > 0)
                def _():
                    start = pl.multiple_of(pend_ref[e] - MOE_BLK, MOE_BLK)
                    cp = pltpu.make_async_copy(staged.at[0, pl.ds(0, MOE_BLK)],
                                               xs_ref.at[pl.ds(start, MOE_BLK)], sems.at[0])
                    getattr(cp, action)()
                return carry
            lax.fori_loop(0, N_EXPERTS, body, 0)

        def unused(action):
            def body(b, carry):
                cp = pltpu.make_async_copy(
                    staged.at[0, pl.ds(0, MOE_BLK)],
                    xs_ref.at[pl.ds(pl.multiple_of(b * MOE_BLK, MOE_BLK), MOE_BLK)], sems.at[0])
                getattr(cp, action)()
                return carry
            lax.fori_loop(pend_ref[N_EXPERTS - 1] // MOE_BLK, N_BLK, body, 0)

        tail("start")
        unused("start")
        tail("wait")
        unused("wait")

    pos = pos_ref[...]
    piota = lax.broadcasted_iota(I32, (LROWS, MERGE_TM), 0)
    hit = piota == pos[0:1, :]
    for k in range(1, TOP_K):
        hit = hit | (piota == pos[k:k + 1, :])
    staged[slot] = _pack_rows(_mm(jnp.where(hit, 1.0, 0.0), h2_ref[...]), already_bf16=True)

    _seg_start(seg_ref, j, xs_ref, staged.at[slot], True, sems.at[1 + slot])

    lag = MOVE_SLOTS - 1

    def wait_tile(t):
        s = lax.rem(t, MOVE_SLOTS)
        _seg_wait(seg_ref, t, xs_ref, staged.at[s], True, sems.at[1 + s])

    @pl.when(j >= lag)
    def _():
        wait_tile(j - lag)

    @pl.when(j == N_TILES - 1)
    def _():
        for back in reversed(range(lag)):
            wait_tile(j - back)


def _dispatch(pends, padded, segtab, pos, h2):
    grid_spec = pltpu.PrefetchScalarGridSpec(
        num_scalar_prefetch=3,
        grid=(N_TILES,),
        in_specs=[
            pl.BlockSpec((TOP_K, MERGE_TM), lambda j, pe, pa, sg: (0, j)),
            pl.BlockSpec((MERGE_TM, D_MODEL), lambda j, pe, pa, sg: (j, 0)),
        ],
        out_specs=pl.BlockSpec(memory_space=pl.ANY),
        scratch_shapes=[pltpu.VMEM((MOVE_SLOTS, LROWS, PACK_W), U32),
                        pltpu.SemaphoreType.DMA((1 + MOVE_SLOTS,))],
    )
    return pl.pallas_call(
        _dispatch_kernel,
        grid_spec=grid_spec,
        out_shape=jax.ShapeDtypeStruct((N_SLOTS, PACK_W), U32),
        compiler_params=pltpu.CompilerParams(
            dimension_semantics=("arbitrary",), vmem_limit_bytes=VMEM_LIMIT),
        name="dispatch",
    )(pends, padded, segtab, pos, h2)


SPLIT_SUB = 256


def _expert_kernel(be_ref, nused_ref, xs_ref, wgu_ref, bg_ref, bl_ref, wd_ref, bd_ref, perm_ref, o_ref,
                   wg_s, wl_s, wd_s):
    i = pl.program_id(0)
    used = i < nused_ref[0]
    fresh = jnp.logical_or(i == 0, be_ref[i] != be_ref[jnp.maximum(i - 1, 0)])

    @pl.when(jnp.logical_and(used, fresh))
    def _():
        half = SPLIT_SUB // 2
        for s in range(2 * D_EXPERT // SPLIT_SUB):
            r = _mm(wgu_ref[:, s * SPLIT_SUB:(s + 1) * SPLIT_SUB], perm_ref[...])
            wg_s[:, s * half:(s + 1) * half] = r[:, :half].astype(BF16)
            wl_s[:, s * half:(s + 1) * half] = r[:, half:].astype(BF16)
        wd_s[...] = wd_ref[...].astype(BF16)

    @pl.when(used)
    def _():
        xb = _unpack_rows(xs_ref[...])
        g = jnp.minimum(_mm(xb, wg_s[...]) + bg_ref[...], SWIGLU_LIMIT)
        l = jnp.clip(_mm(xb, wl_s[...]) + bl_ref[...], -SWIGLU_LIMIT, SWIGLU_LIMIT)
        act = g * jax.nn.sigmoid(SWIGLU_ALPHA * g) * (l + 1.0)
        o_ref[...] = _pack_rows(_mm(act, wd_s[...]) + bd_ref[...])

    @pl.when(jnp.logical_not(used))
    def _():
        o_ref[...] = jnp.zeros_like(o_ref)


def _experts(block_expert, nused, xs, w_gu, bg, bl, w_dn, bd):
    col = jnp.arange(SPLIT_SUB)
    src = jnp.where(col < SPLIT_SUB // 2, 2 * col, 2 * (col - SPLIT_SUB // 2) + 1)
    perm = (jnp.arange(SPLIT_SUB)[:, None] == src[None, :]).astype(BF16)
    blk = lambda i, be, nu: (jnp.minimum(i, nu[0] - 1), 0)
    per_expert = lambda shape: pl.BlockSpec((None,) + shape, lambda i, be, nu: (be[i], 0, 0))
    grid_spec = pltpu.PrefetchScalarGridSpec(
        num_scalar_prefetch=2,
        grid=(N_BLK,),
        in_specs=[pl.BlockSpec((MOE_BLK, PACK_W), blk),
                  per_expert((D_MODEL, 2 * D_EXPERT)), per_expert((1, D_EXPERT)), per_expert((1, D_EXPERT)),
                  per_expert((D_EXPERT, D_MODEL)), per_expert((1, D_MODEL)),
                  pl.BlockSpec((SPLIT_SUB, SPLIT_SUB), lambda i, be, nu: (0, 0))],
        out_specs=pl.BlockSpec((MOE_BLK, PACK_W), lambda i, be, nu: (i, 0)),
        scratch_shapes=[pltpu.VMEM((D_MODEL, D_EXPERT), BF16), pltpu.VMEM((D_MODEL, D_EXPERT), BF16),
                        pltpu.VMEM((D_EXPERT, D_MODEL), BF16)],
    )
    return pl.pallas_call(
        _expert_kernel,
        grid_spec=grid_spec,
        out_shape=jax.ShapeDtypeStruct((N_SLOTS, PACK_W), U32),
        compiler_params=pltpu.CompilerParams(
            dimension_semantics=("arbitrary",), vmem_limit_bytes=VMEM_LIMIT),
        name="experts",
    )(block_expert, nused, xs, w_gu, bg, bl, w_dn, bd, perm)


def _combine_kernel(seg_ref, eo_ref, xnew_ref, g2_ref, pos_ref, wt_ref, o_ref, gbuf, sems):
    j = pl.program_id(0)
    slot = lax.rem(j, MOVE_SLOTS)
    ahead = MOVE_SLOTS - 1

    def fetch(tile):
        sl = lax.rem(tile, MOVE_SLOTS)
        _seg_start(seg_ref, tile, eo_ref, gbuf.at[sl], False, sems.at[sl])

    @pl.when(j == 0)
    def _():
        gbuf[...] = jnp.zeros_like(gbuf)
        for t in range(ahead):
            fetch(jnp.int32(t))

    @pl.when(j + ahead < N_TILES)
    def _():
        fetch(j + ahead)

    _seg_wait(seg_ref, j, eo_ref, gbuf.at[slot], False, sems.at[slot])

    pos = pos_ref[...]
    wt = wt_ref[...]
    lane = lax.broadcasted_iota(I32, (MERGE_TM, LROWS), 1)
    wsel = jnp.zeros((MERGE_TM, LROWS), F32)
    for k in range(TOP_K):
        wsel = jnp.where(lane == pos[:, k:k + 1], wt[:, k:k + 1], wsel)
    acc = _mm(wsel, _unpack_rows(gbuf[slot]))
    o_ref[...] = xnew_ref[...] + g2_ref[...] * acc


def _combine(segtab, eo, x_new, mod3, pos_t, wts_t):
    tiles_per_batch = SEQ // MERGE_TM
    grid_spec = pltpu.PrefetchScalarGridSpec(
        num_scalar_prefetch=1,
        grid=(N_TILES,),
        in_specs=[
            pl.BlockSpec(memory_space=pl.ANY),
            pl.BlockSpec((MERGE_TM, D_MODEL), lambda j, sg: (j, 0)),
            pl.BlockSpec((None, 1, D_MODEL), lambda j, sg: (j // tiles_per_batch, 0, 5)),
            pl.BlockSpec((MERGE_TM, TOP_K), lambda j, sg: (j, 0)),
            pl.BlockSpec((MERGE_TM, TOP_K), lambda j, sg: (j, 0)),
        ],
        out_specs=pl.BlockSpec((MERGE_TM, D_MODEL), lambda j, sg: (j, 0)),
        scratch_shapes=[pltpu.VMEM((MOVE_SLOTS, LROWS, PACK_W), U32),
                        pltpu.SemaphoreType.DMA((MOVE_SLOTS,))],
    )
    return pl.pallas_call(
        _combine_kernel,
        grid_spec=grid_spec,
        out_shape=jax.ShapeDtypeStruct((N_TOK, D_MODEL), F32),
        compiler_params=pltpu.CompilerParams(
            dimension_semantics=("arbitrary",), vmem_limit_bytes=VMEM_LIMIT),
        name="combine",
    )(segtab, eo, x_new, mod3, pos_t, wts_t)


def _pad_heads(w, heads, dim):
    lead = w.shape[:-1]
    w = w.reshape(lead + (heads, dim))
    w = jnp.pad(w, [(0, 0)] * len(lead) + [(0, 0), (0, LANES - dim)])
    return w.reshape(lead + (heads * LANES,))


def _rope_tables(dim, offset, repeat):
    pos = jnp.arange(SEQ, dtype=jnp.int32)
    row = (pos // GRID_W).astype(F32)
    col = (pos % GRID_W).astype(F32)
    q = dim // 4
    freqs = ROPE_THETA ** (-jnp.arange(q, dtype=F32) / q)
    ang_r = row[:, None] * freqs
    ang_c = col[:, None] * freqs
    ang = jnp.concatenate([ang_r, ang_r, ang_c, ang_c], axis=-1)
    cos, sin = jnp.cos(ang), jnp.sin(ang)
    first = (jnp.arange(dim) % (2 * q)) < q
    sa = jnp.where(first, -sin, 0.0)
    sb = jnp.where(first, 0.0, sin)
    pad = lambda t, v: jnp.pad(jnp.tile(t, (1, repeat)), ((0, 0), (offset, LANES - offset - dim * repeat)),
                               constant_values=v)
    return pad(cos, 1.0), pad(sa, 0.0), pad(sb, 0.0)


def kernel(x, c, ctx, c_ctx, w_ada, b_ada, norm1_g, norm2_g, w_in, mla_q_a_g, mla_kv_a_g, w_q_up, w_kv_up,
           mla_q_g, mla_k_g, swa_q_g, swa_k_g, swa_sink, w_branch_a, w_branch_b, w_out, w_router, b_router,
           w_gate_up, b_gate_up, w_down, b_down):
    assert x.shape == (BATCH, SEQ, D_MODEL) and ctx.shape == (BATCH, CTX_LEN, D_MODEL)
    assert w_ada.shape[0] == 1, "single layer"

    cond = jnp.concatenate([c, c_ctx[None], jnp.zeros((MOD_ROWS - BATCH - 1, D_MODEL), F32)], axis=0)
    mod = _ada(cond, w_ada[0], b_ada[0][None])
    mod3 = mod.reshape(MOD_ROWS, 1, N_MOD * D_MODEL)

    wi = w_in[0]
    o1, o2, o3 = MLA_KV_RANK, MLA_KV_RANK + MLA_ROPE, MLA_KV_RANK + MLA_ROPE + SWA_KV_HEADS * SWA_HD
    kv_cols = o3 + SWA_KV_HEADS * SWA_HD
    q1 = kv_cols + MLA_Q_RANK
    q2 = q1 + SWA_HEADS * SWA_HD
    q3 = q2 + D_MODEL
    krope_grp = jnp.pad(wi[:, o1:o2], ((0, 0), (MLA_NOPE, LANES - MLA_QK)))
    win = jnp.concatenate([
        wi[:, 0:o1], wi[:, o2:o3], krope_grp, wi[:, kv_cols:q1], wi[:, q1:q2], wi[:, q2:q3], wi[:, q3:]],
        axis=1).astype(BF16)
    assert win.shape == (D_MODEL, C_END)
    wvst = wi[:, o3:kv_cols].T.astype(BF16)
    wkv = w_kv_up[0].reshape(MLA_KV_RANK, MLA_HEADS, MLA_NOPE + MLA_V)
    wkn = _pad_heads(wkv[:, :, :MLA_NOPE].reshape(MLA_KV_RANK, -1), MLA_HEADS, MLA_NOPE).astype(BF16)
    wvt = wkv[:, :, MLA_NOPE:].reshape(MLA_KV_RANK, MLA_HEADS * MLA_V).T.astype(BF16)
    wqup = _pad_heads(w_q_up[0], MLA_HEADS, MLA_QK).astype(BF16)
    gk_t = _pad_heads(jnp.tile(mla_k_g[0], MLA_HEADS)[None], MLA_HEADS, MLA_QK)
    gq_t = _pad_heads(jnp.tile(mla_q_g[0] * (MLA_QK ** -0.5 * LOG2E), MLA_HEADS)[None], MLA_HEADS, MLA_QK)
    gks_t = jnp.tile(swa_k_g[0], SWA_KV_HEADS)[None]
    gqs_t = jnp.tile(swa_q_g[0] * (SWA_HD ** -0.5 * LOG2E), SWA_HEADS)[None]

    def lane_to_head(n_lanes, width):
        ind = ((jnp.arange(n_lanes) // width)[:, None] == jnp.arange(LANES)[None, :]).astype(BF16)
        return ind, jnp.concatenate([ind.T, ind.T], axis=0)

    ind, indt = lane_to_head(MLA_HEADS * LANES, LANES)
    ind64, indt64 = lane_to_head(SWA_HEADS * SWA_HD, SWA_HD)
    consts = [norm1_g[0][None], mla_kv_a_g[0][None], wkn, wvt, wvst, gk_t, gks_t, ind, indt, ind64, indt64]
    q_consts = [mla_q_a_g[0][None], wqup, gq_t, gqs_t]

    tabs_lat = _rope_tables(MLA_ROPE, MLA_NOPE, 1) + _rope_tables(SWA_HD, 0, LANES // SWA_HD)
    ones = jnp.ones((PROJ_TM, LANES), F32)
    zeros = jnp.zeros((PROJ_TM, LANES), F32)
    tabs_ctx = (ones, zeros, zeros, ones, zeros, zeros)

    x2d = x.reshape(N_TOK, D_MODEL)
    tiles_per_batch = SEQ // PROJ_TM
    k_mla, vt_mla, k_swa, vt_swa, q_mla, q_swa, sig_a, sig_b = _proj(
        x2d, mod3, lambda i: i // tiles_per_batch, lambda i: i % tiles_per_batch,
        tabs_lat, consts, q_consts, win, True, "proj_latent")
    kc_mla, vct_mla, kc_swa, vct_swa = _proj(
        ctx.reshape(BATCH * CTX_LEN, D_MODEL), mod3, lambda i: BATCH, lambda i: 0,
        tabs_ctx, consts, q_consts, win[:, :C_KVEND], False, "proj_ctx")

    o_a = _mla_attn(q_mla, k_mla, vt_mla, kc_mla, vct_mla)
    o_b = _swa_attn(swa_sink[0], q_swa, k_swa, vt_swa, kc_swa, vct_swa)

    x_new, h2, pos, top_w, seg = _merge(
        o_a, o_b, sig_a, sig_b, x2d, mod3,
        w_branch_a[0].astype(BF16), w_branch_b[0].astype(BF16), w_out[0].astype(BF16),
        norm2_g[0][None], w_router[0].T, b_router[0][:, None])

    seg3 = seg.reshape(N_TILES, SUBLANES, LANES)
    lstart, rows, gcarry = (seg3[:, r, :N_EXPERTS] for r in range(3))
    total = gcarry[-1] + rows[-1]
    padded = (total + MOE_BLK - 1) // MOE_BLK * MOE_BLK
    pends = jnp.cumsum(padded).astype(I32)
    pstarts = pends - padded
    nused = (pends[-1] // MOE_BLK).astype(I32)[None]
    blk_start = jnp.arange(N_BLK, dtype=I32) * MOE_BLK
    block_expert = jnp.minimum(
        jnp.sum((blk_start[:, None] >= pends[None, :]).astype(I32), axis=1), N_EXPERTS - 1)
    assert LROWS <= 1 << CH_SRC_BITS and N_SLOTS < 1 << (31 - CH_SRC_BITS) and SEG_TOTAL < SEG_W
    units = (rows // SEG_ALIGN)[:, None, :]
    cls = jnp.arange(len(SEG_BITS), dtype=I32)[None, :, None]
    has = (units >> cls) & 1
    done = ((units >> (cls + 1)) << (cls + 1)) * SEG_ALIGN
    word = ((pstarts[None, None, :] + gcarry[:, None, :] + done) << CH_SRC_BITS) + lstart[:, None, :] + done
    place = jnp.cumsum(has, axis=2) - has
    lists = jnp.sum(jnp.where((place[..., None] == jnp.arange(N_EXPERTS, dtype=I32)) & (has[..., None] == 1),
                              word[..., None], 0), axis=2)
    segtab = jnp.concatenate(
        [lists.reshape(N_TILES, SEG_COUNT), jnp.sum(has, axis=2), jnp.sum(rows, axis=1, keepdims=True),
         jnp.zeros((N_TILES, SEG_W - SEG_TOTAL - 1), I32)], axis=1).reshape(-1).astype(I32)

    xs = _dispatch(pends, padded.astype(I32), segtab, pos, h2)
    bgu = b_gate_up[0]
    eo = _experts(block_expert, nused, xs, w_gate_up[0], bgu[:, None, 0::2], bgu[:, None, 1::2],
                  w_down[0], b_down[0][:, None, :])
    out = _combine(segtab, eo, x_new, mod3, pos.T, top_w.T)
    return out.reshape(BATCH, SEQ, D_MODEL)
```
